```python
import math
import jax, jax.numpy as jnp
from jax import lax
import numpy as np


D_MODEL = 1024
BATCH = 16
SEQ = 2048
DEPTH = 4

N_MIXERS = 3
Q_BLK = 128
NORM_EPS = 1e-5
RMS_EPS = 1e-6

REL_BUCKETS = 32
REL_MAX_DIST = 2048
REL_HEADS = 16

A_HEADS = 16
A_HEAD_DIM = 128
A_IDX_HEADS = 8
A_IDX_DIM = 64
A_TOPK_MAX = 256
A_Q = A_HEADS * A_HEAD_DIM
A_SPLITS = (A_Q, A_Q + A_HEAD_DIM, A_Q + 2 * A_HEAD_DIM,
            A_Q + 2 * A_HEAD_DIM + A_IDX_HEADS * A_IDX_DIM,
            A_Q + 2 * A_HEAD_DIM + A_IDX_HEADS * A_IDX_DIM + A_IDX_DIM)
A_IN = A_SPLITS[-1] + A_IDX_HEADS

B_GROUPS = ((128, 1), (512, 4), (2048, 16))
B_HEADS = 16
B_HEAD_DIM = 64
B_IN = len(B_GROUPS) * 3 * B_HEADS * B_HEAD_DIM

C_HEADS = 16
C_Q_RANK = 256
C_KV_RANK = 128
C_NOPE = 64
C_ROPE = 32
C_V = 64
C_IN = C_Q_RANK + C_KV_RANK + C_ROPE
ROPE_BASE = 10000.0

N_EXPERTS = 64
TOP_K = 8
N_GROUPS = 8
TOPK_GROUPS = 4
D_EXPERT = 256
D_SHARED = 256
ROUTED_SCALE = 2.5
MOE_BLK = 256

DN_ALPHA = (2 * DEPTH) ** 0.25
DN_BETA = (8 * DEPTH) ** -0.25
N_A = len(range(0, DEPTH, N_MIXERS))
N_B = len(range(1, DEPTH, N_MIXERS))
N_C = len(range(2, DEPTH, N_MIXERS))

kernel_name = 'hybrid_dsa_dilated_mla_moe_deepnorm'


def layer_norm(x, g, b):
    xf = x.astype(jnp.float32)
    mu = jnp.mean(xf, axis=-1, keepdims=True)
    var = jnp.mean(jnp.square(xf - mu), axis=-1, keepdims=True)
    return ((xf - mu) * lax.rsqrt(var + NORM_EPS) * g.astype(jnp.float32) + b.astype(jnp.float32)).astype(x.dtype)


def rms_norm(x, g):
    xf = x.astype(jnp.float32)
    y = xf * lax.rsqrt(jnp.mean(jnp.square(xf), axis=-1, keepdims=True) + RMS_EPS)
    return (y * g.astype(jnp.float32)).astype(x.dtype)


def rope(x, pos):
    half = x.shape[-1] // 2
    inv = ROPE_BASE ** (-jnp.arange(half, dtype=jnp.float32) / half)
    ang = pos.astype(jnp.float32)[:, None] * inv[None, :]
    cos, sin = jnp.cos(ang)[:, None, :], jnp.sin(ang)[:, None, :]
    xf = x.astype(jnp.float32)
    x1, x2 = xf[..., :half], xf[..., half:]
    return jnp.concatenate([x1 * cos - x2 * sin, x1 * sin + x2 * cos], axis=-1).astype(x.dtype)


def t5_bucket(dist):
    exact = REL_BUCKETS // 2
    d_f = jnp.maximum(dist, 1).astype(jnp.float32)
    large = exact + (jnp.log(d_f / exact) / math.log(REL_MAX_DIST / exact) * (REL_BUCKETS - exact)).astype(jnp.int32)
    return jnp.where(dist < exact, dist, jnp.minimum(large, REL_BUCKETS - 1))


def t5_bias(rel_bias, dist):
    return rel_bias[t5_bucket(dist)]


def to_blocks(a):
    b, s = a.shape[0], a.shape[1]
    return jnp.swapaxes(a.reshape(b, s // Q_BLK, Q_BLK, *a.shape[2:]), 0, 1)


def from_blocks(a):
    a = jnp.swapaxes(a, 0, 1)
    return a.reshape(a.shape[0], a.shape[1] * a.shape[2], *a.shape[3:])


def dsa_mixer(x, w_in, w_o, rel_bias):
    bsz, seq, _ = x.shape
    k_sel = min(A_TOPK_MAX, seq // 4)
    q, k, v, qi, ki, wi = jnp.split(x @ w_in, list(A_SPLITS), axis=-1)
    q = q.reshape(bsz, seq, A_HEADS, A_HEAD_DIM)
    qi = qi.reshape(bsz, seq, A_IDX_HEADS, A_IDX_DIM)
    ki = ki.astype(jnp.float32)
    pos = jnp.arange(seq)
    gather = jax.vmap(lambda a, i: a[i])

    def block(args):
        start, q_b, qi_b, wi_b = args
        t = start + jnp.arange(Q_BLK)
        rel = jax.nn.relu(jnp.einsum('bthe,bse->bhts', qi_b.astype(jnp.float32), ki) * A_IDX_DIM ** -0.5)
        score = jnp.einsum('bth,bhts->bts', wi_b.astype(jnp.float32) * A_IDX_HEADS ** -0.5, rel)
        score = jnp.where(pos[None, :] <= t[:, None], score, -jnp.inf)
        _, idx = lax.top_k(score, k_sel)
        k_g, v_g = gather(k, idx), gather(v, idx)
        logits = jnp.einsum('bthe,btke->bhtk', q_b, k_g).astype(jnp.float32) * A_HEAD_DIM ** -0.5
        dist = t[None, :, None] - idx
        logits = logits + jnp.moveaxis(t5_bias(rel_bias, jnp.maximum(dist, 0)), -1, 1).astype(jnp.float32)
        logits = jnp.where((dist >= 0)[:, None], logits, -jnp.inf)
        p = jax.nn.softmax(logits, axis=-1).astype(v.dtype)
        return jnp.einsum('bhtk,btke->bthe', p, v_g)

    starts = jnp.arange(seq // Q_BLK) * Q_BLK
    o = lax.map(block, (starts, to_blocks(q), to_blocks(qi), to_blocks(wi)))
    return from_blocks(o).reshape(bsz, seq, A_Q) @ w_o


def dilated_group(q, k, v, rel_bias, window, dil):
    bsz, seq, nh, hd = q.shape
    n = window // dil
    span = n * dil
    sp = -(-seq // span) * span
    nb = sp // span

    def split(a):
        a = jnp.pad(a, ((0, 0), (0, sp - seq), (0, 0), (0, 0)))
        return a.reshape(bsz, nb, n, dil, nh, hd)

    def with_prev(a):
        prev = jnp.pad(a, ((0, 0), (1, 0), (0, 0), (0, 0), (0, 0), (0, 0)))[:, :-1]
        return jnp.concatenate([prev, a], axis=2)

    qs = split(q)
    kk, vv = with_prev(split(k)), with_prev(split(v))
    logits = jnp.einsum('bcirhe,bcjrhe->bcrhij', qs, kk).astype(jnp.float32) * hd ** -0.5
    ii = jnp.arange(n)[:, None]
    jj = jnp.arange(2 * n)[None, :]
    delta = n + ii - jj
    band = (delta >= 0) & (delta <= n)
    exists = (jnp.arange(nb)[:, None, None] > 0) | (jj[None] >= n)
    mask = band[None] & exists
    bias = jnp.transpose(t5_bias(rel_bias, jnp.maximum(delta, 0) * dil), (2, 0, 1))
    logits = logits + bias.astype(jnp.float32)
    logits = jnp.where(mask[None, :, None, None], logits, -jnp.inf)
    lse = jax.nn.logsumexp(logits, axis=-1)
    p = jnp.exp(logits - lse[..., None]).astype(v.dtype)
    o = jnp.einsum('bcrhij,bcjrhe->bcirhe', p, vv).reshape(bsz, sp, nh, hd)[:, :seq]
    lse = jnp.transpose(lse, (0, 1, 4, 2, 3)).reshape(bsz, sp, nh)[:, :seq]
    return o, lse


def dilated_mixer(x, w_in, w_o, rel_bias):
    bsz, seq, _ = x.shape
    proj = (x @ w_in).reshape(bsz, seq, len(B_GROUPS), 3, B_HEADS, B_HEAD_DIM)
    outs, lses = [], []
    for g, (window, dil) in enumerate(B_GROUPS):
        o, l = dilated_group(proj[:, :, g, 0], proj[:, :, g, 1], proj[:, :, g, 2], rel_bias, window, dil)
        outs.append(o)
        lses.append(l)
    wts = jax.nn.softmax(jnp.stack(lses, axis=2), axis=2)
    o = jnp.einsum('bsgh,bsghe->bshe', wts.astype(x.dtype), jnp.stack(outs, axis=2))
    return o.reshape(bsz, seq, B_HEADS * B_HEAD_DIM) @ w_o


def mla_mixer(x, w_in, q_norm, kv_norm, w_q_up, w_kv_up, w_o):
    bsz, seq, _ = x.shape
    c_q, c_kv, k_r = jnp.split(x @ w_in, [C_Q_RANK, C_Q_RANK + C_KV_RANK], axis=-1)
    q = (rms_norm(c_q, q_norm) @ w_q_up).reshape(bsz, seq, C_HEADS, C_NOPE + C_ROPE)
    kv = (rms_norm(c_kv, kv_norm) @ w_kv_up).reshape(bsz, seq, C_HEADS, C_NOPE + C_V)
    pos = jnp.arange(seq)
    q_nope, q_rope = q[..., :C_NOPE], rope(q[..., C_NOPE:], pos)
    k_rope = rope(k_r[:, :, None, :], pos)[:, :, 0]
    k_nope, v = kv[..., :C_NOPE], kv[..., C_NOPE:]
    scale = (C_NOPE + C_ROPE) ** -0.5

    def block(args):
        start, qn_b, qr_b = args
        t = start + jnp.arange(Q_BLK)
        logits = (jnp.einsum('bthe,bshe->bhts', qn_b, k_nope)
                  + jnp.einsum('bthe,bse->bhts', qr_b, k_rope)).astype(jnp.float32) * scale
        logits = jnp.where(pos[None, :] <= t[:, None], logits, -jnp.inf)
        p = jax.nn.softmax(logits, axis=-1).astype(v.dtype)
        return jnp.einsum('bhts,bshe->bthe', p, v)

    starts = jnp.arange(seq // Q_BLK) * Q_BLK
    o = lax.map(block, (starts, to_blocks(q_nope), to_blocks(q_rope)))
    return from_blocks(o).reshape(bsz, seq, C_HEADS * C_V) @ w_o


def swiglu(x, w_gate, w_up, w_down):
    return (jax.nn.silu(x @ w_gate) * (x @ w_up)) @ w_down


def grouped_experts(tok, eidx, gate, w_gate, w_up, w_down):
    n_tok, d = tok.shape
    n_asg = n_tok * TOP_K
    flat_e = eidx.reshape(-1)
    order = jnp.argsort(flat_e)
    sorted_e = flat_e[order]
    counts = jnp.bincount(flat_e, length=N_EXPERTS)
    padded = (counts + MOE_BLK - 1) // MOE_BLK * MOE_BLK
    pad_end = jnp.cumsum(padded)
    pad_start = pad_end - padded
    start = jnp.cumsum(counts) - counts
    dest = pad_start[sorted_e] + jnp.arange(n_asg) - start[sorted_e]
    n_blk = -(-n_asg // MOE_BLK) + N_EXPERTS
    n_slot = n_blk * MOE_BLK
    slot_tok = jnp.full((n_slot,), n_tok, jnp.int32).at[dest].set((order // TOP_K).astype(jnp.int32))
    slot_w = jnp.zeros((n_slot,), gate.dtype).at[dest].set(gate.reshape(-1)[order])
    blk_e = jnp.minimum(jnp.searchsorted(pad_end, jnp.arange(n_blk) * MOE_BLK, side='right'), N_EXPERTS - 1)
    tok_pad = jnp.concatenate([tok, jnp.zeros((1, d), tok.dtype)], axis=0)

    def block(args):
        e, rows, wts = args
        xb = tok_pad[rows]
        return swiglu(xb, w_gate[e], w_up[e], w_down[e]) * wts[:, None].astype(xb.dtype)

    out = lax.map(block, (blk_e, slot_tok.reshape(n_blk, MOE_BLK), slot_w.reshape(n_blk, MOE_BLK)))
    return jax.ops.segment_sum(out.reshape(n_slot, d), slot_tok, num_segments=n_tok + 1)[:n_tok]


def moe(h, w_router, e_bias, w_gate, w_up, w_down, ws_gate, ws_up, ws_down):
    bsz, seq, d = h.shape
    tok = h.reshape(-1, d)
    n_tok = tok.shape[0]
    scores = jax.nn.sigmoid((tok @ w_router).astype(jnp.float32))
    choice = scores + e_bias.astype(jnp.float32)
    grp_score = lax.top_k(choice.reshape(n_tok, N_GROUPS, N_EXPERTS // N_GROUPS), 2)[0].sum(-1)
    _, gi = lax.top_k(grp_score, TOPK_GROUPS)
    gmask = jnp.sum(jax.nn.one_hot(gi, N_GROUPS, dtype=jnp.float32), axis=1) > 0
    choice = jnp.where(jnp.repeat(gmask, N_EXPERTS // N_GROUPS, axis=1), choice, -jnp.inf)
    _, eidx = lax.top_k(choice, TOP_K)
    gate = jnp.take_along_axis(scores, eidx, axis=-1)
    gate = gate / jnp.sum(gate, axis=-1, keepdims=True) * ROUTED_SCALE
    routed = grouped_experts(tok, eidx, gate, w_gate, w_up, w_down)
    return (routed + swiglu(tok, ws_gate, ws_up, ws_down)).reshape(bsz, seq, d)


def setup_inputs(seed: int = 0) -> dict:
    key = jax.random.key(seed)
    keys = iter(jax.random.split(key, 32))

    def nrm(shape, fan_in, scale=1.0):
        return jax.random.normal(next(keys), shape, jnp.float32) * (scale * fan_in ** -0.5)

    def gain(shape):
        return 1.0 + 0.02 * jax.random.normal(next(keys), shape, jnp.float32)

    d = D_MODEL
    return {
        'x': jax.random.normal(next(keys), (BATCH, SEQ, d), jnp.float32),
        'rel_bias': 0.2 * jax.random.normal(next(keys), (REL_BUCKETS, REL_HEADS), jnp.float32),
        'a_w_in': nrm((N_A, d, A_IN), d),
        'a_w_o': nrm((N_A, A_Q, d), A_Q, DN_BETA),
        'b_w_in': nrm((N_B, d, B_IN), d),
        'b_w_o': nrm((N_B, B_HEADS * B_HEAD_DIM, d), B_HEADS * B_HEAD_DIM, DN_BETA),
        'c_w_in': nrm((N_C, d, C_IN), d),
        'c_q_norm': gain((N_C, C_Q_RANK)),
        'c_kv_norm': gain((N_C, C_KV_RANK)),
        'c_w_q_up': nrm((N_C, C_Q_RANK, C_HEADS * (C_NOPE + C_ROPE)), C_Q_RANK),
        'c_w_kv_up': nrm((N_C, C_KV_RANK, C_HEADS * (C_NOPE + C_V)), C_KV_RANK),
        'c_w_o': nrm((N_C, C_HEADS * C_V, d), C_HEADS * C_V, DN_BETA),
        'ln_g': gain((DEPTH, 2, d)),
        'ln_b': 0.02 * jax.random.normal(next(keys), (DEPTH, 2, d), jnp.float32),
        'moe_w_router': nrm((DEPTH, d, N_EXPERTS), d),
        'moe_bias': 0.01 * jax.random.normal(next(keys), (DEPTH, N_EXPERTS), jnp.float32),
        'moe_w_gate': nrm((DEPTH, N_EXPERTS, d, D_EXPERT), d),
        'moe_w_up': nrm((DEPTH, N_EXPERTS, d, D_EXPERT), d),
        'moe_w_down': nrm((DEPTH, N_EXPERTS, D_EXPERT, d), D_EXPERT, DN_BETA),
        'moe_ws_gate': nrm((DEPTH, d, D_SHARED), d),
        'moe_ws_up': nrm((DEPTH, d, D_SHARED), d),
        'moe_ws_down': nrm((DEPTH, D_SHARED, d), D_SHARED, DN_BETA),
    }


def reference(x, rel_bias, a_w_in, a_w_o, b_w_in, b_w_o, c_w_in, c_q_norm, c_kv_norm,
              c_w_q_up, c_w_kv_up, c_w_o, ln_g, ln_b, moe_w_router, moe_bias, moe_w_gate,
              moe_w_up, moe_w_down, moe_ws_gate, moe_ws_up, moe_ws_down):
    h = x
    for layer in range(DEPTH):
        kind, slot = layer % N_MIXERS, layer // N_MIXERS
        if kind == 0:
            mix = dsa_mixer(h, a_w_in[slot], a_w_o[slot], rel_bias)
        elif kind == 1:
            mix = dilated_mixer(h, b_w_in[slot], b_w_o[slot], rel_bias)
        else:
            mix = mla_mixer(h, c_w_in[slot], c_q_norm[slot], c_kv_norm[slot],
                            c_w_q_up[slot], c_w_kv_up[slot], c_w_o[slot])
        h = layer_norm(DN_ALPHA * h + mix, ln_g[layer, 0], ln_b[layer, 0])
        ff = moe(h, moe_w_router[layer], moe_bias[layer], moe_w_gate[layer], moe_w_up[layer],
                 moe_w_down[layer], moe_ws_gate[layer], moe_ws_up[layer], moe_ws_down[layer])
        h = layer_norm(DN_ALPHA * h + ff, ln_g[layer, 1], ln_b[layer, 1])
    return h
```

```python
import functools
import math

import jax
import jax.numpy as jnp
from jax import lax
from jax.experimental import pallas as pl
from jax.experimental.pallas import tpu as pltpu

F32 = jnp.float32
BF16 = jnp.bfloat16
I32 = jnp.int32

LANES = 128
VMEM_LIMIT_BYTES = 56 * 1024 * 1024

D_MODEL = 1024
DEPTH = 4
N_MIXERS = 3
NORM_EPS = 1e-5
RMS_EPS = 1e-6
REL_BUCKETS = 32
REL_MAX_DIST = 2048
A_HEADS = 16
A_HEAD_DIM = 128
A_IDX_HEADS = 8
A_IDX_DIM = 64
A_TOPK_MAX = 256
A_Q = A_HEADS * A_HEAD_DIM
B_GROUPS = ((128, 1), (512, 4), (2048, 16))
B_HEADS = 16
B_HEAD_DIM = 64
B_N = 128
C_HEADS = 16
C_Q_RANK = 256
C_KV_RANK = 128
C_NOPE = 64
C_ROPE = 32
C_V = 64
ROPE_BASE = 10000.0
N_EXPERTS = 64
TOP_K = 8
N_GROUPS = 8
TOPK_GROUPS = 4
D_EXPERT = 256
ROUTED_SCALE = 2.5
DN_ALPHA = (2 * DEPTH) ** 0.25

Q_BLK = 128
KEY_CHUNK = 512
MOE_BLK = 256
MASK_NEG = -1e30
INT_MIN = -(2 ** 31)


def _params(*sem):
    return pltpu.CompilerParams(dimension_semantics=sem, vmem_limit_bytes=VMEM_LIMIT_BYTES)


def _mm_kernel(x_ref, w_ref, o_ref):
    o_ref[...] = jnp.dot(x_ref[...], w_ref[...], preferred_element_type=F32).astype(o_ref.dtype)


def matmul(x, w, out_dtype, tm=1024, tn=None):
    m, k = x.shape
    n = w.shape[1]
    tn = n if tn is None else tn
    tm = min(tm, m)
    return pl.pallas_call(
        _mm_kernel,
        out_shape=jax.ShapeDtypeStruct((m, n), out_dtype),
        grid=(n // tn, m // tm),
        in_specs=[pl.BlockSpec((tm, k), lambda j, i: (i, 0)),
                  pl.BlockSpec((k, tn), lambda j, i: (0, j))],
        out_specs=pl.BlockSpec((tm, tn), lambda j, i: (i, j)),
        compiler_params=_params("parallel", "parallel"),
        name="matmul",
    )(x, w)


def _layer_norm_rows(z, g, b):
    mu = jnp.mean(z, axis=-1, keepdims=True)
    zc = z - mu
    var = jnp.mean(zc * zc, axis=-1, keepdims=True)
    return zc * lax.rsqrt(var + NORM_EPS) * g + b


def _mm_ln_kernel(x_ref, w_ref, h_ref, g_ref, b_ref, o_ref, ob_ref):
    y = jnp.dot(x_ref[...], w_ref[...], preferred_element_type=F32)
    out = _layer_norm_rows(DN_ALPHA * h_ref[...] + y, g_ref[...], b_ref[...])
    o_ref[...] = out
    ob_ref[...] = out.astype(BF16)


def matmul_residual_ln(x, w, h, g, b, tm=512):
    m, k = x.shape
    d = w.shape[1]
    tm = min(tm, m)
    row = lambda i: (i, 0)
    fixed = lambda i: (0, 0)
    return pl.pallas_call(
        _mm_ln_kernel,
        out_shape=(jax.ShapeDtypeStruct((m, d), F32), jax.ShapeDtypeStruct((m, d), BF16)),
        grid=(m // tm,),
        in_specs=[pl.BlockSpec((tm, k), row), pl.BlockSpec((k, d), fixed),
                  pl.BlockSpec((tm, d), row), pl.BlockSpec((1, d), fixed),
                  pl.BlockSpec((1, d), fixed)],
        out_specs=(pl.BlockSpec((tm, d), row), pl.BlockSpec((tm, d), row)),
        compiler_params=_params("parallel"),
        name="matmul_residual_ln",
    )(x, w, h, g.reshape(1, d), b.reshape(1, d))


def _t5_bucket(dist):
    exact = REL_BUCKETS // 2
    d_f = jnp.maximum(dist, 1).astype(F32)
    large = exact + (jnp.log(d_f / exact) / math.log(REL_MAX_DIST / exact)
                     * (REL_BUCKETS - exact)).astype(I32)
    return jnp.where(dist < exact, dist, jnp.minimum(large, REL_BUCKETS - 1))


def _bias_by_distance(rel_bias, dist):
    return rel_bias[_t5_bucket(dist)]


def _dsa_kernel(k_sel, seq, q_ref, kv_ref, iq_ref, ik_ref, strip_ref, o_ref,
                key_sc, mask_sc, cut_sc, qs_sc, s_sc, p_sc, acc_sc, m_sc, l_sc, a_sc):
    i = pl.program_id(1)
    t0 = i * Q_BLK
    rows = lax.broadcasted_iota(I32, (Q_BLK, seq), 0) + t0
    cols = lax.broadcasted_iota(I32, (Q_BLK, seq), 1)
    valid = cols <= rows

    ik = ik_ref[0].astype(BF16)
    w_all = iq_ref[0, :, A_IDX_HEADS * LANES:] * ((A_IDX_DIM * A_IDX_HEADS) ** -0.5)
    score = jnp.zeros((Q_BLK, seq), F32)
    for h in range(A_IDX_HEADS):
        qh = iq_ref[0, :, h * LANES:(h + 1) * LANES].astype(BF16)
        rel = lax.dot_general(qh, ik, (((1,), (1,)), ((), ())), preferred_element_type=F32)
        score = score + w_all[:, A_IDX_DIM + h:A_IDX_DIM + h + 1] * jnp.maximum(rel, 0.0)
    score = jnp.where(score == 0.0, 0.0, score)

    bits = pltpu.bitcast(score, I32)
    key = jnp.where(valid, bits ^ ((bits >> 31) & 0x7FFFFFFF), INT_MIN)
    key_sc[...] = key

    def search(it, ans_u):
        cand_u = ans_u | lax.shift_left(jnp.int32(1), 31 - it)
        cand_s = cand_u ^ INT_MIN
        cnt = jnp.sum(jnp.where(key_sc[...] >= cand_s, 1.0, 0.0), axis=-1, keepdims=True)
        return jnp.where(cnt >= k_sel, cand_u, ans_u)

    ans_u = lax.fori_loop(0, 32, search, jnp.zeros((Q_BLK, 1), I32))
    thr = ans_u ^ INT_MIN

    key = key_sc[...]
    gt = key > thr
    eq = key == thr
    need = k_sel - jnp.sum(jnp.where(gt, 1.0, 0.0), axis=-1, keepdims=True)
    n_eq = jnp.sum(jnp.where(eq, 1.0, 0.0), axis=-1, keepdims=True)
    cut_sc[...] = jnp.full((Q_BLK, 1), seq, I32)
    surplus = jnp.where((n_eq > need) & (thr != INT_MIN), 1.0, 0.0)

    @pl.when(jnp.max(surplus) > 0.0)
    def _():
        def tie_search(it, ans):
            cand = ans | lax.shift_left(jnp.int32(1), (seq.bit_length() - 1) - it)
            hit = (key_sc[...] == thr) & (cols < cand)
            cnt = jnp.sum(jnp.where(hit, 1.0, 0.0), axis=-1, keepdims=True)
            return jnp.where(cnt < need, cand, ans)
        cut_sc[...] = lax.fori_loop(0, seq.bit_length(), tie_search, jnp.zeros((Q_BLK, 1), I32))

    selected = valid & (gt | (eq & (cols <= cut_sc[...])))
    mask_sc[...] = jnp.where(selected, 0.0, MASK_NEG)

    for h in range(A_HEADS):
        qs_sc[h * Q_BLK:(h + 1) * Q_BLK, :] = q_ref[0, :, h * A_HEAD_DIM:(h + 1) * A_HEAD_DIM]
    m_sc[...] = jnp.full(m_sc.shape, MASK_NEG, F32)
    l_sc[...] = jnp.zeros(l_sc.shape, F32)
    acc_sc[...] = jnp.zeros(acc_sc.shape, F32)
    n_strip_blk = seq // Q_BLK - 1

    def chunk(c, carry):
        k0 = pl.multiple_of(c * KEY_CHUNK, KEY_CHUNK)
        kc = kv_ref[0, pl.ds(k0, KEY_CHUNK), :A_HEAD_DIM]
        vc = kv_ref[0, pl.ds(k0, KEY_CHUNK), A_HEAD_DIM:]
        s_sc[...] = lax.dot_general(qs_sc[...], kc, (((1,), (1,)), ((), ())),
                                    preferred_element_type=F32)
        w0 = pl.multiple_of((c * (KEY_CHUNK // Q_BLK) - i + n_strip_blk) * Q_BLK, Q_BLK)
        mk = mask_sc[:, pl.ds(k0, KEY_CHUNK)]
        for h in range(A_HEADS):
            r = slice(h * Q_BLK, (h + 1) * Q_BLK)
            s = s_sc[r, :] + strip_ref[h, :, pl.ds(w0, KEY_CHUNK)].astype(F32) + mk
            m_old = m_sc[r, :]
            m_new = jnp.maximum(m_old, jnp.max(s, axis=-1, keepdims=True))
            alpha = jnp.exp(m_old - m_new)
            p = jnp.exp(s - m_new)
            l_sc[r, :] = alpha * l_sc[r, :] + jnp.sum(p, axis=-1, keepdims=True)
            m_sc[r, :] = m_new
            a_sc[r, :] = alpha
            p_sc[r, :] = p.astype(BF16)
        pv = jnp.dot(p_sc[...], vc, preferred_element_type=F32)
        acc_sc[...] = acc_sc[...] * a_sc[...] + pv
        return carry

    lax.fori_loop(0, (t0 + Q_BLK + KEY_CHUNK - 1) // KEY_CHUNK, chunk, 0)
    for h in range(A_HEADS):
        r = slice(h * Q_BLK, (h + 1) * Q_BLK)
        o_ref[0, :, h * A_HEAD_DIM:(h + 1) * A_HEAD_DIM] = (acc_sc[r, :] / l_sc[r, :]).astype(BF16)


def dsa_attention(qkv, idx, strip, bsz, seq):
    k_sel = min(A_TOPK_MAX, seq // 4)
    rows = A_HEADS * Q_BLK
    n_kv_blk = A_Q // (2 * A_HEAD_DIM)
    n_ik_blk = A_IDX_HEADS
    idx_w = (A_IDX_HEADS + 1) * LANES
    return pl.pallas_call(
        functools.partial(_dsa_kernel, k_sel, seq),
        out_shape=jax.ShapeDtypeStruct((bsz, seq, A_Q), BF16),
        grid=(bsz, seq // Q_BLK),
        in_specs=[pl.BlockSpec((1, Q_BLK, A_Q), lambda b, i: (b, i, 0)),
                  pl.BlockSpec((1, seq, 2 * A_HEAD_DIM), lambda b, i: (b, 0, n_kv_blk)),
                  pl.BlockSpec((1, Q_BLK, idx_w), lambda b, i: (b, i, 0)),
                  pl.BlockSpec((1, seq, LANES), lambda b, i: (b, 0, n_ik_blk)),
                  pl.BlockSpec(strip.shape, lambda b, i: (0, 0, 0))],
        out_specs=pl.BlockSpec((1, Q_BLK, A_Q), lambda b, i: (b, i, 0)),
        scratch_shapes=[pltpu.VMEM((Q_BLK, seq), I32),
                        pltpu.VMEM((Q_BLK, seq), F32),
                        pltpu.VMEM((Q_BLK, 1), I32),
                        pltpu.VMEM((rows, A_HEAD_DIM), BF16),
                        pltpu.VMEM((rows, KEY_CHUNK), F32),
                        pltpu.VMEM((rows, KEY_CHUNK), BF16),
                        pltpu.VMEM((rows, A_HEAD_DIM), F32),
                        pltpu.VMEM((rows, 1), F32),
                        pltpu.VMEM((rows, 1), F32),
                        pltpu.VMEM((rows, 1), F32)],
        compiler_params=_params("parallel", "arbitrary"),
        name="dsa_attention",
    )(qkv, qkv, idx, idx, strip)


def dsa_bias_strip(rel_bias, seq):
    width = seq + KEY_CHUNK - Q_BLK
    tq = jnp.arange(Q_BLK)[:, None]
    c = jnp.arange(width)[None, :]
    dist = jnp.maximum(tq - c + (seq - Q_BLK), 0)
    return jnp.transpose(_bias_by_distance(rel_bias, dist), (2, 0, 1)).astype(BF16)


def dsa_weights(w_in):
    d = w_in.shape[0]
    wq = w_in[:, :A_Q] * (A_HEAD_DIM ** -0.5)
    wkv = w_in[:, A_Q:A_Q + 2 * A_HEAD_DIM]
    o = A_Q + 2 * A_HEAD_DIM
    n_qi = A_IDX_HEADS * A_IDX_DIM
    wqi = w_in[:, o:o + n_qi].reshape(d, A_IDX_HEADS, A_IDX_DIM)
    wqi = jnp.pad(wqi, ((0, 0), (0, 0), (0, LANES - A_IDX_DIM))).reshape(d, A_IDX_HEADS * LANES)
    wkw = jnp.pad(w_in[:, o + n_qi:], ((0, 0), (0, LANES - A_IDX_DIM - A_IDX_HEADS)))
    return (jnp.concatenate([wq, wkv], axis=1).astype(BF16),
            jnp.concatenate([wqi, wkw], axis=1).astype(BF16))


def dsa_mixer(hb, w_in, w_o, strip, h, g, b, bsz, seq):
    w_qkv, w_idx = dsa_weights(w_in)
    qkv = matmul(hb, w_qkv, BF16, tn=w_qkv.shape[1] // 2)
    idx = matmul(hb, w_idx, F32)
    o = dsa_attention(qkv.reshape(bsz, seq, -1), idx.reshape(bsz, seq, -1), strip, bsz, seq)
    return matmul_residual_ln(o.reshape(bsz * seq, A_Q), w_o.astype(BF16), h, g, b)


def _dilated_kernel(q_ref, kp_ref, kc_ref, vp_ref, vc_ref, bias_ref, o_ref, lse_ref):
    c = pl.program_id(2)
    col = lax.broadcasted_iota(I32, (B_N, 2 * B_N), 1)
    has_prev = (col >= B_N) | (c > 0)
    lane = lax.broadcasted_iota(I32, (B_N, LANES), 1)
    lse_tile = jnp.zeros((B_N, LANES), F32)
    pair = []
    for h in range(B_HEADS):
        hs = slice(h * B_HEAD_DIM, (h + 1) * B_HEAD_DIM)
        kk = jnp.concatenate([kp_ref[0, :, hs], kc_ref[0, :, hs]], axis=0)
        vv = jnp.concatenate([vp_ref[0, :, hs], vc_ref[0, :, hs]], axis=0)
        s = lax.dot_general(q_ref[0, :, hs], kk, (((1,), (1,)), ((), ())),
                            preferred_element_type=F32)
        s = jnp.where(has_prev, s + bias_ref[h], MASK_NEG)
        m = jnp.max(s, axis=-1, keepdims=True)
        p = jnp.exp(s - m)
        l = jnp.sum(p, axis=-1, keepdims=True)
        pair.append(jnp.dot(p.astype(BF16), vv, preferred_element_type=F32) / l)
        lse_tile = jnp.where(lane == h, m + jnp.log(l), lse_tile)
        if len(pair) == LANES // B_HEAD_DIM:
            o_ref[0, :, (h + 1) * B_HEAD_DIM - LANES:(h + 1) * B_HEAD_DIM] = (
                jnp.concatenate(pair, axis=-1).astype(BF16))
            pair = []
    lse_ref[0] = lse_tile


def dilated_group(proj, bias, g, dil, bsz, seq):
    width = B_HEADS * B_HEAD_DIM
    per_pos = proj.shape[-1] // width
    length = seq // dil
    pv = proj.reshape(bsz, length, dil * proj.shape[-1])
    blk = (1, B_N, width)

    def spec(which, prev):
        if prev:
            return pl.BlockSpec(blk, lambda b, r, c: (b, jnp.maximum(c - 1, 0), r * per_pos + g * 3 + which))
        return pl.BlockSpec(blk, lambda b, r, c: (b, c, r * per_pos + g * 3 + which))

    o, lse = pl.pallas_call(
        _dilated_kernel,
        out_shape=(jax.ShapeDtypeStruct((bsz, length, dil * width), BF16),
                   jax.ShapeDtypeStruct((bsz, length, dil * LANES), F32)),
        grid=(bsz, dil, length // B_N),
        in_specs=[spec(0, False), spec(1, True), spec(1, False), spec(2, True), spec(2, False),
                  pl.BlockSpec(bias.shape, lambda b, r, c: (0, 0, 0))],
        out_specs=(pl.BlockSpec(blk, lambda b, r, c: (b, c, r)),
                   pl.BlockSpec((1, B_N, LANES), lambda b, r, c: (b, c, r))),
        compiler_params=_params("parallel", "parallel", "parallel"),
        name="dilated_group",
    )(pv, pv, pv, pv, pv, bias)
    return o.reshape(bsz * seq, width), lse.reshape(bsz * seq, LANES)


def dilated_bias(rel_bias, dil):
    ii = jnp.arange(B_N)[:, None]
    jj = jnp.arange(2 * B_N)[None, :]
    delta = B_N + ii - jj
    band = (delta >= 0) & (delta <= B_N)
    bias = jnp.transpose(_bias_by_distance(rel_bias, jnp.maximum(delta, 0) * dil), (2, 0, 1))
    return jnp.where(band[None], bias.astype(F32), MASK_NEG)


def _dilated_merge_kernel(o0_ref, o1_ref, o2_ref, l0_ref, l1_ref, l2_ref, e_ref, w_ref,
                          h_ref, g_ref, b_ref, o_ref, ob_ref):
    l0, l1, l2 = l0_ref[...], l1_ref[...], l2_ref[...]
    m = jnp.maximum(jnp.maximum(l0, l1), l2)
    e0, e1, e2 = jnp.exp(l0 - m), jnp.exp(l1 - m), jnp.exp(l2 - m)
    inv = 1.0 / (e0 + e1 + e2)
    e_mat = e_ref[...]

    def spread(wt):
        hi = wt.astype(BF16)
        lo = (wt - hi.astype(F32)).astype(BF16)
        return (jnp.dot(hi, e_mat, preferred_element_type=F32)
                + jnp.dot(lo, e_mat, preferred_element_type=F32))

    mix = (spread(e0 * inv) * o0_ref[...].astype(F32) + spread(e1 * inv) * o1_ref[...].astype(F32)
           + spread(e2 * inv) * o2_ref[...].astype(F32))
    y = jnp.dot(mix.astype(BF16), w_ref[...], preferred_element_type=F32)
    out = _layer_norm_rows(DN_ALPHA * h_ref[...] + y, g_ref[...], b_ref[...])
    o_ref[...] = out
    ob_ref[...] = out.astype(BF16)


def dilated_merge(outs, lses, w_o, h, g, b, tm=512):
    m, d = h.shape
    width = B_HEADS * B_HEAD_DIM
    tm = min(tm, m)
    expand = (jnp.arange(LANES)[:, None] == (jnp.arange(width)[None, :] // B_HEAD_DIM)).astype(BF16)
    row = lambda i: (i, 0)
    fixed = lambda i: (0, 0)
    return pl.pallas_call(
        _dilated_merge_kernel,
        out_shape=(jax.ShapeDtypeStruct((m, d), F32), jax.ShapeDtypeStruct((m, d), BF16)),
        grid=(m // tm,),
        in_specs=[pl.BlockSpec((tm, width), row)] * 3 + [pl.BlockSpec((tm, LANES), row)] * 3
                 + [pl.BlockSpec((LANES, width), fixed), pl.BlockSpec((width, d), fixed),
                    pl.BlockSpec((tm, d), row), pl.BlockSpec((1, d), fixed), pl.BlockSpec((1, d), fixed)],
        out_specs=(pl.BlockSpec((tm, d), row), pl.BlockSpec((tm, d), row)),
        compiler_params=_params("parallel"),
        name="dilated_merge",
    )(*outs, *lses, expand, w_o.astype(BF16), h, g.reshape(1, d), b.reshape(1, d))


def dilated_mixer(hb, w_in, w_o, rel_bias, h, g, b, bsz, seq):
    d = w_in.shape[0]
    col_scale = jnp.where((jnp.arange(w_in.shape[1]) // (B_HEADS * B_HEAD_DIM)) % 3 == 0,
                          B_HEAD_DIM ** -0.5, 1.0)
    proj = matmul(hb, (w_in * col_scale[None, :]).astype(BF16), BF16, tn=1024)
    proj = proj.reshape(bsz, seq, -1)
    outs, lses = [], []
    for gi, (window, dil) in enumerate(B_GROUPS):
        assert window // dil == B_N and seq % window == 0
        o, lse = dilated_group(proj, dilated_bias(rel_bias, dil), gi, dil, bsz, seq)
        outs.append(o)
        lses.append(lse)
    return dilated_merge(outs, lses, w_o, h, g, b)


C_PAD = 128


def _mla_prep_kernel(x_ref, win_ref, qn_ref, kvn_ref, wqa_ref, wqb_ref, wka_ref, wv_ref,
                     cos_ref, sin_ref, q_ref, k_ref, v_ref):
    c = jnp.dot(x_ref[...], win_ref[...], preferred_element_type=F32)
    cos, sin = cos_ref[...], sin_ref[...]

    def rms(v, gain):
        return (v * lax.rsqrt(jnp.mean(v * v, axis=-1, keepdims=True) + RMS_EPS) * gain).astype(BF16)

    nq = rms(c[:, :C_Q_RANK], qn_ref[...])
    nkv = rms(c[:, C_Q_RANK:C_Q_RANK + C_KV_RANK], kvn_ref[...])
    o = C_Q_RANK + C_KV_RANK
    k_rope = c[:, o:o + C_PAD] * cos + c[:, o + C_PAD:o + 2 * C_PAD] * sin
    qa = jnp.dot(nq, wqa_ref[...], preferred_element_type=F32)
    qb = jnp.dot(nq, wqb_ref[...], preferred_element_type=F32)
    kn = jnp.dot(nkv, wka_ref[...], preferred_element_type=F32)
    for h in range(C_HEADS):
        hs = slice(h * C_PAD, (h + 1) * C_PAD)
        q_ref[:, hs] = (qa[:, hs] * cos + qb[:, hs] * sin).astype(BF16)
        k_ref[:, hs] = (kn[:, hs] + k_rope).astype(BF16)
    v_ref[...] = jnp.dot(nkv, wv_ref[...], preferred_element_type=F32).astype(BF16)


def _rot_half_cols(w):
    half = w.shape[-1] // 2
    return jnp.concatenate([-w[..., half:], w[..., :half]], axis=-1)


def mla_prep(hb, w_in, q_norm, kv_norm, w_q_up, w_kv_up, bsz, seq, tm=512):
    d = w_in.shape[0]
    scale = (C_NOPE + C_ROPE) ** -0.5
    pad_r = C_PAD - C_NOPE - C_ROPE
    w_kr = w_in[:, C_Q_RANK + C_KV_RANK:]

    def rope_slot(w):
        return jnp.pad(w, ((0, 0), (C_NOPE, pad_r)))

    win = jnp.concatenate([w_in[:, :C_Q_RANK + C_KV_RANK], rope_slot(w_kr),
                           rope_slot(_rot_half_cols(w_kr))], axis=1).astype(BF16)
    wq = w_q_up.reshape(C_Q_RANK, C_HEADS, C_NOPE + C_ROPE) * scale
    wqa = jnp.pad(wq, ((0, 0), (0, 0), (0, pad_r))).reshape(C_Q_RANK, C_HEADS * C_PAD).astype(BF16)
    wqb = jnp.pad(_rot_half_cols(wq[..., C_NOPE:]), ((0, 0), (0, 0), (C_NOPE, pad_r)))
    wqb = wqb.reshape(C_Q_RANK, C_HEADS * C_PAD).astype(BF16)
    wkv = w_kv_up.reshape(C_KV_RANK, C_HEADS, C_NOPE + C_V)
    wka = jnp.pad(wkv[..., :C_NOPE], ((0, 0), (0, 0), (0, C_PAD - C_NOPE)))
    wka = wka.reshape(C_KV_RANK, C_HEADS * C_PAD).astype(BF16)
    wv = wkv[..., C_NOPE:].reshape(C_KV_RANK, C_HEADS * C_V).astype(BF16)

    half = C_ROPE // 2
    inv = ROPE_BASE ** (-jnp.arange(half, dtype=F32) / half)
    ang = jnp.arange(seq, dtype=F32)[:, None] * inv[None, :]
    ones, zeros = jnp.ones((seq, C_NOPE), F32), jnp.zeros((seq, pad_r), F32)
    cos = jnp.concatenate([ones, jnp.cos(ang), jnp.cos(ang), zeros], axis=1)
    sin = jnp.concatenate([0 * ones, jnp.sin(ang), jnp.sin(ang), zeros], axis=1)

    m = bsz * seq
    tm = min(tm, seq)
    n_pos_blk = seq // tm
    row = lambda i: (i, 0)
    fixed = lambda i: (0, 0)
    pos = lambda i: (i % n_pos_blk, 0)
    full = lambda a: pl.BlockSpec(a.shape, fixed)
    qw, vw = C_HEADS * C_PAD, C_HEADS * C_V
    return pl.pallas_call(
        _mla_prep_kernel,
        out_shape=(jax.ShapeDtypeStruct((m, qw), BF16), jax.ShapeDtypeStruct((m, qw), BF16),
                   jax.ShapeDtypeStruct((m, vw), BF16)),
        grid=(m // tm,),
        in_specs=[pl.BlockSpec((tm, d), row), full(win),
                  pl.BlockSpec((1, C_Q_RANK), fixed), pl.BlockSpec((1, C_KV_RANK), fixed),
                  full(wqa), full(wqb), full(wka), full(wv),
                  pl.BlockSpec((tm, C_PAD), pos), pl.BlockSpec((tm, C_PAD), pos)],
        out_specs=(pl.BlockSpec((tm, qw), row), pl.BlockSpec((tm, qw), row),
                   pl.BlockSpec((tm, vw), row)),
        compiler_params=_params("parallel"),
        name="mla_prep",
    )(hb, win, q_norm.reshape(1, -1), kv_norm.reshape(1, -1), wqa, wqb, wka, wv, cos, sin)


MLA_TQ = 512


def _mla_attn_kernel(q_ref, k_ref, v_ref, o_ref):
    i = pl.program_id(2)
    tq = MLA_TQ
    rows = lax.broadcasted_iota(I32, (tq, tq), 0)
    cols = lax.broadcasted_iota(I32, (tq, tq), 1)
    outs = []
    for hh in range(2):
        q = q_ref[0, :, hh * C_PAD:(hh + 1) * C_PAD]

        def step(c, carry, diag):
            m_old, l_old, acc = carry
            k0 = pl.multiple_of(c * tq, tq)
            kc = k_ref[0, pl.ds(k0, tq), hh * C_PAD:(hh + 1) * C_PAD]
            vc = v_ref[0, pl.ds(k0, tq), hh * C_V:(hh + 1) * C_V]
            s = lax.dot_general(q, kc, (((1,), (1,)), ((), ())), preferred_element_type=F32)
            if diag:
                s = jnp.where(cols <= rows, s, MASK_NEG)
            m_new = jnp.maximum(m_old, jnp.max(s, axis=-1, keepdims=True))
            alpha = jnp.exp(m_old - m_new)
            p = jnp.exp(s - m_new)
            l_new = alpha * l_old + jnp.sum(p, axis=-1, keepdims=True)
            acc = alpha * acc + jnp.dot(p.astype(BF16), vc, preferred_element_type=F32)
            return m_new, l_new, acc

        init = (jnp.full((tq, 1), MASK_NEG, F32), jnp.zeros((tq, 1), F32), jnp.zeros((tq, C_V), F32))
        carry = lax.fori_loop(0, i, functools.partial(step, diag=False), init)
        _, l, acc = step(i, carry, diag=True)
        outs.append(acc / l)
    o_ref[0] = jnp.concatenate(outs, axis=-1).astype(BF16)


def mla_attention(q, k, v, bsz, seq):
    qw, vw = C_HEADS * C_PAD, C_HEADS * C_V
    return pl.pallas_call(
        _mla_attn_kernel,
        out_shape=jax.ShapeDtypeStruct((bsz, seq, vw), BF16),
        grid=(bsz, C_HEADS // 2, seq // MLA_TQ),
        in_specs=[pl.BlockSpec((1, MLA_TQ, 2 * C_PAD), lambda b, hp, i: (b, i, hp)),
                  pl.BlockSpec((1, seq, 2 * C_PAD), lambda b, hp, i: (b, 0, hp)),
                  pl.BlockSpec((1, seq, 2 * C_V), lambda b, hp, i: (b, 0, hp))],
        out_specs=pl.BlockSpec((1, MLA_TQ, 2 * C_V), lambda b, hp, i: (b, i, hp)),
        compiler_params=_params("parallel", "parallel", "arbitrary"),
        name="mla_attention",
    )(q.reshape(bsz, seq, qw), k.reshape(bsz, seq, qw), v.reshape(bsz, seq, vw))


def mla_mixer(hb, w_in, q_norm, kv_norm, w_q_up, w_kv_up, w_o, h, g, b, bsz, seq):
    q, k, v = mla_prep(hb, w_in, q_norm, kv_norm, w_q_up, w_kv_up, bsz, seq)
    o = mla_attention(q, k, v, bsz, seq)
    return matmul_residual_ln(o.reshape(bsz * seq, -1), w_o.astype(BF16), h, g, b)


def _router_kernel(h_ref, wr_ref, eb_ref, eidx_ref, gate_ref):
    tm = h_ref.shape[0]
    per = N_EXPERTS // N_GROUPS
    logits = lax.dot_general(wr_ref[...], h_ref[...], (((1,), (1,)), ((), ())),
                             precision=lax.Precision.HIGHEST, preferred_element_type=F32)
    scores = 1.0 / (1.0 + jnp.exp(-logits))
    s3 = scores.reshape(N_GROUPS, per, tm)
    c3 = (scores + eb_ref[...]).reshape(N_GROUPS, per, tm)
    neg_inf = -jnp.inf
    j_idx = lax.broadcasted_iota(I32, (N_GROUPS, per, tm), 1).astype(F32)
    g_idx = lax.broadcasted_iota(I32, (N_GROUPS, per, tm), 0).astype(F32)
    flat = g_idx * per + j_idx

    m1 = jnp.max(c3, axis=1, keepdims=True)
    first = jnp.min(jnp.where(c3 == m1, j_idx, per), axis=1, keepdims=True)
    m2 = jnp.max(jnp.where(j_idx == first, neg_inf, c3), axis=1, keepdims=True)
    gs = m1 + m2

    gi = lax.broadcasted_iota(I32, (N_GROUPS, 1, tm), 0).astype(F32)
    keep = jnp.zeros((N_GROUPS, 1, tm), jnp.bool_)
    cur = gs
    for _ in range(TOPK_GROUPS):
        mx = jnp.max(cur, axis=0, keepdims=True)
        pick = gi == jnp.min(jnp.where(cur == mx, gi, N_GROUPS), axis=0, keepdims=True)
        keep = keep | pick
        cur = jnp.where(pick, neg_inf, cur)
    cur = jnp.where(keep, c3, neg_inf)

    idxs, gates = [], []
    for _ in range(TOP_K):
        mx = jnp.max(jnp.max(cur, axis=1, keepdims=True), axis=0, keepdims=True)
        cand = jnp.where(cur == mx, flat, N_EXPERTS)
        fi = jnp.min(jnp.min(cand, axis=1, keepdims=True), axis=0, keepdims=True)
        pick = flat == fi
        gates.append(jnp.sum(jnp.sum(jnp.where(pick, s3, 0.0), axis=1, keepdims=True), axis=0))
        idxs.append(fi[0])
        cur = jnp.where(pick, neg_inf, cur)
    gate = jnp.concatenate(gates, axis=0)
    gate = gate / jnp.sum(gate, axis=0, keepdims=True) * ROUTED_SCALE
    eidx_ref[...] = jnp.concatenate(idxs, axis=0).astype(I32)
    gate_ref[...] = gate


def moe_router(h, w_router, e_bias, tm=512):
    m, d = h.shape
    tm = min(tm, m)
    return pl.pallas_call(
        _router_kernel,
        out_shape=(jax.ShapeDtypeStruct((TOP_K, m), I32), jax.ShapeDtypeStruct((TOP_K, m), F32)),
        grid=(m // tm,),
        in_specs=[pl.BlockSpec((tm, d), lambda i: (i, 0)),
                  pl.BlockSpec((N_EXPERTS, d), lambda i: (0, 0)),
                  pl.BlockSpec((N_EXPERTS, 1), lambda i: (0, 0))],
        out_specs=(pl.BlockSpec((TOP_K, tm), lambda i: (0, i)),
                   pl.BlockSpec((TOP_K, tm), lambda i: (0, i))),
        compiler_params=_params("parallel"),
        name="moe_router",
    )(h, w_router.T, e_bias.reshape(N_EXPERTS, 1))


def _silu(x):
    return x / (1.0 + jnp.exp(-x))


def _experts_kernel(blk_e_ref, x_ref, wgu_ref, wd_ref, sw_ref, o_ref):
    del blk_e_ref
    gu = jnp.dot(x_ref[...], wgu_ref[0], preferred_element_type=F32)
    act = (_silu(gu[:, :D_EXPERT]) * gu[:, D_EXPERT:]).astype(BF16)
    y = jnp.dot(act, wd_ref[0], preferred_element_type=F32)
    o_ref[...] = (y * sw_ref[...]).astype(o_ref.dtype)


def grouped_experts(x_sorted, blk_e, slot_w, wgu, wd):
    n_slot, d = x_sorted.shape
    n_blk = n_slot // MOE_BLK
    grid_spec = pltpu.PrefetchScalarGridSpec(
        num_scalar_prefetch=1,
        grid=(n_blk,),
        in_specs=[pl.BlockSpec((MOE_BLK, d), lambda i, be: (i, 0)),
                  pl.BlockSpec((1, d, 2 * D_EXPERT), lambda i, be: (be[i], 0, 0)),
                  pl.BlockSpec((1, D_EXPERT, d), lambda i, be: (be[i], 0, 0)),
                  pl.BlockSpec((MOE_BLK, 1), lambda i, be: (i, 0))],
        out_specs=pl.BlockSpec((MOE_BLK, d), lambda i, be: (i, 0)),
    )
    return pl.pallas_call(
        _experts_kernel,
        out_shape=jax.ShapeDtypeStruct((n_slot, d), BF16),
        grid_spec=grid_spec,
        compiler_params=_params("arbitrary"),
        name="grouped_experts",
    )(blk_e, x_sorted, wgu, wd, slot_w.reshape(n_slot, 1))


def _moe_finish_kernel(hb_ref, r_ref, wgu_ref, wd_ref, h_ref, g_ref, b_ref, o_ref, ob_ref):
    gu = jnp.dot(hb_ref[...], wgu_ref[...], preferred_element_type=F32)
    act = (_silu(gu[:, :D_EXPERT]) * gu[:, D_EXPERT:]).astype(BF16)
    shared = jnp.dot(act, wd_ref[...], preferred_element_type=F32)
    z = DN_ALPHA * h_ref[...] + (r_ref[...].astype(F32) + shared)
    out = _layer_norm_rows(z, g_ref[...], b_ref[...])
    o_ref[...] = out
    ob_ref[...] = out.astype(BF16)


def moe_finish(hb, routed, ws_gu, ws_d, h, g, b, tm=512):
    m, d = h.shape
    tm = min(tm, m)
    row = lambda i: (i, 0)
    fixed = lambda i: (0, 0)
    return pl.pallas_call(
        _moe_finish_kernel,
        out_shape=(jax.ShapeDtypeStruct((m, d), F32), jax.ShapeDtypeStruct((m, d), BF16)),
        grid=(m // tm,),
        in_specs=[pl.BlockSpec((tm, d), row), pl.BlockSpec((tm, d), row),
                  pl.BlockSpec(ws_gu.shape, fixed), pl.BlockSpec(ws_d.shape, fixed),
                  pl.BlockSpec((tm, d), row), pl.BlockSpec((1, d), fixed), pl.BlockSpec((1, d), fixed)],
        out_specs=(pl.BlockSpec((tm, d), row), pl.BlockSpec((tm, d), row)),
        compiler_params=_params("parallel"),
        name="moe_finish",
    )(hb, routed, ws_gu, ws_d, h, g.reshape(1, d), b.reshape(1, d))


def moe_layer(h, hb, w_router, e_bias, wgu, wd, ws_gu, ws_d, g, b):
    n_tok, d = h.shape
    eidx, gate = moe_router(h, w_router, e_bias)
    flat_e = eidx.T.reshape(-1)
    flat_g = gate.T.reshape(-1)
    n_asg = n_tok * TOP_K
    order = jnp.argsort(flat_e, stable=True).astype(I32)
    counts = jnp.sum(flat_e[:, None] == jnp.arange(N_EXPERTS)[None, :], axis=0).astype(I32)
    padded = (counts + MOE_BLK - 1) // MOE_BLK * MOE_BLK
    pad_end = jnp.cumsum(padded)
    pad_start = pad_end - padded
    start = jnp.cumsum(counts) - counts
    n_blk = -(-n_asg // MOE_BLK) + N_EXPERTS
    n_slot = n_blk * MOE_BLK
    blk_e = jnp.minimum(jnp.searchsorted(pad_end, jnp.arange(n_blk) * MOE_BLK, side='right'),
                        N_EXPERTS - 1).astype(I32)
    slot = jnp.arange(n_slot, dtype=I32)
    slot_e = jnp.repeat(blk_e, MOE_BLK)
    within = slot - pad_start[slot_e]
    live = (within < counts[slot_e]) & (slot < pad_end[N_EXPERTS - 1])
    src = order[jnp.clip(start[slot_e] + within, 0, n_asg - 1)]
    slot_tok = jnp.where(live, src // TOP_K, n_tok)
    slot_w = jnp.where(live, flat_g[src], 0.0)
    sorted_e = flat_e[order]
    dest = pad_start[sorted_e] + jnp.arange(n_asg, dtype=I32) - start[sorted_e]
    asg_slot = jnp.zeros((n_asg,), I32).at[order].set(dest)

    hb_pad = jnp.concatenate([hb, jnp.zeros((1, d), hb.dtype)], axis=0)
    x_sorted = jnp.take(hb_pad, slot_tok, axis=0)
    y = grouped_experts(x_sorted, blk_e, slot_w, wgu, wd)
    routed = jnp.sum(jnp.take(y, asg_slot.reshape(n_tok, TOP_K), axis=0).astype(F32), axis=1)
    return moe_finish(hb, routed, ws_gu, ws_d, h, g, b)


def kernel(x, rel_bias, a_w_in, a_w_o, b_w_in, b_w_o, c_w_in, c_q_norm, c_kv_norm, c_w_q_up,
           c_w_kv_up, c_w_o, ln_g, ln_b, moe_w_router, moe_bias, moe_w_gate, moe_w_up,
           moe_w_down, moe_ws_gate, moe_ws_up, moe_ws_down):
    bsz, seq, d = x.shape
    depth = ln_g.shape[0]
    h = x.reshape(bsz * seq, d)
    hb = h.astype(BF16)
    strip = dsa_bias_strip(rel_bias, seq)
    for layer in range(depth):
        kind, slot = layer % N_MIXERS, layer // N_MIXERS
        g0, b0 = ln_g[layer, 0], ln_b[layer, 0]
        if kind == 0:
            h, hb = dsa_mixer(hb, a_w_in[slot], a_w_o[slot], strip, h, g0, b0, bsz, seq)
        elif kind == 1:
            h, hb = dilated_mixer(hb, b_w_in[slot], b_w_o[slot], rel_bias, h, g0, b0, bsz, seq)
        else:
            h, hb = mla_mixer(hb, c_w_in[slot], c_q_norm[slot], c_kv_norm[slot], c_w_q_up[slot],
                              c_w_kv_up[slot], c_w_o[slot], h, g0, b0, bsz, seq)
        wgu = jnp.concatenate([moe_w_gate[layer], moe_w_up[layer]], axis=-1).astype(BF16)
        ws_gu = jnp.concatenate([moe_ws_gate[layer], moe_ws_up[layer]], axis=-1).astype(BF16)
        h, hb = moe_layer(h, hb, moe_w_router[layer], moe_bias[layer], wgu,
                          moe_w_down[layer].astype(BF16), ws_gu, moe_ws_down[layer].astype(BF16),
                          ln_g[layer, 1], ln_b[layer, 1])
    return h.reshape(bsz, seq, d)
```

```python
import functools
import math

import jax
import jax.numpy as jnp
from jax import lax
from jax.experimental import pallas as pl
from jax.experimental.pallas import tpu as pltpu

F32 = jnp.float32
BF16 = jnp.bfloat16
I32 = jnp.int32

LANES = 128
VMEM_LIMIT_BYTES = 56 * 1024 * 1024

D_MODEL = 1024
DEPTH = 4
N_MIXERS = 3
NORM_EPS = 1e-5
RMS_EPS = 1e-6
REL_BUCKETS = 32
REL_MAX_DIST = 2048
A_HEADS = 16
A_HEAD_DIM = 128
A_IDX_HEADS = 8
A_IDX_DIM = 64
A_TOPK_MAX = 256
A_Q = A_HEADS * A_HEAD_DIM
B_GROUPS = ((128, 1), (512, 4), (2048, 16))
B_HEADS = 16
B_HEAD_DIM = 64
B_N = 128
C_HEADS = 16
C_Q_RANK = 256
C_KV_RANK = 128
C_NOPE = 64
C_ROPE = 32
C_V = 64
ROPE_BASE = 10000.0
N_EXPERTS = 64
TOP_K = 8
N_GROUPS = 8
TOPK_GROUPS = 4
D_EXPERT = 256
ROUTED_SCALE = 2.5
DN_ALPHA = (2 * DEPTH) ** 0.25

Q_BLK = 128
KEY_CHUNK = 512
MOE_BLK = 256
MASK_NEG = -1e30
INT_MIN = -(2 ** 31)


def _params(*sem):
    return pltpu.CompilerParams(dimension_semantics=sem, vmem_limit_bytes=VMEM_LIMIT_BYTES)


def _mm_kernel(x_ref, w_ref, o_ref):
    o_ref[...] = jnp.dot(x_ref[...], w_ref[...], preferred_element_type=F32).astype(o_ref.dtype)


def matmul(x, w, out_dtype, tm=1024, tn=None):
    m, k = x.shape
    n = w.shape[1]
    tn = n if tn is None else tn
    tm = min(tm, m)
    return pl.pallas_call(
        _mm_kernel,
        out_shape=jax.ShapeDtypeStruct((m, n), out_dtype),
        grid=(n // tn, m // tm),
        in_specs=[pl.BlockSpec((tm, k), lambda j, i: (i, 0)),
                  pl.BlockSpec((k, tn), lambda j, i: (0, j))],
        out_specs=pl.BlockSpec((tm, tn), lambda j, i: (i, j)),
        compiler_params=_params("parallel", "parallel"),
        name="matmul",
    )(x, w)


def _layer_norm_rows(z, g, b):
    mu = jnp.mean(z, axis=-1, keepdims=True)
    zc = z - mu
    var = jnp.mean(zc * zc, axis=-1, keepdims=True)
    return zc * lax.rsqrt(var + NORM_EPS) * g + b


def _mm_ln_kernel(x_ref, w_ref, h_ref, g_ref, b_ref, o_ref, ob_ref):
    y = jnp.dot(x_ref[...], w_ref[...], preferred_element_type=F32)
    out = _layer_norm_rows(DN_ALPHA * h_ref[...] + y, g_ref[...], b_ref[...])
    o_ref[...] = out
    ob_ref[...] = out.astype(BF16)


def matmul_residual_ln(x, w, h, g, b, tm=512):
    m, k = x.shape
    d = w.shape[1]
    tm = min(tm, m)
    row = lambda i: (i, 0)
    fixed = lambda i: (0, 0)
    return pl.pallas_call(
        _mm_ln_kernel,
        out_shape=(jax.ShapeDtypeStruct((m, d), F32), jax.ShapeDtypeStruct((m, d), BF16)),
        grid=(m // tm,),
        in_specs=[pl.BlockSpec((tm, k), row), pl.BlockSpec((k, d), fixed),
                  pl.BlockSpec((tm, d), row), pl.BlockSpec((1, d), fixed),
                  pl.BlockSpec((1, d), fixed)],
        out_specs=(pl.BlockSpec((tm, d), row), pl.BlockSpec((tm, d), row)),
        compiler_params=_params("parallel"),
        name="matmul_residual_ln",
    )(x, w, h, g.reshape(1, d), b.reshape(1, d))


def _t5_bucket(dist):
    exact = REL_BUCKETS // 2
    d_f = jnp.maximum(dist, 1).astype(F32)
    large = exact + (jnp.log(d_f / exact) / math.log(REL_MAX_DIST / exact)
                     * (REL_BUCKETS - exact)).astype(I32)
    return jnp.where(dist < exact, dist, jnp.minimum(large, REL_BUCKETS - 1))


def _bias_by_distance(rel_bias, dist):
    return rel_bias[_t5_bucket(dist)]


def _dsa_kernel(k_sel, seq, q_ref, kv_ref, iq_ref, ik_ref, strip_ref, o_ref,
                key_sc, mask_sc, cut_sc, qs_sc, s_sc, p_sc, acc_sc, m_sc, l_sc, a_sc):
    i = pl.program_id(1)
    t0 = i * Q_BLK
    rows = lax.broadcasted_iota(I32, (Q_BLK, seq), 0) + t0
    cols = lax.broadcasted_iota(I32, (Q_BLK, seq), 1)
    valid = cols <= rows

    ik = ik_ref[0].astype(BF16)
    w_all = iq_ref[0, :, A_IDX_HEADS * LANES:] * ((A_IDX_DIM * A_IDX_HEADS) ** -0.5)
    score = jnp.zeros((Q_BLK, seq), F32)
    for h in range(A_IDX_HEADS):
        qh = iq_ref[0, :, h * LANES:(h + 1) * LANES].astype(BF16)
        rel = lax.dot_general(qh, ik, (((1,), (1,)), ((), ())), preferred_element_type=F32)
        score = score + w_all[:, A_IDX_DIM + h:A_IDX_DIM + h + 1] * jnp.maximum(rel, 0.0)
    score = jnp.where(score == 0.0, 0.0, score)

    bits = pltpu.bitcast(score, I32)
    key = jnp.where(valid, bits ^ ((bits >> 31) & 0x7FFFFFFF), INT_MIN)
    key_sc[...] = key

    def search(it, ans_u):
        cand_u = ans_u | lax.shift_left(jnp.int32(1), 31 - it)
        cand_s = cand_u ^ INT_MIN
        cnt = jnp.sum(jnp.where(key_sc[...] >= cand_s, 1.0, 0.0), axis=-1, keepdims=True)
        return jnp.where(cnt >= k_sel, cand_u, ans_u)

    ans_u = lax.fori_loop(0, 32, search, jnp.zeros((Q_BLK, 1), I32))
    thr = ans_u ^ INT_MIN

    key = key_sc[...]
    gt = key > thr
    eq = key == thr
    need = k_sel - jnp.sum(jnp.where(gt, 1.0, 0.0), axis=-1, keepdims=True)
    n_eq = jnp.sum(jnp.where(eq, 1.0, 0.0), axis=-1, keepdims=True)
    cut_sc[...] = jnp.full((Q_BLK, 1), seq, I32)
    surplus = jnp.where((n_eq > need) & (thr != INT_MIN), 1.0, 0.0)

    @pl.when(jnp.max(surplus) > 0.0)
    def _():
        def tie_search(it, ans):
            cand = ans | lax.shift_left(jnp.int32(1), (seq.bit_length() - 1) - it)
            hit = (key_sc[...] == thr) & (cols < cand)
            cnt = jnp.sum(jnp.where(hit, 1.0, 0.0), axis=-1, keepdims=True)
            return jnp.where(cnt < need, cand, ans)
        cut_sc[...] = lax.fori_loop(0, seq.bit_length(), tie_search, jnp.zeros((Q_BLK, 1), I32))

    selected = valid & (gt | (eq & (cols <= cut_sc[...])))
    mask_sc[...] = jnp.where(selected, 0.0, MASK_NEG)

    for h in range(A_HEADS):
        qs_sc[h * Q_BLK:(h + 1) * Q_BLK, :] = q_ref[0, :, h * A_HEAD_DIM:(h + 1) * A_HEAD_DIM]
    m_sc[...] = jnp.full(m_sc.shape, MASK_NEG, F32)
    l_sc[...] = jnp.zeros(l_sc.shape, F32)
    acc_sc[...] = jnp.zeros(acc_sc.shape, F32)
    n_strip_blk = seq // Q_BLK - 1

    def chunk(c, carry):
        k0 = pl.multiple_of(c * KEY_CHUNK, KEY_CHUNK)
        kc = kv_ref[0, pl.ds(k0, KEY_CHUNK), :A_HEAD_DIM]
        vc = kv_ref[0, pl.ds(k0, KEY_CHUNK), A_HEAD_DIM:]
        s_sc[...] = lax.dot_general(qs_sc[...], kc, (((1,), (1,)), ((), ())),
                                    preferred_element_type=F32)
        w0 = pl.multiple_of((c * (KEY_CHUNK // Q_BLK) - i + n_strip_blk) * Q_BLK, Q_BLK)
        mk = mask_sc[:, pl.ds(k0, KEY_CHUNK)]
        for h in range(A_HEADS):
            r = slice(h * Q_BLK, (h + 1) * Q_BLK)
            s = s_sc[r, :] + strip_ref[h, :, pl.ds(w0, KEY_CHUNK)].astype(F32) + mk
            m_old = m_sc[r, :]
            m_new = jnp.maximum(m_old, jnp.max(s, axis=-1, keepdims=True))
            alpha = jnp.exp(m_old - m_new)
            p = jnp.exp(s - m_new)
            l_sc[r, :] = alpha * l_sc[r, :] + jnp.sum(p, axis=-1, keepdims=True)
            m_sc[r, :] = m_new
            a_sc[r, :] = alpha
            p_sc[r, :] = p.astype(BF16)
        pv = jnp.dot(p_sc[...], vc, preferred_element_type=F32)
        acc_sc[...] = acc_sc[...] * a_sc[...] + pv
        return carry

    lax.fori_loop(0, (t0 + Q_BLK + KEY_CHUNK - 1) // KEY_CHUNK, chunk, 0)
    for h in range(A_HEADS):
        r = slice(h * Q_BLK, (h + 1) * Q_BLK)
        o_ref[0, :, h * A_HEAD_DIM:(h + 1) * A_HEAD_DIM] = (acc_sc[r, :] / l_sc[r, :]).astype(BF16)


def dsa_attention(qkv, idx, strip, bsz, seq):
    k_sel = min(A_TOPK_MAX, seq // 4)
    rows = A_HEADS * Q_BLK
    n_kv_blk = A_Q // (2 * A_HEAD_DIM)
    n_ik_blk = A_IDX_HEADS
    idx_w = (A_IDX_HEADS + 1) * LANES
    return pl.pallas_call(
        functools.partial(_dsa_kernel, k_sel, seq),
        out_shape=jax.ShapeDtypeStruct((bsz, seq, A_Q), BF16),
        grid=(bsz, seq // Q_BLK),
        in_specs=[pl.BlockSpec((1, Q_BLK, A_Q), lambda b, i: (b, i, 0)),
                  pl.BlockSpec((1, seq, 2 * A_HEAD_DIM), lambda b, i: (b, 0, n_kv_blk)),
                  pl.BlockSpec((1, Q_BLK, idx_w), lambda b, i: (b, i, 0)),
                  pl.BlockSpec((1, seq, LANES), lambda b, i: (b, 0, n_ik_blk)),
                  pl.BlockSpec(strip.shape, lambda b, i: (0, 0, 0))],
        out_specs=pl.BlockSpec((1, Q_BLK, A_Q), lambda b, i: (b, i, 0)),
        scratch_shapes=[pltpu.VMEM((Q_BLK, seq), I32),
                        pltpu.VMEM((Q_BLK, seq), F32),
                        pltpu.VMEM((Q_BLK, 1), I32),
                        pltpu.VMEM((rows, A_HEAD_DIM), BF16),
                        pltpu.VMEM((rows, KEY_CHUNK), F32),
                        pltpu.VMEM((rows, KEY_CHUNK), BF16),
                        pltpu.VMEM((rows, A_HEAD_DIM), F32),
                        pltpu.VMEM((rows, 1), F32),
                        pltpu.VMEM((rows, 1), F32),
                        pltpu.VMEM((rows, 1), F32)],
        compiler_params=_params("parallel", "arbitrary"),
        name="dsa_attention",
    )(qkv, qkv, idx, idx, strip)


def dsa_bias_strip(rel_bias, seq):
    width = seq + KEY_CHUNK - Q_BLK
    tq = jnp.arange(Q_BLK)[:, None]
    c = jnp.arange(width)[None, :]
    dist = jnp.maximum(tq - c + (seq - Q_BLK), 0)
    return jnp.transpose(_bias_by_distance(rel_bias, dist), (2, 0, 1)).astype(BF16)


def dsa_weights(w_in):
    d = w_in.shape[0]
    wq = w_in[:, :A_Q] * (A_HEAD_DIM ** -0.5)
    wkv = w_in[:, A_Q:A_Q + 2 * A_HEAD_DIM]
    o = A_Q + 2 * A_HEAD_DIM
    n_qi = A_IDX_HEADS * A_IDX_DIM
    wqi = w_in[:, o:o + n_qi].reshape(d, A_IDX_HEADS, A_IDX_DIM)
    wqi = jnp.pad(wqi, ((0, 0), (0, 0), (0, LANES - A_IDX_DIM))).reshape(d, A_IDX_HEADS * LANES)
    wkw = jnp.pad(w_in[:, o + n_qi:], ((0, 0), (0, LANES - A_IDX_DIM - A_IDX_HEADS)))
    return (jnp.concatenate([wq, wkv], axis=1).astype(BF16),
            jnp.concatenate([wqi, wkw], axis=1).astype(BF16))


def dsa_mixer(hb, w_in, w_o, strip, h, g, b, bsz, seq):
    w_qkv, w_idx = dsa_weights(w_in)
    qkv = matmul(hb, w_qkv, BF16, tn=w_qkv.shape[1] // 2)
    idx = matmul(hb, w_idx, F32)
    o = dsa_attention(qkv.reshape(bsz, seq, -1), idx.reshape(bsz, seq, -1), strip, bsz, seq)
    return matmul_residual_ln(o.reshape(bsz * seq, A_Q), w_o.astype(BF16), h, g, b)


def _dilated_kernel(q_ref, kp_ref, kc_ref, vp_ref, vc_ref, bias_ref, o_ref, lse_ref):
    c = pl.program_id(1)
    col = lax.broadcasted_iota(I32, (B_N, 2 * B_N), 1)
    has_prev = (col >= B_N) | (c > 0)
    lane = lax.broadcasted_iota(I32, (B_N, LANES), 1)
    lse_tile = jnp.zeros((B_N, LANES), F32)
    pair = []
    for h in range(B_HEADS):
        hs = slice(h * B_HEAD_DIM, (h + 1) * B_HEAD_DIM)
        kk = jnp.concatenate([kp_ref[0, :, hs], kc_ref[0, :, hs]], axis=0)
        vv = jnp.concatenate([vp_ref[0, :, hs], vc_ref[0, :, hs]], axis=0)
        s = lax.dot_general(q_ref[0, :, hs], kk, (((1,), (1,)), ((), ())),
                            preferred_element_type=F32)
        s = jnp.where(has_prev, s + bias_ref[h], MASK_NEG)
        m = jnp.max(s, axis=-1, keepdims=True)
        p = jnp.exp(s - m)
        l = jnp.sum(p, axis=-1, keepdims=True)
        pair.append(jnp.dot(p.astype(BF16), vv, preferred_element_type=F32) / l)
        lse_tile = jnp.where(lane == h, m + jnp.log(l), lse_tile)
        if len(pair) == LANES // B_HEAD_DIM:
            o_ref[0, :, (h + 1) * B_HEAD_DIM - LANES:(h + 1) * B_HEAD_DIM] = (
                jnp.concatenate(pair, axis=-1).astype(BF16))
            pair = []
    lse_ref[0] = lse_tile


def dilated_group(proj, bias, n_seq, length):
    width = B_HEADS * B_HEAD_DIM
    pv = proj.reshape(n_seq, length, 3 * width)
    blk = (1, B_N, width)

    def spec(which, prev):
        if prev:
            return pl.BlockSpec(blk, lambda s, c: (s, jnp.maximum(c - 1, 0), which))
        return pl.BlockSpec(blk, lambda s, c: (s, c, which))

    return pl.pallas_call(
        _dilated_kernel,
        out_shape=(jax.ShapeDtypeStruct((n_seq, length, width), BF16),
                   jax.ShapeDtypeStruct((n_seq, length, LANES), F32)),
        grid=(n_seq, length // B_N),
        in_specs=[spec(0, False), spec(1, True), spec(1, False), spec(2, True), spec(2, False),
                  pl.BlockSpec(bias.shape, lambda s, c: (0, 0, 0))],
        out_specs=(pl.BlockSpec(blk, lambda s, c: (s, c, 0)),
                   pl.BlockSpec((1, B_N, LANES), lambda s, c: (s, c, 0))),
        compiler_params=_params("parallel", "parallel"),
        name="dilated_group",
    )(pv, pv, pv, pv, pv, bias)


def dilated_bias(rel_bias, dil):
    ii = jnp.arange(B_N)[:, None]
    jj = jnp.arange(2 * B_N)[None, :]
    delta = B_N + ii - jj
    band = (delta >= 0) & (delta <= B_N)
    bias = jnp.transpose(_bias_by_distance(rel_bias, jnp.maximum(delta, 0) * dil), (2, 0, 1))
    return jnp.where(band[None], bias.astype(F32), MASK_NEG)


def _dilated_merge_kernel(dils, o0_ref, o1_ref, o2_ref, l0_ref, l1_ref, l2_ref, e_ref, w_ref,
                          h_ref, g_ref, b_ref, o_ref, ob_ref, *nat_sc):
    def natural(o_g, l_g, dil, scratch):
        if dil == 1:
            return o_g[0, 0].astype(F32), l_g[0, 0]
        o_sc, l_sc = scratch
        n_tiles, n = o_sc.shape[0], o_sc.shape[1] // dil
        for r in range(dil):
            for j in range(n_tiles):
                o_sc[j, pl.ds(r, n, stride=dil), :] = o_g[0, r, :, j * LANES:(j + 1) * LANES].astype(F32)
            l_sc[pl.ds(r, n, stride=dil), :] = l_g[0, r]
        return jnp.concatenate([o_sc[j] for j in range(n_tiles)], axis=-1), l_sc[...]

    nat = []
    for gi, (o_g, l_g) in enumerate(((o0_ref, l0_ref), (o1_ref, l1_ref), (o2_ref, l2_ref))):
        nat.append(natural(o_g, l_g, dils[gi], nat_sc[2 * gi:2 * gi + 2]))
    (v0, l0), (v1, l1), (v2, l2) = nat
    m = jnp.maximum(jnp.maximum(l0, l1), l2)
    e0, e1, e2 = jnp.exp(l0 - m), jnp.exp(l1 - m), jnp.exp(l2 - m)
    inv = 1.0 / (e0 + e1 + e2)
    e_mat = e_ref[...]

    def spread(wt):
        hi = wt.astype(BF16)
        lo = (wt - hi.astype(F32)).astype(BF16)
        return (jnp.dot(hi, e_mat, preferred_element_type=F32)
                + jnp.dot(lo, e_mat, preferred_element_type=F32))

    mix = spread(e0 * inv) * v0 + spread(e1 * inv) * v1 + spread(e2 * inv) * v2
    y = jnp.dot(mix.astype(BF16), w_ref[...], preferred_element_type=F32)
    out = _layer_norm_rows(DN_ALPHA * h_ref[...] + y, g_ref[...], b_ref[...])
    o_ref[...] = out
    ob_ref[...] = out.astype(BF16)


def dilated_merge(outs, lses, dils, w_o, h, g, b, bsz, seq, tm=512):
    m, d = h.shape
    width = B_HEADS * B_HEAD_DIM
    tm = min(tm, seq)
    n_j = seq // tm
    expand = (jnp.arange(LANES)[:, None] == (jnp.arange(width)[None, :] // B_HEAD_DIM)).astype(BF16)
    row = lambda bi, j: (bi * n_j + j, 0)
    fixed = lambda bi, j: (0, 0)
    grouped = lambda bi, j: (bi, 0, j, 0)
    o_specs = [pl.BlockSpec((1, dl, tm // dl, width), grouped) for dl in dils]
    l_specs = [pl.BlockSpec((1, dl, tm // dl, LANES), grouped) for dl in dils]
    scratch = []
    for dl in dils:
        if dl > 1:
            scratch += [pltpu.VMEM((width // LANES, tm, LANES), F32), pltpu.VMEM((tm, LANES), F32)]
        else:
            scratch += [pltpu.VMEM((8, LANES), F32), pltpu.VMEM((8, LANES), F32)]
    outs = [o.reshape(bsz, dl, seq // dl, width) for o, dl in zip(outs, dils)]
    lses = [l.reshape(bsz, dl, seq // dl, LANES) for l, dl in zip(lses, dils)]
    return pl.pallas_call(
        functools.partial(_dilated_merge_kernel, tuple(dils)),
        out_shape=(jax.ShapeDtypeStruct((m, d), F32), jax.ShapeDtypeStruct((m, d), BF16)),
        grid=(bsz, n_j),
        in_specs=o_specs + l_specs
                 + [pl.BlockSpec((LANES, width), fixed), pl.BlockSpec((width, d), fixed),
                    pl.BlockSpec((tm, d), row), pl.BlockSpec((1, d), fixed), pl.BlockSpec((1, d), fixed)],
        out_specs=(pl.BlockSpec((tm, d), row), pl.BlockSpec((tm, d), row)),
        scratch_shapes=scratch,
        compiler_params=_params("parallel", "parallel"),
        name="dilated_merge",
    )(*outs, *lses, expand, w_o.astype(BF16), h, g.reshape(1, d), b.reshape(1, d))


def dilated_mixer(hb_by_dil, w_in, w_o, rel_bias, h, g, b, bsz, seq):
    width = B_HEADS * B_HEAD_DIM
    outs, lses, dils = [], [], []
    for gi, (window, dil) in enumerate(B_GROUPS):
        assert window // dil == B_N and seq % window == 0
        w_g = w_in[:, gi * 3 * width:(gi + 1) * 3 * width]
        w_g = w_g.at[:, :width].multiply(B_HEAD_DIM ** -0.5)
        proj = matmul(hb_by_dil[dil], w_g.astype(BF16), BF16, tn=3 * width // 2)
        o, lse = dilated_group(proj, dilated_bias(rel_bias, dil), bsz * dil, seq // dil)
        outs.append(o)
        lses.append(lse)
        dils.append(dil)
    return dilated_merge(outs, lses, dils, w_o, h, g, b, bsz, seq)


C_PAD = 128


def _mla_prep_kernel(x_ref, win_ref, qn_ref, kvn_ref, wqa_ref, wqb_ref, wka_ref, wv_ref,
                     cos_ref, sin_ref, q_ref, k_ref, v_ref):
    c = jnp.dot(x_ref[...], win_ref[...], preferred_element_type=F32)
    cos, sin = cos_ref[...], sin_ref[...]

    def rms(v, gain):
        return (v * lax.rsqrt(jnp.mean(v * v, axis=-1, keepdims=True) + RMS_EPS) * gain).astype(BF16)

    nq = rms(c[:, :C_Q_RANK], qn_ref[...])
    nkv = rms(c[:, C_Q_RANK:C_Q_RANK + C_KV_RANK], kvn_ref[...])
    o = C_Q_RANK + C_KV_RANK
    k_rope = c[:, o:o + C_PAD] * cos + c[:, o + C_PAD:o + 2 * C_PAD] * sin
    qa = jnp.dot(nq, wqa_ref[...], preferred_element_type=F32)
    qb = jnp.dot(nq, wqb_ref[...], preferred_element_type=F32)
    kn = jnp.dot(nkv, wka_ref[...], preferred_element_type=F32)
    for h in range(C_HEADS):
        hs = slice(h * C_PAD, (h + 1) * C_PAD)
        q_ref[:, hs] = (qa[:, hs] * cos + qb[:, hs] * sin).astype(BF16)
        k_ref[:, hs] = (kn[:, hs] + k_rope).astype(BF16)
    v_ref[...] = jnp.dot(nkv, wv_ref[...], preferred_element_type=F32).astype(BF16)


def _rot_half_cols(w):
    half = w.shape[-1] // 2
    return jnp.concatenate([-w[..., half:], w[..., :half]], axis=-1)


def mla_prep(hb, w_in, q_norm, kv_norm, w_q_up, w_kv_up, bsz, seq, tm=512):
    d = w_in.shape[0]
    scale = (C_NOPE + C_ROPE) ** -0.5
    pad_r = C_PAD - C_NOPE - C_ROPE
    w_kr = w_in[:, C_Q_RANK + C_KV_RANK:]

    def rope_slot(w):
        return jnp.pad(w, ((0, 0), (C_NOPE, pad_r)))

    win = jnp.concatenate([w_in[:, :C_Q_RANK + C_KV_RANK], rope_slot(w_kr),
                           rope_slot(_rot_half_cols(w_kr))], axis=1).astype(BF16)
    wq = w_q_up.reshape(C_Q_RANK, C_HEADS, C_NOPE + C_ROPE) * scale
    wqa = jnp.pad(wq, ((0, 0), (0, 0), (0, pad_r))).reshape(C_Q_RANK, C_HEADS * C_PAD).astype(BF16)
    wqb = jnp.pad(_rot_half_cols(wq[..., C_NOPE:]), ((0, 0), (0, 0), (C_NOPE, pad_r)))
    wqb = wqb.reshape(C_Q_RANK, C_HEADS * C_PAD).astype(BF16)
    wkv = w_kv_up.reshape(C_KV_RANK, C_HEADS, C_NOPE + C_V)
    wka = jnp.pad(wkv[..., :C_NOPE], ((0, 0), (0, 0), (0, C_PAD - C_NOPE)))
    wka = wka.reshape(C_KV_RANK, C_HEADS * C_PAD).astype(BF16)
    wv = wkv[..., C_NOPE:].reshape(C_KV_RANK, C_HEADS * C_V).astype(BF16)

    half = C_ROPE // 2
    inv = ROPE_BASE ** (-jnp.arange(half, dtype=F32) / half)
    ang = jnp.arange(seq, dtype=F32)[:, None] * inv[None, :]
    ones, zeros = jnp.ones((seq, C_NOPE), F32), jnp.zeros((seq, pad_r), F32)
    cos = jnp.concatenate([ones, jnp.cos(ang), jnp.cos(ang), zeros], axis=1)
    sin = jnp.concatenate([0 * ones, jnp.sin(ang), jnp.sin(ang), zeros], axis=1)

    m = bsz * seq
    tm = min(tm, seq)
    n_pos_blk = seq // tm
    row = lambda i: (i, 0)
    fixed = lambda i: (0, 0)
    pos = lambda i: (i % n_pos_blk, 0)
    full = lambda a: pl.BlockSpec(a.shape, fixed)
    qw, vw = C_HEADS * C_PAD, C_HEADS * C_V
    return pl.pallas_call(
        _mla_prep_kernel,
        out_shape=(jax.ShapeDtypeStruct((m, qw), BF16), jax.ShapeDtypeStruct((m, qw), BF16),
                   jax.ShapeDtypeStruct((m, vw), BF16)),
        grid=(m // tm,),
        in_specs=[pl.BlockSpec((tm, d), row), full(win),
                  pl.BlockSpec((1, C_Q_RANK), fixed), pl.BlockSpec((1, C_KV_RANK), fixed),
                  full(wqa), full(wqb), full(wka), full(wv),
                  pl.BlockSpec((tm, C_PAD), pos), pl.BlockSpec((tm, C_PAD), pos)],
        out_specs=(pl.BlockSpec((tm, qw), row), pl.BlockSpec((tm, qw), row),
                   pl.BlockSpec((tm, vw), row)),
        compiler_params=_params("parallel"),
        name="mla_prep",
    )(hb, win, q_norm.reshape(1, -1), kv_norm.reshape(1, -1), wqa, wqb, wka, wv, cos, sin)


MLA_TQ = 512


def _mla_attn_kernel(q_ref, k_ref, v_ref, o_ref):
    i = pl.program_id(2)
    tq = MLA_TQ
    rows = lax.broadcasted_iota(I32, (tq, tq), 0)
    cols = lax.broadcasted_iota(I32, (tq, tq), 1)
    outs = []
    for hh in range(2):
        q = q_ref[0, :, hh * C_PAD:(hh + 1) * C_PAD]

        def step(c, carry, diag):
            m_old, l_old, acc = carry
            k0 = pl.multiple_of(c * tq, tq)
            kc = k_ref[0, pl.ds(k0, tq), hh * C_PAD:(hh + 1) * C_PAD]
            vc = v_ref[0, pl.ds(k0, tq), hh * C_V:(hh + 1) * C_V]
            s = lax.dot_general(q, kc, (((1,), (1,)), ((), ())), preferred_element_type=F32)
            if diag:
                s = jnp.where(cols <= rows, s, MASK_NEG)
            m_new = jnp.maximum(m_old, jnp.max(s, axis=-1, keepdims=True))
            alpha = jnp.exp(m_old - m_new)
            p = jnp.exp(s - m_new)
            l_new = alpha * l_old + jnp.sum(p, axis=-1, keepdims=True)
            acc = alpha * acc + jnp.dot(p.astype(BF16), vc, preferred_element_type=F32)
            return m_new, l_new, acc

        init = (jnp.full((tq, 1), MASK_NEG, F32), jnp.zeros((tq, 1), F32), jnp.zeros((tq, C_V), F32))
        carry = lax.fori_loop(0, i, functools.partial(step, diag=False), init)
        _, l, acc = step(i, carry, diag=True)
        outs.append(acc / l)
    o_ref[0] = jnp.concatenate(outs, axis=-1).astype(BF16)


def mla_attention(q, k, v, bsz, seq):
    qw, vw = C_HEADS * C_PAD, C_HEADS * C_V
    return pl.pallas_call(
        _mla_attn_kernel,
        out_shape=jax.ShapeDtypeStruct((bsz, seq, vw), BF16),
        grid=(bsz, C_HEADS // 2, seq // MLA_TQ),
        in_specs=[pl.BlockSpec((1, MLA_TQ, 2 * C_PAD), lambda b, hp, i: (b, i, hp)),
                  pl.BlockSpec((1, seq, 2 * C_PAD), lambda b, hp, i: (b, 0, hp)),
                  pl.BlockSpec((1, seq, 2 * C_V), lambda b, hp, i: (b, 0, hp))],
        out_specs=pl.BlockSpec((1, MLA_TQ, 2 * C_V), lambda b, hp, i: (b, i, hp)),
        compiler_params=_params("parallel", "parallel", "arbitrary"),
        name="mla_attention",
    )(q.reshape(bsz, seq, qw), k.reshape(bsz, seq, qw), v.reshape(bsz, seq, vw))


def mla_mixer(hb, w_in, q_norm, kv_norm, w_q_up, w_kv_up, w_o, h, g, b, bsz, seq):
    q, k, v = mla_prep(hb, w_in, q_norm, kv_norm, w_q_up, w_kv_up, bsz, seq)
    o = mla_attention(q, k, v, bsz, seq)
    return matmul_residual_ln(o.reshape(bsz * seq, -1), w_o.astype(BF16), h, g, b)


def _router_kernel(h_ref, wr_ref, eb_ref, eidx_ref, gate_ref):
    tm = h_ref.shape[0]
    per = N_EXPERTS // N_GROUPS
    logits = lax.dot_general(wr_ref[...], h_ref[...], (((1,), (1,)), ((), ())),
                             precision=lax.Precision.HIGHEST, preferred_element_type=F32)
    scores = 1.0 / (1.0 + jnp.exp(-logits))
    s3 = scores.reshape(N_GROUPS, per, tm)
    c3 = (scores + eb_ref[...]).reshape(N_GROUPS, per, tm)
    neg_inf = -jnp.inf
    j_idx = lax.broadcasted_iota(I32, (N_GROUPS, per, tm), 1).astype(F32)
    g_idx = lax.broadcasted_iota(I32, (N_GROUPS, per, tm), 0).astype(F32)
    flat = g_idx * per + j_idx

    m1 = jnp.max(c3, axis=1, keepdims=True)
    first = jnp.min(jnp.where(c3 == m1, j_idx, per), axis=1, keepdims=True)
    m2 = jnp.max(jnp.where(j_idx == first, neg_inf, c3), axis=1, keepdims=True)
    gs = m1 + m2

    gi = lax.broadcasted_iota(I32, (N_GROUPS, 1, tm), 0).astype(F32)
    keep = jnp.zeros((N_GROUPS, 1, tm), jnp.bool_)
    cur = gs
    for _ in range(TOPK_GROUPS):
        mx = jnp.max(cur, axis=0, keepdims=True)
        pick = gi == jnp.min(jnp.where(cur == mx, gi, N_GROUPS), axis=0, keepdims=True)
        keep = keep | pick
        cur = jnp.where(pick, neg_inf, cur)
    cur = jnp.where(keep, c3, neg_inf)

    idxs, gates = [], []
    for _ in range(TOP_K):
        mx = jnp.max(jnp.max(cur, axis=1, keepdims=True), axis=0, keepdims=True)
        cand = jnp.where(cur == mx, flat, N_EXPERTS)
        fi = jnp.min(jnp.min(cand, axis=1, keepdims=True), axis=0, keepdims=True)
        pick = flat == fi
        gates.append(jnp.sum(jnp.sum(jnp.where(pick, s3, 0.0), axis=1, keepdims=True), axis=0))
        idxs.append(fi[0])
        cur = jnp.where(pick, neg_inf, cur)
    gate = jnp.concatenate(gates, axis=0)
    gate = gate / jnp.sum(gate, axis=0, keepdims=True) * ROUTED_SCALE
    eidx_ref[...] = jnp.concatenate(idxs, axis=0).astype(I32)
    gate_ref[...] = gate


def moe_router(h, w_router, e_bias, tm=512):
    m, d = h.shape
    tm = min(tm, m)
    return pl.pallas_call(
        _router_kernel,
        out_shape=(jax.ShapeDtypeStruct((TOP_K, m), I32), jax.ShapeDtypeStruct((TOP_K, m), F32)),
        grid=(m // tm,),
        in_specs=[pl.BlockSpec((tm, d), lambda i: (i, 0)),
                  pl.BlockSpec((N_EXPERTS, d), lambda i: (0, 0)),
                  pl.BlockSpec((N_EXPERTS, 1), lambda i: (0, 0))],
        out_specs=(pl.BlockSpec((TOP_K, tm), lambda i: (0, i)),
                   pl.BlockSpec((TOP_K, tm), lambda i: (0, i))),
        compiler_params=_params("parallel"),
        name="moe_router",
    )(h, w_router.T, e_bias.reshape(N_EXPERTS, 1))


def _silu(x):
    return x / (1.0 + jnp.exp(-x))


RANK_TM = 1024


def _rank_kernel(eidx_ref, tri_ref, dest_ref, cnt_ref, start_ref, cnt_sc, run_sc):
    p, i = pl.program_id(0), pl.program_id(1)
    tm = eidx_ref.shape[1]
    e = eidx_ref[...]
    ex = lax.broadcasted_iota(I32, (N_EXPERTS, tm), 0)
    onehot = jnp.zeros((N_EXPERTS, tm), F32)
    for k in range(TOP_K):
        onehot = onehot + jnp.where(e[k:k + 1, :] == ex, 1.0, 0.0)
    tile_cnt = jnp.sum(onehot, axis=-1, keepdims=True)

    @pl.when((p == 0) & (i == 0))
    def _():
        cnt_sc[...] = jnp.zeros(cnt_sc.shape, F32)

    @pl.when(p == 0)
    def _():
        cnt_sc[...] += tile_cnt

    @pl.when((p == 1) & (i == 0))
    def _():
        cnt = cnt_sc[...]
        padded = jnp.floor((cnt + (MOE_BLK - 1)) * (1.0 / MOE_BLK)) * MOE_BLK
        below = (lax.broadcasted_iota(I32, (N_EXPERTS, N_EXPERTS), 1)
                 < lax.broadcasted_iota(I32, (N_EXPERTS, N_EXPERTS), 0)).astype(F32)
        start = jnp.dot(below, jnp.broadcast_to(padded, (N_EXPERTS, LANES)),
                        precision=lax.Precision.HIGHEST, preferred_element_type=F32)
        run_sc[...] = start[:, :1]
        cnt_ref[...] = jnp.broadcast_to(cnt, (N_EXPERTS, LANES))
        start_ref[...] = start

    @pl.when(p == 1)
    def _():
        before = jnp.dot(onehot.astype(BF16), tri_ref[...], preferred_element_type=F32) + run_sc[...]
        rows = [jnp.sum(jnp.where(e[k:k + 1, :] == ex, before, 0.0), axis=0, keepdims=True)
                for k in range(TOP_K)]
        dest_ref[...] = jnp.concatenate(rows, axis=0).astype(I32)
        run_sc[...] += tile_cnt


def moe_rank(eidx):
    n_tok = eidx.shape[1]
    tm = min(RANK_TM, n_tok)
    tri = (jnp.arange(tm)[:, None] < jnp.arange(tm)[None, :]).astype(BF16)
    stat = jax.ShapeDtypeStruct((N_EXPERTS, LANES), F32)
    dest, cnt, start = pl.pallas_call(
        _rank_kernel,
        out_shape=(jax.ShapeDtypeStruct((TOP_K, n_tok), I32), stat, stat),
        grid=(2, n_tok // tm),
        in_specs=[pl.BlockSpec((TOP_K, tm), lambda p, i: (0, i)),
                  pl.BlockSpec((tm, tm), lambda p, i: (0, 0))],
        out_specs=(pl.BlockSpec((TOP_K, tm), lambda p, i: (0, i * p)),
                   pl.BlockSpec((N_EXPERTS, LANES), lambda p, i: (0, 0)),
                   pl.BlockSpec((N_EXPERTS, LANES), lambda p, i: (0, 0))),
        scratch_shapes=[pltpu.VMEM((N_EXPERTS, 1), F32), pltpu.VMEM((N_EXPERTS, 1), F32)],
        compiler_params=_params("arbitrary", "arbitrary"),
        name="moe_rank",
    )(eidx, tri)
    return dest, cnt[:, 0].astype(I32), start[:, 0].astype(I32)


def _experts_kernel(blk_e_ref, x_ref, wg_ref, wu_ref, wd_ref, o_ref):
    del blk_e_ref
    x = x_ref[...]
    gate = jnp.dot(x, wg_ref[0], preferred_element_type=F32)
    up = jnp.dot(x, wu_ref[0], preferred_element_type=F32)
    act = (_silu(gate) * up).astype(BF16)
    o_ref[...] = jnp.dot(act, wd_ref[0], preferred_element_type=F32).astype(o_ref.dtype)


def grouped_experts(x_sorted, blk_e, wg, wu, wd):
    n_slot, d = x_sorted.shape
    n_blk = n_slot // MOE_BLK
    by_expert = lambda i, be: (be[i], 0, 0)
    grid_spec = pltpu.PrefetchScalarGridSpec(
        num_scalar_prefetch=1,
        grid=(n_blk,),
        in_specs=[pl.BlockSpec((MOE_BLK, d), lambda i, be: (i, 0)),
                  pl.BlockSpec((1, d, D_EXPERT), by_expert),
                  pl.BlockSpec((1, d, D_EXPERT), by_expert),
                  pl.BlockSpec((1, D_EXPERT, d), by_expert)],
        out_specs=pl.BlockSpec((MOE_BLK, d), lambda i, be: (i, 0)),
    )
    return pl.pallas_call(
        _experts_kernel,
        out_shape=jax.ShapeDtypeStruct((n_slot, d), BF16),
        grid_spec=grid_spec,
        compiler_params=_params("arbitrary"),
        name="grouped_experts",
    )(blk_e, x_sorted, wg, wu, wd)


def _moe_finish_kernel(dils, hb_ref, y_ref, gt_ref, wg_ref, wu_ref, wd_ref, h_ref, g_ref, b_ref,
                       o_ref, ob_ref, *rest):
    x = hb_ref[...]
    d = x.shape[1]
    act = (_silu(jnp.dot(x, wg_ref[...], preferred_element_type=F32))
           * jnp.dot(x, wu_ref[...], preferred_element_type=F32)).astype(BF16)
    ff = jnp.dot(act, wd_ref[...], preferred_element_type=F32)
    gt = gt_ref[...]
    for k in range(TOP_K):
        ff = ff + gt[:, k:k + 1] * y_ref[:, k * d:(k + 1) * d].astype(F32)
    out = _layer_norm_rows(DN_ALPHA * h_ref[...] + ff, g_ref[...], b_ref[...])
    o_ref[...] = out
    ob_ref[...] = out.astype(BF16)
    if dils:
        perm_refs, nat_sc = rest[:-1], rest[-1]
        for j in range(d // LANES):
            nat_sc[j] = out[:, j * LANES:(j + 1) * LANES]
        for p_ref, dil in zip(perm_refs, dils):
            n = out.shape[0] // dil
            for r in range(dil):
                for j in range(d // LANES):
                    p_ref[0, r, :, j * LANES:(j + 1) * LANES] = (
                        nat_sc[j, pl.ds(r, n, stride=dil), :].astype(BF16))


def moe_finish(hb, y_tok, gate_tok, ws_g, ws_u, ws_d, h, g, b, seq, dils=(), tm=256):
    m, d = h.shape
    tm = min(tm, seq)
    n_j = seq // tm
    row = lambda i: (i, 0)
    fixed = lambda i: (0, 0)
    out_shape = [jax.ShapeDtypeStruct((m, d), F32), jax.ShapeDtypeStruct((m, d), BF16)]
    out_specs = [pl.BlockSpec((tm, d), row), pl.BlockSpec((tm, d), row)]
    for dil in dils:
        out_shape.append(jax.ShapeDtypeStruct((m // seq, dil, seq // dil, d), BF16))
        out_specs.append(pl.BlockSpec((1, dil, tm // dil, d), lambda i: (i // n_j, 0, i % n_j, 0)))
    res = pl.pallas_call(
        functools.partial(_moe_finish_kernel, tuple(dils)),
        out_shape=tuple(out_shape),
        grid=(m // tm,),
        in_specs=[pl.BlockSpec((tm, d), row), pl.BlockSpec((tm, TOP_K * d), row),
                  pl.BlockSpec((tm, TOP_K), row),
                  pl.BlockSpec(ws_g.shape, fixed), pl.BlockSpec(ws_u.shape, fixed),
                  pl.BlockSpec(ws_d.shape, fixed),
                  pl.BlockSpec((tm, d), row), pl.BlockSpec((1, d), fixed), pl.BlockSpec((1, d), fixed)],
        out_specs=tuple(out_specs),
        scratch_shapes=[pltpu.VMEM((d // LANES, tm, LANES), F32)] if dils else [],
        compiler_params=_params("parallel"),
        name="moe_finish",
    )(hb, y_tok, gate_tok, ws_g, ws_u, ws_d, h, g.reshape(1, d), b.reshape(1, d))
    return res[0], res[1], [r.reshape(m, d) for r in res[2:]]


SORT_E_SHIFT = 19


def moe_layer(h, hb, w_router, e_bias, wg, wu, wd, ws_g, ws_u, ws_d, g, b, seq, dils=()):
    n_tok, d = h.shape
    n_asg = n_tok * TOP_K
    n_pad = N_EXPERTS * MOE_BLK
    assert n_asg <= 1 << (SORT_E_SHIFT - 1) and n_pad <= 1 << (SORT_E_SHIFT - 1)
    eidx, gate = moe_router(h, w_router, e_bias)
    dest, counts, pad_start = moe_rank(eidx)
    padded = (counts + MOE_BLK - 1) // MOE_BLK * MOE_BLK
    pad_end = pad_start + padded
    n_blk = n_asg // MOE_BLK + N_EXPERTS
    blk_start = jnp.arange(n_blk, dtype=I32) * MOE_BLK
    blk_e = jnp.minimum(jnp.sum(pad_end[None, :] <= blk_start[:, None], axis=1), N_EXPERTS - 1)

    key_real = (eidx.T.reshape(-1) << SORT_E_SHIFT) | jnp.arange(n_asg, dtype=I32)
    i_pad = jnp.arange(n_pad, dtype=I32)
    e_pad = jnp.sum(jnp.cumsum(padded - counts)[None, :] <= i_pad[:, None], axis=1).astype(I32)
    key_pad = (e_pad << SORT_E_SHIFT) | (1 << (SORT_E_SHIFT - 1)) | i_pad
    low = jnp.sort(jnp.concatenate([key_real, key_pad])) & ((1 << SORT_E_SHIFT) - 1)
    slot_tok = jnp.where(low < (1 << (SORT_E_SHIFT - 1)), low // TOP_K, n_tok)

    hb_pad = jnp.concatenate([hb, jnp.zeros((1, d), hb.dtype)], axis=0)
    x_sorted = jnp.take(hb_pad, slot_tok, axis=0)
    y = grouped_experts(x_sorted, blk_e.astype(I32), wg, wu, wd)
    y_tok = jnp.take(y, dest.T.reshape(-1), axis=0).reshape(n_tok, TOP_K * d)
    return moe_finish(hb, y_tok, gate.T, ws_g, ws_u, ws_d, h, g, b, seq, dils)


def kernel(x, rel_bias, a_w_in, a_w_o, b_w_in, b_w_o, c_w_in, c_q_norm, c_kv_norm, c_w_q_up,
           c_w_kv_up, c_w_o, ln_g, ln_b, moe_w_router, moe_bias, moe_w_gate, moe_w_up,
           moe_w_down, moe_ws_gate, moe_ws_up, moe_ws_down):
    bsz, seq, d = x.shape
    depth = ln_g.shape[0]
    h = x.reshape(bsz * seq, d)
    hb = h.astype(BF16)
    strip = dsa_bias_strip(rel_bias, seq)
    extra_dils = tuple(dil for _, dil in B_GROUPS if dil > 1)
    hb_perm = []
    for layer in range(depth):
        kind, slot = layer % N_MIXERS, layer // N_MIXERS
        g0, b0 = ln_g[layer, 0], ln_b[layer, 0]
        if kind == 0:
            h, hb = dsa_mixer(hb, a_w_in[slot], a_w_o[slot], strip, h, g0, b0, bsz, seq)
        elif kind == 1:
            hb_by_dil = {1: hb, **dict(zip(extra_dils, hb_perm))}
            h, hb = dilated_mixer(hb_by_dil, b_w_in[slot], b_w_o[slot], rel_bias, h, g0, b0, bsz, seq)
        else:
            h, hb = mla_mixer(hb, c_w_in[slot], c_q_norm[slot], c_kv_norm[slot], c_w_q_up[slot],
                              c_w_kv_up[slot], c_w_o[slot], h, g0, b0, bsz, seq)
        next_dilated = layer + 1 < depth and (layer + 1) % N_MIXERS == 1
        h, hb, hb_perm = moe_layer(h, hb, moe_w_router[layer], moe_bias[layer],
                                   moe_w_gate[layer].astype(BF16), moe_w_up[layer].astype(BF16),
                                   moe_w_down[layer].astype(BF16), moe_ws_gate[layer].astype(BF16),
                                   moe_ws_up[layer].astype(BF16), moe_ws_down[layer].astype(BF16),
                                   ln_g[layer, 1], ln_b[layer, 1], seq,
                                   extra_dils if next_dilated else ())
    return h.reshape(bsz, seq, d)
```

```python
import functools
import math

import jax
import jax.numpy as jnp
from jax import lax
from jax.experimental import pallas as pl
from jax.experimental.pallas import tpu as pltpu

F32 = jnp.float32
BF16 = jnp.bfloat16
I32 = jnp.int32

LANES = 128
VMEM_LIMIT_BYTES = 56 * 1024 * 1024

D_MODEL = 1024
DEPTH = 4
N_MIXERS = 3
NORM_EPS = 1e-5
RMS_EPS = 1e-6
REL_BUCKETS = 32
REL_MAX_DIST = 2048
A_HEADS = 16
A_HEAD_DIM = 128
A_IDX_HEADS = 8
A_IDX_DIM = 64
A_TOPK_MAX = 256
A_Q = A_HEADS * A_HEAD_DIM
B_GROUPS = ((128, 1), (512, 4), (2048, 16))
B_HEADS = 16
B_HEAD_DIM = 64
B_N = 128
C_HEADS = 16
C_Q_RANK = 256
C_KV_RANK = 128
C_NOPE = 64
C_ROPE = 32
C_V = 64
ROPE_BASE = 10000.0
N_EXPERTS = 64
TOP_K = 8
N_GROUPS = 8
TOPK_GROUPS = 4
D_EXPERT = 256
ROUTED_SCALE = 2.5
DN_ALPHA = (2 * DEPTH) ** 0.25

Q_BLK = 128
KEY_CHUNK = 512
MOE_BLK = 512
MASK_NEG = -1e30
INT_MIN = -(2 ** 31)


def _params(*sem):
    return pltpu.CompilerParams(dimension_semantics=sem, vmem_limit_bytes=VMEM_LIMIT_BYTES)


def _mm_kernel(x_ref, w_ref, o_ref):
    o_ref[...] = jnp.dot(x_ref[...], w_ref[...], preferred_element_type=F32).astype(o_ref.dtype)


def matmul(x, w, out_dtype, tm=1024, tn=None):
    m, k = x.shape
    n = w.shape[1]
    tn = n if tn is None else tn
    tm = min(tm, m)
    return pl.pallas_call(
        _mm_kernel,
        out_shape=jax.ShapeDtypeStruct((m, n), out_dtype),
        grid=(n // tn, m // tm),
        in_specs=[pl.BlockSpec((tm, k), lambda j, i: (i, 0)),
                  pl.BlockSpec((k, tn), lambda j, i: (0, j))],
        out_specs=pl.BlockSpec((tm, tn), lambda j, i: (i, j)),
        compiler_params=_params("parallel", "parallel"),
        name="matmul",
    )(x, w)


def _layer_norm_rows(z, g, b):
    mu = jnp.mean(z, axis=-1, keepdims=True)
    zc = z - mu
    var = jnp.mean(zc * zc, axis=-1, keepdims=True)
    return zc * lax.rsqrt(var + NORM_EPS) * g + b


def _mm_ln_kernel(x_ref, w_ref, h_ref, g_ref, b_ref, o_ref, ob_ref):
    y = jnp.dot(x_ref[...], w_ref[...], preferred_element_type=F32)
    out = _layer_norm_rows(DN_ALPHA * h_ref[...] + y, g_ref[...], b_ref[...])
    o_ref[...] = out
    ob_ref[...] = out.astype(BF16)


def matmul_residual_ln(x, w, h, g, b, tm=512):
    m, k = x.shape
    d = w.shape[1]
    tm = min(tm, m)
    row = lambda i: (i, 0)
    fixed = lambda i: (0, 0)
    return pl.pallas_call(
        _mm_ln_kernel,
        out_shape=(jax.ShapeDtypeStruct((m, d), F32), jax.ShapeDtypeStruct((m, d), BF16)),
        grid=(m // tm,),
        in_specs=[pl.BlockSpec((tm, k), row), pl.BlockSpec((k, d), fixed),
                  pl.BlockSpec((tm, d), row), pl.BlockSpec((1, d), fixed),
                  pl.BlockSpec((1, d), fixed)],
        out_specs=(pl.BlockSpec((tm, d), row), pl.BlockSpec((tm, d), row)),
        compiler_params=_params("parallel"),
        name="matmul_residual_ln",
    )(x, w, h, g.reshape(1, d), b.reshape(1, d))


def _t5_bucket(dist):
    exact = REL_BUCKETS // 2
    d_f = jnp.maximum(dist, 1).astype(F32)
    large = exact + (jnp.log(d_f / exact) / math.log(REL_MAX_DIST / exact)
                     * (REL_BUCKETS - exact)).astype(I32)
    return jnp.where(dist < exact, dist, jnp.minimum(large, REL_BUCKETS - 1))


def _bias_by_distance(rel_bias, dist):
    return rel_bias[_t5_bucket(dist)]


def _toeplitz(f, rows, cols, off):
    length = f.shape[-1]
    period = rows + cols - 1
    u = f[:, jnp.clip(off + rows - 1 - jnp.arange(period), 0, length - 1)]
    skew = jnp.tile(u, (1, rows + 1))[:, :rows * (period + 1)].reshape(-1, rows, period + 1)
    return skew[:, ::-1, :cols]


DSA_SUB = 32
LOG2E = math.log2(math.e)


def _dsa_select(k_sel, n_keys, t0, iq_ref, ik_ref, key_sc, mask_sc, cut_sc):
    rows = lax.broadcasted_iota(I32, (Q_BLK, n_keys), 0) + t0
    cols = lax.broadcasted_iota(I32, (Q_BLK, n_keys), 1)
    valid = cols <= rows

    ik = ik_ref[0, :n_keys, :].astype(BF16)
    w_all = iq_ref[0, :, A_IDX_HEADS * LANES:] * ((A_IDX_DIM * A_IDX_HEADS) ** -0.5)
    score = jnp.zeros((Q_BLK, n_keys), F32)
    for h in range(A_IDX_HEADS):
        qh = iq_ref[0, :, h * LANES:(h + 1) * LANES].astype(BF16)
        rel = lax.dot_general(qh, ik, (((1,), (1,)), ((), ())), preferred_element_type=F32)
        score = score + w_all[:, A_IDX_DIM + h:A_IDX_DIM + h + 1] * jnp.maximum(rel, 0.0)
    score = jnp.where(score == 0.0, 0.0, score)

    bits = pltpu.bitcast(score, I32)
    key_sc[:, :n_keys] = jnp.where(valid, bits ^ ((bits >> 31) & 0x7FFFFFFF), INT_MIN)

    def search(it, ans_u):
        cand_u = ans_u | lax.shift_left(jnp.int32(1), 31 - it)
        cand_s = cand_u ^ INT_MIN
        cnt = jnp.sum(jnp.where(key_sc[:, :n_keys] >= cand_s, 1.0, 0.0), axis=-1, keepdims=True)
        return jnp.where(cnt >= k_sel, cand_u, ans_u)

    thr = lax.fori_loop(0, 32, search, jnp.zeros((Q_BLK, 1), I32)) ^ INT_MIN

    key = key_sc[:, :n_keys]
    gt = key > thr
    eq = key == thr
    need = k_sel - jnp.sum(jnp.where(gt, 1.0, 0.0), axis=-1, keepdims=True)
    n_eq = jnp.sum(jnp.where(eq, 1.0, 0.0), axis=-1, keepdims=True)
    cut_sc[...] = jnp.full((Q_BLK, 1), n_keys, I32)
    surplus = jnp.where((n_eq > need) & (thr != INT_MIN), 1.0, 0.0)

    @pl.when(jnp.max(surplus) > 0.0)
    def _():
        def tie_search(it, ans):
            cand = ans | lax.shift_left(jnp.int32(1), (n_keys.bit_length() - 1) - it)
            hit = (key_sc[:, :n_keys] == thr) & (cols < cand)
            cnt = jnp.sum(jnp.where(hit, 1.0, 0.0), axis=-1, keepdims=True)
            return jnp.where(cnt < need, cand, ans)
        cut_sc[...] = lax.fori_loop(0, n_keys.bit_length(), tie_search, jnp.zeros((Q_BLK, 1), I32))

    selected = valid & (gt | (eq & (cols <= cut_sc[...])))
    mask_sc[:, :n_keys] = jnp.where(selected, 0.0, MASK_NEG)


def _dsa_kernel(k_sel, seq, q_ref, kv_ref, iq_ref, ik_ref, strip_ref, o_ref,
                key_sc, mask_sc, cut_sc, qs_sc, s_sc, p_sc, acc_sc, m_sc, a_sc, ve_sc):
    i = pl.program_id(1)
    t0 = i * Q_BLK
    blk_per_chunk = KEY_CHUNK // Q_BLK

    @pl.when(i == 0)
    def _():
        ve_sc[:, :A_HEAD_DIM] = kv_ref[0, :, A_HEAD_DIM:]
        ve_sc[:, A_HEAD_DIM:] = jnp.ones((seq, A_HEAD_DIM), BF16)

    for j in range(seq // KEY_CHUNK):
        @pl.when(i // blk_per_chunk == j)
        def _(j=j):
            _dsa_select(k_sel, (j + 1) * KEY_CHUNK, t0, iq_ref, ik_ref, key_sc, mask_sc, cut_sc)

    for h in range(A_HEADS):
        qs_sc[h * Q_BLK:(h + 1) * Q_BLK, :] = q_ref[0, :, h * A_HEAD_DIM:(h + 1) * A_HEAD_DIM]
    m_sc[...] = jnp.full(m_sc.shape, MASK_NEG, F32)
    acc_sc[...] = jnp.zeros(acc_sc.shape, F32)
    n_strip_blk = seq // Q_BLK - 1
    n_tile = KEY_CHUNK // LANES

    def chunk(c, carry):
        k0 = pl.multiple_of(c * KEY_CHUNK, KEY_CHUNK)
        kc = kv_ref[0, pl.ds(k0, KEY_CHUNK), :A_HEAD_DIM]
        s_sc[...] = lax.dot_general(qs_sc[...], kc, (((1,), (1,)), ((), ())),
                                    preferred_element_type=F32)
        w0 = (c * blk_per_chunk - i + n_strip_blk) * Q_BLK
        for h in range(A_HEADS):
            for rb in range(Q_BLK // DSA_SUB):
                qr = slice(rb * DSA_SUB, (rb + 1) * DSA_SUB)
                r = slice(h * Q_BLK + rb * DSA_SUB, h * Q_BLK + (rb + 1) * DSA_SUB)
                t = []
                for jt in range(n_tile):
                    ws = pl.ds(pl.multiple_of(w0 + jt * LANES, LANES), LANES)
                    ks = pl.ds(pl.multiple_of(k0 + jt * LANES, LANES), LANES)
                    t.append(s_sc[r, jt * LANES:(jt + 1) * LANES]
                             + strip_ref[h, qr, ws].astype(F32) + mask_sc[qr, ks])
                mx = functools.reduce(jnp.maximum, t)
                m_old = m_sc[r, :]
                m_new = jnp.maximum(m_old, jnp.max(mx, axis=-1, keepdims=True))
                a_sc[r, :] = jnp.exp2(m_old - m_new)
                m_sc[r, :] = m_new
                for jt in range(n_tile):
                    p_sc[r, jt * LANES:(jt + 1) * LANES] = jnp.exp2(t[jt] - m_new).astype(BF16)
        pv = jnp.dot(p_sc[...], ve_sc[pl.ds(k0, KEY_CHUNK), :], preferred_element_type=F32)
        for half in range(2):
            hs = slice(half * A_HEAD_DIM, (half + 1) * A_HEAD_DIM)
            acc_sc[:, hs] = acc_sc[:, hs] * a_sc[...] + pv[:, hs]
        return carry

    lax.fori_loop(0, (t0 + Q_BLK + KEY_CHUNK - 1) // KEY_CHUNK, chunk, 0)
    for h in range(A_HEADS):
        r = slice(h * Q_BLK, (h + 1) * Q_BLK)
        o_ref[0, :, h * A_HEAD_DIM:(h + 1) * A_HEAD_DIM] = (
            acc_sc[r, :A_HEAD_DIM] / acc_sc[r, A_HEAD_DIM:]).astype(BF16)


def dsa_attention(qkv, idx, strip, bsz, seq):
    assert seq % KEY_CHUNK == 0
    k_sel = min(A_TOPK_MAX, seq // 4)
    rows = A_HEADS * Q_BLK
    n_kv_blk = A_Q // (2 * A_HEAD_DIM)
    n_ik_blk = A_IDX_HEADS
    idx_w = (A_IDX_HEADS + 1) * LANES
    return pl.pallas_call(
        functools.partial(_dsa_kernel, k_sel, seq),
        out_shape=jax.ShapeDtypeStruct((bsz, seq, A_Q), BF16),
        grid=(bsz, seq // Q_BLK),
        in_specs=[pl.BlockSpec((1, Q_BLK, A_Q), lambda b, i: (b, i, 0)),
                  pl.BlockSpec((1, seq, 2 * A_HEAD_DIM), lambda b, i: (b, 0, n_kv_blk)),
                  pl.BlockSpec((1, Q_BLK, idx_w), lambda b, i: (b, i, 0)),
                  pl.BlockSpec((1, seq, LANES), lambda b, i: (b, 0, n_ik_blk)),
                  pl.BlockSpec(strip.shape, lambda b, i: (0, 0, 0))],
        out_specs=pl.BlockSpec((1, Q_BLK, A_Q), lambda b, i: (b, i, 0)),
        scratch_shapes=[pltpu.VMEM((Q_BLK, seq), I32),
                        pltpu.VMEM((Q_BLK, seq), F32),
                        pltpu.VMEM((Q_BLK, 1), I32),
                        pltpu.VMEM((rows, A_HEAD_DIM), BF16),
                        pltpu.VMEM((rows, KEY_CHUNK), F32),
                        pltpu.VMEM((rows, KEY_CHUNK), BF16),
                        pltpu.VMEM((rows, 2 * A_HEAD_DIM), F32),
                        pltpu.VMEM((rows, LANES), F32),
                        pltpu.VMEM((rows, LANES), F32),
                        pltpu.VMEM((seq, 2 * A_HEAD_DIM), BF16)],
        compiler_params=_params("parallel", "arbitrary"),
        name="dsa_attention",
    )(qkv, qkv, idx, idx, strip)


def dsa_bias_strip(rel_bias, seq):
    width = seq + KEY_CHUNK - Q_BLK
    by_dist = _bias_by_distance(rel_bias, jnp.arange(seq)).T
    return (_toeplitz(by_dist, Q_BLK, width, seq - Q_BLK) * LOG2E).astype(BF16)


def dsa_weights(w_in):
    d = w_in.shape[0]
    wq = w_in[:, :A_Q] * (A_HEAD_DIM ** -0.5 * LOG2E)
    wkv = w_in[:, A_Q:A_Q + 2 * A_HEAD_DIM]
    o = A_Q + 2 * A_HEAD_DIM
    n_qi = A_IDX_HEADS * A_IDX_DIM
    wqi = w_in[:, o:o + n_qi].reshape(d, A_IDX_HEADS, A_IDX_DIM)
    wqi = jnp.pad(wqi, ((0, 0), (0, 0), (0, LANES - A_IDX_DIM))).reshape(d, A_IDX_HEADS * LANES)
    wkw = jnp.pad(w_in[:, o + n_qi:], ((0, 0), (0, LANES - A_IDX_DIM - A_IDX_HEADS)))
    return (jnp.concatenate([wq, wkv], axis=1).astype(BF16),
            jnp.concatenate([wqi, wkw], axis=1).astype(BF16))


def dsa_mixer(hb, w_in, w_o, strip, h, g, b, bsz, seq):
    w_qkv, w_idx = dsa_weights(w_in)
    qkv = matmul(hb, w_qkv, BF16, tn=w_qkv.shape[1] // 2)
    idx = matmul(hb, w_idx, F32)
    o = dsa_attention(qkv.reshape(bsz, seq, -1), idx.reshape(bsz, seq, -1), strip, bsz, seq)
    return matmul_residual_ln(o.reshape(bsz * seq, A_Q), w_o.astype(BF16), h, g, b)


def _dilated_kernel(q_ref, kp_ref, kc_ref, vp_ref, vc_ref, bias_ref, o_ref, lse_ref):
    c = pl.program_id(1)
    col = lax.broadcasted_iota(I32, (B_N, 2 * B_N), 1)
    has_prev = (col >= B_N) | (c > 0)
    lane = lax.broadcasted_iota(I32, (B_N, LANES), 1)
    lse_tile = jnp.zeros((B_N, LANES), F32)
    pair = []
    for h in range(B_HEADS):
        hs = slice(h * B_HEAD_DIM, (h + 1) * B_HEAD_DIM)
        kk = jnp.concatenate([kp_ref[0, :, hs], kc_ref[0, :, hs]], axis=0)
        vv = jnp.concatenate([vp_ref[0, :, hs], vc_ref[0, :, hs]], axis=0)
        s = lax.dot_general(q_ref[0, :, hs], kk, (((1,), (1,)), ((), ())),
                            preferred_element_type=F32)
        s = jnp.where(has_prev, s + bias_ref[h], MASK_NEG)
        m = jnp.max(s, axis=-1, keepdims=True)
        p = jnp.exp(s - m)
        l = jnp.sum(p, axis=-1, keepdims=True)
        pair.append(jnp.dot(p.astype(BF16), vv, preferred_element_type=F32) / l)
        lse_tile = jnp.where(lane == h, m + jnp.log(l), lse_tile)
        if len(pair) == LANES // B_HEAD_DIM:
            o_ref[0, :, (h + 1) * B_HEAD_DIM - LANES:(h + 1) * B_HEAD_DIM] = (
                jnp.concatenate(pair, axis=-1).astype(BF16))
            pair = []
    lse_ref[0] = lse_tile


def dilated_group(proj, bias, n_seq, length):
    width = B_HEADS * B_HEAD_DIM
    pv = proj.reshape(n_seq, length, 3 * width)
    blk = (1, B_N, width)

    def spec(which, prev):
        if prev:
            return pl.BlockSpec(blk, lambda s, c: (s, jnp.maximum(c - 1, 0), which))
        return pl.BlockSpec(blk, lambda s, c: (s, c, which))

    return pl.pallas_call(
        _dilated_kernel,
        out_shape=(jax.ShapeDtypeStruct((n_seq, length, width), BF16),
                   jax.ShapeDtypeStruct((n_seq, length, LANES), F32)),
        grid=(n_seq, length // B_N),
        in_specs=[spec(0, False), spec(1, True), spec(1, False), spec(2, True), spec(2, False),
                  pl.BlockSpec(bias.shape, lambda s, c: (0, 0, 0))],
        out_specs=(pl.BlockSpec(blk, lambda s, c: (s, c, 0)),
                   pl.BlockSpec((1, B_N, LANES), lambda s, c: (s, c, 0))),
        compiler_params=_params("parallel", "parallel"),
        name="dilated_group",
    )(pv, pv, pv, pv, pv, bias)


def dilated_bias(rel_bias, dil):
    ii = jnp.arange(B_N)[:, None]
    jj = jnp.arange(2 * B_N)[None, :]
    delta = B_N + ii - jj
    band = (delta >= 0) & (delta <= B_N)
    by_delta = _bias_by_distance(rel_bias, jnp.arange(2 * B_N) * dil).T
    bias = _toeplitz(by_delta, B_N, 2 * B_N, B_N)
    return jnp.where(band[None], bias.astype(F32), MASK_NEG)


def _dilated_merge_kernel(dils, o0_ref, o1_ref, o2_ref, l0_ref, l1_ref, l2_ref, e_ref, w_ref,
                          h_ref, g_ref, b_ref, o_ref, ob_ref, *nat_sc):
    def natural(o_g, l_g, dil, scratch):
        if dil == 1:
            return o_g[0, 0].astype(F32), l_g[0, 0]
        o_sc, l_sc = scratch
        n_tiles, n = o_sc.shape[0], o_sc.shape[1] // dil
        for r in range(dil):
            for j in range(n_tiles):
                o_sc[j, pl.ds(r, n, stride=dil), :] = o_g[0, r, :, j * LANES:(j + 1) * LANES].astype(F32)
            l_sc[pl.ds(r, n, stride=dil), :] = l_g[0, r]
        return jnp.concatenate([o_sc[j] for j in range(n_tiles)], axis=-1), l_sc[...]

    nat = []
    for gi, (o_g, l_g) in enumerate(((o0_ref, l0_ref), (o1_ref, l1_ref), (o2_ref, l2_ref))):
        nat.append(natural(o_g, l_g, dils[gi], nat_sc[2 * gi:2 * gi + 2]))
    (v0, l0), (v1, l1), (v2, l2) = nat
    m = jnp.maximum(jnp.maximum(l0, l1), l2)
    e0, e1, e2 = jnp.exp(l0 - m), jnp.exp(l1 - m), jnp.exp(l2 - m)
    inv = 1.0 / (e0 + e1 + e2)
    e_mat = e_ref[...]

    def spread(wt):
        hi = wt.astype(BF16)
        lo = (wt - hi.astype(F32)).astype(BF16)
        return (jnp.dot(hi, e_mat, preferred_element_type=F32)
                + jnp.dot(lo, e_mat, preferred_element_type=F32))

    mix = spread(e0 * inv) * v0 + spread(e1 * inv) * v1 + spread(e2 * inv) * v2
    y = jnp.dot(mix.astype(BF16), w_ref[...], preferred_element_type=F32)
    out = _layer_norm_rows(DN_ALPHA * h_ref[...] + y, g_ref[...], b_ref[...])
    o_ref[...] = out
    ob_ref[...] = out.astype(BF16)


def dilated_merge(outs, lses, dils, w_o, h, g, b, bsz, seq, tm=512):
    m, d = h.shape
    width = B_HEADS * B_HEAD_DIM
    tm = min(tm, seq)
    n_j = seq // tm
    expand = (jnp.arange(LANES)[:, None] == (jnp.arange(width)[None, :] // B_HEAD_DIM)).astype(BF16)
    row = lambda bi, j: (bi * n_j + j, 0)
    fixed = lambda bi, j: (0, 0)
    grouped = lambda bi, j: (bi, 0, j, 0)
    o_specs = [pl.BlockSpec((1, dl, tm // dl, width), grouped) for dl in dils]
    l_specs = [pl.BlockSpec((1, dl, tm // dl, LANES), grouped) for dl in dils]
    scratch = []
    for dl in dils:
        if dl > 1:
            scratch += [pltpu.VMEM((width // LANES, tm, LANES), F32), pltpu.VMEM((tm, LANES), F32)]
        else:
            scratch += [pltpu.VMEM((8, LANES), F32), pltpu.VMEM((8, LANES), F32)]
    outs = [o.reshape(bsz, dl, seq // dl, width) for o, dl in zip(outs, dils)]
    lses = [l.reshape(bsz, dl, seq // dl, LANES) for l, dl in zip(lses, dils)]
    return pl.pallas_call(
        functools.partial(_dilated_merge_kernel, tuple(dils)),
        out_shape=(jax.ShapeDtypeStruct((m, d), F32), jax.ShapeDtypeStruct((m, d), BF16)),
        grid=(bsz, n_j),
        in_specs=o_specs + l_specs
                 + [pl.BlockSpec((LANES, width), fixed), pl.BlockSpec((width, d), fixed),
                    pl.BlockSpec((tm, d), row), pl.BlockSpec((1, d), fixed), pl.BlockSpec((1, d), fixed)],
        out_specs=(pl.BlockSpec((tm, d), row), pl.BlockSpec((tm, d), row)),
        scratch_shapes=scratch,
        compiler_params=_params("parallel", "parallel"),
        name="dilated_merge",
    )(*outs, *lses, expand, w_o.astype(BF16), h, g.reshape(1, d), b.reshape(1, d))


def dilated_mixer(hb_by_dil, w_in, w_o, rel_bias, h, g, b, bsz, seq):
    width = B_HEADS * B_HEAD_DIM
    outs, lses, dils = [], [], []
    for gi, (window, dil) in enumerate(B_GROUPS):
        assert window // dil == B_N and seq % window == 0
        w_g = w_in[:, gi * 3 * width:(gi + 1) * 3 * width]
        w_g = w_g.at[:, :width].multiply(B_HEAD_DIM ** -0.5)
        proj = matmul(hb_by_dil[dil], w_g.astype(BF16), BF16, tn=3 * width // 2)
        o, lse = dilated_group(proj, dilated_bias(rel_bias, dil), bsz * dil, seq // dil)
        outs.append(o)
        lses.append(lse)
        dils.append(dil)
    return dilated_merge(outs, lses, dils, w_o, h, g, b, bsz, seq)


C_PAD = 128


def _mla_prep_kernel(x_ref, win_ref, qn_ref, kvn_ref, wqa_ref, wqb_ref, wka_ref, wv_ref,
                     cos_ref, sin_ref, q_ref, k_ref, v_ref):
    c = jnp.dot(x_ref[...], win_ref[...], preferred_element_type=F32)
    cos, sin = cos_ref[...], sin_ref[...]

    def rms(v, gain):
        return (v * lax.rsqrt(jnp.mean(v * v, axis=-1, keepdims=True) + RMS_EPS) * gain).astype(BF16)

    nq = rms(c[:, :C_Q_RANK], qn_ref[...])
    nkv = rms(c[:, C_Q_RANK:C_Q_RANK + C_KV_RANK], kvn_ref[...])
    o = C_Q_RANK + C_KV_RANK
    k_rope = c[:, o:o + C_PAD] * cos + c[:, o + C_PAD:o + 2 * C_PAD] * sin
    qa = jnp.dot(nq, wqa_ref[...], preferred_element_type=F32)
    qb = jnp.dot(nq, wqb_ref[...], preferred_element_type=F32)
    kn = jnp.dot(nkv, wka_ref[...], preferred_element_type=F32)
    for h in range(C_HEADS):
        hs = slice(h * C_PAD, (h + 1) * C_PAD)
        q_ref[:, hs] = (qa[:, hs] * cos + qb[:, hs] * sin).astype(BF16)
        k_ref[:, hs] = (kn[:, hs] + k_rope).astype(BF16)
    v_ref[...] = jnp.dot(nkv, wv_ref[...], preferred_element_type=F32).astype(BF16)


def _rot_half_cols(w):
    half = w.shape[-1] // 2
    return jnp.concatenate([-w[..., half:], w[..., :half]], axis=-1)


def mla_prep(hb, w_in, q_norm, kv_norm, w_q_up, w_kv_up, bsz, seq, tm=512):
    d = w_in.shape[0]
    scale = (C_NOPE + C_ROPE) ** -0.5
    pad_r = C_PAD - C_NOPE - C_ROPE
    w_kr = w_in[:, C_Q_RANK + C_KV_RANK:]

    def rope_slot(w):
        return jnp.pad(w, ((0, 0), (C_NOPE, pad_r)))

    win = jnp.concatenate([w_in[:, :C_Q_RANK + C_KV_RANK], rope_slot(w_kr),
                           rope_slot(_rot_half_cols(w_kr))], axis=1).astype(BF16)
    wq = w_q_up.reshape(C_Q_RANK, C_HEADS, C_NOPE + C_ROPE) * scale
    wqa = jnp.pad(wq, ((0, 0), (0, 0), (0, pad_r))).reshape(C_Q_RANK, C_HEADS * C_PAD).astype(BF16)
    wqb = jnp.pad(_rot_half_cols(wq[..., C_NOPE:]), ((0, 0), (0, 0), (C_NOPE, pad_r)))
    wqb = wqb.reshape(C_Q_RANK, C_HEADS * C_PAD).astype(BF16)
    wkv = w_kv_up.reshape(C_KV_RANK, C_HEADS, C_NOPE + C_V)
    wka = jnp.pad(wkv[..., :C_NOPE], ((0, 0), (0, 0), (0, C_PAD - C_NOPE)))
    wka = wka.reshape(C_KV_RANK, C_HEADS * C_PAD).astype(BF16)
    wv = wkv[..., C_NOPE:].reshape(C_KV_RANK, C_HEADS * C_V).astype(BF16)

    half = C_ROPE // 2
    inv = ROPE_BASE ** (-jnp.arange(half, dtype=F32) / half)
    ang = jnp.arange(seq, dtype=F32)[:, None] * inv[None, :]
    ones, zeros = jnp.ones((seq, C_NOPE), F32), jnp.zeros((seq, pad_r), F32)
    cos = jnp.concatenate([ones, jnp.cos(ang), jnp.cos(ang), zeros], axis=1)
    sin = jnp.concatenate([0 * ones, jnp.sin(ang), jnp.sin(ang), zeros], axis=1)

    m = bsz * seq
    tm = min(tm, seq)
    n_pos_blk = seq // tm
    row = lambda i: (i, 0)
    fixed = lambda i: (0, 0)
    pos = lambda i: (i % n_pos_blk, 0)
    full = lambda a: pl.BlockSpec(a.shape, fixed)
    qw, vw = C_HEADS * C_PAD, C_HEADS * C_V
    return pl.pallas_call(
        _mla_prep_kernel,
        out_shape=(jax.ShapeDtypeStruct((m, qw), BF16), jax.ShapeDtypeStruct((m, qw), BF16),
                   jax.ShapeDtypeStruct((m, vw), BF16)),
        grid=(m // tm,),
        in_specs=[pl.BlockSpec((tm, d), row), full(win),
                  pl.BlockSpec((1, C_Q_RANK), fixed), pl.BlockSpec((1, C_KV_RANK), fixed),
                  full(wqa), full(wqb), full(wka), full(wv),
                  pl.BlockSpec((tm, C_PAD), pos), pl.BlockSpec((tm, C_PAD), pos)],
        out_specs=(pl.BlockSpec((tm, qw), row), pl.BlockSpec((tm, qw), row),
                   pl.BlockSpec((tm, vw), row)),
        compiler_params=_params("parallel"),
        name="mla_prep",
    )(hb, win, q_norm.reshape(1, -1), kv_norm.reshape(1, -1), wqa, wqb, wka, wv, cos, sin)


MLA_TQ = 512


def _mla_attn_kernel(q_ref, k_ref, v_ref, o_ref):
    i = pl.program_id(2)
    tq = MLA_TQ
    rows = lax.broadcasted_iota(I32, (tq, tq), 0)
    cols = lax.broadcasted_iota(I32, (tq, tq), 1)
    outs = []
    for hh in range(2):
        q = q_ref[0, :, hh * C_PAD:(hh + 1) * C_PAD]

        def step(c, carry, diag):
            m_old, l_old, acc = carry
            k0 = pl.multiple_of(c * tq, tq)
            kc = k_ref[0, pl.ds(k0, tq), hh * C_PAD:(hh + 1) * C_PAD]
            vc = v_ref[0, pl.ds(k0, tq), hh * C_V:(hh + 1) * C_V]
            s = lax.dot_general(q, kc, (((1,), (1,)), ((), ())), preferred_element_type=F32)
            if diag:
                s = jnp.where(cols <= rows, s, MASK_NEG)
            m_new = jnp.maximum(m_old, jnp.max(s, axis=-1, keepdims=True))
            alpha = jnp.exp(m_old - m_new)
            p = jnp.exp(s - m_new)
            l_new = alpha * l_old + jnp.sum(p, axis=-1, keepdims=True)
            acc = alpha * acc + jnp.dot(p.astype(BF16), vc, preferred_element_type=F32)
            return m_new, l_new, acc

        init = (jnp.full((tq, 1), MASK_NEG, F32), jnp.zeros((tq, 1), F32), jnp.zeros((tq, C_V), F32))
        carry = lax.fori_loop(0, i, functools.partial(step, diag=False), init)
        _, l, acc = step(i, carry, diag=True)
        outs.append(acc / l)
    o_ref[0] = jnp.concatenate(outs, axis=-1).astype(BF16)


def mla_attention(q, k, v, bsz, seq):
    qw, vw = C_HEADS * C_PAD, C_HEADS * C_V
    return pl.pallas_call(
        _mla_attn_kernel,
        out_shape=jax.ShapeDtypeStruct((bsz, seq, vw), BF16),
        grid=(bsz, C_HEADS // 2, seq // MLA_TQ),
        in_specs=[pl.BlockSpec((1, MLA_TQ, 2 * C_PAD), lambda b, hp, i: (b, i, hp)),
                  pl.BlockSpec((1, seq, 2 * C_PAD), lambda b, hp, i: (b, 0, hp)),
                  pl.BlockSpec((1, seq, 2 * C_V), lambda b, hp, i: (b, 0, hp))],
        out_specs=pl.BlockSpec((1, MLA_TQ, 2 * C_V), lambda b, hp, i: (b, i, hp)),
        compiler_params=_params("parallel", "parallel", "arbitrary"),
        name="mla_attention",
    )(q.reshape(bsz, seq, qw), k.reshape(bsz, seq, qw), v.reshape(bsz, seq, vw))


def mla_mixer(hb, w_in, q_norm, kv_norm, w_q_up, w_kv_up, w_o, h, g, b, bsz, seq):
    q, k, v = mla_prep(hb, w_in, q_norm, kv_norm, w_q_up, w_kv_up, bsz, seq)
    o = mla_attention(q, k, v, bsz, seq)
    return matmul_residual_ln(o.reshape(bsz * seq, -1), w_o.astype(BF16), h, g, b)


def _router_kernel(h_ref, wr_ref, eb_ref, eidx_ref, gate_ref):
    tm = h_ref.shape[0]
    per = N_EXPERTS // N_GROUPS
    logits = lax.dot_general(wr_ref[...], h_ref[...], (((1,), (1,)), ((), ())),
                             precision=lax.Precision.HIGHEST, preferred_element_type=F32)
    scores = 1.0 / (1.0 + jnp.exp(-logits))
    s3 = scores.reshape(N_GROUPS, per, tm)
    c3 = (scores + eb_ref[...]).reshape(N_GROUPS, per, tm)
    neg_inf = -jnp.inf
    j_idx = lax.broadcasted_iota(I32, (N_GROUPS, per, tm), 1).astype(F32)
    g_idx = lax.broadcasted_iota(I32, (N_GROUPS, per, tm), 0).astype(F32)
    flat = g_idx * per + j_idx

    m1 = jnp.max(c3, axis=1, keepdims=True)
    first = jnp.min(jnp.where(c3 == m1, j_idx, per), axis=1, keepdims=True)
    m2 = jnp.max(jnp.where(j_idx == first, neg_inf, c3), axis=1, keepdims=True)
    gs = m1 + m2

    gi = lax.broadcasted_iota(I32, (N_GROUPS, 1, tm), 0).astype(F32)
    keep = jnp.zeros((N_GROUPS, 1, tm), jnp.bool_)
    cur = gs
    for _ in range(TOPK_GROUPS):
        mx = jnp.max(cur, axis=0, keepdims=True)
        pick = gi == jnp.min(jnp.where(cur == mx, gi, N_GROUPS), axis=0, keepdims=True)
        keep = keep | pick
        cur = jnp.where(pick, neg_inf, cur)
    cur = jnp.where(keep, c3, neg_inf)

    idxs, gates = [], []
    for _ in range(TOP_K):
        mx = jnp.max(jnp.max(cur, axis=1, keepdims=True), axis=0, keepdims=True)
        cand = jnp.where(cur == mx, flat, N_EXPERTS)
        fi = jnp.min(jnp.min(cand, axis=1, keepdims=True), axis=0, keepdims=True)
        pick = flat == fi
        gates.append(jnp.sum(jnp.sum(jnp.where(pick, s3, 0.0), axis=1, keepdims=True), axis=0))
        idxs.append(fi[0])
        cur = jnp.where(pick, neg_inf, cur)
    gate = jnp.concatenate(gates, axis=0)
    gate = gate / jnp.sum(gate, axis=0, keepdims=True) * ROUTED_SCALE
    eidx_ref[...] = jnp.concatenate(idxs, axis=0).astype(I32)
    gate_ref[...] = gate


def moe_router(h, w_router, e_bias, tm=512):
    m, d = h.shape
    tm = min(tm, m)
    return pl.pallas_call(
        _router_kernel,
        out_shape=(jax.ShapeDtypeStruct((TOP_K, m), I32), jax.ShapeDtypeStruct((TOP_K, m), F32)),
        grid=(m // tm,),
        in_specs=[pl.BlockSpec((tm, d), lambda i: (i, 0)),
                  pl.BlockSpec((N_EXPERTS, d), lambda i: (0, 0)),
                  pl.BlockSpec((N_EXPERTS, 1), lambda i: (0, 0))],
        out_specs=(pl.BlockSpec((TOP_K, tm), lambda i: (0, i)),
                   pl.BlockSpec((TOP_K, tm), lambda i: (0, i))),
        compiler_params=_params("parallel"),
        name="moe_router",
    )(h, w_router.T, e_bias.reshape(N_EXPERTS, 1))


def _silu(x):
    return x / (1.0 + jnp.exp(-x))


RANK_TM = 1024


def _rank_kernel(eidx_ref, tri_ref, dest_ref, cnt_ref, start_ref, cnt_sc, run_sc):
    p, i = pl.program_id(0), pl.program_id(1)
    tm = eidx_ref.shape[1]
    e = eidx_ref[...]
    ex = lax.broadcasted_iota(I32, (N_EXPERTS, tm), 0)
    onehot = jnp.zeros((N_EXPERTS, tm), F32)
    for k in range(TOP_K):
        onehot = onehot + jnp.where(e[k:k + 1, :] == ex, 1.0, 0.0)
    tile_cnt = jnp.sum(onehot, axis=-1, keepdims=True)

    @pl.when((p == 0) & (i == 0))
    def _():
        cnt_sc[...] = jnp.zeros(cnt_sc.shape, F32)

    @pl.when(p == 0)
    def _():
        cnt_sc[...] += tile_cnt

    @pl.when((p == 1) & (i == 0))
    def _():
        cnt = cnt_sc[...]
        padded = jnp.floor((cnt + (MOE_BLK - 1)) * (1.0 / MOE_BLK)) * MOE_BLK
        below = (lax.broadcasted_iota(I32, (N_EXPERTS, N_EXPERTS), 1)
                 < lax.broadcasted_iota(I32, (N_EXPERTS, N_EXPERTS), 0)).astype(F32)
        start = jnp.dot(below, jnp.broadcast_to(padded, (N_EXPERTS, LANES)),
                        precision=lax.Precision.HIGHEST, preferred_element_type=F32)
        run_sc[...] = start[:, :1]
        cnt_ref[...] = jnp.broadcast_to(cnt, (N_EXPERTS, LANES))
        start_ref[...] = start

    @pl.when(p == 1)
    def _():
        before = jnp.dot(onehot.astype(BF16), tri_ref[...], preferred_element_type=F32) + run_sc[...]
        rows = [jnp.sum(jnp.where(e[k:k + 1, :] == ex, before, 0.0), axis=0, keepdims=True)
                for k in range(TOP_K)]
        dest_ref[...] = jnp.concatenate(rows, axis=0).astype(I32)
        run_sc[...] += tile_cnt


def moe_rank(eidx):
    n_tok = eidx.shape[1]
    tm = min(RANK_TM, n_tok)
    tri = (jnp.arange(tm)[:, None] < jnp.arange(tm)[None, :]).astype(BF16)
    stat = jax.ShapeDtypeStruct((N_EXPERTS, LANES), F32)
    dest, cnt, start = pl.pallas_call(
        _rank_kernel,
        out_shape=(jax.ShapeDtypeStruct((TOP_K, n_tok), I32), stat, stat),
        grid=(2, n_tok // tm),
        in_specs=[pl.BlockSpec((TOP_K, tm), lambda p, i: (0, i)),
                  pl.BlockSpec((tm, tm), lambda p, i: (0, 0))],
        out_specs=(pl.BlockSpec((TOP_K, tm), lambda p, i: (0, i * p)),
                   pl.BlockSpec((N_EXPERTS, LANES), lambda p, i: (0, 0)),
                   pl.BlockSpec((N_EXPERTS, LANES), lambda p, i: (0, 0))),
        scratch_shapes=[pltpu.VMEM((N_EXPERTS, 1), F32), pltpu.VMEM((N_EXPERTS, 1), F32)],
        compiler_params=_params("arbitrary", "arbitrary"),
        name="moe_rank",
    )(eidx, tri)
    return dest, cnt[:, 0].astype(I32), start[:, 0].astype(I32)


def _experts_kernel(blk_e_ref, x_ref, wg_ref, wu_ref, wd_ref, o_ref):
    del blk_e_ref
    x = x_ref[...]
    gate = jnp.dot(x, wg_ref[0], preferred_element_type=F32)
    up = jnp.dot(x, wu_ref[0], preferred_element_type=F32)
    act = (_silu(gate) * up).astype(BF16)
    o_ref[...] = jnp.dot(act, wd_ref[0], preferred_element_type=F32).astype(o_ref.dtype)


def grouped_experts(x_sorted, blk_e, wg, wu, wd):
    n_slot, d = x_sorted.shape
    n_blk = n_slot // MOE_BLK
    by_expert = lambda i, be: (be[i], 0, 0)
    grid_spec = pltpu.PrefetchScalarGridSpec(
        num_scalar_prefetch=1,
        grid=(n_blk,),
        in_specs=[pl.BlockSpec((MOE_BLK, d), lambda i, be: (i, 0)),
                  pl.BlockSpec((1, d, D_EXPERT), by_expert),
                  pl.BlockSpec((1, d, D_EXPERT), by_expert),
                  pl.BlockSpec((1, D_EXPERT, d), by_expert)],
        out_specs=pl.BlockSpec((MOE_BLK, d), lambda i, be: (i, 0)),
    )
    return pl.pallas_call(
        _experts_kernel,
        out_shape=jax.ShapeDtypeStruct((n_slot, d), BF16),
        grid_spec=grid_spec,
        compiler_params=_params("arbitrary"),
        name="grouped_experts",
    )(blk_e, x_sorted, wg, wu, wd)


def _moe_finish_kernel(dils, hb_ref, y_ref, gt_ref, wg_ref, wu_ref, wd_ref, h_ref, g_ref, b_ref,
                       o_ref, ob_ref, *rest):
    x = hb_ref[...]
    d = x.shape[1]
    act = (_silu(jnp.dot(x, wg_ref[...], preferred_element_type=F32))
           * jnp.dot(x, wu_ref[...], preferred_element_type=F32)).astype(BF16)
    ff = jnp.dot(act, wd_ref[...], preferred_element_type=F32)
    gt = gt_ref[...]
    for k in range(TOP_K):
        ff = ff + gt[:, k:k + 1] * y_ref[k].astype(F32)
    out = _layer_norm_rows(DN_ALPHA * h_ref[...] + ff, g_ref[...], b_ref[...])
    o_ref[...] = out
    ob_ref[...] = out.astype(BF16)
    if dils:
        perm_refs, nat_sc = rest[:-1], rest[-1]
        for j in range(d // LANES):
            nat_sc[j] = out[:, j * LANES:(j + 1) * LANES]
        for p_ref, dil in zip(perm_refs, dils):
            n = out.shape[0] // dil
            for r in range(dil):
                for j in range(d // LANES):
                    p_ref[0, r, :, j * LANES:(j + 1) * LANES] = (
                        nat_sc[j, pl.ds(r, n, stride=dil), :].astype(BF16))


def moe_finish(hb, y_tok, gate_tok, ws_g, ws_u, ws_d, h, g, b, seq, dils=(), tm=256):
    m, d = h.shape
    tm = min(tm, seq)
    n_j = seq // tm
    row = lambda i: (i, 0)
    fixed = lambda i: (0, 0)
    out_shape = [jax.ShapeDtypeStruct((m, d), F32), jax.ShapeDtypeStruct((m, d), BF16)]
    out_specs = [pl.BlockSpec((tm, d), row), pl.BlockSpec((tm, d), row)]
    for dil in dils:
        out_shape.append(jax.ShapeDtypeStruct((m // seq, dil, seq // dil, d), BF16))
        out_specs.append(pl.BlockSpec((1, dil, tm // dil, d), lambda i: (i // n_j, 0, i % n_j, 0)))
    res = pl.pallas_call(
        functools.partial(_moe_finish_kernel, tuple(dils)),
        out_shape=tuple(out_shape),
        grid=(m // tm,),
        in_specs=[pl.BlockSpec((tm, d), row), pl.BlockSpec((TOP_K, tm, d), lambda i: (0, i, 0)),
                  pl.BlockSpec((tm, TOP_K), row),
                  pl.BlockSpec(ws_g.shape, fixed), pl.BlockSpec(ws_u.shape, fixed),
                  pl.BlockSpec(ws_d.shape, fixed),
                  pl.BlockSpec((tm, d), row), pl.BlockSpec((1, d), fixed), pl.BlockSpec((1, d), fixed)],
        out_specs=tuple(out_specs),
        scratch_shapes=[pltpu.VMEM((d // LANES, tm, LANES), F32)] if dils else [],
        compiler_params=_params("parallel"),
        name="moe_finish",
    )(hb, y_tok, gate_tok, ws_g, ws_u, ws_d, h, g.reshape(1, d), b.reshape(1, d))
    return res[0], res[1], [r.reshape(m, d) for r in res[2:]]


SORT_E_SHIFT = 19


def moe_layer(h, hb, w_router, e_bias, wg, wu, wd, ws_g, ws_u, ws_d, g, b, seq, dils=()):
    n_tok, d = h.shape
    n_asg = n_tok * TOP_K
    n_pad = N_EXPERTS * MOE_BLK
    assert n_asg <= 1 << (SORT_E_SHIFT - 1) and n_pad <= 1 << (SORT_E_SHIFT - 1)
    eidx, gate = moe_router(h, w_router, e_bias)
    dest, counts, pad_start = moe_rank(eidx)
    padded = (counts + MOE_BLK - 1) // MOE_BLK * MOE_BLK
    pad_end = pad_start + padded
    n_blk = n_asg // MOE_BLK + N_EXPERTS
    blk_start = jnp.arange(n_blk, dtype=I32) * MOE_BLK
    blk_e = jnp.minimum(jnp.sum(pad_end[None, :] <= blk_start[:, None], axis=1), N_EXPERTS - 1)

    key_real = (eidx.T.reshape(-1) << SORT_E_SHIFT) | jnp.arange(n_asg, dtype=I32)
    i_pad = jnp.arange(n_pad, dtype=I32)
    e_pad = jnp.sum(jnp.cumsum(padded - counts)[None, :] <= i_pad[:, None], axis=1).astype(I32)
    key_pad = (e_pad << SORT_E_SHIFT) | (1 << (SORT_E_SHIFT - 1)) | i_pad
    low = jnp.sort(jnp.concatenate([key_real, key_pad])) & ((1 << SORT_E_SHIFT) - 1)
    slot_tok = jnp.where(low < (1 << (SORT_E_SHIFT - 1)), low // TOP_K,
                         jnp.arange(n_asg + n_pad, dtype=I32) % n_tok)

    x_sorted = hb.at[slot_tok].get(mode="promise_in_bounds")
    y = grouped_experts(x_sorted, blk_e.astype(I32), wg, wu, wd)
    y_k = y.at[dest.reshape(-1)].get(mode="promise_in_bounds").reshape(TOP_K, n_tok, d)
    return moe_finish(hb, y_k, gate.T, ws_g, ws_u, ws_d, h, g, b, seq, dils)


def kernel(x, rel_bias, a_w_in, a_w_o, b_w_in, b_w_o, c_w_in, c_q_norm, c_kv_norm, c_w_q_up,
           c_w_kv_up, c_w_o, ln_g, ln_b, moe_w_router, moe_bias, moe_w_gate, moe_w_up,
           moe_w_down, moe_ws_gate, moe_ws_up, moe_ws_down):
    bsz, seq, d = x.shape
    depth = ln_g.shape[0]
    h = x.reshape(bsz * seq, d)
    hb = h.astype(BF16)
    strip = dsa_bias_strip(rel_bias, seq)
    extra_dils = tuple(dil for _, dil in B_GROUPS if dil > 1)
    hb_perm = []
    for layer in range(depth):
        kind, slot = layer % N_MIXERS, layer // N_MIXERS
        g0, b0 = ln_g[layer, 0], ln_b[layer, 0]
        if kind == 0:
            h, hb = dsa_mixer(hb, a_w_in[slot], a_w_o[slot], strip, h, g0, b0, bsz, seq)
        elif kind == 1:
            hb_by_dil = {1: hb, **dict(zip(extra_dils, hb_perm))}
            h, hb = dilated_mixer(hb_by_dil, b_w_in[slot], b_w_o[slot], rel_bias, h, g0, b0, bsz, seq)
        else:
            h, hb = mla_mixer(hb, c_w_in[slot], c_q_norm[slot], c_kv_norm[slot], c_w_q_up[slot],
                              c_w_kv_up[slot], c_w_o[slot], h, g0, b0, bsz, seq)
        next_dilated = layer + 1 < depth and (layer + 1) % N_MIXERS == 1
        h, hb, hb_perm = moe_layer(h, hb, moe_w_router[layer], moe_bias[layer],
                                   moe_w_gate[layer].astype(BF16), moe_w_up[layer].astype(BF16),
                                   moe_w_down[layer].astype(BF16), moe_ws_gate[layer].astype(BF16),
                                   moe_ws_up[layer].astype(BF16), moe_ws_down[layer].astype(BF16),
                                   ln_g[layer, 1], ln_b[layer, 1], seq,
                                   extra_dils if next_dilated else ())
    return h.reshape(bsz, seq, d)
```

```python
import functools
import math

import jax
import jax.numpy as jnp
from jax import lax
from jax.experimental import pallas as pl
from jax.experimental.pallas import tpu as pltpu

F32 = jnp.float32
BF16 = jnp.bfloat16
I32 = jnp.int32

LANES = 128
VMEM_LIMIT_BYTES = 56 * 1024 * 1024

D_MODEL = 1024
DEPTH = 4
N_MIXERS = 3
NORM_EPS = 1e-5
RMS_EPS = 1e-6
REL_BUCKETS = 32
REL_MAX_DIST = 2048
A_HEADS = 16
A_HEAD_DIM = 128
A_IDX_HEADS = 8
A_IDX_DIM = 64
A_TOPK_MAX = 256
A_Q = A_HEADS * A_HEAD_DIM
B_GROUPS = ((128, 1), (512, 4), (2048, 16))
B_HEADS = 16
B_HEAD_DIM = 64
B_N = 128
C_HEADS = 16
C_Q_RANK = 256
C_KV_RANK = 128
C_NOPE = 64
C_ROPE = 32
C_V = 64
ROPE_BASE = 10000.0
N_EXPERTS = 64
TOP_K = 8
N_GROUPS = 8
TOPK_GROUPS = 4
D_EXPERT = 256
ROUTED_SCALE = 2.5
DN_ALPHA = (2 * DEPTH) ** 0.25

Q_BLK = 128
KEY_CHUNK = 512
MOE_BLK = 512
MASK_NEG = -1e30
INT_MIN = -(2 ** 31)


def _params(*sem):
    return pltpu.CompilerParams(dimension_semantics=sem, vmem_limit_bytes=VMEM_LIMIT_BYTES)


def _mm_kernel(x_ref, w_ref, o_ref):
    o_ref[...] = jnp.dot(x_ref[...], w_ref[...], preferred_element_type=F32).astype(o_ref.dtype)


def matmul(x, w, out_dtype, tm=1024, tn=None):
    m, k = x.shape
    n = w.shape[1]
    tn = n if tn is None else tn
    tm = min(tm, m)
    return pl.pallas_call(
        _mm_kernel,
        out_shape=jax.ShapeDtypeStruct((m, n), out_dtype),
        grid=(n // tn, m // tm),
        in_specs=[pl.BlockSpec((tm, k), lambda j, i: (i, 0)),
                  pl.BlockSpec((k, tn), lambda j, i: (0, j))],
        out_specs=pl.BlockSpec((tm, tn), lambda j, i: (i, j)),
        compiler_params=_params("parallel", "parallel"),
        name="matmul",
    )(x, w)


def _layer_norm_rows(z, g, b):
    mu = jnp.mean(z, axis=-1, keepdims=True)
    zc = z - mu
    var = jnp.mean(zc * zc, axis=-1, keepdims=True)
    return zc * lax.rsqrt(var + NORM_EPS) * g + b


def _mm_ln_kernel(x_ref, w_ref, h_ref, g_ref, b_ref, o_ref, ob_ref):
    y = jnp.dot(x_ref[...], w_ref[...], preferred_element_type=F32)
    out = _layer_norm_rows(DN_ALPHA * h_ref[...] + y, g_ref[...], b_ref[...])
    o_ref[...] = out
    ob_ref[...] = out.astype(BF16)


def matmul_residual_ln(x, w, h, g, b, tm=512):
    m, k = x.shape
    d = w.shape[1]
    tm = min(tm, m)
    row = lambda i: (i, 0)
    fixed = lambda i: (0, 0)
    return pl.pallas_call(
        _mm_ln_kernel,
        out_shape=(jax.ShapeDtypeStruct((m, d), F32), jax.ShapeDtypeStruct((m, d), BF16)),
        grid=(m // tm,),
        in_specs=[pl.BlockSpec((tm, k), row), pl.BlockSpec((k, d), fixed),
                  pl.BlockSpec((tm, d), row), pl.BlockSpec((1, d), fixed),
                  pl.BlockSpec((1, d), fixed)],
        out_specs=(pl.BlockSpec((tm, d), row), pl.BlockSpec((tm, d), row)),
        compiler_params=_params("parallel"),
        name="matmul_residual_ln",
    )(x, w, h, g.reshape(1, d), b.reshape(1, d))


def _t5_bucket(dist):
    exact = REL_BUCKETS // 2
    d_f = jnp.maximum(dist, 1).astype(F32)
    large = exact + (jnp.log(d_f / exact) / math.log(REL_MAX_DIST / exact)
                     * (REL_BUCKETS - exact)).astype(I32)
    return jnp.where(dist < exact, dist, jnp.minimum(large, REL_BUCKETS - 1))


def _bias_by_distance(rel_bias, dist):
    return rel_bias[_t5_bucket(dist)]


def _toeplitz(f, rows, cols, off):
    length = f.shape[-1]
    period = rows + cols - 1
    u = f[:, jnp.clip(off + rows - 1 - jnp.arange(period), 0, length - 1)]
    skew = jnp.tile(u, (1, rows + 1))[:, :rows * (period + 1)].reshape(-1, rows, period + 1)
    return skew[:, ::-1, :cols]


DSA_SUB = 32
LOG2E = math.log2(math.e)


def _dsa_select(k_sel, n_keys, t0, iq_ref, ik_ref, key_sc, mask_sc, cut_sc):
    rows = lax.broadcasted_iota(I32, (Q_BLK, n_keys), 0) + t0
    cols = lax.broadcasted_iota(I32, (Q_BLK, n_keys), 1)
    valid = cols <= rows

    ik = ik_ref[0, :n_keys, :].astype(BF16)
    w_all = iq_ref[0, :, A_IDX_HEADS * LANES:] * ((A_IDX_DIM * A_IDX_HEADS) ** -0.5)
    score = jnp.zeros((Q_BLK, n_keys), F32)
    for h in range(A_IDX_HEADS):
        qh = iq_ref[0, :, h * LANES:(h + 1) * LANES].astype(BF16)
        rel = lax.dot_general(qh, ik, (((1,), (1,)), ((), ())), preferred_element_type=F32)
        score = score + w_all[:, A_IDX_DIM + h:A_IDX_DIM + h + 1] * jnp.maximum(rel, 0.0)
    score = jnp.where(score == 0.0, 0.0, score)

    bits = pltpu.bitcast(score, I32)
    key_sc[:, :n_keys] = jnp.where(valid, bits ^ ((bits >> 31) & 0x7FFFFFFF), INT_MIN)

    def search(it, ans_u):
        cand_u = ans_u | lax.shift_left(jnp.int32(1), 31 - it)
        cand_s = cand_u ^ INT_MIN
        cnt = jnp.sum(jnp.where(key_sc[:, :n_keys] >= cand_s, 1.0, 0.0), axis=-1, keepdims=True)
        return jnp.where(cnt >= k_sel, cand_u, ans_u)

    thr = lax.fori_loop(0, 32, search, jnp.zeros((Q_BLK, 1), I32)) ^ INT_MIN

    key = key_sc[:, :n_keys]
    gt = key > thr
    eq = key == thr
    need = k_sel - jnp.sum(jnp.where(gt, 1.0, 0.0), axis=-1, keepdims=True)
    n_eq = jnp.sum(jnp.where(eq, 1.0, 0.0), axis=-1, keepdims=True)
    cut_sc[...] = jnp.full((Q_BLK, 1), n_keys, I32)
    surplus = jnp.where((n_eq > need) & (thr != INT_MIN), 1.0, 0.0)

    @pl.when(jnp.max(surplus) > 0.0)
    def _():
        def tie_search(it, ans):
            cand = ans | lax.shift_left(jnp.int32(1), (n_keys.bit_length() - 1) - it)
            hit = (key_sc[:, :n_keys] == thr) & (cols < cand)
            cnt = jnp.sum(jnp.where(hit, 1.0, 0.0), axis=-1, keepdims=True)
            return jnp.where(cnt < need, cand, ans)
        cut_sc[...] = lax.fori_loop(0, n_keys.bit_length(), tie_search, jnp.zeros((Q_BLK, 1), I32))

    selected = valid & (gt | (eq & (cols <= cut_sc[...])))
    mask_sc[:, :n_keys] = jnp.where(selected, 0.0, MASK_NEG)


def _dsa_kernel(k_sel, seq, q_ref, kv_ref, iq_ref, ik_ref, strip_ref, o_ref,
                key_sc, mask_sc, cut_sc, qs_sc, s_sc, p_sc, acc_sc, m_sc, a_sc, ve_sc):
    i = pl.program_id(1)
    t0 = i * Q_BLK
    blk_per_chunk = KEY_CHUNK // Q_BLK

    @pl.when(i == 0)
    def _():
        ve_sc[:, :A_HEAD_DIM] = kv_ref[0, :, A_HEAD_DIM:]
        ve_sc[:, A_HEAD_DIM:] = jnp.ones((seq, A_HEAD_DIM), BF16)

    for j in range(seq // KEY_CHUNK):
        @pl.when(i // blk_per_chunk == j)
        def _(j=j):
            _dsa_select(k_sel, (j + 1) * KEY_CHUNK, t0, iq_ref, ik_ref, key_sc, mask_sc, cut_sc)

    for h in range(A_HEADS):
        qs_sc[h * Q_BLK:(h + 1) * Q_BLK, :] = q_ref[0, :, h * A_HEAD_DIM:(h + 1) * A_HEAD_DIM]
    m_sc[...] = jnp.full(m_sc.shape, MASK_NEG, F32)
    acc_sc[...] = jnp.zeros(acc_sc.shape, F32)
    n_strip_blk = seq // Q_BLK - 1
    n_tile = KEY_CHUNK // LANES

    def chunk(c, carry):
        k0 = pl.multiple_of(c * KEY_CHUNK, KEY_CHUNK)
        kc = kv_ref[0, pl.ds(k0, KEY_CHUNK), :A_HEAD_DIM]
        s_sc[...] = lax.dot_general(qs_sc[...], kc, (((1,), (1,)), ((), ())),
                                    preferred_element_type=F32)
        w0 = (c * blk_per_chunk - i + n_strip_blk) * Q_BLK
        for h in range(A_HEADS):
            for rb in range(Q_BLK // DSA_SUB):
                qr = slice(rb * DSA_SUB, (rb + 1) * DSA_SUB)
                r = slice(h * Q_BLK + rb * DSA_SUB, h * Q_BLK + (rb + 1) * DSA_SUB)
                t = []
                for jt in range(n_tile):
                    ws = pl.ds(pl.multiple_of(w0 + jt * LANES, LANES), LANES)
                    ks = pl.ds(pl.multiple_of(k0 + jt * LANES, LANES), LANES)
                    t.append(s_sc[r, jt * LANES:(jt + 1) * LANES]
                             + strip_ref[h, qr, ws].astype(F32) + mask_sc[qr, ks])
                mx = functools.reduce(jnp.maximum, t)
                m_old = m_sc[r, :]
                m_new = jnp.maximum(m_old, jnp.max(mx, axis=-1, keepdims=True))
                a_sc[r, :] = jnp.exp2(m_old - m_new)
                m_sc[r, :] = m_new
                for jt in range(n_tile):
                    p_sc[r, jt * LANES:(jt + 1) * LANES] = jnp.exp2(t[jt] - m_new).astype(BF16)
        pv = jnp.dot(p_sc[...], ve_sc[pl.ds(k0, KEY_CHUNK), :], preferred_element_type=F32)
        for half in range(2):
            hs = slice(half * A_HEAD_DIM, (half + 1) * A_HEAD_DIM)
            acc_sc[:, hs] = acc_sc[:, hs] * a_sc[...] + pv[:, hs]
        return carry

    lax.fori_loop(0, (t0 + Q_BLK + KEY_CHUNK - 1) // KEY_CHUNK, chunk, 0)
    for h in range(A_HEADS):
        r = slice(h * Q_BLK, (h + 1) * Q_BLK)
        o_ref[0, :, h * A_HEAD_DIM:(h + 1) * A_HEAD_DIM] = (
            acc_sc[r, :A_HEAD_DIM] / acc_sc[r, A_HEAD_DIM:]).astype(BF16)


def dsa_attention(qkv, idx, strip, bsz, seq):
    assert seq % KEY_CHUNK == 0
    k_sel = min(A_TOPK_MAX, seq // 4)
    rows = A_HEADS * Q_BLK
    n_kv_blk = A_Q // (2 * A_HEAD_DIM)
    n_ik_blk = A_IDX_HEADS
    idx_w = (A_IDX_HEADS + 1) * LANES
    return pl.pallas_call(
        functools.partial(_dsa_kernel, k_sel, seq),
        out_shape=jax.ShapeDtypeStruct((bsz, seq, A_Q), BF16),
        grid=(bsz, seq // Q_BLK),
        in_specs=[pl.BlockSpec((1, Q_BLK, A_Q), lambda b, i: (b, i, 0)),
                  pl.BlockSpec((1, seq, 2 * A_HEAD_DIM), lambda b, i: (b, 0, n_kv_blk)),
                  pl.BlockSpec((1, Q_BLK, idx_w), lambda b, i: (b, i, 0)),
                  pl.BlockSpec((1, seq, LANES), lambda b, i: (b, 0, n_ik_blk)),
                  pl.BlockSpec(strip.shape, lambda b, i: (0, 0, 0))],
        out_specs=pl.BlockSpec((1, Q_BLK, A_Q), lambda b, i: (b, i, 0)),
        scratch_shapes=[pltpu.VMEM((Q_BLK, seq), I32),
                        pltpu.VMEM((Q_BLK, seq), F32),
                        pltpu.VMEM((Q_BLK, 1), I32),
                        pltpu.VMEM((rows, A_HEAD_DIM), BF16),
                        pltpu.VMEM((rows, KEY_CHUNK), F32),
                        pltpu.VMEM((rows, KEY_CHUNK), BF16),
                        pltpu.VMEM((rows, 2 * A_HEAD_DIM), F32),
                        pltpu.VMEM((rows, LANES), F32),
                        pltpu.VMEM((rows, LANES), F32),
                        pltpu.VMEM((seq, 2 * A_HEAD_DIM), BF16)],
        compiler_params=_params("parallel", "arbitrary"),
        name="dsa_attention",
    )(qkv, qkv, idx, idx, strip)


def dsa_bias_strip(rel_bias, seq):
    width = seq + KEY_CHUNK - Q_BLK
    by_dist = _bias_by_distance(rel_bias, jnp.arange(seq)).T
    return (_toeplitz(by_dist, Q_BLK, width, seq - Q_BLK) * LOG2E).astype(BF16)


def dsa_weights(w_in):
    d = w_in.shape[0]
    wq = w_in[:, :A_Q] * (A_HEAD_DIM ** -0.5 * LOG2E)
    wkv = w_in[:, A_Q:A_Q + 2 * A_HEAD_DIM]
    o = A_Q + 2 * A_HEAD_DIM
    n_qi = A_IDX_HEADS * A_IDX_DIM
    wqi = w_in[:, o:o + n_qi].reshape(d, A_IDX_HEADS, A_IDX_DIM)
    wqi = jnp.pad(wqi, ((0, 0), (0, 0), (0, LANES - A_IDX_DIM))).reshape(d, A_IDX_HEADS * LANES)
    wkw = jnp.pad(w_in[:, o + n_qi:], ((0, 0), (0, LANES - A_IDX_DIM - A_IDX_HEADS)))
    return (jnp.concatenate([wq, wkv], axis=1).astype(BF16),
            jnp.concatenate([wqi, wkw], axis=1).astype(BF16))


def dsa_mixer(hb, w_in, w_o, strip, h, g, b, bsz, seq):
    w_qkv, w_idx = dsa_weights(w_in)
    qkv = matmul(hb, w_qkv, BF16, tn=w_qkv.shape[1] // 2)
    idx = matmul(hb, w_idx, F32)
    o = dsa_attention(qkv.reshape(bsz, seq, -1), idx.reshape(bsz, seq, -1), strip, bsz, seq)
    return matmul_residual_ln(o.reshape(bsz * seq, A_Q), w_o.astype(BF16), h, g, b)


def _dilated_kernel(q_ref, kp_ref, kc_ref, vp_ref, vc_ref, bias_ref, o_ref, lse_ref):
    c = pl.program_id(1)
    col = lax.broadcasted_iota(I32, (B_N, 2 * B_N), 1)
    has_prev = (col >= B_N) | (c > 0)
    lane = lax.broadcasted_iota(I32, (B_N, LANES), 1)
    lse_tile = jnp.zeros((B_N, LANES), F32)
    pair = []
    for h in range(B_HEADS):
        hs = slice(h * B_HEAD_DIM, (h + 1) * B_HEAD_DIM)
        kk = jnp.concatenate([kp_ref[0, :, hs], kc_ref[0, :, hs]], axis=0)
        vv = jnp.concatenate([vp_ref[0, :, hs], vc_ref[0, :, hs]], axis=0)
        s = lax.dot_general(q_ref[0, :, hs], kk, (((1,), (1,)), ((), ())),
                            preferred_element_type=F32)
        s = jnp.where(has_prev, s + bias_ref[h], MASK_NEG)
        m = jnp.max(s, axis=-1, keepdims=True)
        p = jnp.exp(s - m)
        l = jnp.sum(p, axis=-1, keepdims=True)
        pair.append(jnp.dot(p.astype(BF16), vv, preferred_element_type=F32) / l)
        lse_tile = jnp.where(lane == h, m + jnp.log(l), lse_tile)
        if len(pair) == LANES // B_HEAD_DIM:
            o_ref[0, :, (h + 1) * B_HEAD_DIM - LANES:(h + 1) * B_HEAD_DIM] = (
                jnp.concatenate(pair, axis=-1).astype(BF16))
            pair = []
    lse_ref[0] = lse_tile


def dilated_group(proj, bias, n_seq, length):
    width = B_HEADS * B_HEAD_DIM
    pv = proj.reshape(n_seq, length, 3 * width)
    blk = (1, B_N, width)

    def spec(which, prev):
        if prev:
            return pl.BlockSpec(blk, lambda s, c: (s, jnp.maximum(c - 1, 0), which))
        return pl.BlockSpec(blk, lambda s, c: (s, c, which))

    return pl.pallas_call(
        _dilated_kernel,
        out_shape=(jax.ShapeDtypeStruct((n_seq, length, width), BF16),
                   jax.ShapeDtypeStruct((n_seq, length, LANES), F32)),
        grid=(n_seq, length // B_N),
        in_specs=[spec(0, False), spec(1, True), spec(1, False), spec(2, True), spec(2, False),
                  pl.BlockSpec(bias.shape, lambda s, c: (0, 0, 0))],
        out_specs=(pl.BlockSpec(blk, lambda s, c: (s, c, 0)),
                   pl.BlockSpec((1, B_N, LANES), lambda s, c: (s, c, 0))),
        compiler_params=_params("parallel", "parallel"),
        name="dilated_group",
    )(pv, pv, pv, pv, pv, bias)


def dilated_bias(rel_bias, dil):
    ii = jnp.arange(B_N)[:, None]
    jj = jnp.arange(2 * B_N)[None, :]
    delta = B_N + ii - jj
    band = (delta >= 0) & (delta <= B_N)
    by_delta = _bias_by_distance(rel_bias, jnp.arange(2 * B_N) * dil).T
    bias = _toeplitz(by_delta, B_N, 2 * B_N, B_N)
    return jnp.where(band[None], bias.astype(F32), MASK_NEG)


def _dilated_merge_kernel(dils, o0_ref, o1_ref, o2_ref, l0_ref, l1_ref, l2_ref, e_ref, w_ref,
                          h_ref, g_ref, b_ref, o_ref, ob_ref, *nat_sc):
    def natural(o_g, l_g, dil, scratch):
        if dil == 1:
            return o_g[0, 0].astype(F32), l_g[0, 0]
        o_sc, l_sc = scratch
        n_tiles, n = o_sc.shape[0], o_sc.shape[1] // dil
        for r in range(dil):
            for j in range(n_tiles):
                o_sc[j, pl.ds(r, n, stride=dil), :] = o_g[0, r, :, j * LANES:(j + 1) * LANES].astype(F32)
            l_sc[pl.ds(r, n, stride=dil), :] = l_g[0, r]
        return jnp.concatenate([o_sc[j] for j in range(n_tiles)], axis=-1), l_sc[...]

    nat = []
    for gi, (o_g, l_g) in enumerate(((o0_ref, l0_ref), (o1_ref, l1_ref), (o2_ref, l2_ref))):
        nat.append(natural(o_g, l_g, dils[gi], nat_sc[2 * gi:2 * gi + 2]))
    (v0, l0), (v1, l1), (v2, l2) = nat
    m = jnp.maximum(jnp.maximum(l0, l1), l2)
    e0, e1, e2 = jnp.exp(l0 - m), jnp.exp(l1 - m), jnp.exp(l2 - m)
    inv = 1.0 / (e0 + e1 + e2)
    e_mat = e_ref[...]

    def spread(wt):
        hi = wt.astype(BF16)
        lo = (wt - hi.astype(F32)).astype(BF16)
        return (jnp.dot(hi, e_mat, preferred_element_type=F32)
                + jnp.dot(lo, e_mat, preferred_element_type=F32))

    mix = spread(e0 * inv) * v0 + spread(e1 * inv) * v1 + spread(e2 * inv) * v2
    y = jnp.dot(mix.astype(BF16), w_ref[...], preferred_element_type=F32)
    out = _layer_norm_rows(DN_ALPHA * h_ref[...] + y, g_ref[...], b_ref[...])
    o_ref[...] = out
    ob_ref[...] = out.astype(BF16)


def dilated_merge(outs, lses, dils, w_o, h, g, b, bsz, seq, tm=512):
    m, d = h.shape
    width = B_HEADS * B_HEAD_DIM
    tm = min(tm, seq)
    n_j = seq // tm
    expand = (jnp.arange(LANES)[:, None] == (jnp.arange(width)[None, :] // B_HEAD_DIM)).astype(BF16)
    row = lambda bi, j: (bi * n_j + j, 0)
    fixed = lambda bi, j: (0, 0)
    grouped = lambda bi, j: (bi, 0, j, 0)
    o_specs = [pl.BlockSpec((1, dl, tm // dl, width), grouped) for dl in dils]
    l_specs = [pl.BlockSpec((1, dl, tm // dl, LANES), grouped) for dl in dils]
    scratch = []
    for dl in dils:
        if dl > 1:
            scratch += [pltpu.VMEM((width // LANES, tm, LANES), F32), pltpu.VMEM((tm, LANES), F32)]
        else:
            scratch += [pltpu.VMEM((8, LANES), F32), pltpu.VMEM((8, LANES), F32)]
    outs = [o.reshape(bsz, dl, seq // dl, width) for o, dl in zip(outs, dils)]
    lses = [l.reshape(bsz, dl, seq // dl, LANES) for l, dl in zip(lses, dils)]
    return pl.pallas_call(
        functools.partial(_dilated_merge_kernel, tuple(dils)),
        out_shape=(jax.ShapeDtypeStruct((m, d), F32), jax.ShapeDtypeStruct((m, d), BF16)),
        grid=(bsz, n_j),
        in_specs=o_specs + l_specs
                 + [pl.BlockSpec((LANES, width), fixed), pl.BlockSpec((width, d), fixed),
                    pl.BlockSpec((tm, d), row), pl.BlockSpec((1, d), fixed), pl.BlockSpec((1, d), fixed)],
        out_specs=(pl.BlockSpec((tm, d), row), pl.BlockSpec((tm, d), row)),
        scratch_shapes=scratch,
        compiler_params=_params("parallel", "parallel"),
        name="dilated_merge",
    )(*outs, *lses, expand, w_o.astype(BF16), h, g.reshape(1, d), b.reshape(1, d))


def dilated_mixer(hb_by_dil, w_in, w_o, rel_bias, h, g, b, bsz, seq):
    width = B_HEADS * B_HEAD_DIM
    outs, lses, dils = [], [], []
    for gi, (window, dil) in enumerate(B_GROUPS):
        assert window // dil == B_N and seq % window == 0
        w_g = w_in[:, gi * 3 * width:(gi + 1) * 3 * width]
        w_g = w_g.at[:, :width].multiply(B_HEAD_DIM ** -0.5)
        proj = matmul(hb_by_dil[dil], w_g.astype(BF16), BF16, tn=3 * width // 2)
        o, lse = dilated_group(proj, dilated_bias(rel_bias, dil), bsz * dil, seq // dil)
        outs.append(o)
        lses.append(lse)
        dils.append(dil)
    return dilated_merge(outs, lses, dils, w_o, h, g, b, bsz, seq)


C_PAD = 128


def _mla_prep_kernel(x_ref, win_ref, qn_ref, kvn_ref, wqa_ref, wqb_ref, wka_ref, wv_ref,
                     cos_ref, sin_ref, vone_ref, q_ref, k_ref, v_ref):
    c = jnp.dot(x_ref[...], win_ref[...], preferred_element_type=F32)
    cos, sin = cos_ref[...], sin_ref[...]

    def rms(v, gain):
        return (v * lax.rsqrt(jnp.mean(v * v, axis=-1, keepdims=True) + RMS_EPS) * gain).astype(BF16)

    nq = rms(c[:, :C_Q_RANK], qn_ref[...])
    nkv = rms(c[:, C_Q_RANK:C_Q_RANK + C_KV_RANK], kvn_ref[...])
    o = C_Q_RANK + C_KV_RANK
    k_rope = c[:, o:o + C_PAD] * cos + c[:, o + C_PAD:o + 2 * C_PAD] * sin
    qa = jnp.dot(nq, wqa_ref[...], preferred_element_type=F32)
    qb = jnp.dot(nq, wqb_ref[...], preferred_element_type=F32)
    kn = jnp.dot(nkv, wka_ref[...], preferred_element_type=F32)
    for h in range(C_HEADS):
        hs = slice(h * C_PAD, (h + 1) * C_PAD)
        q_ref[:, hs] = (qa[:, hs] * cos + qb[:, hs] * sin).astype(BF16)
        k_ref[:, hs] = (kn[:, hs] + k_rope).astype(BF16)
    v_ref[...] = (jnp.dot(nkv, wv_ref[...], preferred_element_type=F32) + vone_ref[...]).astype(BF16)


def _rot_half_cols(w):
    half = w.shape[-1] // 2
    return jnp.concatenate([-w[..., half:], w[..., :half]], axis=-1)


def mla_prep(hb, w_in, q_norm, kv_norm, w_q_up, w_kv_up, bsz, seq, tm=512):
    d = w_in.shape[0]
    scale = (C_NOPE + C_ROPE) ** -0.5 * LOG2E
    pad_r = C_PAD - C_NOPE - C_ROPE
    w_kr = w_in[:, C_Q_RANK + C_KV_RANK:]

    def rope_slot(w):
        return jnp.pad(w, ((0, 0), (C_NOPE, pad_r)))

    win = jnp.concatenate([w_in[:, :C_Q_RANK + C_KV_RANK], rope_slot(w_kr),
                           rope_slot(_rot_half_cols(w_kr))], axis=1).astype(BF16)
    wq = w_q_up.reshape(C_Q_RANK, C_HEADS, C_NOPE + C_ROPE) * scale
    wqa = jnp.pad(wq, ((0, 0), (0, 0), (0, pad_r))).reshape(C_Q_RANK, C_HEADS * C_PAD).astype(BF16)
    wqb = jnp.pad(_rot_half_cols(wq[..., C_NOPE:]), ((0, 0), (0, 0), (C_NOPE, pad_r)))
    wqb = wqb.reshape(C_Q_RANK, C_HEADS * C_PAD).astype(BF16)
    wkv = w_kv_up.reshape(C_KV_RANK, C_HEADS, C_NOPE + C_V)
    wka = jnp.pad(wkv[..., :C_NOPE], ((0, 0), (0, 0), (0, C_PAD - C_NOPE)))
    wka = wka.reshape(C_KV_RANK, C_HEADS * C_PAD).astype(BF16)
    wv = jnp.pad(wkv[..., C_NOPE:], ((0, 0), (0, 0), (0, LANES - C_V)))
    wv = wv.reshape(C_KV_RANK, C_HEADS * LANES).astype(BF16)
    v_ones = jnp.tile(jnp.concatenate([jnp.zeros((1, C_V), F32), jnp.ones((1, LANES - C_V), F32)], axis=1),
                      (1, C_HEADS))

    half = C_ROPE // 2
    inv = ROPE_BASE ** (-jnp.arange(half, dtype=F32) / half)
    ang = jnp.arange(seq, dtype=F32)[:, None] * inv[None, :]
    ones, zeros = jnp.ones((seq, C_NOPE), F32), jnp.zeros((seq, pad_r), F32)
    cos = jnp.concatenate([ones, jnp.cos(ang), jnp.cos(ang), zeros], axis=1)
    sin = jnp.concatenate([0 * ones, jnp.sin(ang), jnp.sin(ang), zeros], axis=1)

    m = bsz * seq
    tm = min(tm, seq)
    n_pos_blk = seq // tm
    row = lambda i: (i, 0)
    fixed = lambda i: (0, 0)
    pos = lambda i: (i % n_pos_blk, 0)
    full = lambda a: pl.BlockSpec(a.shape, fixed)
    qw, vw = C_HEADS * C_PAD, C_HEADS * LANES
    return pl.pallas_call(
        _mla_prep_kernel,
        out_shape=(jax.ShapeDtypeStruct((m, qw), BF16), jax.ShapeDtypeStruct((m, qw), BF16),
                   jax.ShapeDtypeStruct((m, vw), BF16)),
        grid=(m // tm,),
        in_specs=[pl.BlockSpec((tm, d), row), full(win),
                  pl.BlockSpec((1, C_Q_RANK), fixed), pl.BlockSpec((1, C_KV_RANK), fixed),
                  full(wqa), full(wqb), full(wka), full(wv),
                  pl.BlockSpec((tm, C_PAD), pos), pl.BlockSpec((tm, C_PAD), pos), full(v_ones)],
        out_specs=(pl.BlockSpec((tm, qw), row), pl.BlockSpec((tm, qw), row),
                   pl.BlockSpec((tm, vw), row)),
        compiler_params=_params("parallel"),
        name="mla_prep",
    )(hb, win, q_norm.reshape(1, -1), kv_norm.reshape(1, -1), wqa, wqb, wka, wv, cos, sin, v_ones)


MLA_TQ = 512


MLA_SUB = 32


def _mla_attn_kernel(q_ref, k_ref, v_ref, o_ref, s_sc, p_sc, m_sc, a_sc, acc_sc):
    i = pl.program_id(2)
    tq = MLA_TQ
    n_tile = tq // LANES
    row_in_sub = lax.broadcasted_iota(I32, (MLA_SUB, LANES), 0)
    col_in_tile = lax.broadcasted_iota(I32, (MLA_SUB, LANES), 1)
    outs = []
    for hh in range(2):
        q = q_ref[0, :, hh * C_PAD:(hh + 1) * C_PAD]
        m_sc[...] = jnp.full(m_sc.shape, MASK_NEG, F32)
        acc_sc[...] = jnp.zeros(acc_sc.shape, F32)

        def step(c, diag):
            k0 = pl.multiple_of(c * tq, tq)
            kc = k_ref[0, pl.ds(k0, tq), hh * C_PAD:(hh + 1) * C_PAD]
            s_sc[...] = lax.dot_general(q, kc, (((1,), (1,)), ((), ())), preferred_element_type=F32)
            for rb in range(tq // MLA_SUB):
                r = slice(rb * MLA_SUB, (rb + 1) * MLA_SUB)
                t = [s_sc[r, jt * LANES:(jt + 1) * LANES] for jt in range(n_tile)]
                if diag:
                    t = [jnp.where(col_in_tile + jt * LANES <= row_in_sub + rb * MLA_SUB, t[jt], MASK_NEG)
                         for jt in range(n_tile)]
                m_old = m_sc[r, :]
                m_new = jnp.maximum(m_old, jnp.max(functools.reduce(jnp.maximum, t), axis=-1, keepdims=True))
                a_sc[r, :] = jnp.exp2(m_old - m_new)
                m_sc[r, :] = m_new
                for jt in range(n_tile):
                    p_sc[r, jt * LANES:(jt + 1) * LANES] = jnp.exp2(t[jt] - m_new).astype(BF16)
            pv = jnp.dot(p_sc[...], v_ref[0, pl.ds(k0, tq), hh * LANES:(hh + 1) * LANES],
                         preferred_element_type=F32)
            acc_sc[...] = acc_sc[...] * a_sc[...] + pv

        def body(c, carry):
            step(c, False)
            return carry

        lax.fori_loop(0, i, body, 0)
        step(i, True)
        outs.append(acc_sc[:, :C_V] / acc_sc[:, C_V:])
    o_ref[0] = jnp.concatenate(outs, axis=-1).astype(BF16)


def mla_attention(q, k, v, bsz, seq):
    qw, vw = C_HEADS * C_PAD, C_HEADS * C_V
    return pl.pallas_call(
        _mla_attn_kernel,
        out_shape=jax.ShapeDtypeStruct((bsz, seq, vw), BF16),
        grid=(bsz, C_HEADS // 2, seq // MLA_TQ),
        in_specs=[pl.BlockSpec((1, MLA_TQ, 2 * C_PAD), lambda b, hp, i: (b, i, hp)),
                  pl.BlockSpec((1, seq, 2 * C_PAD), lambda b, hp, i: (b, 0, hp)),
                  pl.BlockSpec((1, seq, 2 * LANES), lambda b, hp, i: (b, 0, hp))],
        out_specs=pl.BlockSpec((1, MLA_TQ, 2 * C_V), lambda b, hp, i: (b, i, hp)),
        scratch_shapes=[pltpu.VMEM((MLA_TQ, MLA_TQ), F32), pltpu.VMEM((MLA_TQ, MLA_TQ), BF16),
                        pltpu.VMEM((MLA_TQ, LANES), F32), pltpu.VMEM((MLA_TQ, LANES), F32),
                        pltpu.VMEM((MLA_TQ, LANES), F32)],
        compiler_params=_params("parallel", "parallel", "arbitrary"),
        name="mla_attention",
    )(q.reshape(bsz, seq, qw), k.reshape(bsz, seq, qw), v.reshape(bsz, seq, C_HEADS * LANES))


def mla_mixer(hb, w_in, q_norm, kv_norm, w_q_up, w_kv_up, w_o, h, g, b, bsz, seq):
    q, k, v = mla_prep(hb, w_in, q_norm, kv_norm, w_q_up, w_kv_up, bsz, seq)
    o = mla_attention(q, k, v, bsz, seq)
    return matmul_residual_ln(o.reshape(bsz * seq, -1), w_o.astype(BF16), h, g, b)


def _router_kernel(h_ref, wr_ref, eb_ref, eidx_ref, gate_ref):
    tm = h_ref.shape[0]
    per = N_EXPERTS // N_GROUPS
    logits = lax.dot_general(wr_ref[...], h_ref[...], (((1,), (1,)), ((), ())),
                             precision=lax.Precision.HIGHEST, preferred_element_type=F32)
    scores = 1.0 / (1.0 + jnp.exp(-logits))
    s3 = scores.reshape(N_GROUPS, per, tm)
    c3 = (scores + eb_ref[...]).reshape(N_GROUPS, per, tm)
    neg_inf = -jnp.inf
    j_idx = lax.broadcasted_iota(I32, (N_GROUPS, per, tm), 1).astype(F32)
    g_idx = lax.broadcasted_iota(I32, (N_GROUPS, per, tm), 0).astype(F32)
    flat = g_idx * per + j_idx

    m1 = jnp.max(c3, axis=1, keepdims=True)
    first = jnp.min(jnp.where(c3 == m1, j_idx, per), axis=1, keepdims=True)
    m2 = jnp.max(jnp.where(j_idx == first, neg_inf, c3), axis=1, keepdims=True)
    gs = m1 + m2

    gi = lax.broadcasted_iota(I32, (N_GROUPS, 1, tm), 0).astype(F32)
    keep = jnp.zeros((N_GROUPS, 1, tm), jnp.bool_)
    cur = gs
    for _ in range(TOPK_GROUPS):
        mx = jnp.max(cur, axis=0, keepdims=True)
        pick = gi == jnp.min(jnp.where(cur == mx, gi, N_GROUPS), axis=0, keepdims=True)
        keep = keep | pick
        cur = jnp.where(pick, neg_inf, cur)
    cur = jnp.where(keep, c3, neg_inf)

    idxs, gates = [], []
    for _ in range(TOP_K):
        mx = jnp.max(jnp.max(cur, axis=1, keepdims=True), axis=0, keepdims=True)
        cand = jnp.where(cur == mx, flat, N_EXPERTS)
        fi = jnp.min(jnp.min(cand, axis=1, keepdims=True), axis=0, keepdims=True)
        pick = flat == fi
        gates.append(jnp.sum(jnp.sum(jnp.where(pick, s3, 0.0), axis=1, keepdims=True), axis=0))
        idxs.append(fi[0])
        cur = jnp.where(pick, neg_inf, cur)
    gate = jnp.concatenate(gates, axis=0)
    gate = gate / jnp.sum(gate, axis=0, keepdims=True) * ROUTED_SCALE
    eidx_ref[...] = jnp.concatenate(idxs, axis=0).astype(I32)
    gate_ref[...] = gate


def moe_router(h, w_router, e_bias, tm=512):
    m, d = h.shape
    tm = min(tm, m)
    return pl.pallas_call(
        _router_kernel,
        out_shape=(jax.ShapeDtypeStruct((TOP_K, m), I32), jax.ShapeDtypeStruct((TOP_K, m), F32)),
        grid=(m // tm,),
        in_specs=[pl.BlockSpec((tm, d), lambda i: (i, 0)),
                  pl.BlockSpec((N_EXPERTS, d), lambda i: (0, 0)),
                  pl.BlockSpec((N_EXPERTS, 1), lambda i: (0, 0))],
        out_specs=(pl.BlockSpec((TOP_K, tm), lambda i: (0, i)),
                   pl.BlockSpec((TOP_K, tm), lambda i: (0, i))),
        compiler_params=_params("parallel"),
        name="moe_router",
    )(h, w_router.T, e_bias.reshape(N_EXPERTS, 1))


def _silu(x):
    return x / (1.0 + jnp.exp(-x))


RANK_TM = 1024


def _rank_kernel(eidx_ref, tri_ref, dest_ref, cnt_ref, start_ref, cnt_sc, run_sc):
    p, i = pl.program_id(0), pl.program_id(1)
    tm = eidx_ref.shape[1]
    e = eidx_ref[...]
    ex = lax.broadcasted_iota(I32, (N_EXPERTS, tm), 0)
    onehot = jnp.zeros((N_EXPERTS, tm), F32)
    for k in range(TOP_K):
        onehot = onehot + jnp.where(e[k:k + 1, :] == ex, 1.0, 0.0)
    tile_cnt = jnp.sum(onehot, axis=-1, keepdims=True)

    @pl.when((p == 0) & (i == 0))
    def _():
        cnt_sc[...] = jnp.zeros(cnt_sc.shape, F32)

    @pl.when(p == 0)
    def _():
        cnt_sc[...] += tile_cnt

    @pl.when((p == 1) & (i == 0))
    def _():
        cnt = cnt_sc[...]
        padded = jnp.floor((cnt + (MOE_BLK - 1)) * (1.0 / MOE_BLK)) * MOE_BLK
        below = (lax.broadcasted_iota(I32, (N_EXPERTS, N_EXPERTS), 1)
                 < lax.broadcasted_iota(I32, (N_EXPERTS, N_EXPERTS), 0)).astype(F32)
        start = jnp.dot(below, jnp.broadcast_to(padded, (N_EXPERTS, LANES)),
                        precision=lax.Precision.HIGHEST, preferred_element_type=F32)
        run_sc[...] = start[:, :1]
        cnt_ref[...] = jnp.broadcast_to(cnt, (N_EXPERTS, LANES))
        start_ref[...] = start

    @pl.when(p == 1)
    def _():
        before = jnp.dot(onehot.astype(BF16), tri_ref[...], preferred_element_type=F32) + run_sc[...]
        rows = [jnp.sum(jnp.where(e[k:k + 1, :] == ex, before, 0.0), axis=0, keepdims=True)
                for k in range(TOP_K)]
        dest_ref[...] = jnp.concatenate(rows, axis=0).astype(I32)
        run_sc[...] += tile_cnt


def moe_rank(eidx):
    n_tok = eidx.shape[1]
    tm = min(RANK_TM, n_tok)
    tri = (jnp.arange(tm)[:, None] < jnp.arange(tm)[None, :]).astype(BF16)
    stat = jax.ShapeDtypeStruct((N_EXPERTS, LANES), F32)
    dest, cnt, start = pl.pallas_call(
        _rank_kernel,
        out_shape=(jax.ShapeDtypeStruct((TOP_K, n_tok), I32), stat, stat),
        grid=(2, n_tok // tm),
        in_specs=[pl.BlockSpec((TOP_K, tm), lambda p, i: (0, i)),
                  pl.BlockSpec((tm, tm), lambda p, i: (0, 0))],
        out_specs=(pl.BlockSpec((TOP_K, tm), lambda p, i: (0, i * p)),
                   pl.BlockSpec((N_EXPERTS, LANES), lambda p, i: (0, 0)),
                   pl.BlockSpec((N_EXPERTS, LANES), lambda p, i: (0, 0))),
        scratch_shapes=[pltpu.VMEM((N_EXPERTS, 1), F32), pltpu.VMEM((N_EXPERTS, 1), F32)],
        compiler_params=_params("arbitrary", "arbitrary"),
        name="moe_rank",
    )(eidx, tri)
    return dest, cnt[:, 0].astype(I32), start[:, 0].astype(I32)


def _experts_kernel(blk_e_ref, x_ref, wg_ref, wu_ref, wd_ref, o_ref, wg_sc, wu_sc, wd_sc):
    i = pl.program_id(0)

    @pl.when((i == 0) | (blk_e_ref[i] != blk_e_ref[jnp.maximum(i - 1, 0)]))
    def _():
        wg_sc[...] = wg_ref[0].astype(BF16)
        wu_sc[...] = wu_ref[0].astype(BF16)
        wd_sc[...] = wd_ref[0].astype(BF16)

    x = x_ref[...]
    gate = jnp.dot(x, wg_sc[...], preferred_element_type=F32)
    up = jnp.dot(x, wu_sc[...], preferred_element_type=F32)
    act = (_silu(gate) * up).astype(BF16)
    o_ref[...] = jnp.dot(act, wd_sc[...], preferred_element_type=F32).astype(o_ref.dtype)


def grouped_experts(x_sorted, blk_e, wg, wu, wd):
    n_slot, d = x_sorted.shape
    n_blk = n_slot // MOE_BLK
    by_expert = lambda i, be: (be[i], 0, 0)
    grid_spec = pltpu.PrefetchScalarGridSpec(
        num_scalar_prefetch=1,
        grid=(n_blk,),
        in_specs=[pl.BlockSpec((MOE_BLK, d), lambda i, be: (i, 0)),
                  pl.BlockSpec((1, d, D_EXPERT), by_expert),
                  pl.BlockSpec((1, d, D_EXPERT), by_expert),
                  pl.BlockSpec((1, D_EXPERT, d), by_expert)],
        out_specs=pl.BlockSpec((MOE_BLK, d), lambda i, be: (i, 0)),
        scratch_shapes=[pltpu.VMEM((d, D_EXPERT), BF16), pltpu.VMEM((d, D_EXPERT), BF16),
                        pltpu.VMEM((D_EXPERT, d), BF16)],
    )
    return pl.pallas_call(
        _experts_kernel,
        out_shape=jax.ShapeDtypeStruct((n_slot, d), BF16),
        grid_spec=grid_spec,
        compiler_params=_params("arbitrary"),
        name="grouped_experts",
    )(blk_e, x_sorted, wg, wu, wd)


def _moe_finish_kernel(dils, hb_ref, y_ref, gt_ref, wg_ref, wu_ref, wd_ref, h_ref, g_ref, b_ref,
                       o_ref, ob_ref, *rest):
    x = hb_ref[...]
    d = x.shape[1]
    act = (_silu(jnp.dot(x, wg_ref[...], preferred_element_type=F32))
           * jnp.dot(x, wu_ref[...], preferred_element_type=F32)).astype(BF16)
    ff = jnp.dot(act, wd_ref[...], preferred_element_type=F32)
    gt = gt_ref[...]
    for k in range(TOP_K):
        ff = ff + gt[:, k:k + 1] * y_ref[k].astype(F32)
    out = _layer_norm_rows(DN_ALPHA * h_ref[...] + ff, g_ref[...], b_ref[...])
    o_ref[...] = out
    ob_ref[...] = out.astype(BF16)
    if dils:
        perm_refs, nat_sc = rest[:-1], rest[-1]
        for j in range(d // LANES):
            nat_sc[j] = out[:, j * LANES:(j + 1) * LANES]
        for p_ref, dil in zip(perm_refs, dils):
            n = out.shape[0] // dil
            for r in range(dil):
                for j in range(d // LANES):
                    p_ref[0, r, :, j * LANES:(j + 1) * LANES] = (
                        nat_sc[j, pl.ds(r, n, stride=dil), :].astype(BF16))


def moe_finish(hb, y_tok, gate_tok, ws_g, ws_u, ws_d, h, g, b, seq, dils=(), tm=256):
    m, d = h.shape
    tm = min(tm, seq)
    n_j = seq // tm
    row = lambda i: (i, 0)
    fixed = lambda i: (0, 0)
    out_shape = [jax.ShapeDtypeStruct((m, d), F32), jax.ShapeDtypeStruct((m, d), BF16)]
    out_specs = [pl.BlockSpec((tm, d), row), pl.BlockSpec((tm, d), row)]
    for dil in dils:
        out_shape.append(jax.ShapeDtypeStruct((m // seq, dil, seq // dil, d), BF16))
        out_specs.append(pl.BlockSpec((1, dil, tm // dil, d), lambda i: (i // n_j, 0, i % n_j, 0)))
    res = pl.pallas_call(
        functools.partial(_moe_finish_kernel, tuple(dils)),
        out_shape=tuple(out_shape),
        grid=(m // tm,),
        in_specs=[pl.BlockSpec((tm, d), row), pl.BlockSpec((TOP_K, tm, d), lambda i: (0, i, 0)),
                  pl.BlockSpec((tm, TOP_K), row),
                  pl.BlockSpec(ws_g.shape, fixed), pl.BlockSpec(ws_u.shape, fixed),
                  pl.BlockSpec(ws_d.shape, fixed),
                  pl.BlockSpec((tm, d), row), pl.BlockSpec((1, d), fixed), pl.BlockSpec((1, d), fixed)],
        out_specs=tuple(out_specs),
        scratch_shapes=[pltpu.VMEM((d // LANES, tm, LANES), F32)] if dils else [],
        compiler_params=_params("parallel"),
        name="moe_finish",
    )(hb, y_tok, gate_tok, ws_g, ws_u, ws_d, h, g.reshape(1, d), b.reshape(1, d))
    return res[0], res[1], [r.reshape(m, d) for r in res[2:]]


SORT_E_SHIFT = 19


def moe_layer(h, hb, w_router, e_bias, wg, wu, wd, ws_g, ws_u, ws_d, g, b, seq, dils=()):
    n_tok, d = h.shape
    n_asg = n_tok * TOP_K
    n_pad = N_EXPERTS * MOE_BLK
    assert n_asg <= 1 << (SORT_E_SHIFT - 1) and n_pad <= 1 << (SORT_E_SHIFT - 1)
    eidx, gate = moe_router(h, w_router, e_bias)
    dest, counts, pad_start = moe_rank(eidx)
    padded = (counts + MOE_BLK - 1) // MOE_BLK * MOE_BLK
    pad_end = pad_start + padded
    n_blk = n_asg // MOE_BLK + N_EXPERTS
    blk_start = jnp.arange(n_blk, dtype=I32) * MOE_BLK
    blk_e = jnp.minimum(jnp.sum(pad_end[None, :] <= blk_start[:, None], axis=1), N_EXPERTS - 1)

    key_real = (eidx.T.reshape(-1) << SORT_E_SHIFT) | jnp.arange(n_asg, dtype=I32)
    i_pad = jnp.arange(n_pad, dtype=I32)
    e_pad = jnp.sum(jnp.cumsum(padded - counts)[None, :] <= i_pad[:, None], axis=1).astype(I32)
    key_pad = (e_pad << SORT_E_SHIFT) | (1 << (SORT_E_SHIFT - 1)) | i_pad
    low = jnp.sort(jnp.concatenate([key_real, key_pad])) & ((1 << SORT_E_SHIFT) - 1)
    slot_tok = jnp.where(low < (1 << (SORT_E_SHIFT - 1)), low // TOP_K,
                         jnp.arange(n_asg + n_pad, dtype=I32) % n_tok)

    x_sorted = hb.at[slot_tok].get(mode="promise_in_bounds")
    y = grouped_experts(x_sorted, blk_e.astype(I32), wg, wu, wd)
    y_k = y.at[dest.reshape(-1)].get(mode="promise_in_bounds").reshape(TOP_K, n_tok, d)
    return moe_finish(hb, y_k, gate.T, ws_g, ws_u, ws_d, h, g, b, seq, dils)


def kernel(x, rel_bias, a_w_in, a_w_o, b_w_in, b_w_o, c_w_in, c_q_norm, c_kv_norm, c_w_q_up,
           c_w_kv_up, c_w_o, ln_g, ln_b, moe_w_router, moe_bias, moe_w_gate, moe_w_up,
           moe_w_down, moe_ws_gate, moe_ws_up, moe_ws_down):
    bsz, seq, d = x.shape
    depth = ln_g.shape[0]
    h = x.reshape(bsz * seq, d)
    hb = h.astype(BF16)
    strip = dsa_bias_strip(rel_bias, seq)
    extra_dils = tuple(dil for _, dil in B_GROUPS if dil > 1)
    hb_perm = []
    for layer in range(depth):
        kind, slot = layer % N_MIXERS, layer // N_MIXERS
        g0, b0 = ln_g[layer, 0], ln_b[layer, 0]
        if kind == 0:
            h, hb = dsa_mixer(hb, a_w_in[slot], a_w_o[slot], strip, h, g0, b0, bsz, seq)
        elif kind == 1:
            hb_by_dil = {1: hb, **dict(zip(extra_dils, hb_perm))}
            h, hb = dilated_mixer(hb_by_dil, b_w_in[slot], b_w_o[slot], rel_bias, h, g0, b0, bsz, seq)
        else:
            h, hb = mla_mixer(hb, c_w_in[slot], c_q_norm[slot], c_kv_norm[slot], c_w_q_up[slot],
                              c_w_kv_up[slot], c_w_o[slot], h, g0, b0, bsz, seq)
        next_dilated = layer + 1 < depth and (layer + 1) % N_MIXERS == 1
        h, hb, hb_perm = moe_layer(h, hb, moe_w_router[layer], moe_bias[layer],
                                   moe_w_gate[layer], moe_w_up[layer],
                                   moe_w_down[layer], moe_ws_gate[layer].astype(BF16),
                                   moe_ws_up[layer].astype(BF16), moe_ws_down[layer].astype(BF16),
                                   ln_g[layer, 1], ln_b[layer, 1], seq,
                                   extra_dils if next_dilated else ())
    return h.reshape(bsz, seq, d)
```

```python
import functools
import math

import jax
import jax.numpy as jnp
from jax import lax
from jax.experimental import pallas as pl
from jax.experimental.pallas import tpu as pltpu

F32 = jnp.float32
BF16 = jnp.bfloat16
I32 = jnp.int32

LANES = 128
VMEM_LIMIT_BYTES = 56 * 1024 * 1024

D_MODEL = 1024
DEPTH = 4
N_MIXERS = 3
NORM_EPS = 1e-5
RMS_EPS = 1e-6
REL_BUCKETS = 32
REL_MAX_DIST = 2048
A_HEADS = 16
A_HEAD_DIM = 128
A_IDX_HEADS = 8
A_IDX_DIM = 64
A_TOPK_MAX = 256
A_Q = A_HEADS * A_HEAD_DIM
B_GROUPS = ((128, 1), (512, 4), (2048, 16))
B_HEADS = 16
B_HEAD_DIM = 64
B_N = 128
C_HEADS = 16
C_Q_RANK = 256
C_KV_RANK = 128
C_NOPE = 64
C_ROPE = 32
C_V = 64
ROPE_BASE = 10000.0
N_EXPERTS = 64
TOP_K = 8
N_GROUPS = 8
TOPK_GROUPS = 4
D_EXPERT = 256
ROUTED_SCALE = 2.5
DN_ALPHA = (2 * DEPTH) ** 0.25

Q_BLK = 128
KEY_CHUNK = 512
MOE_BLK = 512
MASK_NEG = -1e30
INT_MIN = -(2 ** 31)


def _params(*sem):
    return pltpu.CompilerParams(dimension_semantics=sem, vmem_limit_bytes=VMEM_LIMIT_BYTES)


def _mm_kernel(x_ref, w_ref, o_ref):
    o_ref[...] = jnp.dot(x_ref[...], w_ref[...], preferred_element_type=F32).astype(o_ref.dtype)


def matmul(x, w, out_dtype, tm=1024, tn=None):
    m, k = x.shape
    n = w.shape[1]
    tn = n if tn is None else tn
    tm = min(tm, m)
    return pl.pallas_call(
        _mm_kernel,
        out_shape=jax.ShapeDtypeStruct((m, n), out_dtype),
        grid=(n // tn, m // tm),
        in_specs=[pl.BlockSpec((tm, k), lambda j, i: (i, 0)),
                  pl.BlockSpec((k, tn), lambda j, i: (0, j))],
        out_specs=pl.BlockSpec((tm, tn), lambda j, i: (i, j)),
        compiler_params=_params("parallel", "parallel"),
        name="matmul",
    )(x, w)


def _layer_norm_rows(z, g, b):
    mu = jnp.mean(z, axis=-1, keepdims=True)
    zc = z - mu
    var = jnp.mean(zc * zc, axis=-1, keepdims=True)
    return zc * lax.rsqrt(var + NORM_EPS) * g + b


def _mm_ln_kernel(x_ref, w_ref, h_ref, g_ref, b_ref, o_ref, ob_ref):
    y = jnp.dot(x_ref[...], w_ref[...], preferred_element_type=F32)
    out = _layer_norm_rows(DN_ALPHA * h_ref[...] + y, g_ref[...], b_ref[...])
    o_ref[...] = out
    ob_ref[...] = out.astype(BF16)


def matmul_residual_ln(x, w, h, g, b, tm=512):
    m, k = x.shape
    d = w.shape[1]
    tm = min(tm, m)
    row = lambda i: (i, 0)
    fixed = lambda i: (0, 0)
    return pl.pallas_call(
        _mm_ln_kernel,
        out_shape=(jax.ShapeDtypeStruct((m, d), F32), jax.ShapeDtypeStruct((m, d), BF16)),
        grid=(m // tm,),
        in_specs=[pl.BlockSpec((tm, k), row), pl.BlockSpec((k, d), fixed),
                  pl.BlockSpec((tm, d), row), pl.BlockSpec((1, d), fixed),
                  pl.BlockSpec((1, d), fixed)],
        out_specs=(pl.BlockSpec((tm, d), row), pl.BlockSpec((tm, d), row)),
        compiler_params=_params("parallel"),
        name="matmul_residual_ln",
    )(x, w, h, g.reshape(1, d), b.reshape(1, d))


def _t5_bucket(dist):
    exact = REL_BUCKETS // 2
    d_f = jnp.maximum(dist, 1).astype(F32)
    large = exact + (jnp.log(d_f / exact) / math.log(REL_MAX_DIST / exact)
                     * (REL_BUCKETS - exact)).astype(I32)
    return jnp.where(dist < exact, dist, jnp.minimum(large, REL_BUCKETS - 1))


def _bias_by_distance(rel_bias, dist):
    return rel_bias[_t5_bucket(dist)]


def _toeplitz(f, rows, cols, off):
    length = f.shape[-1]
    period = rows + cols - 1
    u = f[:, jnp.clip(off + rows - 1 - jnp.arange(period), 0, length - 1)]
    skew = jnp.tile(u, (1, rows + 1))[:, :rows * (period + 1)].reshape(-1, rows, period + 1)
    return skew[:, ::-1, :cols]


DSA_SUB = 32
LOG2E = math.log2(math.e)


def _dsa_select(k_sel, n_keys, t0, iq_ref, ik_ref, key_sc, mask_sc, cut_sc):
    rows = lax.broadcasted_iota(I32, (Q_BLK, n_keys), 0) + t0
    cols = lax.broadcasted_iota(I32, (Q_BLK, n_keys), 1)
    valid = cols <= rows

    ik = ik_ref[0, :n_keys, :].astype(BF16)
    w_all = iq_ref[0, :, A_IDX_HEADS * LANES:] * ((A_IDX_DIM * A_IDX_HEADS) ** -0.5)
    score = jnp.zeros((Q_BLK, n_keys), F32)
    for h in range(A_IDX_HEADS):
        qh = iq_ref[0, :, h * LANES:(h + 1) * LANES].astype(BF16)
        rel = lax.dot_general(qh, ik, (((1,), (1,)), ((), ())), preferred_element_type=F32)
        score = score + w_all[:, A_IDX_DIM + h:A_IDX_DIM + h + 1] * jnp.maximum(rel, 0.0)
    score = jnp.where(score == 0.0, 0.0, score)

    bits = pltpu.bitcast(score, I32)
    key_sc[:, :n_keys] = jnp.where(valid, bits ^ ((bits >> 31) & 0x7FFFFFFF), INT_MIN)

    def search(it, ans_u):
        cand_u = ans_u | lax.shift_left(jnp.int32(1), 31 - it)
        cand_s = cand_u ^ INT_MIN
        cnt = jnp.sum(jnp.where(key_sc[:, :n_keys] >= cand_s, 1.0, 0.0), axis=-1, keepdims=True)
        return jnp.where(cnt >= k_sel, cand_u, ans_u)

    thr = lax.fori_loop(0, 32, search, jnp.zeros((Q_BLK, 1), I32)) ^ INT_MIN

    key = key_sc[:, :n_keys]
    gt = key > thr
    eq = key == thr
    need = k_sel - jnp.sum(jnp.where(gt, 1.0, 0.0), axis=-1, keepdims=True)
    n_eq = jnp.sum(jnp.where(eq, 1.0, 0.0), axis=-1, keepdims=True)
    cut_sc[...] = jnp.full((Q_BLK, 1), n_keys, I32)
    surplus = jnp.where((n_eq > need) & (thr != INT_MIN), 1.0, 0.0)

    @pl.when(jnp.max(surplus) > 0.0)
    def _():
        def tie_search(it, ans):
            cand = ans | lax.shift_left(jnp.int32(1), (n_keys.bit_length() - 1) - it)
            hit = (key_sc[:, :n_keys] == thr) & (cols < cand)
            cnt = jnp.sum(jnp.where(hit, 1.0, 0.0), axis=-1, keepdims=True)
            return jnp.where(cnt < need, cand, ans)
        cut_sc[...] = lax.fori_loop(0, n_keys.bit_length(), tie_search, jnp.zeros((Q_BLK, 1), I32))

    selected = valid & (gt | (eq & (cols <= cut_sc[...])))
    mask_sc[:, :n_keys] = jnp.where(selected, 0.0, MASK_NEG)


def _dsa_kernel(k_sel, seq, q_ref, kv_ref, iq_ref, ik_ref, strip_ref, o_ref,
                key_sc, mask_sc, cut_sc, qs_sc, s_sc, p_sc, acc_sc, m_sc, a_sc, ve_sc):
    i = pl.program_id(1)
    t0 = i * Q_BLK
    blk_per_chunk = KEY_CHUNK // Q_BLK

    @pl.when(i == 0)
    def _():
        ve_sc[:, :A_HEAD_DIM] = kv_ref[0, :, A_HEAD_DIM:]
        ve_sc[:, A_HEAD_DIM:] = jnp.ones((seq, A_HEAD_DIM), BF16)

    for j in range(seq // KEY_CHUNK):
        @pl.when(i // blk_per_chunk == j)
        def _(j=j):
            _dsa_select(k_sel, (j + 1) * KEY_CHUNK, t0, iq_ref, ik_ref, key_sc, mask_sc, cut_sc)

    for h in range(A_HEADS):
        qs_sc[h * Q_BLK:(h + 1) * Q_BLK, :] = q_ref[0, :, h * A_HEAD_DIM:(h + 1) * A_HEAD_DIM]
    m_sc[...] = jnp.full(m_sc.shape, MASK_NEG, F32)
    acc_sc[...] = jnp.zeros(acc_sc.shape, F32)
    n_strip_blk = seq // Q_BLK - 1
    n_tile = KEY_CHUNK // LANES

    def chunk(c, carry):
        k0 = pl.multiple_of(c * KEY_CHUNK, KEY_CHUNK)
        kc = kv_ref[0, pl.ds(k0, KEY_CHUNK), :A_HEAD_DIM]
        s_sc[...] = lax.dot_general(qs_sc[...], kc, (((1,), (1,)), ((), ())),
                                    preferred_element_type=F32)
        w0 = (c * blk_per_chunk - i + n_strip_blk) * Q_BLK
        for h in range(A_HEADS):
            for rb in range(Q_BLK // DSA_SUB):
                qr = slice(rb * DSA_SUB, (rb + 1) * DSA_SUB)
                r = slice(h * Q_BLK + rb * DSA_SUB, h * Q_BLK + (rb + 1) * DSA_SUB)
                t = []
                for jt in range(n_tile):
                    ws = pl.ds(pl.multiple_of(w0 + jt * LANES, LANES), LANES)
                    ks = pl.ds(pl.multiple_of(k0 + jt * LANES, LANES), LANES)
                    t.append(s_sc[r, jt * LANES:(jt + 1) * LANES]
                             + strip_ref[h, qr, ws].astype(F32) + mask_sc[qr, ks])
                mx = functools.reduce(jnp.maximum, t)
                m_old = m_sc[r, :]
                m_new = jnp.maximum(m_old, jnp.max(mx, axis=-1, keepdims=True))
                a_sc[r, :] = jnp.exp2(m_old - m_new)
                m_sc[r, :] = m_new
                for jt in range(n_tile):
                    p_sc[r, jt * LANES:(jt + 1) * LANES] = jnp.exp2(t[jt] - m_new).astype(BF16)
        pv = jnp.dot(p_sc[...], ve_sc[pl.ds(k0, KEY_CHUNK), :], preferred_element_type=F32)
        for half in range(2):
            hs = slice(half * A_HEAD_DIM, (half + 1) * A_HEAD_DIM)
            acc_sc[:, hs] = acc_sc[:, hs] * a_sc[...] + pv[:, hs]
        return carry

    lax.fori_loop(0, (t0 + Q_BLK + KEY_CHUNK - 1) // KEY_CHUNK, chunk, 0)
    for h in range(A_HEADS):
        r = slice(h * Q_BLK, (h + 1) * Q_BLK)
        o_ref[0, :, h * A_HEAD_DIM:(h + 1) * A_HEAD_DIM] = (
            acc_sc[r, :A_HEAD_DIM] / acc_sc[r, A_HEAD_DIM:]).astype(BF16)


def dsa_attention(qkv, idx, strip, bsz, seq):
    assert seq % KEY_CHUNK == 0
    k_sel = min(A_TOPK_MAX, seq // 4)
    rows = A_HEADS * Q_BLK
    n_kv_blk = A_Q // (2 * A_HEAD_DIM)
    n_ik_blk = A_IDX_HEADS
    idx_w = (A_IDX_HEADS + 1) * LANES
    return pl.pallas_call(
        functools.partial(_dsa_kernel, k_sel, seq),
        out_shape=jax.ShapeDtypeStruct((bsz, seq, A_Q), BF16),
        grid=(bsz, seq // Q_BLK),
        in_specs=[pl.BlockSpec((1, Q_BLK, A_Q), lambda b, i: (b, i, 0)),
                  pl.BlockSpec((1, seq, 2 * A_HEAD_DIM), lambda b, i: (b, 0, n_kv_blk)),
                  pl.BlockSpec((1, Q_BLK, idx_w), lambda b, i: (b, i, 0)),
                  pl.BlockSpec((1, seq, LANES), lambda b, i: (b, 0, n_ik_blk)),
                  pl.BlockSpec(strip.shape, lambda b, i: (0, 0, 0))],
        out_specs=pl.BlockSpec((1, Q_BLK, A_Q), lambda b, i: (b, i, 0)),
        scratch_shapes=[pltpu.VMEM((Q_BLK, seq), I32),
                        pltpu.VMEM((Q_BLK, seq), F32),
                        pltpu.VMEM((Q_BLK, 1), I32),
                        pltpu.VMEM((rows, A_HEAD_DIM), BF16),
                        pltpu.VMEM((rows, KEY_CHUNK), F32),
                        pltpu.VMEM((rows, KEY_CHUNK), BF16),
                        pltpu.VMEM((rows, 2 * A_HEAD_DIM), F32),
                        pltpu.VMEM((rows, LANES), F32),
                        pltpu.VMEM((rows, LANES), F32),
                        pltpu.VMEM((seq, 2 * A_HEAD_DIM), BF16)],
        compiler_params=_params("parallel", "arbitrary"),
        name="dsa_attention",
    )(qkv, qkv, idx, idx, strip)


def dsa_bias_strip(rel_bias, seq):
    width = seq + KEY_CHUNK - Q_BLK
    by_dist = _bias_by_distance(rel_bias, jnp.arange(seq)).T
    return (_toeplitz(by_dist, Q_BLK, width, seq - Q_BLK) * LOG2E).astype(BF16)


def dsa_weights(w_in):
    d = w_in.shape[0]
    wq = w_in[:, :A_Q] * (A_HEAD_DIM ** -0.5 * LOG2E)
    wkv = w_in[:, A_Q:A_Q + 2 * A_HEAD_DIM]
    o = A_Q + 2 * A_HEAD_DIM
    n_qi = A_IDX_HEADS * A_IDX_DIM
    wqi = w_in[:, o:o + n_qi].reshape(d, A_IDX_HEADS, A_IDX_DIM)
    wqi = jnp.pad(wqi, ((0, 0), (0, 0), (0, LANES - A_IDX_DIM))).reshape(d, A_IDX_HEADS * LANES)
    wkw = jnp.pad(w_in[:, o + n_qi:], ((0, 0), (0, LANES - A_IDX_DIM - A_IDX_HEADS)))
    return (jnp.concatenate([wq, wkv], axis=1).astype(BF16),
            jnp.concatenate([wqi, wkw], axis=1).astype(BF16))


def dsa_mixer(hb, w_in, w_o, strip, h, g, b, bsz, seq):
    w_qkv, w_idx = dsa_weights(w_in)
    qkv = matmul(hb, w_qkv, BF16, tn=w_qkv.shape[1] // 2)
    idx = matmul(hb, w_idx, F32)
    o = dsa_attention(qkv.reshape(bsz, seq, -1), idx.reshape(bsz, seq, -1), strip, bsz, seq)
    return matmul_residual_ln(o.reshape(bsz * seq, A_Q), w_o.astype(BF16), h, g, b)


def _dilated_kernel(q_ref, kp_ref, kc_ref, vp_ref, vc_ref, bias_ref, o_ref, lse_ref):
    c = pl.program_id(1)
    col = lax.broadcasted_iota(I32, (B_N, 2 * B_N), 1)
    has_prev = (col >= B_N) | (c > 0)
    lane = lax.broadcasted_iota(I32, (B_N, LANES), 1)
    lse_tile = jnp.zeros((B_N, LANES), F32)
    pair = []
    for h in range(B_HEADS):
        hs = slice(h * B_HEAD_DIM, (h + 1) * B_HEAD_DIM)
        kk = jnp.concatenate([kp_ref[0, :, hs], kc_ref[0, :, hs]], axis=0)
        vv = jnp.concatenate([vp_ref[0, :, hs], vc_ref[0, :, hs]], axis=0)
        s = lax.dot_general(q_ref[0, :, hs], kk, (((1,), (1,)), ((), ())),
                            preferred_element_type=F32)
        s = jnp.where(has_prev, s + bias_ref[h], MASK_NEG)
        m = jnp.max(s, axis=-1, keepdims=True)
        p = jnp.exp(s - m)
        l = jnp.sum(p, axis=-1, keepdims=True)
        pair.append(jnp.dot(p.astype(BF16), vv, preferred_element_type=F32) / l)
        lse_tile = jnp.where(lane == h, m + jnp.log(l), lse_tile)
        if len(pair) == LANES // B_HEAD_DIM:
            o_ref[0, :, (h + 1) * B_HEAD_DIM - LANES:(h + 1) * B_HEAD_DIM] = (
                jnp.concatenate(pair, axis=-1).astype(BF16))
            pair = []
    lse_ref[0] = lse_tile


def dilated_group(proj, bias, n_seq, length):
    width = B_HEADS * B_HEAD_DIM
    pv = proj.reshape(n_seq, length, 3 * width)
    blk = (1, B_N, width)

    def spec(which, prev):
        if prev:
            return pl.BlockSpec(blk, lambda s, c: (s, jnp.maximum(c - 1, 0), which))
        return pl.BlockSpec(blk, lambda s, c: (s, c, which))

    return pl.pallas_call(
        _dilated_kernel,
        out_shape=(jax.ShapeDtypeStruct((n_seq, length, width), BF16),
                   jax.ShapeDtypeStruct((n_seq, length, LANES), F32)),
        grid=(n_seq, length // B_N),
        in_specs=[spec(0, False), spec(1, True), spec(1, False), spec(2, True), spec(2, False),
                  pl.BlockSpec(bias.shape, lambda s, c: (0, 0, 0))],
        out_specs=(pl.BlockSpec(blk, lambda s, c: (s, c, 0)),
                   pl.BlockSpec((1, B_N, LANES), lambda s, c: (s, c, 0))),
        compiler_params=_params("parallel", "parallel"),
        name="dilated_group",
    )(pv, pv, pv, pv, pv, bias)


def dilated_bias(rel_bias, dil):
    ii = jnp.arange(B_N)[:, None]
    jj = jnp.arange(2 * B_N)[None, :]
    delta = B_N + ii - jj
    band = (delta >= 0) & (delta <= B_N)
    by_delta = _bias_by_distance(rel_bias, jnp.arange(2 * B_N) * dil).T
    bias = _toeplitz(by_delta, B_N, 2 * B_N, B_N)
    return jnp.where(band[None], bias.astype(F32), MASK_NEG)


def _dilated_merge_kernel(dils, o0_ref, o1_ref, o2_ref, l0_ref, l1_ref, l2_ref, e_ref, w_ref,
                          h_ref, g_ref, b_ref, o_ref, ob_ref, *nat_sc):
    def natural(o_g, l_g, dil, scratch):
        if dil == 1:
            return o_g[0, 0].astype(F32), l_g[0, 0]
        o_sc, l_sc = scratch
        n_tiles, n = o_sc.shape[0], o_sc.shape[1] // dil
        for r in range(dil):
            for j in range(n_tiles):
                o_sc[j, pl.ds(r, n, stride=dil), :] = o_g[0, r, :, j * LANES:(j + 1) * LANES].astype(F32)
            l_sc[pl.ds(r, n, stride=dil), :] = l_g[0, r]
        return jnp.concatenate([o_sc[j] for j in range(n_tiles)], axis=-1), l_sc[...]

    nat = []
    for gi, (o_g, l_g) in enumerate(((o0_ref, l0_ref), (o1_ref, l1_ref), (o2_ref, l2_ref))):
        nat.append(natural(o_g, l_g, dils[gi], nat_sc[2 * gi:2 * gi + 2]))
    (v0, l0), (v1, l1), (v2, l2) = nat
    m = jnp.maximum(jnp.maximum(l0, l1), l2)
    e0, e1, e2 = jnp.exp(l0 - m), jnp.exp(l1 - m), jnp.exp(l2 - m)
    inv = 1.0 / (e0 + e1 + e2)
    e_mat = e_ref[...]

    def spread(wt):
        hi = wt.astype(BF16)
        lo = (wt - hi.astype(F32)).astype(BF16)
        return (jnp.dot(hi, e_mat, preferred_element_type=F32)
                + jnp.dot(lo, e_mat, preferred_element_type=F32))

    mix = spread(e0 * inv) * v0 + spread(e1 * inv) * v1 + spread(e2 * inv) * v2
    y = jnp.dot(mix.astype(BF16), w_ref[...], preferred_element_type=F32)
    out = _layer_norm_rows(DN_ALPHA * h_ref[...] + y, g_ref[...], b_ref[...])
    o_ref[...] = out
    ob_ref[...] = out.astype(BF16)


def dilated_merge(outs, lses, dils, w_o, h, g, b, bsz, seq, tm=512):
    m, d = h.shape
    width = B_HEADS * B_HEAD_DIM
    tm = min(tm, seq)
    n_j = seq // tm
    expand = (jnp.arange(LANES)[:, None] == (jnp.arange(width)[None, :] // B_HEAD_DIM)).astype(BF16)
    row = lambda bi, j: (bi * n_j + j, 0)
    fixed = lambda bi, j: (0, 0)
    grouped = lambda bi, j: (bi, 0, j, 0)
    o_specs = [pl.BlockSpec((1, dl, tm // dl, width), grouped) for dl in dils]
    l_specs = [pl.BlockSpec((1, dl, tm // dl, LANES), grouped) for dl in dils]
    scratch = []
    for dl in dils:
        if dl > 1:
            scratch += [pltpu.VMEM((width // LANES, tm, LANES), F32), pltpu.VMEM((tm, LANES), F32)]
        else:
            scratch += [pltpu.VMEM((8, LANES), F32), pltpu.VMEM((8, LANES), F32)]
    outs = [o.reshape(bsz, dl, seq // dl, width) for o, dl in zip(outs, dils)]
    lses = [l.reshape(bsz, dl, seq // dl, LANES) for l, dl in zip(lses, dils)]
    return pl.pallas_call(
        functools.partial(_dilated_merge_kernel, tuple(dils)),
        out_shape=(jax.ShapeDtypeStruct((m, d), F32), jax.ShapeDtypeStruct((m, d), BF16)),
        grid=(bsz, n_j),
        in_specs=o_specs + l_specs
                 + [pl.BlockSpec((LANES, width), fixed), pl.BlockSpec((width, d), fixed),
                    pl.BlockSpec((tm, d), row), pl.BlockSpec((1, d), fixed), pl.BlockSpec((1, d), fixed)],
        out_specs=(pl.BlockSpec((tm, d), row), pl.BlockSpec((tm, d), row)),
        scratch_shapes=scratch,
        compiler_params=_params("parallel", "parallel"),
        name="dilated_merge",
    )(*outs, *lses, expand, w_o.astype(BF16), h, g.reshape(1, d), b.reshape(1, d))


def dilated_mixer(hb_by_dil, w_in, w_o, rel_bias, h, g, b, bsz, seq):
    width = B_HEADS * B_HEAD_DIM
    outs, lses, dils = [], [], []
    for gi, (window, dil) in enumerate(B_GROUPS):
        assert window // dil == B_N and seq % window == 0
        w_g = w_in[:, gi * 3 * width:(gi + 1) * 3 * width]
        w_g = w_g.at[:, :width].multiply(B_HEAD_DIM ** -0.5)
        proj = matmul(hb_by_dil[dil], w_g.astype(BF16), BF16, tn=3 * width // 2)
        o, lse = dilated_group(proj, dilated_bias(rel_bias, dil), bsz * dil, seq // dil)
        outs.append(o)
        lses.append(lse)
        dils.append(dil)
    return dilated_merge(outs, lses, dils, w_o, h, g, b, bsz, seq)


C_PAD = 128


def _mla_prep_kernel(x_ref, win_ref, qn_ref, kvn_ref, wqa_ref, wqb_ref, wka_ref, wv_ref,
                     cos_ref, sin_ref, vone_ref, q_ref, k_ref, v_ref):
    c = jnp.dot(x_ref[...], win_ref[...], preferred_element_type=F32)
    cos, sin = cos_ref[...], sin_ref[...]

    def rms(v, gain):
        return (v * lax.rsqrt(jnp.mean(v * v, axis=-1, keepdims=True) + RMS_EPS) * gain).astype(BF16)

    nq = rms(c[:, :C_Q_RANK], qn_ref[...])
    nkv = rms(c[:, C_Q_RANK:C_Q_RANK + C_KV_RANK], kvn_ref[...])
    o = C_Q_RANK + C_KV_RANK
    k_rope = c[:, o:o + C_PAD] * cos + c[:, o + C_PAD:o + 2 * C_PAD] * sin
    qa = jnp.dot(nq, wqa_ref[...], preferred_element_type=F32)
    qb = jnp.dot(nq, wqb_ref[...], preferred_element_type=F32)
    kn = jnp.dot(nkv, wka_ref[...], preferred_element_type=F32)
    for h in range(C_HEADS):
        hs = slice(h * C_PAD, (h + 1) * C_PAD)
        q_ref[:, hs] = (qa[:, hs] * cos + qb[:, hs] * sin).astype(BF16)
        k_ref[:, hs] = (kn[:, hs] + k_rope).astype(BF16)
    v_ref[...] = (jnp.dot(nkv, wv_ref[...], preferred_element_type=F32) + vone_ref[...]).astype(BF16)


def _rot_half_cols(w):
    half = w.shape[-1] // 2
    return jnp.concatenate([-w[..., half:], w[..., :half]], axis=-1)


def mla_prep(hb, w_in, q_norm, kv_norm, w_q_up, w_kv_up, bsz, seq, tm=512):
    d = w_in.shape[0]
    scale = (C_NOPE + C_ROPE) ** -0.5 * LOG2E
    pad_r = C_PAD - C_NOPE - C_ROPE
    w_kr = w_in[:, C_Q_RANK + C_KV_RANK:]

    def rope_slot(w):
        return jnp.pad(w, ((0, 0), (C_NOPE, pad_r)))

    win = jnp.concatenate([w_in[:, :C_Q_RANK + C_KV_RANK], rope_slot(w_kr),
                           rope_slot(_rot_half_cols(w_kr))], axis=1).astype(BF16)
    wq = w_q_up.reshape(C_Q_RANK, C_HEADS, C_NOPE + C_ROPE) * scale
    wqa = jnp.pad(wq, ((0, 0), (0, 0), (0, pad_r))).reshape(C_Q_RANK, C_HEADS * C_PAD).astype(BF16)
    wqb = jnp.pad(_rot_half_cols(wq[..., C_NOPE:]), ((0, 0), (0, 0), (C_NOPE, pad_r)))
    wqb = wqb.reshape(C_Q_RANK, C_HEADS * C_PAD).astype(BF16)
    wkv = w_kv_up.reshape(C_KV_RANK, C_HEADS, C_NOPE + C_V)
    wka = jnp.pad(wkv[..., :C_NOPE], ((0, 0), (0, 0), (0, C_PAD - C_NOPE)))
    wka = wka.reshape(C_KV_RANK, C_HEADS * C_PAD).astype(BF16)
    wv = jnp.pad(wkv[..., C_NOPE:], ((0, 0), (0, 0), (0, LANES - C_V)))
    wv = wv.reshape(C_KV_RANK, C_HEADS * LANES).astype(BF16)
    v_ones = jnp.tile(jnp.concatenate([jnp.zeros((1, C_V), F32), jnp.ones((1, LANES - C_V), F32)], axis=1),
                      (1, C_HEADS))

    half = C_ROPE // 2
    inv = ROPE_BASE ** (-jnp.arange(half, dtype=F32) / half)
    ang = jnp.arange(seq, dtype=F32)[:, None] * inv[None, :]
    ones, zeros = jnp.ones((seq, C_NOPE), F32), jnp.zeros((seq, pad_r), F32)
    cos = jnp.concatenate([ones, jnp.cos(ang), jnp.cos(ang), zeros], axis=1)
    sin = jnp.concatenate([0 * ones, jnp.sin(ang), jnp.sin(ang), zeros], axis=1)

    m = bsz * seq
    tm = min(tm, seq)
    n_pos_blk = seq // tm
    row = lambda i: (i, 0)
    fixed = lambda i: (0, 0)
    pos = lambda i: (i % n_pos_blk, 0)
    full = lambda a: pl.BlockSpec(a.shape, fixed)
    qw, vw = C_HEADS * C_PAD, C_HEADS * LANES
    return pl.pallas_call(
        _mla_prep_kernel,
        out_shape=(jax.ShapeDtypeStruct((m, qw), BF16), jax.ShapeDtypeStruct((m, qw), BF16),
                   jax.ShapeDtypeStruct((m, vw), BF16)),
        grid=(m // tm,),
        in_specs=[pl.BlockSpec((tm, d), row), full(win),
                  pl.BlockSpec((1, C_Q_RANK), fixed), pl.BlockSpec((1, C_KV_RANK), fixed),
                  full(wqa), full(wqb), full(wka), full(wv),
                  pl.BlockSpec((tm, C_PAD), pos), pl.BlockSpec((tm, C_PAD), pos), full(v_ones)],
        out_specs=(pl.BlockSpec((tm, qw), row), pl.BlockSpec((tm, qw), row),
                   pl.BlockSpec((tm, vw), row)),
        compiler_params=_params("parallel"),
        name="mla_prep",
    )(hb, win, q_norm.reshape(1, -1), kv_norm.reshape(1, -1), wqa, wqb, wka, wv, cos, sin, v_ones)


MLA_TQ = 512


MLA_SUB = 32


def _mla_attn_kernel(q_ref, k_ref, v_ref, o_ref, s_sc, p_sc, m_sc, a_sc, acc_sc):
    i = pl.program_id(2)
    tq = MLA_TQ
    n_tile = tq // LANES
    row_in_sub = lax.broadcasted_iota(I32, (MLA_SUB, LANES), 0)
    col_in_tile = lax.broadcasted_iota(I32, (MLA_SUB, LANES), 1)
    outs = []
    for hh in range(2):
        q = q_ref[0, :, hh * C_PAD:(hh + 1) * C_PAD]
        m_sc[...] = jnp.full(m_sc.shape, MASK_NEG, F32)
        acc_sc[...] = jnp.zeros(acc_sc.shape, F32)

        def step(c, diag):
            k0 = pl.multiple_of(c * tq, tq)
            kc = k_ref[0, pl.ds(k0, tq), hh * C_PAD:(hh + 1) * C_PAD]
            s_sc[...] = lax.dot_general(q, kc, (((1,), (1,)), ((), ())), preferred_element_type=F32)
            for rb in range(tq // MLA_SUB):
                r = slice(rb * MLA_SUB, (rb + 1) * MLA_SUB)
                t = [s_sc[r, jt * LANES:(jt + 1) * LANES] for jt in range(n_tile)]
                if diag:
                    t = [jnp.where(col_in_tile + jt * LANES <= row_in_sub + rb * MLA_SUB, t[jt], MASK_NEG)
                         for jt in range(n_tile)]
                m_old = m_sc[r, :]
                m_new = jnp.maximum(m_old, jnp.max(functools.reduce(jnp.maximum, t), axis=-1, keepdims=True))
                a_sc[r, :] = jnp.exp2(m_old - m_new)
                m_sc[r, :] = m_new
                for jt in range(n_tile):
                    p_sc[r, jt * LANES:(jt + 1) * LANES] = jnp.exp2(t[jt] - m_new).astype(BF16)
            pv = jnp.dot(p_sc[...], v_ref[0, pl.ds(k0, tq), hh * LANES:(hh + 1) * LANES],
                         preferred_element_type=F32)
            acc_sc[...] = acc_sc[...] * a_sc[...] + pv

        def body(c, carry):
            step(c, False)
            return carry

        lax.fori_loop(0, i, body, 0)
        step(i, True)
        outs.append(acc_sc[:, :C_V] / acc_sc[:, C_V:])
    o_ref[0] = jnp.concatenate(outs, axis=-1).astype(BF16)


def mla_attention(q, k, v, bsz, seq):
    qw, vw = C_HEADS * C_PAD, C_HEADS * C_V
    return pl.pallas_call(
        _mla_attn_kernel,
        out_shape=jax.ShapeDtypeStruct((bsz, seq, vw), BF16),
        grid=(bsz, C_HEADS // 2, seq // MLA_TQ),
        in_specs=[pl.BlockSpec((1, MLA_TQ, 2 * C_PAD), lambda b, hp, i: (b, i, hp)),
                  pl.BlockSpec((1, seq, 2 * C_PAD), lambda b, hp, i: (b, 0, hp)),
                  pl.BlockSpec((1, seq, 2 * LANES), lambda b, hp, i: (b, 0, hp))],
        out_specs=pl.BlockSpec((1, MLA_TQ, 2 * C_V), lambda b, hp, i: (b, i, hp)),
        scratch_shapes=[pltpu.VMEM((MLA_TQ, MLA_TQ), F32), pltpu.VMEM((MLA_TQ, MLA_TQ), BF16),
                        pltpu.VMEM((MLA_TQ, LANES), F32), pltpu.VMEM((MLA_TQ, LANES), F32),
                        pltpu.VMEM((MLA_TQ, LANES), F32)],
        compiler_params=_params("parallel", "parallel", "arbitrary"),
        name="mla_attention",
    )(q.reshape(bsz, seq, qw), k.reshape(bsz, seq, qw), v.reshape(bsz, seq, C_HEADS * LANES))


def mla_mixer(hb, w_in, q_norm, kv_norm, w_q_up, w_kv_up, w_o, h, g, b, bsz, seq):
    q, k, v = mla_prep(hb, w_in, q_norm, kv_norm, w_q_up, w_kv_up, bsz, seq)
    o = mla_attention(q, k, v, bsz, seq)
    return matmul_residual_ln(o.reshape(bsz * seq, -1), w_o.astype(BF16), h, g, b)


def _router_kernel(h_ref, wr_ref, eb_ref, eidx_ref, gate_ref):
    tm = h_ref.shape[0]
    per = N_EXPERTS // N_GROUPS
    logits = lax.dot_general(wr_ref[...], h_ref[...], (((1,), (1,)), ((), ())),
                             precision=lax.Precision.HIGHEST, preferred_element_type=F32)
    scores = 1.0 / (1.0 + jnp.exp(-logits))
    s3 = scores.reshape(N_GROUPS, per, tm)
    c3 = (scores + eb_ref[...]).reshape(N_GROUPS, per, tm)
    neg_inf = -jnp.inf
    j_idx = lax.broadcasted_iota(I32, (N_GROUPS, per, tm), 1).astype(F32)
    g_idx = lax.broadcasted_iota(I32, (N_GROUPS, per, tm), 0).astype(F32)
    flat = g_idx * per + j_idx

    m1 = jnp.max(c3, axis=1, keepdims=True)
    first = jnp.min(jnp.where(c3 == m1, j_idx, per), axis=1, keepdims=True)
    m2 = jnp.max(jnp.where(j_idx == first, neg_inf, c3), axis=1, keepdims=True)
    gs = m1 + m2

    gi = lax.broadcasted_iota(I32, (N_GROUPS, 1, tm), 0).astype(F32)
    keep = jnp.zeros((N_GROUPS, 1, tm), jnp.bool_)
    cur = gs
    for _ in range(TOPK_GROUPS):
        mx = jnp.max(cur, axis=0, keepdims=True)
        pick = gi == jnp.min(jnp.where(cur == mx, gi, N_GROUPS), axis=0, keepdims=True)
        keep = keep | pick
        cur = jnp.where(pick, neg_inf, cur)
    cur = jnp.where(keep, c3, neg_inf)

    idxs, gates = [], []
    for _ in range(TOP_K):
        mx = jnp.max(jnp.max(cur, axis=1, keepdims=True), axis=0, keepdims=True)
        cand = jnp.where(cur == mx, flat, N_EXPERTS)
        fi = jnp.min(jnp.min(cand, axis=1, keepdims=True), axis=0, keepdims=True)
        pick = flat == fi
        gates.append(jnp.sum(jnp.sum(jnp.where(pick, s3, 0.0), axis=1, keepdims=True), axis=0))
        idxs.append(fi[0])
        cur = jnp.where(pick, neg_inf, cur)
    gate = jnp.concatenate(gates, axis=0)
    gate = gate / jnp.sum(gate, axis=0, keepdims=True) * ROUTED_SCALE
    eidx_ref[...] = jnp.concatenate(idxs, axis=0).astype(I32)
    gate_ref[...] = gate


def moe_router(h, w_router, e_bias, row0, m, tm=512):
    d = h.shape[1]
    tm = min(tm, m)
    blk0 = row0 // tm
    return pl.pallas_call(
        _router_kernel,
        out_shape=(jax.ShapeDtypeStruct((TOP_K, m), I32), jax.ShapeDtypeStruct((TOP_K, m), F32)),
        grid=(m // tm,),
        in_specs=[pl.BlockSpec((tm, d), lambda i: (i + blk0, 0)),
                  pl.BlockSpec((N_EXPERTS, d), lambda i: (0, 0)),
                  pl.BlockSpec((N_EXPERTS, 1), lambda i: (0, 0))],
        out_specs=(pl.BlockSpec((TOP_K, tm), lambda i: (0, i)),
                   pl.BlockSpec((TOP_K, tm), lambda i: (0, i))),
        compiler_params=_params("parallel"),
        name="moe_router",
    )(h, w_router.T, e_bias.reshape(N_EXPERTS, 1))


def _silu(x):
    return x / (1.0 + jnp.exp(-x))


RANK_TM = 1024


def _rank_kernel(eidx_ref, tri_ref, dest_ref, cnt_ref, start_ref, cnt_sc, run_sc):
    p, i = pl.program_id(0), pl.program_id(1)
    tm = eidx_ref.shape[1]
    e = eidx_ref[...]
    ex = lax.broadcasted_iota(I32, (N_EXPERTS, tm), 0)
    onehot = jnp.zeros((N_EXPERTS, tm), F32)
    for k in range(TOP_K):
        onehot = onehot + jnp.where(e[k:k + 1, :] == ex, 1.0, 0.0)
    tile_cnt = jnp.sum(onehot, axis=-1, keepdims=True)

    @pl.when((p == 0) & (i == 0))
    def _():
        cnt_sc[...] = jnp.zeros(cnt_sc.shape, F32)

    @pl.when(p == 0)
    def _():
        cnt_sc[...] += tile_cnt

    @pl.when((p == 1) & (i == 0))
    def _():
        cnt = cnt_sc[...]
        padded = jnp.floor((cnt + (MOE_BLK - 1)) * (1.0 / MOE_BLK)) * MOE_BLK
        below = (lax.broadcasted_iota(I32, (N_EXPERTS, N_EXPERTS), 1)
                 < lax.broadcasted_iota(I32, (N_EXPERTS, N_EXPERTS), 0)).astype(F32)
        start = jnp.dot(below, jnp.broadcast_to(padded, (N_EXPERTS, LANES)),
                        precision=lax.Precision.HIGHEST, preferred_element_type=F32)
        run_sc[...] = start[:, :1]
        cnt_ref[...] = jnp.broadcast_to(cnt, (N_EXPERTS, LANES))
        start_ref[...] = start

    @pl.when(p == 1)
    def _():
        before = jnp.dot(onehot.astype(BF16), tri_ref[...], preferred_element_type=F32) + run_sc[...]
        rows = [jnp.sum(jnp.where(e[k:k + 1, :] == ex, before, 0.0), axis=0, keepdims=True)
                for k in range(TOP_K)]
        dest_ref[...] = jnp.concatenate(rows, axis=0).astype(I32)
        run_sc[...] += tile_cnt


def moe_rank(eidx):
    n_tok = eidx.shape[1]
    tm = min(RANK_TM, n_tok)
    tri = (jnp.arange(tm)[:, None] < jnp.arange(tm)[None, :]).astype(BF16)
    stat = jax.ShapeDtypeStruct((N_EXPERTS, LANES), F32)
    dest, cnt, start = pl.pallas_call(
        _rank_kernel,
        out_shape=(jax.ShapeDtypeStruct((TOP_K, n_tok), I32), stat, stat),
        grid=(2, n_tok // tm),
        in_specs=[pl.BlockSpec((TOP_K, tm), lambda p, i: (0, i)),
                  pl.BlockSpec((tm, tm), lambda p, i: (0, 0))],
        out_specs=(pl.BlockSpec((TOP_K, tm), lambda p, i: (0, i * p)),
                   pl.BlockSpec((N_EXPERTS, LANES), lambda p, i: (0, 0)),
                   pl.BlockSpec((N_EXPERTS, LANES), lambda p, i: (0, 0))),
        scratch_shapes=[pltpu.VMEM((N_EXPERTS, 1), F32), pltpu.VMEM((N_EXPERTS, 1), F32)],
        compiler_params=_params("arbitrary", "arbitrary"),
        name="moe_rank",
    )(eidx, tri)
    return dest, cnt[:, 0].astype(I32), start[:, 0].astype(I32)


def _experts_kernel(blk_e_ref, x_ref, wg_ref, wu_ref, wd_ref, o_ref, wg_sc, wu_sc, wd_sc):
    i = pl.program_id(0)

    @pl.when((i == 0) | (blk_e_ref[i] != blk_e_ref[jnp.maximum(i - 1, 0)]))
    def _():
        wg_sc[...] = wg_ref[...].astype(BF16)
        wu_sc[...] = wu_ref[...].astype(BF16)
        wd_sc[...] = wd_ref[...].astype(BF16)

    x = x_ref[...]
    gate = jnp.dot(x, wg_sc[...], preferred_element_type=F32)
    up = jnp.dot(x, wu_sc[...], preferred_element_type=F32)
    act = (_silu(gate) * up).astype(BF16)
    o_ref[...] = jnp.dot(act, wd_sc[...], preferred_element_type=F32).astype(o_ref.dtype)


def grouped_experts(x_sorted, blk_e, wg, wu, wd, layer):
    n_slot, d = x_sorted.shape
    n_blk = n_slot // MOE_BLK
    by_expert = lambda i, be: (layer, be[i], 0, 0)
    grid_spec = pltpu.PrefetchScalarGridSpec(
        num_scalar_prefetch=1,
        grid=(n_blk,),
        in_specs=[pl.BlockSpec((MOE_BLK, d), lambda i, be: (i, 0)),
                  pl.BlockSpec((None, None, d, D_EXPERT), by_expert),
                  pl.BlockSpec((None, None, d, D_EXPERT), by_expert),
                  pl.BlockSpec((None, None, D_EXPERT, d), by_expert)],
        out_specs=pl.BlockSpec((MOE_BLK, d), lambda i, be: (i, 0)),
        scratch_shapes=[pltpu.VMEM((d, D_EXPERT), BF16), pltpu.VMEM((d, D_EXPERT), BF16),
                        pltpu.VMEM((D_EXPERT, d), BF16)],
    )
    return pl.pallas_call(
        _experts_kernel,
        out_shape=jax.ShapeDtypeStruct((n_slot, d), BF16),
        grid_spec=grid_spec,
        compiler_params=_params("arbitrary"),
        name="grouped_experts",
    )(blk_e, x_sorted, wg, wu, wd)


def _moe_finish_kernel(dils, hb_ref, y_ref, gt_ref, wg_ref, wu_ref, wd_ref, h_ref, g_ref, b_ref,
                       o_ref, ob_ref, *rest):
    x = hb_ref[...]
    d = x.shape[1]
    act = (_silu(jnp.dot(x, wg_ref[...], preferred_element_type=F32))
           * jnp.dot(x, wu_ref[...], preferred_element_type=F32)).astype(BF16)
    ff = jnp.dot(act, wd_ref[...], preferred_element_type=F32)
    gt = gt_ref[...]
    for k in range(TOP_K):
        ff = ff + gt[:, k:k + 1] * y_ref[k].astype(F32)
    out = _layer_norm_rows(DN_ALPHA * h_ref[...] + ff, g_ref[...], b_ref[...])
    o_ref[...] = out
    ob_ref[...] = out.astype(BF16)
    if dils:
        perm_refs, nat_sc = rest[:-1], rest[-1]
        for j in range(d // LANES):
            nat_sc[j] = out[:, j * LANES:(j + 1) * LANES]
        for p_ref, dil in zip(perm_refs, dils):
            n = out.shape[0] // dil
            for r in range(dil):
                for j in range(d // LANES):
                    p_ref[0, r, :, j * LANES:(j + 1) * LANES] = (
                        nat_sc[j, pl.ds(r, n, stride=dil), :].astype(BF16))


def moe_finish(hb, y_tok, gate_tok, ws_g, ws_u, ws_d, h, g, b, seq, row0, m, dils=(), tm=256):
    d = h.shape[1]
    tm = min(tm, seq)
    n_j = seq // tm
    blk0 = row0 // tm
    row = lambda i: (i, 0)
    row_in = lambda i: (i + blk0, 0)
    fixed = lambda i: (0, 0)
    out_shape = [jax.ShapeDtypeStruct((m, d), F32), jax.ShapeDtypeStruct((m, d), BF16)]
    out_specs = [pl.BlockSpec((tm, d), row), pl.BlockSpec((tm, d), row)]
    for dil in dils:
        out_shape.append(jax.ShapeDtypeStruct((m // seq, dil, seq // dil, d), BF16))
        out_specs.append(pl.BlockSpec((1, dil, tm // dil, d), lambda i: (i // n_j, 0, i % n_j, 0)))
    res = pl.pallas_call(
        functools.partial(_moe_finish_kernel, tuple(dils)),
        out_shape=tuple(out_shape),
        grid=(m // tm,),
        in_specs=[pl.BlockSpec((tm, d), row_in), pl.BlockSpec((TOP_K, tm, d), lambda i: (0, i, 0)),
                  pl.BlockSpec((tm, TOP_K), row),
                  pl.BlockSpec(ws_g.shape, fixed), pl.BlockSpec(ws_u.shape, fixed),
                  pl.BlockSpec(ws_d.shape, fixed),
                  pl.BlockSpec((tm, d), row_in), pl.BlockSpec((1, d), fixed), pl.BlockSpec((1, d), fixed)],
        out_specs=tuple(out_specs),
        scratch_shapes=[pltpu.VMEM((d // LANES, tm, LANES), F32)] if dils else [],
        compiler_params=_params("parallel"),
        name="moe_finish",
    )(hb, y_tok, gate_tok, ws_g, ws_u, ws_d, h, g.reshape(1, d), b.reshape(1, d))
    return res[0], res[1], [r.reshape(m, d) for r in res[2:]]


SORT_E_SHIFT = 19


MOE_SPLITS = 2


def moe_layer(h, hb, w_router, e_bias, wg, wu, wd, layer, ws_g, ws_u, ws_d, g, b, seq, dils=()):
    n_all = h.shape[0]
    n_split = MOE_SPLITS if (n_all // seq) % MOE_SPLITS == 0 else 1
    parts = [moe_tokens(h, hb, w_router, e_bias, wg, wu, wd, layer, ws_g, ws_u, ws_d, g, b, seq, dils,
                        sp * (n_all // n_split), n_all // n_split) for sp in range(n_split)]
    cat = lambda xs: xs[0] if len(xs) == 1 else jnp.concatenate(xs, axis=0)
    return (cat([p[0] for p in parts]), cat([p[1] for p in parts]),
            [cat([p[2][j] for p in parts]) for j in range(len(dils))])


def moe_tokens(h, hb, w_router, e_bias, wg, wu, wd, layer, ws_g, ws_u, ws_d, g, b, seq, dils, row0, n_tok):
    d = h.shape[1]
    n_asg = n_tok * TOP_K
    n_pad = N_EXPERTS * MOE_BLK
    assert n_asg <= 1 << (SORT_E_SHIFT - 1) and n_pad <= 1 << (SORT_E_SHIFT - 1)
    eidx, gate = moe_router(h, w_router, e_bias, row0, n_tok)
    dest, counts, pad_start = moe_rank(eidx)
    padded = (counts + MOE_BLK - 1) // MOE_BLK * MOE_BLK
    pad_end = pad_start + padded
    n_blk = n_asg // MOE_BLK + N_EXPERTS
    blk_start = jnp.arange(n_blk, dtype=I32) * MOE_BLK
    blk_e = jnp.minimum(jnp.sum(pad_end[None, :] <= blk_start[:, None], axis=1), N_EXPERTS - 1)

    key_real = (eidx.T.reshape(-1) << SORT_E_SHIFT) | jnp.arange(n_asg, dtype=I32)
    i_pad = jnp.arange(n_pad, dtype=I32)
    e_pad = jnp.sum(jnp.cumsum(padded - counts)[None, :] <= i_pad[:, None], axis=1).astype(I32)
    key_pad = (e_pad << SORT_E_SHIFT) | (1 << (SORT_E_SHIFT - 1)) | i_pad
    low = jnp.sort(jnp.concatenate([key_real, key_pad])) & ((1 << SORT_E_SHIFT) - 1)
    slot_tok = jnp.where(low < (1 << (SORT_E_SHIFT - 1)), low // TOP_K,
                         jnp.arange(n_asg + n_pad, dtype=I32) % n_tok)

    x_sorted = hb.at[slot_tok + row0].get(mode="promise_in_bounds")
    y = grouped_experts(x_sorted, blk_e.astype(I32), wg, wu, wd, layer)
    y_k = y.at[dest.reshape(-1)].get(mode="promise_in_bounds").reshape(TOP_K, n_tok, d)
    return moe_finish(hb, y_k, gate.T, ws_g, ws_u, ws_d, h, g, b, seq, row0, n_tok, dils)


def kernel(x, rel_bias, a_w_in, a_w_o, b_w_in, b_w_o, c_w_in, c_q_norm, c_kv_norm, c_w_q_up,
           c_w_kv_up, c_w_o, ln_g, ln_b, moe_w_router, moe_bias, moe_w_gate, moe_w_up,
           moe_w_down, moe_ws_gate, moe_ws_up, moe_ws_down):
    bsz, seq, d = x.shape
    depth = ln_g.shape[0]
    h = x.reshape(bsz * seq, d)
    hb = h.astype(BF16)
    strip = dsa_bias_strip(rel_bias, seq)
    extra_dils = tuple(dil for _, dil in B_GROUPS if dil > 1)
    hb_perm = []
    for layer in range(depth):
        kind, slot = layer % N_MIXERS, layer // N_MIXERS
        g0, b0 = ln_g[layer, 0], ln_b[layer, 0]
        if kind == 0:
            h, hb = dsa_mixer(hb, a_w_in[slot], a_w_o[slot], strip, h, g0, b0, bsz, seq)
        elif kind == 1:
            hb_by_dil = {1: hb, **dict(zip(extra_dils, hb_perm))}
            h, hb = dilated_mixer(hb_by_dil, b_w_in[slot], b_w_o[slot], rel_bias, h, g0, b0, bsz, seq)
        else:
            h, hb = mla_mixer(hb, c_w_in[slot], c_q_norm[slot], c_kv_norm[slot], c_w_q_up[slot],
                              c_w_kv_up[slot], c_w_o[slot], h, g0, b0, bsz, seq)
        next_dilated = layer + 1 < depth and (layer + 1) % N_MIXERS == 1
        h, hb, hb_perm = moe_layer(h, hb, moe_w_router[layer], moe_bias[layer],
                                   moe_w_gate, moe_w_up, moe_w_down, layer,
                                   moe_ws_gate[layer].astype(BF16),
                                   moe_ws_up[layer].astype(BF16), moe_ws_down[layer].astype(BF16),
                                   ln_g[layer, 1], ln_b[layer, 1], seq,
                                   extra_dils if next_dilated else ())
    return h.reshape(bsz, seq, d)
```

```python
import functools
import math

import jax
import jax.numpy as jnp
from jax import lax
from jax.experimental import pallas as pl
from jax.experimental.pallas import tpu as pltpu

F32 = jnp.float32
BF16 = jnp.bfloat16
I32 = jnp.int32

LANES = 128
VMEM_LIMIT_BYTES = 56 * 1024 * 1024

D_MODEL = 1024
DEPTH = 4
N_MIXERS = 3
NORM_EPS = 1e-5
RMS_EPS = 1e-6
REL_BUCKETS = 32
REL_MAX_DIST = 2048
A_HEADS = 16
A_HEAD_DIM = 128
A_IDX_HEADS = 8
A_IDX_DIM = 64
A_TOPK_MAX = 256
A_Q = A_HEADS * A_HEAD_DIM
B_GROUPS = ((128, 1), (512, 4), (2048, 16))
B_HEADS = 16
B_HEAD_DIM = 64
B_N = 128
C_HEADS = 16
C_Q_RANK = 256
C_KV_RANK = 128
C_NOPE = 64
C_ROPE = 32
C_V = 64
ROPE_BASE = 10000.0
N_EXPERTS = 64
TOP_K = 8
N_GROUPS = 8
TOPK_GROUPS = 4
D_EXPERT = 256
ROUTED_SCALE = 2.5
DN_ALPHA = (2 * DEPTH) ** 0.25

Q_BLK = 128
SEL_BLK = 512
KEY_CHUNK = 512
MOE_BLK = 512
MASK_NEG = -1e30
INT_MIN = -(2 ** 31)


def _params(*sem):
    return pltpu.CompilerParams(dimension_semantics=sem, vmem_limit_bytes=VMEM_LIMIT_BYTES)


def _mm_kernel(x_ref, w_ref, o_ref):
    o_ref[...] = jnp.dot(x_ref[...], w_ref[...], preferred_element_type=F32).astype(o_ref.dtype)


def matmul(x, w, out_dtype, tm=1024, tn=None):
    m, k = x.shape
    n = w.shape[1]
    tn = n if tn is None else tn
    tm = min(tm, m)
    return pl.pallas_call(
        _mm_kernel,
        out_shape=jax.ShapeDtypeStruct((m, n), out_dtype),
        grid=(n // tn, m // tm),
        in_specs=[pl.BlockSpec((tm, k), lambda j, i: (i, 0)),
                  pl.BlockSpec((k, tn), lambda j, i: (0, j))],
        out_specs=pl.BlockSpec((tm, tn), lambda j, i: (i, j)),
        compiler_params=_params("parallel", "parallel"),
        name="matmul",
    )(x, w)


def _layer_norm_rows(z, g, b):
    mu = jnp.mean(z, axis=-1, keepdims=True)
    zc = z - mu
    var = jnp.mean(zc * zc, axis=-1, keepdims=True)
    return zc * lax.rsqrt(var + NORM_EPS) * g + b


def _mm_ln_kernel(x_ref, w_ref, h_ref, g_ref, b_ref, o_ref, ob_ref):
    y = jnp.dot(x_ref[...], w_ref[...], preferred_element_type=F32)
    out = _layer_norm_rows(DN_ALPHA * h_ref[...] + y, g_ref[...], b_ref[...])
    o_ref[...] = out
    ob_ref[...] = out.astype(BF16)


def matmul_residual_ln(x, w, h, g, b, tm=512):
    m, k = x.shape
    d = w.shape[1]
    tm = min(tm, m)
    row = lambda i: (i, 0)
    fixed = lambda i: (0, 0)
    return pl.pallas_call(
        _mm_ln_kernel,
        out_shape=(jax.ShapeDtypeStruct((m, d), F32), jax.ShapeDtypeStruct((m, d), BF16)),
        grid=(m // tm,),
        in_specs=[pl.BlockSpec((tm, k), row), pl.BlockSpec((k, d), fixed),
                  pl.BlockSpec((tm, d), row), pl.BlockSpec((1, d), fixed),
                  pl.BlockSpec((1, d), fixed)],
        out_specs=(pl.BlockSpec((tm, d), row), pl.BlockSpec((tm, d), row)),
        compiler_params=_params("parallel"),
        name="matmul_residual_ln",
    )(x, w, h, g.reshape(1, d), b.reshape(1, d))


def _t5_bucket(dist):
    exact = REL_BUCKETS // 2
    d_f = jnp.maximum(dist, 1).astype(F32)
    large = exact + (jnp.log(d_f / exact) / math.log(REL_MAX_DIST / exact)
                     * (REL_BUCKETS - exact)).astype(I32)
    return jnp.where(dist < exact, dist, jnp.minimum(large, REL_BUCKETS - 1))


def _bias_by_distance(rel_bias, dist):
    return rel_bias[_t5_bucket(dist)]


def _toeplitz(f, rows, cols, off):
    length = f.shape[-1]
    period = rows + cols - 1
    u = f[:, jnp.clip(off + rows - 1 - jnp.arange(period), 0, length - 1)]
    skew = jnp.tile(u, (1, rows + 1))[:, :rows * (period + 1)].reshape(-1, rows, period + 1)
    return skew[:, ::-1, :cols]


DSA_SUB = 32
LOG2E = math.log2(math.e)


def _dsa_select(k_sel, n_keys, t0, iq_ref, ik_ref, key_sc, mask_sc, cut_sc):
    rows = lax.broadcasted_iota(I32, (SEL_BLK, n_keys), 0) + t0
    cols = lax.broadcasted_iota(I32, (SEL_BLK, n_keys), 1)
    valid = cols <= rows

    ik = ik_ref[0, :n_keys, :].astype(BF16)
    w_all = iq_ref[0, :, A_IDX_HEADS * LANES:] * ((A_IDX_DIM * A_IDX_HEADS) ** -0.5)
    score = jnp.zeros((SEL_BLK, n_keys), F32)
    for h in range(A_IDX_HEADS):
        qh = iq_ref[0, :, h * LANES:(h + 1) * LANES].astype(BF16)
        rel = lax.dot_general(qh, ik, (((1,), (1,)), ((), ())), preferred_element_type=F32)
        score = score + w_all[:, A_IDX_DIM + h:A_IDX_DIM + h + 1] * jnp.maximum(rel, 0.0)
    score = jnp.where(score == 0.0, 0.0, score)

    bits = pltpu.bitcast(score, I32)
    key_sc[:, :n_keys] = jnp.where(valid, bits ^ ((bits >> 31) & 0x7FFFFFFF), INT_MIN)

    def search(it, ans_u):
        cand_u = ans_u | lax.shift_left(jnp.int32(1), 31 - it)
        cand_s = cand_u ^ INT_MIN
        cnt = jnp.sum(jnp.where(key_sc[:, :n_keys] >= cand_s, 1.0, 0.0), axis=-1, keepdims=True)
        return jnp.where(cnt >= k_sel, cand_u, ans_u)

    thr = lax.fori_loop(0, 32, search, jnp.zeros((SEL_BLK, 1), I32)) ^ INT_MIN

    key = key_sc[:, :n_keys]
    gt = key > thr
    eq = key == thr
    need = k_sel - jnp.sum(jnp.where(gt, 1.0, 0.0), axis=-1, keepdims=True)
    n_eq = jnp.sum(jnp.where(eq, 1.0, 0.0), axis=-1, keepdims=True)
    cut_sc[...] = jnp.full((SEL_BLK, 1), n_keys, I32)
    surplus = jnp.where((n_eq > need) & (thr != INT_MIN), 1.0, 0.0)

    @pl.when(jnp.max(surplus) > 0.0)
    def _():
        def tie_search(it, ans):
            cand = ans | lax.shift_left(jnp.int32(1), (n_keys.bit_length() - 1) - it)
            hit = (key_sc[:, :n_keys] == thr) & (cols < cand)
            cnt = jnp.sum(jnp.where(hit, 1.0, 0.0), axis=-1, keepdims=True)
            return jnp.where(cnt < need, cand, ans)
        cut_sc[...] = lax.fori_loop(0, n_keys.bit_length(), tie_search, jnp.zeros((SEL_BLK, 1), I32))

    selected = valid & (gt | (eq & (cols <= cut_sc[...])))
    mask_sc[:, :n_keys] = jnp.where(selected, 0.0, MASK_NEG)


def _dsa_kernel(k_sel, seq, q_ref, kv_ref, iq_ref, ik_ref, strip_ref, o_ref,
                key_sc, mask_sc, cut_sc, qs_sc, s_sc, p_sc, acc_sc, m_sc, a_sc, ve_sc):
    i = pl.program_id(1)
    t0 = i * SEL_BLK
    blk_per_chunk = KEY_CHUNK // Q_BLK

    @pl.when(i == 0)
    def _():
        ve_sc[:, :A_HEAD_DIM] = kv_ref[0, :, A_HEAD_DIM:]
        ve_sc[:, A_HEAD_DIM:] = jnp.ones((seq, A_HEAD_DIM), BF16)

    for j in range(seq // KEY_CHUNK):
        @pl.when(t0 // KEY_CHUNK == j)
        def _(j=j):
            _dsa_select(k_sel, (j + 1) * KEY_CHUNK, t0, iq_ref, ik_ref, key_sc, mask_sc, cut_sc)

    n_strip_blk = seq // Q_BLK - 1
    n_tile = KEY_CHUNK // LANES

    def attend(sub, carry):
        _dsa_attend(sub, i * (SEL_BLK // Q_BLK) + sub)
        return carry

    def _dsa_attend(sub, qi):
        r0 = pl.multiple_of(sub * Q_BLK, Q_BLK)
        for h in range(A_HEADS):
            qs_sc[h * Q_BLK:(h + 1) * Q_BLK, :] = q_ref[0, pl.ds(r0, Q_BLK),
                                                        h * A_HEAD_DIM:(h + 1) * A_HEAD_DIM]
        m_sc[...] = jnp.full(m_sc.shape, MASK_NEG, F32)
        acc_sc[...] = jnp.zeros(acc_sc.shape, F32)
        lax.fori_loop(0, (qi * Q_BLK + Q_BLK + KEY_CHUNK - 1) // KEY_CHUNK,
                      functools.partial(chunk, r0, qi), 0)
        for h in range(A_HEADS):
            r = slice(h * Q_BLK, (h + 1) * Q_BLK)
            o_ref[0, pl.ds(r0, Q_BLK), h * A_HEAD_DIM:(h + 1) * A_HEAD_DIM] = (
                acc_sc[r, :A_HEAD_DIM] / acc_sc[r, A_HEAD_DIM:]).astype(BF16)

    def chunk(r0, qi, c, carry):
        k0 = pl.multiple_of(c * KEY_CHUNK, KEY_CHUNK)
        kc = kv_ref[0, pl.ds(k0, KEY_CHUNK), :A_HEAD_DIM]
        s_sc[...] = lax.dot_general(qs_sc[...], kc, (((1,), (1,)), ((), ())),
                                    preferred_element_type=F32)
        w0 = (c * blk_per_chunk - qi + n_strip_blk) * Q_BLK
        for h in range(A_HEADS):
            for rb in range(Q_BLK // DSA_SUB):
                qr = slice(rb * DSA_SUB, (rb + 1) * DSA_SUB)
                mr = pl.ds(pl.multiple_of(r0 + rb * DSA_SUB, DSA_SUB), DSA_SUB)
                r = slice(h * Q_BLK + rb * DSA_SUB, h * Q_BLK + (rb + 1) * DSA_SUB)
                t = []
                for jt in range(n_tile):
                    ws = pl.ds(pl.multiple_of(w0 + jt * LANES, LANES), LANES)
                    ks = pl.ds(pl.multiple_of(k0 + jt * LANES, LANES), LANES)
                    t.append(s_sc[r, jt * LANES:(jt + 1) * LANES]
                             + strip_ref[h, qr, ws].astype(F32) + mask_sc[mr, ks])
                mx = functools.reduce(jnp.maximum, t)
                m_old = m_sc[r, :]
                m_new = jnp.maximum(m_old, jnp.max(mx, axis=-1, keepdims=True))
                a_sc[r, :] = jnp.exp2(m_old - m_new)
                m_sc[r, :] = m_new
                for jt in range(n_tile):
                    p_sc[r, jt * LANES:(jt + 1) * LANES] = jnp.exp2(t[jt] - m_new).astype(BF16)
        pv = jnp.dot(p_sc[...], ve_sc[pl.ds(k0, KEY_CHUNK), :], preferred_element_type=F32)
        for half in range(2):
            hs = slice(half * A_HEAD_DIM, (half + 1) * A_HEAD_DIM)
            acc_sc[:, hs] = acc_sc[:, hs] * a_sc[...] + pv[:, hs]
        return carry

    lax.fori_loop(0, SEL_BLK // Q_BLK, attend, 0)


def dsa_attention(qkv, idx, strip, bsz, seq):
    assert seq % KEY_CHUNK == 0 and KEY_CHUNK % SEL_BLK == 0 and SEL_BLK % Q_BLK == 0
    k_sel = min(A_TOPK_MAX, seq // 4)
    rows = A_HEADS * Q_BLK
    n_kv_blk = A_Q // (2 * A_HEAD_DIM)
    n_ik_blk = A_IDX_HEADS
    idx_w = (A_IDX_HEADS + 1) * LANES
    return pl.pallas_call(
        functools.partial(_dsa_kernel, k_sel, seq),
        out_shape=jax.ShapeDtypeStruct((bsz, seq, A_Q), BF16),
        grid=(bsz, seq // SEL_BLK),
        in_specs=[pl.BlockSpec((1, SEL_BLK, A_Q), lambda b, i: (b, i, 0)),
                  pl.BlockSpec((1, seq, 2 * A_HEAD_DIM), lambda b, i: (b, 0, n_kv_blk)),
                  pl.BlockSpec((1, SEL_BLK, idx_w), lambda b, i: (b, i, 0)),
                  pl.BlockSpec((1, seq, LANES), lambda b, i: (b, 0, n_ik_blk)),
                  pl.BlockSpec(strip.shape, lambda b, i: (0, 0, 0), pipeline_mode=pl.Buffered(1))],
        out_specs=pl.BlockSpec((1, SEL_BLK, A_Q), lambda b, i: (b, i, 0)),
        scratch_shapes=[pltpu.VMEM((SEL_BLK, seq), I32),
                        pltpu.VMEM((SEL_BLK, seq), F32),
                        pltpu.VMEM((SEL_BLK, 1), I32),
                        pltpu.VMEM((rows, A_HEAD_DIM), BF16),
                        pltpu.VMEM((rows, KEY_CHUNK), F32),
                        pltpu.VMEM((rows, KEY_CHUNK), BF16),
                        pltpu.VMEM((rows, 2 * A_HEAD_DIM), F32),
                        pltpu.VMEM((rows, LANES), F32),
                        pltpu.VMEM((rows, LANES), F32),
                        pltpu.VMEM((seq, 2 * A_HEAD_DIM), BF16)],
        compiler_params=_params("parallel", "arbitrary"),
        name="dsa_attention",
    )(qkv, qkv, idx, idx, strip)


def dsa_bias_strip(rel_bias, seq):
    width = seq + KEY_CHUNK - Q_BLK
    by_dist = _bias_by_distance(rel_bias, jnp.arange(seq)).T
    return (_toeplitz(by_dist, Q_BLK, width, seq - Q_BLK) * LOG2E).astype(BF16)


def dsa_weights(w_in):
    d = w_in.shape[0]
    wq = w_in[:, :A_Q] * (A_HEAD_DIM ** -0.5 * LOG2E)
    wkv = w_in[:, A_Q:A_Q + 2 * A_HEAD_DIM]
    o = A_Q + 2 * A_HEAD_DIM
    n_qi = A_IDX_HEADS * A_IDX_DIM
    wqi = w_in[:, o:o + n_qi].reshape(d, A_IDX_HEADS, A_IDX_DIM)
    wqi = jnp.pad(wqi, ((0, 0), (0, 0), (0, LANES - A_IDX_DIM))).reshape(d, A_IDX_HEADS * LANES)
    wkw = jnp.pad(w_in[:, o + n_qi:], ((0, 0), (0, LANES - A_IDX_DIM - A_IDX_HEADS)))
    return (jnp.concatenate([wq, wkv], axis=1).astype(BF16),
            jnp.concatenate([wqi, wkw], axis=1).astype(BF16))


def dsa_mixer(hb, w_in, w_o, strip, h, g, b, bsz, seq):
    w_qkv, w_idx = dsa_weights(w_in)
    qkv = matmul(hb, w_qkv, BF16, tn=w_qkv.shape[1] // 2)
    idx = matmul(hb, w_idx, F32)
    o = dsa_attention(qkv.reshape(bsz, seq, -1), idx.reshape(bsz, seq, -1), strip, bsz, seq)
    return matmul_residual_ln(o.reshape(bsz * seq, A_Q), w_o.astype(BF16), h, g, b)


def _dilated_kernel(q_ref, kp_ref, kc_ref, vp_ref, vc_ref, bias_ref, o_ref, lse_ref):
    c = pl.program_id(1)
    col = lax.broadcasted_iota(I32, (B_N, 2 * B_N), 1)
    has_prev = (col >= B_N) | (c > 0)
    lane = lax.broadcasted_iota(I32, (B_N, LANES), 1)
    low_half = lane < B_HEAD_DIM
    lse_tile = jnp.zeros((B_N, LANES), F32)
    ones = jnp.ones((2 * B_N, LANES), BF16)
    for pr in range(B_HEADS * B_HEAD_DIM // LANES):
        ps = slice(pr * LANES, (pr + 1) * LANES)
        q2 = q_ref[0, :, ps]
        kk = jnp.concatenate([kp_ref[0, :, ps], kc_ref[0, :, ps]], axis=0)
        ve = jnp.concatenate([jnp.concatenate([vp_ref[0, :, ps], vc_ref[0, :, ps]], axis=0), ones],
                             axis=1)
        out2 = None
        for hh in range(LANES // B_HEAD_DIM):
            h = pr * (LANES // B_HEAD_DIM) + hh
            mine = low_half if hh == 0 else jnp.logical_not(low_half)
            qh = jnp.where(mine, q2, jnp.zeros_like(q2))
            s = lax.dot_general(qh, kk, (((1,), (1,)), ((), ())), preferred_element_type=F32)
            s = jnp.where(has_prev, s + bias_ref[h], MASK_NEG)
            m = jnp.max(s, axis=-1, keepdims=True)
            pv = jnp.dot(jnp.exp2(s - m).astype(BF16), ve, preferred_element_type=F32)
            l = pv[:, LANES:]
            o_h = pv[:, :LANES] / l
            out2 = o_h if out2 is None else jnp.where(mine, o_h, out2)
            lse_tile = jnp.where(lane == h, (m + jnp.log2(l)) * (1.0 / LOG2E), lse_tile)
        o_ref[0, :, ps] = out2.astype(BF16)
    lse_ref[0] = lse_tile


def dilated_group(proj, bias, n_seq, length):
    width = B_HEADS * B_HEAD_DIM
    pv = proj.reshape(n_seq, length, 3 * width)
    blk = (1, B_N, width)

    def spec(which, prev):
        if prev:
            return pl.BlockSpec(blk, lambda s, c: (s, jnp.maximum(c - 1, 0), which))
        return pl.BlockSpec(blk, lambda s, c: (s, c, which))

    return pl.pallas_call(
        _dilated_kernel,
        out_shape=(jax.ShapeDtypeStruct((n_seq, length, width), BF16),
                   jax.ShapeDtypeStruct((n_seq, length, LANES), F32)),
        grid=(n_seq, length // B_N),
        in_specs=[spec(0, False), spec(1, True), spec(1, False), spec(2, True), spec(2, False),
                  pl.BlockSpec(bias.shape, lambda s, c: (0, 0, 0))],
        out_specs=(pl.BlockSpec(blk, lambda s, c: (s, c, 0)),
                   pl.BlockSpec((1, B_N, LANES), lambda s, c: (s, c, 0))),
        compiler_params=_params("parallel", "parallel"),
        name="dilated_group",
    )(pv, pv, pv, pv, pv, bias)


def dilated_bias(rel_bias, dil):
    ii = jnp.arange(B_N)[:, None]
    jj = jnp.arange(2 * B_N)[None, :]
    delta = B_N + ii - jj
    band = (delta >= 0) & (delta <= B_N)
    by_delta = _bias_by_distance(rel_bias, jnp.arange(2 * B_N) * dil).T
    bias = _toeplitz(by_delta, B_N, 2 * B_N, B_N)
    return jnp.where(band[None], bias.astype(F32) * LOG2E, MASK_NEG)


def _dilated_merge_kernel(dils, o0_ref, o1_ref, o2_ref, l0_ref, l1_ref, l2_ref, e_ref, w_ref,
                          h_ref, g_ref, b_ref, o_ref, ob_ref, *nat_sc):
    def natural(o_g, l_g, dil, scratch):
        if dil == 1:
            return o_g[0, 0].astype(F32), l_g[0, 0]
        o_sc, l_sc = scratch
        n_tiles, n = o_sc.shape[0], o_sc.shape[1] // dil
        for r in range(dil):
            for j in range(n_tiles):
                o_sc[j, pl.ds(r, n, stride=dil), :] = o_g[0, r, :, j * LANES:(j + 1) * LANES].astype(F32)
            l_sc[pl.ds(r, n, stride=dil), :] = l_g[0, r]
        return jnp.concatenate([o_sc[j] for j in range(n_tiles)], axis=-1), l_sc[...]

    nat = []
    for gi, (o_g, l_g) in enumerate(((o0_ref, l0_ref), (o1_ref, l1_ref), (o2_ref, l2_ref))):
        nat.append(natural(o_g, l_g, dils[gi], nat_sc[2 * gi:2 * gi + 2]))
    (v0, l0), (v1, l1), (v2, l2) = nat
    m = jnp.maximum(jnp.maximum(l0, l1), l2)
    e0, e1, e2 = jnp.exp(l0 - m), jnp.exp(l1 - m), jnp.exp(l2 - m)
    inv = 1.0 / (e0 + e1 + e2)
    e_mat = e_ref[...]

    def spread(wt):
        hi = wt.astype(BF16)
        lo = (wt - hi.astype(F32)).astype(BF16)
        return (jnp.dot(hi, e_mat, preferred_element_type=F32)
                + jnp.dot(lo, e_mat, preferred_element_type=F32))

    mix = spread(e0 * inv) * v0 + spread(e1 * inv) * v1 + spread(e2 * inv) * v2
    y = jnp.dot(mix.astype(BF16), w_ref[...], preferred_element_type=F32)
    out = _layer_norm_rows(DN_ALPHA * h_ref[...] + y, g_ref[...], b_ref[...])
    o_ref[...] = out
    ob_ref[...] = out.astype(BF16)


def dilated_merge(outs, lses, dils, w_o, h, g, b, bsz, seq, tm=512):
    m, d = h.shape
    width = B_HEADS * B_HEAD_DIM
    tm = min(tm, seq)
    n_j = seq // tm
    expand = (jnp.arange(LANES)[:, None] == (jnp.arange(width)[None, :] // B_HEAD_DIM)).astype(BF16)
    row = lambda bi, j: (bi * n_j + j, 0)
    fixed = lambda bi, j: (0, 0)
    grouped = lambda bi, j: (bi, 0, j, 0)
    o_specs = [pl.BlockSpec((1, dl, tm // dl, width), grouped) for dl in dils]
    l_specs = [pl.BlockSpec((1, dl, tm // dl, LANES), grouped) for dl in dils]
    scratch = []
    for dl in dils:
        if dl > 1:
            scratch += [pltpu.VMEM((width // LANES, tm, LANES), F32), pltpu.VMEM((tm, LANES), F32)]
        else:
            scratch += [pltpu.VMEM((8, LANES), F32), pltpu.VMEM((8, LANES), F32)]
    outs = [o.reshape(bsz, dl, seq // dl, width) for o, dl in zip(outs, dils)]
    lses = [l.reshape(bsz, dl, seq // dl, LANES) for l, dl in zip(lses, dils)]
    return pl.pallas_call(
        functools.partial(_dilated_merge_kernel, tuple(dils)),
        out_shape=(jax.ShapeDtypeStruct((m, d), F32), jax.ShapeDtypeStruct((m, d), BF16)),
        grid=(bsz, n_j),
        in_specs=o_specs + l_specs
                 + [pl.BlockSpec((LANES, width), fixed), pl.BlockSpec((width, d), fixed),
                    pl.BlockSpec((tm, d), row), pl.BlockSpec((1, d), fixed), pl.BlockSpec((1, d), fixed)],
        out_specs=(pl.BlockSpec((tm, d), row), pl.BlockSpec((tm, d), row)),
        scratch_shapes=scratch,
        compiler_params=_params("parallel", "parallel"),
        name="dilated_merge",
    )(*outs, *lses, expand, w_o.astype(BF16), h, g.reshape(1, d), b.reshape(1, d))


def dilated_mixer(hb_by_dil, w_in, w_o, rel_bias, h, g, b, bsz, seq):
    width = B_HEADS * B_HEAD_DIM
    outs, lses, dils = [], [], []
    for gi, (window, dil) in enumerate(B_GROUPS):
        assert window // dil == B_N and seq % window == 0
        w_g = w_in[:, gi * 3 * width:(gi + 1) * 3 * width]
        w_g = w_g.at[:, :width].multiply(B_HEAD_DIM ** -0.5 * LOG2E)
        proj = matmul(hb_by_dil[dil], w_g.astype(BF16), BF16, tn=3 * width // 2)
        o, lse = dilated_group(proj, dilated_bias(rel_bias, dil), bsz * dil, seq // dil)
        outs.append(o)
        lses.append(lse)
        dils.append(dil)
    return dilated_merge(outs, lses, dils, w_o, h, g, b, bsz, seq)


C_PAD = 128


def _mla_prep_kernel(x_ref, win_ref, qn_ref, kvn_ref, wqa_ref, wqb_ref, wka_ref, wv_ref,
                     cos_ref, sin_ref, vone_ref, q_ref, k_ref, v_ref):
    c = jnp.dot(x_ref[...], win_ref[...], preferred_element_type=F32)
    cos, sin = cos_ref[...], sin_ref[...]

    def rms(v, gain):
        return (v * lax.rsqrt(jnp.mean(v * v, axis=-1, keepdims=True) + RMS_EPS) * gain).astype(BF16)

    nq = rms(c[:, :C_Q_RANK], qn_ref[...])
    nkv = rms(c[:, C_Q_RANK:C_Q_RANK + C_KV_RANK], kvn_ref[...])
    o = C_Q_RANK + C_KV_RANK
    k_rope = c[:, o:o + C_PAD] * cos + c[:, o + C_PAD:o + 2 * C_PAD] * sin
    qa = jnp.dot(nq, wqa_ref[...], preferred_element_type=F32)
    qb = jnp.dot(nq, wqb_ref[...], preferred_element_type=F32)
    kn = jnp.dot(nkv, wka_ref[...], preferred_element_type=F32)
    for h in range(C_HEADS):
        hs = slice(h * C_PAD, (h + 1) * C_PAD)
        q_ref[:, hs] = (qa[:, hs] * cos + qb[:, hs] * sin).astype(BF16)
        k_ref[:, hs] = (kn[:, hs] + k_rope).astype(BF16)
    v_ref[...] = (jnp.dot(nkv, wv_ref[...], preferred_element_type=F32) + vone_ref[...]).astype(BF16)


def _rot_half_cols(w):
    half = w.shape[-1] // 2
    return jnp.concatenate([-w[..., half:], w[..., :half]], axis=-1)


def mla_prep(hb, w_in, q_norm, kv_norm, w_q_up, w_kv_up, bsz, seq, tm=512):
    d = w_in.shape[0]
    scale = (C_NOPE + C_ROPE) ** -0.5 * LOG2E
    pad_r = C_PAD - C_NOPE - C_ROPE
    w_kr = w_in[:, C_Q_RANK + C_KV_RANK:]

    def rope_slot(w):
        return jnp.pad(w, ((0, 0), (C_NOPE, pad_r)))

    win = jnp.concatenate([w_in[:, :C_Q_RANK + C_KV_RANK], rope_slot(w_kr),
                           rope_slot(_rot_half_cols(w_kr))], axis=1).astype(BF16)
    wq = w_q_up.reshape(C_Q_RANK, C_HEADS, C_NOPE + C_ROPE) * scale
    wqa = jnp.pad(wq, ((0, 0), (0, 0), (0, pad_r))).reshape(C_Q_RANK, C_HEADS * C_PAD).astype(BF16)
    wqb = jnp.pad(_rot_half_cols(wq[..., C_NOPE:]), ((0, 0), (0, 0), (C_NOPE, pad_r)))
    wqb = wqb.reshape(C_Q_RANK, C_HEADS * C_PAD).astype(BF16)
    wkv = w_kv_up.reshape(C_KV_RANK, C_HEADS, C_NOPE + C_V)
    wka = jnp.pad(wkv[..., :C_NOPE], ((0, 0), (0, 0), (0, C_PAD - C_NOPE)))
    wka = wka.reshape(C_KV_RANK, C_HEADS * C_PAD).astype(BF16)
    wv = jnp.pad(wkv[..., C_NOPE:], ((0, 0), (0, 0), (0, LANES - C_V)))
    wv = wv.reshape(C_KV_RANK, C_HEADS * LANES).astype(BF16)
    v_ones = jnp.tile(jnp.concatenate([jnp.zeros((1, C_V), F32), jnp.ones((1, LANES - C_V), F32)], axis=1),
                      (1, C_HEADS))

    half = C_ROPE // 2
    inv = ROPE_BASE ** (-jnp.arange(half, dtype=F32) / half)
    ang = jnp.arange(seq, dtype=F32)[:, None] * inv[None, :]
    ones, zeros = jnp.ones((seq, C_NOPE), F32), jnp.zeros((seq, pad_r), F32)
    cos = jnp.concatenate([ones, jnp.cos(ang), jnp.cos(ang), zeros], axis=1)
    sin = jnp.concatenate([0 * ones, jnp.sin(ang), jnp.sin(ang), zeros], axis=1)

    m = bsz * seq
    tm = min(tm, seq)
    n_pos_blk = seq // tm
    row = lambda i: (i, 0)
    fixed = lambda i: (0, 0)
    pos = lambda i: (i % n_pos_blk, 0)
    full = lambda a: pl.BlockSpec(a.shape, fixed)
    qw, vw = C_HEADS * C_PAD, C_HEADS * LANES
    return pl.pallas_call(
        _mla_prep_kernel,
        out_shape=(jax.ShapeDtypeStruct((m, qw), BF16), jax.ShapeDtypeStruct((m, qw), BF16),
                   jax.ShapeDtypeStruct((m, vw), BF16)),
        grid=(m // tm,),
        in_specs=[pl.BlockSpec((tm, d), row), full(win),
                  pl.BlockSpec((1, C_Q_RANK), fixed), pl.BlockSpec((1, C_KV_RANK), fixed),
                  full(wqa), full(wqb), full(wka), full(wv),
                  pl.BlockSpec((tm, C_PAD), pos), pl.BlockSpec((tm, C_PAD), pos), full(v_ones)],
        out_specs=(pl.BlockSpec((tm, qw), row), pl.BlockSpec((tm, qw), row),
                   pl.BlockSpec((tm, vw), row)),
        compiler_params=_params("parallel"),
        name="mla_prep",
    )(hb, win, q_norm.reshape(1, -1), kv_norm.reshape(1, -1), wqa, wqb, wka, wv, cos, sin, v_ones)


MLA_TQ = 512


MLA_SUB = 32


def _mla_attn_kernel(q_ref, k_ref, v_ref, o_ref, s_sc, p_sc, m_sc, a_sc, acc_sc):
    i = pl.program_id(2)
    tq = MLA_TQ
    n_tile = tq // LANES
    row_in_sub = lax.broadcasted_iota(I32, (MLA_SUB, LANES), 0)
    col_in_tile = lax.broadcasted_iota(I32, (MLA_SUB, LANES), 1)
    outs = []
    for hh in range(2):
        q = q_ref[0, :, hh * C_PAD:(hh + 1) * C_PAD]
        m_sc[...] = jnp.full(m_sc.shape, MASK_NEG, F32)
        acc_sc[...] = jnp.zeros(acc_sc.shape, F32)

        def step(c, diag):
            k0 = pl.multiple_of(c * tq, tq)
            kc = k_ref[0, pl.ds(k0, tq), hh * C_PAD:(hh + 1) * C_PAD]
            s_sc[...] = lax.dot_general(q, kc, (((1,), (1,)), ((), ())), preferred_element_type=F32)
            for rb in range(tq // MLA_SUB):
                r = slice(rb * MLA_SUB, (rb + 1) * MLA_SUB)
                t = [s_sc[r, jt * LANES:(jt + 1) * LANES] for jt in range(n_tile)]
                if diag:
                    t = [jnp.where(col_in_tile + jt * LANES <= row_in_sub + rb * MLA_SUB, t[jt], MASK_NEG)
                         for jt in range(n_tile)]
                m_old = m_sc[r, :]
                m_new = jnp.maximum(m_old, jnp.max(functools.reduce(jnp.maximum, t), axis=-1, keepdims=True))
                a_sc[r, :] = jnp.exp2(m_old - m_new)
                m_sc[r, :] = m_new
                for jt in range(n_tile):
                    p_sc[r, jt * LANES:(jt + 1) * LANES] = jnp.exp2(t[jt] - m_new).astype(BF16)
            pv = jnp.dot(p_sc[...], v_ref[0, pl.ds(k0, tq), hh * LANES:(hh + 1) * LANES],
                         preferred_element_type=F32)
            acc_sc[...] = acc_sc[...] * a_sc[...] + pv

        def body(c, carry):
            step(c, False)
            return carry

        lax.fori_loop(0, i, body, 0)
        step(i, True)
        outs.append(acc_sc[:, :C_V] / acc_sc[:, C_V:])
    o_ref[0] = jnp.concatenate(outs, axis=-1).astype(BF16)


def mla_attention(q, k, v, bsz, seq):
    qw, vw = C_HEADS * C_PAD, C_HEADS * C_V
    return pl.pallas_call(
        _mla_attn_kernel,
        out_shape=jax.ShapeDtypeStruct((bsz, seq, vw), BF16),
        grid=(bsz, C_HEADS // 2, seq // MLA_TQ),
        in_specs=[pl.BlockSpec((1, MLA_TQ, 2 * C_PAD), lambda b, hp, i: (b, i, hp)),
                  pl.BlockSpec((1, seq, 2 * C_PAD), lambda b, hp, i: (b, 0, hp)),
                  pl.BlockSpec((1, seq, 2 * LANES), lambda b, hp, i: (b, 0, hp))],
        out_specs=pl.BlockSpec((1, MLA_TQ, 2 * C_V), lambda b, hp, i: (b, i, hp)),
        scratch_shapes=[pltpu.VMEM((MLA_TQ, MLA_TQ), F32), pltpu.VMEM((MLA_TQ, MLA_TQ), BF16),
                        pltpu.VMEM((MLA_TQ, LANES), F32), pltpu.VMEM((MLA_TQ, LANES), F32),
                        pltpu.VMEM((MLA_TQ, LANES), F32)],
        compiler_params=_params("parallel", "parallel", "arbitrary"),
        name="mla_attention",
    )(q.reshape(bsz, seq, qw), k.reshape(bsz, seq, qw), v.reshape(bsz, seq, C_HEADS * LANES))


def mla_mixer(hb, w_in, q_norm, kv_norm, w_q_up, w_kv_up, w_o, h, g, b, bsz, seq):
    q, k, v = mla_prep(hb, w_in, q_norm, kv_norm, w_q_up, w_kv_up, bsz, seq)
    o = mla_attention(q, k, v, bsz, seq)
    return matmul_residual_ln(o.reshape(bsz * seq, -1), w_o.astype(BF16), h, g, b)


def _router_kernel(h_ref, wr_ref, eb_ref, eidx_ref, gate_ref):
    tm = h_ref.shape[0]
    per = N_EXPERTS // N_GROUPS
    logits = lax.dot_general(wr_ref[...], h_ref[...], (((1,), (1,)), ((), ())),
                             precision=lax.Precision.HIGHEST, preferred_element_type=F32)
    scores = 1.0 / (1.0 + jnp.exp(-logits))
    s3 = scores.reshape(N_GROUPS, per, tm)
    c3 = (scores + eb_ref[...]).reshape(N_GROUPS, per, tm)
    neg_inf = -jnp.inf
    j_idx = lax.broadcasted_iota(I32, (N_GROUPS, per, tm), 1).astype(F32)
    g_idx = lax.broadcasted_iota(I32, (N_GROUPS, per, tm), 0).astype(F32)
    flat = g_idx * per + j_idx

    m1 = jnp.max(c3, axis=1, keepdims=True)
    first = jnp.min(jnp.where(c3 == m1, j_idx, per), axis=1, keepdims=True)
    m2 = jnp.max(jnp.where(j_idx == first, neg_inf, c3), axis=1, keepdims=True)
    gs = m1 + m2

    gi = lax.broadcasted_iota(I32, (N_GROUPS, 1, tm), 0).astype(F32)
    keep = jnp.zeros((N_GROUPS, 1, tm), jnp.bool_)
    cur = gs
    for _ in range(TOPK_GROUPS):
        mx = jnp.max(cur, axis=0, keepdims=True)
        pick = gi == jnp.min(jnp.where(cur == mx, gi, N_GROUPS), axis=0, keepdims=True)
        keep = keep | pick
        cur = jnp.where(pick, neg_inf, cur)
    cur = jnp.where(keep, c3, neg_inf)

    idxs, gates = [], []
    for _ in range(TOP_K):
        mx = jnp.max(jnp.max(cur, axis=1, keepdims=True), axis=0, keepdims=True)
        cand = jnp.where(cur == mx, flat, N_EXPERTS)
        fi = jnp.min(jnp.min(cand, axis=1, keepdims=True), axis=0, keepdims=True)
        pick = flat == fi
        gates.append(jnp.sum(jnp.sum(jnp.where(pick, s3, 0.0), axis=1, keepdims=True), axis=0))
        idxs.append(fi[0])
        cur = jnp.where(pick, neg_inf, cur)
    gate = jnp.concatenate(gates, axis=0)
    gate = gate / jnp.sum(gate, axis=0, keepdims=True) * ROUTED_SCALE
    eidx_ref[...] = jnp.concatenate(idxs, axis=0).astype(I32)
    gate_ref[...] = gate


def moe_router(h, w_router, e_bias, row0, m, tm=512):
    d = h.shape[1]
    tm = min(tm, m)
    blk0 = row0 // tm
    return pl.pallas_call(
        _router_kernel,
        out_shape=(jax.ShapeDtypeStruct((TOP_K, m), I32), jax.ShapeDtypeStruct((TOP_K, m), F32)),
        grid=(m // tm,),
        in_specs=[pl.BlockSpec((tm, d), lambda i: (i + blk0, 0)),
                  pl.BlockSpec((N_EXPERTS, d), lambda i: (0, 0)),
                  pl.BlockSpec((N_EXPERTS, 1), lambda i: (0, 0))],
        out_specs=(pl.BlockSpec((TOP_K, tm), lambda i: (0, i)),
                   pl.BlockSpec((TOP_K, tm), lambda i: (0, i))),
        compiler_params=_params("parallel"),
        name="moe_router",
    )(h, w_router.T, e_bias.reshape(N_EXPERTS, 1))


def _silu(x):
    return x / (1.0 + jnp.exp(-x))


RANK_TM = 1024


def _rank_kernel(eidx_ref, tri_ref, dest_ref, cnt_ref, start_ref, cnt_sc, run_sc):
    p, i = pl.program_id(0), pl.program_id(1)
    tm = eidx_ref.shape[1]
    e = eidx_ref[...]
    ex = lax.broadcasted_iota(I32, (N_EXPERTS, tm), 0)
    onehot = jnp.zeros((N_EXPERTS, tm), F32)
    for k in range(TOP_K):
        onehot = onehot + jnp.where(e[k:k + 1, :] == ex, 1.0, 0.0)
    tile_cnt = jnp.sum(onehot, axis=-1, keepdims=True)

    @pl.when((p == 0) & (i == 0))
    def _():
        cnt_sc[...] = jnp.zeros(cnt_sc.shape, F32)

    @pl.when(p == 0)
    def _():
        cnt_sc[...] += tile_cnt

    @pl.when((p == 1) & (i == 0))
    def _():
        cnt = cnt_sc[...]
        padded = jnp.floor((cnt + (MOE_BLK - 1)) * (1.0 / MOE_BLK)) * MOE_BLK
        below = (lax.broadcasted_iota(I32, (N_EXPERTS, N_EXPERTS), 1)
                 < lax.broadcasted_iota(I32, (N_EXPERTS, N_EXPERTS), 0)).astype(F32)
        start = jnp.dot(below, jnp.broadcast_to(padded, (N_EXPERTS, LANES)),
                        precision=lax.Precision.HIGHEST, preferred_element_type=F32)
        run_sc[...] = start[:, :1]
        cnt_ref[...] = jnp.broadcast_to(cnt, (N_EXPERTS, LANES))
        start_ref[...] = start

    @pl.when(p == 1)
    def _():
        before = jnp.dot(onehot.astype(BF16), tri_ref[...], preferred_element_type=F32) + run_sc[...]
        rows = [jnp.sum(jnp.where(e[k:k + 1, :] == ex, before, 0.0), axis=0, keepdims=True)
                for k in range(TOP_K)]
        dest_ref[...] = jnp.concatenate(rows, axis=0).astype(I32)
        run_sc[...] += tile_cnt


def moe_rank(eidx):
    n_tok = eidx.shape[1]
    tm = min(RANK_TM, n_tok)
    tri = (jnp.arange(tm)[:, None] < jnp.arange(tm)[None, :]).astype(BF16)
    stat = jax.ShapeDtypeStruct((N_EXPERTS, LANES), F32)
    dest, cnt, start = pl.pallas_call(
        _rank_kernel,
        out_shape=(jax.ShapeDtypeStruct((TOP_K, n_tok), I32), stat, stat),
        grid=(2, n_tok // tm),
        in_specs=[pl.BlockSpec((TOP_K, tm), lambda p, i: (0, i)),
                  pl.BlockSpec((tm, tm), lambda p, i: (0, 0))],
        out_specs=(pl.BlockSpec((TOP_K, tm), lambda p, i: (0, i * p)),
                   pl.BlockSpec((N_EXPERTS, LANES), lambda p, i: (0, 0)),
                   pl.BlockSpec((N_EXPERTS, LANES), lambda p, i: (0, 0))),
        scratch_shapes=[pltpu.VMEM((N_EXPERTS, 1), F32), pltpu.VMEM((N_EXPERTS, 1), F32)],
        compiler_params=_params("arbitrary", "arbitrary"),
        name="moe_rank",
    )(eidx, tri)
    return dest, cnt[:, 0].astype(I32), start[:, 0].astype(I32)


def _experts_kernel(blk_e_ref, x_ref, wg_ref, wu_ref, wd_ref, o_ref, wg_sc, wu_sc, wd_sc):
    i = pl.program_id(0)

    @pl.when((i == 0) | (blk_e_ref[i] != blk_e_ref[jnp.maximum(i - 1, 0)]))
    def _():
        wg_sc[...] = wg_ref[...].astype(BF16)
        wu_sc[...] = wu_ref[...].astype(BF16)
        wd_sc[...] = wd_ref[...].astype(BF16)

    x = x_ref[...]
    gate = jnp.dot(x, wg_sc[...], preferred_element_type=F32)
    up = jnp.dot(x, wu_sc[...], preferred_element_type=F32)
    act = (_silu(gate) * up).astype(BF16)
    o_ref[...] = jnp.dot(act, wd_sc[...], preferred_element_type=F32).astype(o_ref.dtype)


def grouped_experts(x_sorted, blk_e, wg, wu, wd, layer):
    n_slot, d = x_sorted.shape
    n_blk = n_slot // MOE_BLK
    by_expert = lambda i, be: (layer, be[i], 0, 0)
    grid_spec = pltpu.PrefetchScalarGridSpec(
        num_scalar_prefetch=1,
        grid=(n_blk,),
        in_specs=[pl.BlockSpec((MOE_BLK, d), lambda i, be: (i, 0)),
                  pl.BlockSpec((None, None, d, D_EXPERT), by_expert),
                  pl.BlockSpec((None, None, d, D_EXPERT), by_expert),
                  pl.BlockSpec((None, None, D_EXPERT, d), by_expert)],
        out_specs=pl.BlockSpec((MOE_BLK, d), lambda i, be: (i, 0)),
        scratch_shapes=[pltpu.VMEM((d, D_EXPERT), BF16), pltpu.VMEM((d, D_EXPERT), BF16),
                        pltpu.VMEM((D_EXPERT, d), BF16)],
    )
    return pl.pallas_call(
        _experts_kernel,
        out_shape=jax.ShapeDtypeStruct((n_slot, d), BF16),
        grid_spec=grid_spec,
        compiler_params=_params("arbitrary"),
        name="grouped_experts",
    )(blk_e, x_sorted, wg, wu, wd)


def _moe_finish_kernel(dils, hb_ref, y_ref, gt_ref, wg_ref, wu_ref, wd_ref, h_ref, g_ref, b_ref,
                       o_ref, ob_ref, *rest):
    x = hb_ref[...]
    d = x.shape[1]
    act = (_silu(jnp.dot(x, wg_ref[...], preferred_element_type=F32))
           * jnp.dot(x, wu_ref[...], preferred_element_type=F32)).astype(BF16)
    ff = jnp.dot(act, wd_ref[...], preferred_element_type=F32)
    gt = gt_ref[...]
    for k in range(TOP_K):
        ff = ff + gt[:, k:k + 1] * y_ref[k].astype(F32)
    out = _layer_norm_rows(DN_ALPHA * h_ref[...] + ff, g_ref[...], b_ref[...])
    o_ref[...] = out
    ob_ref[...] = out.astype(BF16)
    if dils:
        perm_refs, nat_sc = rest[:-1], rest[-1]
        for j in range(d // LANES):
            nat_sc[j] = out[:, j * LANES:(j + 1) * LANES]
        for p_ref, dil in zip(perm_refs, dils):
            n = out.shape[0] // dil
            for r in range(dil):
                for j in range(d // LANES):
                    p_ref[0, r, :, j * LANES:(j + 1) * LANES] = (
                        nat_sc[j, pl.ds(r, n, stride=dil), :].astype(BF16))


def moe_finish(hb, y_tok, gate_tok, ws_g, ws_u, ws_d, h, g, b, seq, row0, m, dils=(), tm=256):
    d = h.shape[1]
    tm = min(tm, seq)
    n_j = seq // tm
    blk0 = row0 // tm
    row = lambda i: (i, 0)
    row_in = lambda i: (i + blk0, 0)
    fixed = lambda i: (0, 0)
    out_shape = [jax.ShapeDtypeStruct((m, d), F32), jax.ShapeDtypeStruct((m, d), BF16)]
    out_specs = [pl.BlockSpec((tm, d), row), pl.BlockSpec((tm, d), row)]
    for dil in dils:
        out_shape.append(jax.ShapeDtypeStruct((m // seq, dil, seq // dil, d), BF16))
        out_specs.append(pl.BlockSpec((1, dil, tm // dil, d), lambda i: (i // n_j, 0, i % n_j, 0)))
    res = pl.pallas_call(
        functools.partial(_moe_finish_kernel, tuple(dils)),
        out_shape=tuple(out_shape),
        grid=(m // tm,),
        in_specs=[pl.BlockSpec((tm, d), row_in), pl.BlockSpec((TOP_K, tm, d), lambda i: (0, i, 0)),
                  pl.BlockSpec((tm, TOP_K), row),
                  pl.BlockSpec(ws_g.shape, fixed), pl.BlockSpec(ws_u.shape, fixed),
                  pl.BlockSpec(ws_d.shape, fixed),
                  pl.BlockSpec((tm, d), row_in), pl.BlockSpec((1, d), fixed), pl.BlockSpec((1, d), fixed)],
        out_specs=tuple(out_specs),
        scratch_shapes=[pltpu.VMEM((d // LANES, tm, LANES), F32)] if dils else [],
        compiler_params=_params("parallel"),
        name="moe_finish",
    )(hb, y_tok, gate_tok, ws_g, ws_u, ws_d, h, g.reshape(1, d), b.reshape(1, d))
    return res[0], res[1], [r.reshape(m, d) for r in res[2:]]


SORT_E_SHIFT = 19


MOE_SPLITS = 2


def moe_layer(h, hb, w_router, e_bias, wg, wu, wd, layer, ws_g, ws_u, ws_d, g, b, seq, dils=()):
    n_all = h.shape[0]
    n_split = MOE_SPLITS if (n_all // seq) % MOE_SPLITS == 0 else 1
    parts = [moe_tokens(h, hb, w_router, e_bias, wg, wu, wd, layer, ws_g, ws_u, ws_d, g, b, seq, dils,
                        sp * (n_all // n_split), n_all // n_split) for sp in range(n_split)]
    cat = lambda xs: xs[0] if len(xs) == 1 else jnp.concatenate(xs, axis=0)
    return (cat([p[0] for p in parts]), cat([p[1] for p in parts]),
            [cat([p[2][j] for p in parts]) for j in range(len(dils))])


def moe_tokens(h, hb, w_router, e_bias, wg, wu, wd, layer, ws_g, ws_u, ws_d, g, b, seq, dils, row0, n_tok):
    d = h.shape[1]
    n_asg = n_tok * TOP_K
    n_pad = N_EXPERTS * MOE_BLK
    assert n_asg <= 1 << (SORT_E_SHIFT - 1) and n_pad <= 1 << (SORT_E_SHIFT - 1)
    eidx, gate = moe_router(h, w_router, e_bias, row0, n_tok)
    dest, counts, pad_start = moe_rank(eidx)
    padded = (counts + MOE_BLK - 1) // MOE_BLK * MOE_BLK
    pad_end = pad_start + padded
    n_blk = n_asg // MOE_BLK + N_EXPERTS
    blk_start = jnp.arange(n_blk, dtype=I32) * MOE_BLK
    blk_e = jnp.minimum(jnp.sum(pad_end[None, :] <= blk_start[:, None], axis=1), N_EXPERTS - 1)

    key_real = (eidx.T.reshape(-1) << SORT_E_SHIFT) | jnp.arange(n_asg, dtype=I32)
    i_pad = jnp.arange(n_pad, dtype=I32)
    e_pad = jnp.sum(jnp.cumsum(padded - counts)[None, :] <= i_pad[:, None], axis=1).astype(I32)
    key_pad = (e_pad << SORT_E_SHIFT) | (1 << (SORT_E_SHIFT - 1)) | i_pad
    low = jnp.sort(jnp.concatenate([key_real, key_pad])) & ((1 << SORT_E_SHIFT) - 1)
    slot_tok = jnp.where(low < (1 << (SORT_E_SHIFT - 1)), low // TOP_K,
                         jnp.arange(n_asg + n_pad, dtype=I32) % n_tok)

    x_sorted = hb.at[slot_tok + row0].get(mode="promise_in_bounds")
    y = grouped_experts(x_sorted, blk_e.astype(I32), wg, wu, wd, layer)
    y_k = y.at[dest.reshape(-1)].get(mode="promise_in_bounds").reshape(TOP_K, n_tok, d)
    return moe_finish(hb, y_k, gate.T, ws_g, ws_u, ws_d, h, g, b, seq, row0, n_tok, dils)


def kernel(x, rel_bias, a_w_in, a_w_o, b_w_in, b_w_o, c_w_in, c_q_norm, c_kv_norm, c_w_q_up,
           c_w_kv_up, c_w_o, ln_g, ln_b, moe_w_router, moe_bias, moe_w_gate, moe_w_up,
           moe_w_down, moe_ws_gate, moe_ws_up, moe_ws_down):
    bsz, seq, d = x.shape
    depth = ln_g.shape[0]
    h = x.reshape(bsz * seq, d)
    hb = h.astype(BF16)
    strip = dsa_bias_strip(rel_bias, seq)
    extra_dils = tuple(dil for _, dil in B_GROUPS if dil > 1)
    hb_perm = []
    for layer in range(depth):
        kind, slot = layer % N_MIXERS, layer // N_MIXERS
        g0, b0 = ln_g[layer, 0], ln_b[layer, 0]
        if kind == 0:
            h, hb = dsa_mixer(hb, a_w_in[slot], a_w_o[slot], strip, h, g0, b0, bsz, seq)
        elif kind == 1:
            hb_by_dil = {1: hb, **dict(zip(extra_dils, hb_perm))}
            h, hb = dilated_mixer(hb_by_dil, b_w_in[slot], b_w_o[slot], rel_bias, h, g0, b0, bsz, seq)
        else:
            h, hb = mla_mixer(hb, c_w_in[slot], c_q_norm[slot], c_kv_norm[slot], c_w_q_up[slot],
                              c_w_kv_up[slot], c_w_o[slot], h, g0, b0, bsz, seq)
        next_dilated = layer + 1 < depth and (layer + 1) % N_MIXERS == 1
        h, hb, hb_perm = moe_layer(h, hb, moe_w_router[layer], moe_bias[layer],
                                   moe_w_gate, moe_w_up, moe_w_down, layer,
                                   moe_ws_gate[layer].astype(BF16),
                                   moe_ws_up[layer].astype(BF16), moe_ws_down[layer].astype(BF16),
                                   ln_g[layer, 1], ln_b[layer, 1], seq,
                                   extra_dils if next_dilated else ())
    return h.reshape(bsz, seq, d)
```

```python
import functools
import math

import jax
import jax.numpy as jnp
from jax import lax
from jax.experimental import pallas as pl
from jax.experimental.pallas import tpu as pltpu

F32 = jnp.float32
BF16 = jnp.bfloat16
I32 = jnp.int32

LANES = 128
VMEM_LIMIT_BYTES = 56 * 1024 * 1024

D_MODEL = 1024
DEPTH = 4
N_MIXERS = 3
NORM_EPS = 1e-5
RMS_EPS = 1e-6
REL_BUCKETS = 32
REL_MAX_DIST = 2048
A_HEADS = 16
A_HEAD_DIM = 128
A_IDX_HEADS = 8
A_IDX_DIM = 64
A_TOPK_MAX = 256
A_Q = A_HEADS * A_HEAD_DIM
B_GROUPS = ((128, 1), (512, 4), (2048, 16))
B_HEADS = 16
B_HEAD_DIM = 64
B_N = 128
C_HEADS = 16
C_Q_RANK = 256
C_KV_RANK = 128
C_NOPE = 64
C_ROPE = 32
C_V = 64
ROPE_BASE = 10000.0
N_EXPERTS = 64
TOP_K = 8
N_GROUPS = 8
TOPK_GROUPS = 4
D_EXPERT = 256
ROUTED_SCALE = 2.5
DN_ALPHA = (2 * DEPTH) ** 0.25

Q_BLK = 128
SEL_BLK = 512
KEY_CHUNK = 512
MOE_BLK = 512
MASK_NEG = -1e30
INT_MIN = -(2 ** 31)


def _params(*sem):
    return pltpu.CompilerParams(dimension_semantics=sem, vmem_limit_bytes=VMEM_LIMIT_BYTES)


def _mm_kernel(x_ref, w_ref, o_ref):
    o_ref[...] = jnp.dot(x_ref[...], w_ref[...], preferred_element_type=F32).astype(o_ref.dtype)


def matmul(x, w, out_dtype, tm=1024, tn=None):
    m, k = x.shape
    n = w.shape[1]
    tn = n if tn is None else tn
    tm = min(tm, m)
    return pl.pallas_call(
        _mm_kernel,
        out_shape=jax.ShapeDtypeStruct((m, n), out_dtype),
        grid=(n // tn, m // tm),
        in_specs=[pl.BlockSpec((tm, k), lambda j, i: (i, 0)),
                  pl.BlockSpec((k, tn), lambda j, i: (0, j))],
        out_specs=pl.BlockSpec((tm, tn), lambda j, i: (i, j)),
        compiler_params=_params("parallel", "parallel"),
        name="matmul",
    )(x, w)


def _layer_norm_rows(z, g, b):
    mu = jnp.mean(z, axis=-1, keepdims=True)
    zc = z - mu
    var = jnp.mean(zc * zc, axis=-1, keepdims=True)
    return zc * lax.rsqrt(var + NORM_EPS) * g + b


def _mm_ln_kernel(x_ref, w_ref, h_ref, g_ref, b_ref, o_ref, ob_ref):
    y = jnp.dot(x_ref[...], w_ref[...], preferred_element_type=F32)
    out = _layer_norm_rows(DN_ALPHA * h_ref[...] + y, g_ref[...], b_ref[...])
    o_ref[...] = out
    ob_ref[...] = out.astype(BF16)


def matmul_residual_ln(x, w, h, g, b, tm=512):
    m, k = x.shape
    d = w.shape[1]
    tm = min(tm, m)
    row = lambda i: (i, 0)
    fixed = lambda i: (0, 0)
    return pl.pallas_call(
        _mm_ln_kernel,
        out_shape=(jax.ShapeDtypeStruct((m, d), F32), jax.ShapeDtypeStruct((m, d), BF16)),
        grid=(m // tm,),
        in_specs=[pl.BlockSpec((tm, k), row), pl.BlockSpec((k, d), fixed),
                  pl.BlockSpec((tm, d), row), pl.BlockSpec((1, d), fixed),
                  pl.BlockSpec((1, d), fixed)],
        out_specs=(pl.BlockSpec((tm, d), row), pl.BlockSpec((tm, d), row)),
        compiler_params=_params("parallel"),
        name="matmul_residual_ln",
    )(x, w, h, g.reshape(1, d), b.reshape(1, d))


def _t5_bucket(dist):
    exact = REL_BUCKETS // 2
    d_f = jnp.maximum(dist, 1).astype(F32)
    large = exact + (jnp.log(d_f / exact) / math.log(REL_MAX_DIST / exact)
                     * (REL_BUCKETS - exact)).astype(I32)
    return jnp.where(dist < exact, dist, jnp.minimum(large, REL_BUCKETS - 1))


def _bias_by_distance(rel_bias, dist):
    return rel_bias[_t5_bucket(dist)]


def _toeplitz(f, rows, cols, off):
    length = f.shape[-1]
    period = rows + cols - 1
    u = f[:, jnp.clip(off + rows - 1 - jnp.arange(period), 0, length - 1)]
    skew = jnp.tile(u, (1, rows + 1))[:, :rows * (period + 1)].reshape(-1, rows, period + 1)
    return skew[:, ::-1, :cols]


DSA_SUB = 32
LOG2E = math.log2(math.e)


def _dsa_select(k_sel, n_keys, t0, iq_ref, ik_ref, key_sc, mask_sc, cut_sc):
    rows = lax.broadcasted_iota(I32, (SEL_BLK, n_keys), 0) + t0
    cols = lax.broadcasted_iota(I32, (SEL_BLK, n_keys), 1)
    valid = cols <= rows

    ik = ik_ref[0, :n_keys, :].astype(BF16)
    w_all = iq_ref[0, :, A_IDX_HEADS * LANES:] * ((A_IDX_DIM * A_IDX_HEADS) ** -0.5)
    score = jnp.zeros((SEL_BLK, n_keys), F32)
    for h in range(A_IDX_HEADS):
        qh = iq_ref[0, :, h * LANES:(h + 1) * LANES].astype(BF16)
        rel = lax.dot_general(qh, ik, (((1,), (1,)), ((), ())), preferred_element_type=F32)
        score = score + w_all[:, A_IDX_DIM + h:A_IDX_DIM + h + 1] * jnp.maximum(rel, 0.0)
    score = jnp.where(score == 0.0, 0.0, score)

    bits = pltpu.bitcast(score, I32)
    key_sc[:, :n_keys] = jnp.where(valid, bits ^ ((bits >> 31) & 0x7FFFFFFF), INT_MIN)

    def search(it, ans_u):
        cand_u = ans_u | lax.shift_left(jnp.int32(1), 31 - it)
        cand_s = cand_u ^ INT_MIN
        cnt = jnp.sum(jnp.where(key_sc[:, :n_keys] >= cand_s, 1.0, 0.0), axis=-1, keepdims=True)
        return jnp.where(cnt >= k_sel, cand_u, ans_u)

    thr = lax.fori_loop(0, 32, search, jnp.zeros((SEL_BLK, 1), I32)) ^ INT_MIN

    key = key_sc[:, :n_keys]
    gt = key > thr
    eq = key == thr
    need = k_sel - jnp.sum(jnp.where(gt, 1.0, 0.0), axis=-1, keepdims=True)
    n_eq = jnp.sum(jnp.where(eq, 1.0, 0.0), axis=-1, keepdims=True)
    cut_sc[...] = jnp.full((SEL_BLK, 1), n_keys, I32)
    surplus = jnp.where((n_eq > need) & (thr != INT_MIN), 1.0, 0.0)

    @pl.when(jnp.max(surplus) > 0.0)
    def _():
        def tie_search(it, ans):
            cand = ans | lax.shift_left(jnp.int32(1), (n_keys.bit_length() - 1) - it)
            hit = (key_sc[:, :n_keys] == thr) & (cols < cand)
            cnt = jnp.sum(jnp.where(hit, 1.0, 0.0), axis=-1, keepdims=True)
            return jnp.where(cnt < need, cand, ans)
        cut_sc[...] = lax.fori_loop(0, n_keys.bit_length(), tie_search, jnp.zeros((SEL_BLK, 1), I32))

    selected = valid & (gt | (eq & (cols <= cut_sc[...])))
    mask_sc[:, :n_keys] = jnp.where(selected, 0.0, MASK_NEG)


def _dsa_kernel(k_sel, seq, q_ref, kv_ref, iq_ref, ik_ref, strip_ref, o_ref,
                key_sc, mask_sc, cut_sc, qs_sc, s_sc, p_sc, acc_sc, m_sc, a_sc, ve_sc):
    i = pl.program_id(1)
    t0 = i * SEL_BLK
    blk_per_chunk = KEY_CHUNK // Q_BLK

    @pl.when(i == 0)
    def _():
        ve_sc[:, :A_HEAD_DIM] = kv_ref[0, :, A_HEAD_DIM:]
        ve_sc[:, A_HEAD_DIM:] = jnp.ones((seq, A_HEAD_DIM), BF16)

    for j in range(seq // KEY_CHUNK):
        @pl.when(t0 // KEY_CHUNK == j)
        def _(j=j):
            _dsa_select(k_sel, (j + 1) * KEY_CHUNK, t0, iq_ref, ik_ref, key_sc, mask_sc, cut_sc)

    n_strip_blk = seq // Q_BLK - 1
    n_tile = KEY_CHUNK // LANES

    def attend(sub, carry):
        _dsa_attend(sub, i * (SEL_BLK // Q_BLK) + sub)
        return carry

    def _dsa_attend(sub, qi):
        r0 = pl.multiple_of(sub * Q_BLK, Q_BLK)
        for h in range(A_HEADS):
            qs_sc[h * Q_BLK:(h + 1) * Q_BLK, :] = q_ref[0, pl.ds(r0, Q_BLK),
                                                        h * A_HEAD_DIM:(h + 1) * A_HEAD_DIM]
        m_sc[...] = jnp.full(m_sc.shape, MASK_NEG, F32)
        acc_sc[...] = jnp.zeros(acc_sc.shape, F32)
        lax.fori_loop(0, (qi * Q_BLK + Q_BLK + KEY_CHUNK - 1) // KEY_CHUNK,
                      functools.partial(chunk, r0, qi), 0)
        for h in range(A_HEADS):
            r = slice(h * Q_BLK, (h + 1) * Q_BLK)
            o_ref[0, pl.ds(r0, Q_BLK), h * A_HEAD_DIM:(h + 1) * A_HEAD_DIM] = (
                acc_sc[r, :A_HEAD_DIM] / acc_sc[r, A_HEAD_DIM:]).astype(BF16)

    def chunk(r0, qi, c, carry):
        k0 = pl.multiple_of(c * KEY_CHUNK, KEY_CHUNK)
        kc = kv_ref[0, pl.ds(k0, KEY_CHUNK), :A_HEAD_DIM]
        s_sc[...] = lax.dot_general(qs_sc[...], kc, (((1,), (1,)), ((), ())),
                                    preferred_element_type=F32)
        w0 = (c * blk_per_chunk - qi + n_strip_blk) * Q_BLK
        for h in range(A_HEADS):
            for rb in range(Q_BLK // DSA_SUB):
                qr = slice(rb * DSA_SUB, (rb + 1) * DSA_SUB)
                mr = pl.ds(pl.multiple_of(r0 + rb * DSA_SUB, DSA_SUB), DSA_SUB)
                r = slice(h * Q_BLK + rb * DSA_SUB, h * Q_BLK + (rb + 1) * DSA_SUB)
                t = []
                for jt in range(n_tile):
                    ws = pl.ds(pl.multiple_of(w0 + jt * LANES, LANES), LANES)
                    ks = pl.ds(pl.multiple_of(k0 + jt * LANES, LANES), LANES)
                    t.append(s_sc[r, jt * LANES:(jt + 1) * LANES]
                             + strip_ref[h, qr, ws].astype(F32) + mask_sc[mr, ks])
                mx = functools.reduce(jnp.maximum, t)
                m_old = m_sc[r, :]
                m_new = jnp.maximum(m_old, jnp.max(mx, axis=-1, keepdims=True))
                a_sc[r, :] = jnp.exp2(m_old - m_new)
                m_sc[r, :] = m_new
                for jt in range(n_tile):
                    p_sc[r, jt * LANES:(jt + 1) * LANES] = jnp.exp2(t[jt] - m_new).astype(BF16)
        pv = jnp.dot(p_sc[...], ve_sc[pl.ds(k0, KEY_CHUNK), :], preferred_element_type=F32)
        for half in range(2):
            hs = slice(half * A_HEAD_DIM, (half + 1) * A_HEAD_DIM)
            acc_sc[:, hs] = acc_sc[:, hs] * a_sc[...] + pv[:, hs]
        return carry

    lax.fori_loop(0, SEL_BLK // Q_BLK, attend, 0)


def dsa_attention(qkv, idx, strip, bsz, seq):
    assert seq % KEY_CHUNK == 0 and KEY_CHUNK % SEL_BLK == 0 and SEL_BLK % Q_BLK == 0
    k_sel = min(A_TOPK_MAX, seq // 4)
    rows = A_HEADS * Q_BLK
    n_kv_blk = A_Q // (2 * A_HEAD_DIM)
    n_ik_blk = A_IDX_HEADS
    idx_w = (A_IDX_HEADS + 1) * LANES
    return pl.pallas_call(
        functools.partial(_dsa_kernel, k_sel, seq),
        out_shape=jax.ShapeDtypeStruct((bsz, seq, A_Q), BF16),
        grid=(bsz, seq // SEL_BLK),
        in_specs=[pl.BlockSpec((1, SEL_BLK, A_Q), lambda b, i: (b, i, 0)),
                  pl.BlockSpec((1, seq, 2 * A_HEAD_DIM), lambda b, i: (b, 0, n_kv_blk)),
                  pl.BlockSpec((1, SEL_BLK, idx_w), lambda b, i: (b, i, 0)),
                  pl.BlockSpec((1, seq, LANES), lambda b, i: (b, 0, n_ik_blk)),
                  pl.BlockSpec(strip.shape, lambda b, i: (0, 0, 0), pipeline_mode=pl.Buffered(1))],
        out_specs=pl.BlockSpec((1, SEL_BLK, A_Q), lambda b, i: (b, i, 0)),
        scratch_shapes=[pltpu.VMEM((SEL_BLK, seq), I32),
                        pltpu.VMEM((SEL_BLK, seq), F32),
                        pltpu.VMEM((SEL_BLK, 1), I32),
                        pltpu.VMEM((rows, A_HEAD_DIM), BF16),
                        pltpu.VMEM((rows, KEY_CHUNK), F32),
                        pltpu.VMEM((rows, KEY_CHUNK), BF16),
                        pltpu.VMEM((rows, 2 * A_HEAD_DIM), F32),
                        pltpu.VMEM((rows, LANES), F32),
                        pltpu.VMEM((rows, LANES), F32),
                        pltpu.VMEM((seq, 2 * A_HEAD_DIM), BF16)],
        compiler_params=_params("parallel", "arbitrary"),
        name="dsa_attention",
    )(qkv, qkv, idx, idx, strip)


def dsa_bias_strip(rel_bias, seq):
    width = seq + KEY_CHUNK - Q_BLK
    by_dist = _bias_by_distance(rel_bias, jnp.arange(seq)).T
    return (_toeplitz(by_dist, Q_BLK, width, seq - Q_BLK) * LOG2E).astype(BF16)


def dsa_weights(w_in):
    d = w_in.shape[0]
    wq = w_in[:, :A_Q] * (A_HEAD_DIM ** -0.5 * LOG2E)
    wkv = w_in[:, A_Q:A_Q + 2 * A_HEAD_DIM]
    o = A_Q + 2 * A_HEAD_DIM
    n_qi = A_IDX_HEADS * A_IDX_DIM
    wqi = w_in[:, o:o + n_qi].reshape(d, A_IDX_HEADS, A_IDX_DIM)
    wqi = jnp.pad(wqi, ((0, 0), (0, 0), (0, LANES - A_IDX_DIM))).reshape(d, A_IDX_HEADS * LANES)
    wkw = jnp.pad(w_in[:, o + n_qi:], ((0, 0), (0, LANES - A_IDX_DIM - A_IDX_HEADS)))
    return (jnp.concatenate([wq, wkv], axis=1).astype(BF16),
            jnp.concatenate([wqi, wkw], axis=1).astype(BF16))


def dsa_mixer(hb, w_in, w_o, strip, h, g, b, bsz, seq):
    w_qkv, w_idx = dsa_weights(w_in)
    qkv = matmul(hb, w_qkv, BF16, tn=w_qkv.shape[1] // 2)
    idx = matmul(hb, w_idx, F32)
    o = dsa_attention(qkv.reshape(bsz, seq, -1), idx.reshape(bsz, seq, -1), strip, bsz, seq)
    return matmul_residual_ln(o.reshape(bsz * seq, A_Q), w_o.astype(BF16), h, g, b)


def _dilated_kernel(q_ref, kp_ref, kc_ref, vp_ref, vc_ref, bias_ref, o_ref, lse_ref):
    c = pl.program_id(1)
    col = lax.broadcasted_iota(I32, (B_N, 2 * B_N), 1)
    has_prev = (col >= B_N) | (c > 0)
    lane = lax.broadcasted_iota(I32, (B_N, LANES), 1)
    low_half = lane < B_HEAD_DIM
    lse_tile = jnp.zeros((B_N, LANES), F32)
    ones = jnp.ones((2 * B_N, LANES), BF16)
    for pr in range(B_HEADS * B_HEAD_DIM // LANES):
        ps = slice(pr * LANES, (pr + 1) * LANES)
        q2 = q_ref[0, :, ps]
        kk = jnp.concatenate([kp_ref[0, :, ps], kc_ref[0, :, ps]], axis=0)
        ve = jnp.concatenate([jnp.concatenate([vp_ref[0, :, ps], vc_ref[0, :, ps]], axis=0), ones],
                             axis=1)
        out2 = None
        for hh in range(LANES // B_HEAD_DIM):
            h = pr * (LANES // B_HEAD_DIM) + hh
            mine = low_half if hh == 0 else jnp.logical_not(low_half)
            qh = jnp.where(mine, q2, jnp.zeros_like(q2))
            s = lax.dot_general(qh, kk, (((1,), (1,)), ((), ())), preferred_element_type=F32)
            s = jnp.where(has_prev, s + bias_ref[h], MASK_NEG)
            m = jnp.max(s, axis=-1, keepdims=True)
            pv = jnp.dot(jnp.exp2(s - m).astype(BF16), ve, preferred_element_type=F32)
            l = pv[:, LANES:]
            o_h = pv[:, :LANES] / l
            out2 = o_h if out2 is None else jnp.where(mine, o_h, out2)
            lse_tile = jnp.where(lane == h, (m + jnp.log2(l)) * (1.0 / LOG2E), lse_tile)
        o_ref[0, :, ps] = out2.astype(BF16)
    lse_ref[0] = lse_tile


def dilated_group(proj, bias, n_seq, length):
    width = B_HEADS * B_HEAD_DIM
    pv = proj.reshape(n_seq, length, 3 * width)
    blk = (1, B_N, width)

    def spec(which, prev):
        if prev:
            return pl.BlockSpec(blk, lambda s, c: (s, jnp.maximum(c - 1, 0), which))
        return pl.BlockSpec(blk, lambda s, c: (s, c, which))

    return pl.pallas_call(
        _dilated_kernel,
        out_shape=(jax.ShapeDtypeStruct((n_seq, length, width), BF16),
                   jax.ShapeDtypeStruct((n_seq, length, LANES), F32)),
        grid=(n_seq, length // B_N),
        in_specs=[spec(0, False), spec(1, True), spec(1, False), spec(2, True), spec(2, False),
                  pl.BlockSpec(bias.shape, lambda s, c: (0, 0, 0))],
        out_specs=(pl.BlockSpec(blk, lambda s, c: (s, c, 0)),
                   pl.BlockSpec((1, B_N, LANES), lambda s, c: (s, c, 0))),
        compiler_params=_params("parallel", "parallel"),
        name="dilated_group",
    )(pv, pv, pv, pv, pv, bias)


def dilated_bias(rel_bias, dil):
    ii = jnp.arange(B_N)[:, None]
    jj = jnp.arange(2 * B_N)[None, :]
    delta = B_N + ii - jj
    band = (delta >= 0) & (delta <= B_N)
    by_delta = _bias_by_distance(rel_bias, jnp.arange(2 * B_N) * dil).T
    bias = _toeplitz(by_delta, B_N, 2 * B_N, B_N)
    return jnp.where(band[None], bias.astype(F32) * LOG2E, MASK_NEG)


def _dilated_merge_kernel(dils, o0_ref, o1_ref, o2_ref, l0_ref, l1_ref, l2_ref, e_ref, w_ref,
                          h_ref, g_ref, b_ref, o_ref, ob_ref, *nat_sc):
    def natural(o_g, l_g, dil, scratch):
        if dil == 1:
            return o_g[0, 0].astype(F32), l_g[0, 0]
        o_sc, l_sc = scratch
        n_tiles, n = o_sc.shape[0], o_sc.shape[1] // dil
        for r in range(dil):
            for j in range(n_tiles):
                o_sc[j, pl.ds(r, n, stride=dil), :] = o_g[0, r, :, j * LANES:(j + 1) * LANES].astype(F32)
            l_sc[pl.ds(r, n, stride=dil), :] = l_g[0, r]
        return jnp.concatenate([o_sc[j] for j in range(n_tiles)], axis=-1), l_sc[...]

    nat = []
    for gi, (o_g, l_g) in enumerate(((o0_ref, l0_ref), (o1_ref, l1_ref), (o2_ref, l2_ref))):
        nat.append(natural(o_g, l_g, dils[gi], nat_sc[2 * gi:2 * gi + 2]))
    (v0, l0), (v1, l1), (v2, l2) = nat
    m = jnp.maximum(jnp.maximum(l0, l1), l2)
    e0, e1, e2 = jnp.exp(l0 - m), jnp.exp(l1 - m), jnp.exp(l2 - m)
    inv = 1.0 / (e0 + e1 + e2)
    e_mat = e_ref[...]

    def spread(wt):
        hi = wt.astype(BF16)
        lo = (wt - hi.astype(F32)).astype(BF16)
        return (jnp.dot(hi, e_mat, preferred_element_type=F32)
                + jnp.dot(lo, e_mat, preferred_element_type=F32))

    mix = spread(e0 * inv) * v0 + spread(e1 * inv) * v1 + spread(e2 * inv) * v2
    y = jnp.dot(mix.astype(BF16), w_ref[...], preferred_element_type=F32)
    out = _layer_norm_rows(DN_ALPHA * h_ref[...] + y, g_ref[...], b_ref[...])
    o_ref[...] = out
    ob_ref[...] = out.astype(BF16)


def dilated_merge(outs, lses, dils, w_o, h, g, b, bsz, seq, tm=512):
    m, d = h.shape
    width = B_HEADS * B_HEAD_DIM
    tm = min(tm, seq)
    n_j = seq // tm
    expand = (jnp.arange(LANES)[:, None] == (jnp.arange(width)[None, :] // B_HEAD_DIM)).astype(BF16)
    row = lambda bi, j: (bi * n_j + j, 0)
    fixed = lambda bi, j: (0, 0)
    grouped = lambda bi, j: (bi, 0, j, 0)
    o_specs = [pl.BlockSpec((1, dl, tm // dl, width), grouped) for dl in dils]
    l_specs = [pl.BlockSpec((1, dl, tm // dl, LANES), grouped) for dl in dils]
    scratch = []
    for dl in dils:
        if dl > 1:
            scratch += [pltpu.VMEM((width // LANES, tm, LANES), F32), pltpu.VMEM((tm, LANES), F32)]
        else:
            scratch += [pltpu.VMEM((8, LANES), F32), pltpu.VMEM((8, LANES), F32)]
    outs = [o.reshape(bsz, dl, seq // dl, width) for o, dl in zip(outs, dils)]
    lses = [l.reshape(bsz, dl, seq // dl, LANES) for l, dl in zip(lses, dils)]
    return pl.pallas_call(
        functools.partial(_dilated_merge_kernel, tuple(dils)),
        out_shape=(jax.ShapeDtypeStruct((m, d), F32), jax.ShapeDtypeStruct((m, d), BF16)),
        grid=(bsz, n_j),
        in_specs=o_specs + l_specs
                 + [pl.BlockSpec((LANES, width), fixed), pl.BlockSpec((width, d), fixed),
                    pl.BlockSpec((tm, d), row), pl.BlockSpec((1, d), fixed), pl.BlockSpec((1, d), fixed)],
        out_specs=(pl.BlockSpec((tm, d), row), pl.BlockSpec((tm, d), row)),
        scratch_shapes=scratch,
        compiler_params=_params("parallel", "parallel"),
        name="dilated_merge",
    )(*outs, *lses, expand, w_o.astype(BF16), h, g.reshape(1, d), b.reshape(1, d))


def dilated_mixer(hb_by_dil, w_in, w_o, rel_bias, h, g, b, bsz, seq):
    width = B_HEADS * B_HEAD_DIM
    outs, lses, dils = [], [], []
    for gi, (window, dil) in enumerate(B_GROUPS):
        assert window // dil == B_N and seq % window == 0
        w_g = w_in[:, gi * 3 * width:(gi + 1) * 3 * width]
        w_g = w_g.at[:, :width].multiply(B_HEAD_DIM ** -0.5 * LOG2E)
        proj = matmul(hb_by_dil[dil], w_g.astype(BF16), BF16, tn=3 * width // 2)
        o, lse = dilated_group(proj, dilated_bias(rel_bias, dil), bsz * dil, seq // dil)
        outs.append(o)
        lses.append(lse)
        dils.append(dil)
    return dilated_merge(outs, lses, dils, w_o, h, g, b, bsz, seq)


C_PAD = 128


def _mla_prep_kernel(x_ref, win_ref, qn_ref, kvn_ref, wqa_ref, wqb_ref, wka_ref, wv_ref,
                     cos_ref, sin_ref, vone_ref, q_ref, k_ref, v_ref):
    c = jnp.dot(x_ref[...], win_ref[...], preferred_element_type=F32)
    cos, sin = cos_ref[...], sin_ref[...]

    def rms(v, gain):
        return (v * lax.rsqrt(jnp.mean(v * v, axis=-1, keepdims=True) + RMS_EPS) * gain).astype(BF16)

    nq = rms(c[:, :C_Q_RANK], qn_ref[...])
    nkv = rms(c[:, C_Q_RANK:C_Q_RANK + C_KV_RANK], kvn_ref[...])
    o = C_Q_RANK + C_KV_RANK
    k_rope = c[:, o:o + C_PAD] * cos + c[:, o + C_PAD:o + 2 * C_PAD] * sin
    qa = jnp.dot(nq, wqa_ref[...], preferred_element_type=F32)
    qb = jnp.dot(nq, wqb_ref[...], preferred_element_type=F32)
    kn = jnp.dot(nkv, wka_ref[...], preferred_element_type=F32)
    for h in range(C_HEADS):
        hs = slice(h * C_PAD, (h + 1) * C_PAD)
        q_ref[:, hs] = (qa[:, hs] * cos + qb[:, hs] * sin).astype(BF16)
        k_ref[:, hs] = (kn[:, hs] + k_rope).astype(BF16)
    v_ref[...] = (jnp.dot(nkv, wv_ref[...], preferred_element_type=F32) + vone_ref[...]).astype(BF16)


def _rot_half_cols(w):
    half = w.shape[-1] // 2
    return jnp.concatenate([-w[..., half:], w[..., :half]], axis=-1)


def mla_prep(hb, w_in, q_norm, kv_norm, w_q_up, w_kv_up, bsz, seq, tm=512):
    d = w_in.shape[0]
    scale = (C_NOPE + C_ROPE) ** -0.5 * LOG2E
    pad_r = C_PAD - C_NOPE - C_ROPE
    w_kr = w_in[:, C_Q_RANK + C_KV_RANK:]

    def rope_slot(w):
        return jnp.pad(w, ((0, 0), (C_NOPE, pad_r)))

    win = jnp.concatenate([w_in[:, :C_Q_RANK + C_KV_RANK], rope_slot(w_kr),
                           rope_slot(_rot_half_cols(w_kr))], axis=1).astype(BF16)
    wq = w_q_up.reshape(C_Q_RANK, C_HEADS, C_NOPE + C_ROPE) * scale
    wqa = jnp.pad(wq, ((0, 0), (0, 0), (0, pad_r))).reshape(C_Q_RANK, C_HEADS * C_PAD).astype(BF16)
    wqb = jnp.pad(_rot_half_cols(wq[..., C_NOPE:]), ((0, 0), (0, 0), (C_NOPE, pad_r)))
    wqb = wqb.reshape(C_Q_RANK, C_HEADS * C_PAD).astype(BF16)
    wkv = w_kv_up.reshape(C_KV_RANK, C_HEADS, C_NOPE + C_V)
    wka = jnp.pad(wkv[..., :C_NOPE], ((0, 0), (0, 0), (0, C_PAD - C_NOPE)))
    wka = wka.reshape(C_KV_RANK, C_HEADS * C_PAD).astype(BF16)
    wv = jnp.pad(wkv[..., C_NOPE:], ((0, 0), (0, 0), (0, LANES - C_V)))
    wv = wv.reshape(C_KV_RANK, C_HEADS * LANES).astype(BF16)
    v_ones = jnp.tile(jnp.concatenate([jnp.zeros((1, C_V), F32), jnp.ones((1, LANES - C_V), F32)], axis=1),
                      (1, C_HEADS))

    half = C_ROPE // 2
    inv = ROPE_BASE ** (-jnp.arange(half, dtype=F32) / half)
    ang = jnp.arange(seq, dtype=F32)[:, None] * inv[None, :]
    ones, zeros = jnp.ones((seq, C_NOPE), F32), jnp.zeros((seq, pad_r), F32)
    cos = jnp.concatenate([ones, jnp.cos(ang), jnp.cos(ang), zeros], axis=1)
    sin = jnp.concatenate([0 * ones, jnp.sin(ang), jnp.sin(ang), zeros], axis=1)

    m = bsz * seq
    tm = min(tm, seq)
    n_pos_blk = seq // tm
    row = lambda i: (i, 0)
    fixed = lambda i: (0, 0)
    pos = lambda i: (i % n_pos_blk, 0)
    full = lambda a: pl.BlockSpec(a.shape, fixed)
    qw, vw = C_HEADS * C_PAD, C_HEADS * LANES
    return pl.pallas_call(
        _mla_prep_kernel,
        out_shape=(jax.ShapeDtypeStruct((m, qw), BF16), jax.ShapeDtypeStruct((m, qw), BF16),
                   jax.ShapeDtypeStruct((m, vw), BF16)),
        grid=(m // tm,),
        in_specs=[pl.BlockSpec((tm, d), row), full(win),
                  pl.BlockSpec((1, C_Q_RANK), fixed), pl.BlockSpec((1, C_KV_RANK), fixed),
                  full(wqa), full(wqb), full(wka), full(wv),
                  pl.BlockSpec((tm, C_PAD), pos), pl.BlockSpec((tm, C_PAD), pos), full(v_ones)],
        out_specs=(pl.BlockSpec((tm, qw), row), pl.BlockSpec((tm, qw), row),
                   pl.BlockSpec((tm, vw), row)),
        compiler_params=_params("parallel"),
        name="mla_prep",
    )(hb, win, q_norm.reshape(1, -1), kv_norm.reshape(1, -1), wqa, wqb, wka, wv, cos, sin, v_ones)


MLA_TQ = 512


MLA_SUB = 32


def _mla_attn_kernel(q_ref, k_ref, v_ref, o_ref, s_sc, p_sc, m_sc, a_sc, acc_sc):
    i = pl.program_id(2)
    tq = MLA_TQ
    n_tile = tq // LANES
    row_in_sub = lax.broadcasted_iota(I32, (MLA_SUB, LANES), 0)
    col_in_tile = lax.broadcasted_iota(I32, (MLA_SUB, LANES), 1)
    outs = []
    for hh in range(2):
        q = q_ref[0, :, hh * C_PAD:(hh + 1) * C_PAD]
        m_sc[...] = jnp.full(m_sc.shape, MASK_NEG, F32)
        acc_sc[...] = jnp.zeros(acc_sc.shape, F32)

        def step(c, diag):
            k0 = pl.multiple_of(c * tq, tq)
            kc = k_ref[0, pl.ds(k0, tq), hh * C_PAD:(hh + 1) * C_PAD]
            s_sc[...] = lax.dot_general(q, kc, (((1,), (1,)), ((), ())), preferred_element_type=F32)
            for rb in range(tq // MLA_SUB):
                r = slice(rb * MLA_SUB, (rb + 1) * MLA_SUB)
                t = [s_sc[r, jt * LANES:(jt + 1) * LANES] for jt in range(n_tile)]
                if diag:
                    t = [jnp.where(col_in_tile + jt * LANES <= row_in_sub + rb * MLA_SUB, t[jt], MASK_NEG)
                         for jt in range(n_tile)]
                m_old = m_sc[r, :]
                m_new = jnp.maximum(m_old, jnp.max(functools.reduce(jnp.maximum, t), axis=-1, keepdims=True))
                a_sc[r, :] = jnp.exp2(m_old - m_new)
                m_sc[r, :] = m_new
                for jt in range(n_tile):
                    p_sc[r, jt * LANES:(jt + 1) * LANES] = jnp.exp2(t[jt] - m_new).astype(BF16)
            pv = jnp.dot(p_sc[...], v_ref[0, pl.ds(k0, tq), hh * LANES:(hh + 1) * LANES],
                         preferred_element_type=F32)
            acc_sc[...] = acc_sc[...] * a_sc[...] + pv

        def body(c, carry):
            step(c, False)
            return carry

        lax.fori_loop(0, i, body, 0)
        step(i, True)
        outs.append(acc_sc[:, :C_V] / acc_sc[:, C_V:])
    o_ref[0] = jnp.concatenate(outs, axis=-1).astype(BF16)


def mla_attention(q, k, v, bsz, seq):
    qw, vw = C_HEADS * C_PAD, C_HEADS * C_V
    return pl.pallas_call(
        _mla_attn_kernel,
        out_shape=jax.ShapeDtypeStruct((bsz, seq, vw), BF16),
        grid=(bsz, C_HEADS // 2, seq // MLA_TQ),
        in_specs=[pl.BlockSpec((1, MLA_TQ, 2 * C_PAD), lambda b, hp, i: (b, i, hp)),
                  pl.BlockSpec((1, seq, 2 * C_PAD), lambda b, hp, i: (b, 0, hp)),
                  pl.BlockSpec((1, seq, 2 * LANES), lambda b, hp, i: (b, 0, hp))],
        out_specs=pl.BlockSpec((1, MLA_TQ, 2 * C_V), lambda b, hp, i: (b, i, hp)),
        scratch_shapes=[pltpu.VMEM((MLA_TQ, MLA_TQ), F32), pltpu.VMEM((MLA_TQ, MLA_TQ), BF16),
                        pltpu.VMEM((MLA_TQ, LANES), F32), pltpu.VMEM((MLA_TQ, LANES), F32),
                        pltpu.VMEM((MLA_TQ, LANES), F32)],
        compiler_params=_params("parallel", "parallel", "arbitrary"),
        name="mla_attention",
    )(q.reshape(bsz, seq, qw), k.reshape(bsz, seq, qw), v.reshape(bsz, seq, C_HEADS * LANES))


def mla_mixer(hb, w_in, q_norm, kv_norm, w_q_up, w_kv_up, w_o, h, g, b, bsz, seq):
    q, k, v = mla_prep(hb, w_in, q_norm, kv_norm, w_q_up, w_kv_up, bsz, seq)
    o = mla_attention(q, k, v, bsz, seq)
    return matmul_residual_ln(o.reshape(bsz * seq, -1), w_o.astype(BF16), h, g, b)


ROUTER_TM = 1024


def _router_kernel(h_ref, wr_ref, eb_ref, eidx_ref, gate_ref, slab_sc):
    tm = h_ref.shape[0]
    n_sub = tm // LANES
    per = N_EXPERTS // N_GROUPS
    h = h_ref[...]
    h_hi = h.astype(BF16)
    h_lo = (h - h_hi.astype(F32)).astype(BF16)
    nt = (((1,), (1,)), ((), ()))
    both = lax.dot_general(wr_ref[...], h_hi, nt, preferred_element_type=F32)
    logits = (both[:N_EXPERTS] + both[N_EXPERTS:]
              + lax.dot_general(wr_ref[:N_EXPERTS, :], h_lo, nt, preferred_element_type=F32))
    for j in range(n_sub):
        slab_sc[pl.ds(j, N_EXPERTS, stride=n_sub), :] = logits[:, j * LANES:(j + 1) * LANES]
    x = slab_sc[...].reshape(N_GROUPS, per, n_sub, LANES)
    s4 = 1.0 / (1.0 + jnp.exp(-x))
    c4 = s4 + eb_ref[...].reshape(N_GROUPS, per, n_sub, LANES)
    neg_inf = -jnp.inf
    shape4 = (N_GROUPS, per, n_sub, LANES)
    j_idx = lax.broadcasted_iota(I32, shape4, 1).astype(F32)
    g_idx = lax.broadcasted_iota(I32, shape4, 0).astype(F32)
    flat = g_idx * per + j_idx

    m1 = jnp.max(c4, axis=1, keepdims=True)
    first = jnp.min(jnp.where(c4 == m1, j_idx, per), axis=1, keepdims=True)
    m2 = jnp.max(jnp.where(j_idx == first, neg_inf, c4), axis=1, keepdims=True)
    gs = m1 + m2

    gi = lax.broadcasted_iota(I32, (N_GROUPS, 1, n_sub, LANES), 0).astype(F32)
    keep = jnp.zeros((N_GROUPS, 1, n_sub, LANES), jnp.bool_)
    cur = gs
    for _ in range(TOPK_GROUPS):
        mx = jnp.max(cur, axis=0, keepdims=True)
        pick = gi == jnp.min(jnp.where(cur == mx, gi, N_GROUPS), axis=0, keepdims=True)
        keep = keep | pick
        cur = jnp.where(pick, neg_inf, cur)
    cur = jnp.where(keep, c4, neg_inf)

    idxs, gates = [], []
    for _ in range(TOP_K):
        mx = jnp.max(cur, axis=(0, 1), keepdims=True)
        fi = jnp.min(jnp.where(cur == mx, flat, N_EXPERTS), axis=(0, 1), keepdims=True)
        pick = flat == fi
        gates.append(jnp.sum(jnp.where(pick, s4, 0.0), axis=(0, 1)))
        idxs.append(fi[0, 0])
        cur = jnp.where(pick, neg_inf, cur)
    total = functools.reduce(lambda u, v: u + v, gates)
    for k in range(TOP_K):
        eidx_ref[k] = idxs[k].astype(I32)
        gate_ref[k] = gates[k] / total * ROUTED_SCALE


def moe_router(h, w_router, e_bias, row0, m):
    d = h.shape[1]
    tm = min(ROUTER_TM, m)
    n_sub = tm // LANES
    blk0 = row0 // tm
    bias = jnp.broadcast_to(e_bias.reshape(N_EXPERTS, 1, 1), (N_EXPERTS, n_sub, LANES))
    w_hi = w_router.T.astype(BF16)
    w_split = jnp.concatenate([w_hi, (w_router.T - w_hi.astype(F32)).astype(BF16)], axis=0)
    out = jax.ShapeDtypeStruct((TOP_K, m // LANES, LANES), I32)
    eidx, gate = pl.pallas_call(
        _router_kernel,
        out_shape=(out, jax.ShapeDtypeStruct(out.shape, F32)),
        grid=(m // tm,),
        in_specs=[pl.BlockSpec((tm, d), lambda i: (i + blk0, 0)),
                  pl.BlockSpec((2 * N_EXPERTS, d), lambda i: (0, 0)),
                  pl.BlockSpec((N_EXPERTS, n_sub, LANES), lambda i: (0, 0, 0))],
        out_specs=(pl.BlockSpec((TOP_K, n_sub, LANES), lambda i: (0, i, 0)),
                   pl.BlockSpec((TOP_K, n_sub, LANES), lambda i: (0, i, 0))),
        scratch_shapes=[pltpu.VMEM((N_EXPERTS * n_sub, LANES), F32)],
        compiler_params=_params("parallel"),
        name="moe_router",
    )(h, w_split, bias)
    return eidx.reshape(TOP_K, m), gate.reshape(TOP_K, m)


def _silu(x):
    return x / (1.0 + jnp.exp(-x))


RANK_TM = 1024


def _rank_kernel(eidx_ref, tri_ref, dest_ref, cnt_ref, start_ref, cnt_sc, run_sc):
    p, i = pl.program_id(0), pl.program_id(1)
    tm = eidx_ref.shape[1]
    e = eidx_ref[...]
    ex = lax.broadcasted_iota(I32, (N_EXPERTS, tm), 0)
    onehot = jnp.zeros((N_EXPERTS, tm), F32)
    for k in range(TOP_K):
        onehot = onehot + jnp.where(e[k:k + 1, :] == ex, 1.0, 0.0)
    tile_cnt = jnp.sum(onehot, axis=-1, keepdims=True)

    @pl.when((p == 0) & (i == 0))
    def _():
        cnt_sc[...] = jnp.zeros(cnt_sc.shape, F32)

    @pl.when(p == 0)
    def _():
        cnt_sc[...] += tile_cnt

    @pl.when((p == 1) & (i == 0))
    def _():
        cnt = cnt_sc[...]
        padded = jnp.floor((cnt + (MOE_BLK - 1)) * (1.0 / MOE_BLK)) * MOE_BLK
        below = (lax.broadcasted_iota(I32, (N_EXPERTS, N_EXPERTS), 1)
                 < lax.broadcasted_iota(I32, (N_EXPERTS, N_EXPERTS), 0)).astype(F32)
        start = jnp.dot(below, jnp.broadcast_to(padded, (N_EXPERTS, LANES)),
                        precision=lax.Precision.HIGHEST, preferred_element_type=F32)
        run_sc[...] = start[:, :1]
        cnt_ref[...] = jnp.broadcast_to(cnt, (N_EXPERTS, LANES))
        start_ref[...] = start

    @pl.when(p == 1)
    def _():
        before = jnp.dot(onehot.astype(BF16), tri_ref[...], preferred_element_type=F32) + run_sc[...]
        rows = [jnp.sum(jnp.where(e[k:k + 1, :] == ex, before, 0.0), axis=0, keepdims=True)
                for k in range(TOP_K)]
        dest_ref[...] = jnp.concatenate(rows, axis=0).astype(I32)
        run_sc[...] += tile_cnt


def moe_rank(eidx):
    n_tok = eidx.shape[1]
    tm = min(RANK_TM, n_tok)
    tri = (jnp.arange(tm)[:, None] < jnp.arange(tm)[None, :]).astype(BF16)
    stat = jax.ShapeDtypeStruct((N_EXPERTS, LANES), F32)
    dest, cnt, start = pl.pallas_call(
        _rank_kernel,
        out_shape=(jax.ShapeDtypeStruct((TOP_K, n_tok), I32), stat, stat),
        grid=(2, n_tok // tm),
        in_specs=[pl.BlockSpec((TOP_K, tm), lambda p, i: (0, i)),
                  pl.BlockSpec((tm, tm), lambda p, i: (0, 0))],
        out_specs=(pl.BlockSpec((TOP_K, tm), lambda p, i: (0, i * p)),
                   pl.BlockSpec((N_EXPERTS, LANES), lambda p, i: (0, 0)),
                   pl.BlockSpec((N_EXPERTS, LANES), lambda p, i: (0, 0))),
        scratch_shapes=[pltpu.VMEM((N_EXPERTS, 1), F32), pltpu.VMEM((N_EXPERTS, 1), F32)],
        compiler_params=_params("arbitrary", "arbitrary"),
        name="moe_rank",
    )(eidx, tri)
    return dest, cnt[:, 0].astype(I32), start[:, 0].astype(I32)


def _experts_kernel(blk_e_ref, x_ref, wg_ref, wu_ref, wd_ref, o_ref, wg_sc, wu_sc, wd_sc):
    i = pl.program_id(0)

    @pl.when((i == 0) | (blk_e_ref[i] != blk_e_ref[jnp.maximum(i - 1, 0)]))
    def _():
        wg_sc[...] = wg_ref[...].astype(BF16)
        wu_sc[...] = wu_ref[...].astype(BF16)
        wd_sc[...] = wd_ref[...].astype(BF16)

    x = x_ref[...]
    gate = jnp.dot(x, wg_sc[...], preferred_element_type=F32)
    up = jnp.dot(x, wu_sc[...], preferred_element_type=F32)
    act = (_silu(gate) * up).astype(BF16)
    o_ref[...] = jnp.dot(act, wd_sc[...], preferred_element_type=F32).astype(o_ref.dtype)


def grouped_experts(x_sorted, blk_e, wg, wu, wd, layer):
    n_slot, d = x_sorted.shape
    n_blk = n_slot // MOE_BLK
    by_expert = lambda i, be: (layer, be[i], 0, 0)
    grid_spec = pltpu.PrefetchScalarGridSpec(
        num_scalar_prefetch=1,
        grid=(n_blk,),
        in_specs=[pl.BlockSpec((MOE_BLK, d), lambda i, be: (i, 0)),
                  pl.BlockSpec((None, None, d, D_EXPERT), by_expert),
                  pl.BlockSpec((None, None, d, D_EXPERT), by_expert),
                  pl.BlockSpec((None, None, D_EXPERT, d), by_expert)],
        out_specs=pl.BlockSpec((MOE_BLK, d), lambda i, be: (i, 0)),
        scratch_shapes=[pltpu.VMEM((d, D_EXPERT), BF16), pltpu.VMEM((d, D_EXPERT), BF16),
                        pltpu.VMEM((D_EXPERT, d), BF16)],
    )
    return pl.pallas_call(
        _experts_kernel,
        out_shape=jax.ShapeDtypeStruct((n_slot, d), BF16),
        grid_spec=grid_spec,
        compiler_params=_params("arbitrary"),
        name="grouped_experts",
    )(blk_e, x_sorted, wg, wu, wd)


def _moe_finish_kernel(dils, n_prev, hb_ref, y_ref, gt_ref, wg_ref, wu_ref, wd_ref, h_ref, g_ref, b_ref,
                       *rest):
    o_ref, ob_ref = rest[n_prev:n_prev + 2]
    rest = rest[n_prev + 2:]
    x = hb_ref[...]
    d = x.shape[1]
    act = (_silu(jnp.dot(x, wg_ref[...], preferred_element_type=F32))
           * jnp.dot(x, wu_ref[...], preferred_element_type=F32)).astype(BF16)
    ff = jnp.dot(act, wd_ref[...], preferred_element_type=F32)
    gt = gt_ref[...]
    for k in range(TOP_K):
        ff = ff + gt[:, k:k + 1] * y_ref[k].astype(F32)
    out = _layer_norm_rows(DN_ALPHA * h_ref[...] + ff, g_ref[...], b_ref[...])
    o_ref[...] = out
    ob_ref[...] = out.astype(BF16)
    if dils:
        perm_refs, nat_sc = rest[:-1], rest[-1]
        for j in range(d // LANES):
            nat_sc[j] = out[:, j * LANES:(j + 1) * LANES]
        for p_ref, dil in zip(perm_refs, dils):
            n = out.shape[0] // dil
            for r in range(dil):
                for j in range(d // LANES):
                    p_ref[0, r, :, j * LANES:(j + 1) * LANES] = (
                        nat_sc[j, pl.ds(r, n, stride=dil), :].astype(BF16))


def moe_finish(hb, y_tok, gate_tok, ws_g, ws_u, ws_d, h, g, b, seq, row0, m, dils=(), prev=None, tm=256):
    d = h.shape[1]
    tm = min(tm, seq)
    n_j = seq // tm
    blk0 = row0 // tm
    row = lambda i: (i, 0)
    row_in = lambda i: (i + blk0, 0)
    fixed = lambda i: (0, 0)
    n_all = h.shape[0]
    seq0 = row0 // seq
    out_shape = [jax.ShapeDtypeStruct((n_all, d), F32), jax.ShapeDtypeStruct((n_all, d), BF16)]
    out_specs = [pl.BlockSpec((tm, d), row_in), pl.BlockSpec((tm, d), row_in)]
    for dil in dils:
        out_shape.append(jax.ShapeDtypeStruct((n_all // seq, dil, seq // dil, d), BF16))
        out_specs.append(pl.BlockSpec((1, dil, tm // dil, d), lambda i: (i // n_j + seq0, 0, i % n_j, 0)))
    prev = list(prev) if prev is not None else []
    n_fixed_in = 9
    res = pl.pallas_call(
        functools.partial(_moe_finish_kernel, tuple(dils), len(prev)),
        out_shape=tuple(out_shape),
        input_output_aliases={n_fixed_in + j: j for j in range(len(prev))},
        grid=(m // tm,),
        in_specs=[pl.BlockSpec((tm, d), row_in), pl.BlockSpec((TOP_K, tm, d), lambda i: (0, i, 0)),
                  pl.BlockSpec((tm, TOP_K), row),
                  pl.BlockSpec(ws_g.shape, fixed), pl.BlockSpec(ws_u.shape, fixed),
                  pl.BlockSpec(ws_d.shape, fixed),
                  pl.BlockSpec((tm, d), row_in), pl.BlockSpec((1, d), fixed), pl.BlockSpec((1, d), fixed)]
                 + [pl.BlockSpec(memory_space=pl.ANY)] * len(prev),
        out_specs=tuple(out_specs),
        scratch_shapes=[pltpu.VMEM((d // LANES, tm, LANES), F32)] if dils else [],
        compiler_params=_params("parallel"),
        name="moe_finish",
    )(hb, y_tok, gate_tok, ws_g, ws_u, ws_d, h, g.reshape(1, d), b.reshape(1, d), *prev)
    return list(res)


SORT_E_SHIFT = 19


MOE_SPLITS = 2


def moe_layer(h, hb, w_router, e_bias, wg, wu, wd, layer, ws_g, ws_u, ws_d, g, b, seq, dils=()):
    n_all = h.shape[0]
    n_split = MOE_SPLITS if (n_all // seq) % MOE_SPLITS == 0 else 1
    outs = None
    for sp in range(n_split):
        outs = moe_tokens(h, hb, w_router, e_bias, wg, wu, wd, layer, ws_g, ws_u, ws_d, g, b, seq, dils,
                          sp * (n_all // n_split), n_all // n_split, outs)
    return outs[0], outs[1], [p.reshape(n_all, -1) for p in outs[2:]]


def moe_tokens(h, hb, w_router, e_bias, wg, wu, wd, layer, ws_g, ws_u, ws_d, g, b, seq, dils, row0, n_tok,
               prev):
    d = h.shape[1]
    n_asg = n_tok * TOP_K
    n_pad = N_EXPERTS * MOE_BLK
    assert n_asg <= 1 << (SORT_E_SHIFT - 1) and n_pad <= 1 << (SORT_E_SHIFT - 1)
    eidx, gate = moe_router(h, w_router, e_bias, row0, n_tok)
    dest, counts, pad_start = moe_rank(eidx)
    padded = (counts + MOE_BLK - 1) // MOE_BLK * MOE_BLK
    pad_end = pad_start + padded
    n_blk = n_asg // MOE_BLK + N_EXPERTS
    blk_start = jnp.arange(n_blk, dtype=I32) * MOE_BLK
    blk_e = jnp.minimum(jnp.sum(pad_end[None, :] <= blk_start[:, None], axis=1), N_EXPERTS - 1)

    key_real = (eidx.T.reshape(-1) << SORT_E_SHIFT) | jnp.arange(n_asg, dtype=I32)
    i_pad = jnp.arange(n_pad, dtype=I32)
    e_pad = jnp.sum(jnp.cumsum(padded - counts)[None, :] <= i_pad[:, None], axis=1).astype(I32)
    key_pad = (e_pad << SORT_E_SHIFT) | (1 << (SORT_E_SHIFT - 1)) | i_pad
    low = jnp.sort(jnp.concatenate([key_real, key_pad])) & ((1 << SORT_E_SHIFT) - 1)
    slot_tok = jnp.where(low < (1 << (SORT_E_SHIFT - 1)), low // TOP_K,
                         jnp.arange(n_asg + n_pad, dtype=I32) % n_tok)

    x_sorted = hb.at[slot_tok + row0].get(mode="promise_in_bounds")
    y = grouped_experts(x_sorted, blk_e.astype(I32), wg, wu, wd, layer)
    y_k = y.at[dest.reshape(-1)].get(mode="promise_in_bounds").reshape(TOP_K, n_tok, d)
    return moe_finish(hb, y_k, gate.T, ws_g, ws_u, ws_d, h, g, b, seq, row0, n_tok, dils, prev)


def kernel(x, rel_bias, a_w_in, a_w_o, b_w_in, b_w_o, c_w_in, c_q_norm, c_kv_norm, c_w_q_up,
           c_w_kv_up, c_w_o, ln_g, ln_b, moe_w_router, moe_bias, moe_w_gate, moe_w_up,
           moe_w_down, moe_ws_gate, moe_ws_up, moe_ws_down):
    bsz, seq, d = x.shape
    depth = ln_g.shape[0]
    h = x.reshape(bsz * seq, d)
    hb = h.astype(BF16)
    strip = dsa_bias_strip(rel_bias, seq)
    extra_dils = tuple(dil for _, dil in B_GROUPS if dil > 1)
    hb_perm = []
    for layer in range(depth):
        kind, slot = layer % N_MIXERS, layer // N_MIXERS
        g0, b0 = ln_g[layer, 0], ln_b[layer, 0]
        if kind == 0:
            h, hb = dsa_mixer(hb, a_w_in[slot], a_w_o[slot], strip, h, g0, b0, bsz, seq)
        elif kind == 1:
            hb_by_dil = {1: hb, **dict(zip(extra_dils, hb_perm))}
            h, hb = dilated_mixer(hb_by_dil, b_w_in[slot], b_w_o[slot], rel_bias, h, g0, b0, bsz, seq)
        else:
            h, hb = mla_mixer(hb, c_w_in[slot], c_q_norm[slot], c_kv_norm[slot], c_w_q_up[slot],
                              c_w_kv_up[slot], c_w_o[slot], h, g0, b0, bsz, seq)
        next_dilated = layer + 1 < depth and (layer + 1) % N_MIXERS == 1
        h, hb, hb_perm = moe_layer(h, hb, moe_w_router[layer], moe_bias[layer],
                                   moe_w_gate, moe_w_up, moe_w_down, layer,
                                   moe_ws_gate[layer].astype(BF16),
                                   moe_ws_up[layer].astype(BF16), moe_ws_down[layer].astype(BF16),
                                   ln_g[layer, 1], ln_b[layer, 1], seq,
                                   extra_dils if next_dilated else ())
    return h.reshape(bsz, seq, d)
```

```python
import functools
import math

import jax
import jax.numpy as jnp
from jax import lax
from jax.experimental import pallas as pl
from jax.experimental.pallas import tpu as pltpu

F32 = jnp.float32
BF16 = jnp.bfloat16
I32 = jnp.int32

LANES = 128
VMEM_LIMIT_BYTES = 56 * 1024 * 1024

D_MODEL = 1024
DEPTH = 4
N_MIXERS = 3
NORM_EPS = 1e-5
RMS_EPS = 1e-6
REL_BUCKETS = 32
REL_MAX_DIST = 2048
A_HEADS = 16
A_HEAD_DIM = 128
A_IDX_HEADS = 8
A_IDX_DIM = 64
A_TOPK_MAX = 256
A_Q = A_HEADS * A_HEAD_DIM
B_GROUPS = ((128, 1), (512, 4), (2048, 16))
B_HEADS = 16
B_HEAD_DIM = 64
B_N = 128
C_HEADS = 16
C_Q_RANK = 256
C_KV_RANK = 128
C_NOPE = 64
C_ROPE = 32
C_V = 64
ROPE_BASE = 10000.0
N_EXPERTS = 64
TOP_K = 8
N_GROUPS = 8
TOPK_GROUPS = 4
D_EXPERT = 256
ROUTED_SCALE = 2.5
DN_ALPHA = (2 * DEPTH) ** 0.25

Q_BLK = 128
SEL_BLK = 512
KEY_CHUNK = 256
MOE_BLK = 512
MASK_NEG = -1e30
INT_MIN = -(2 ** 31)


def _params(*sem):
    return pltpu.CompilerParams(dimension_semantics=sem, vmem_limit_bytes=VMEM_LIMIT_BYTES)


def _mm_kernel(x_ref, w_ref, o_ref):
    o_ref[...] = jnp.dot(x_ref[...], w_ref[...], preferred_element_type=F32).astype(o_ref.dtype)


def matmul(x, w, out_dtype, tm=1024, tn=None):
    m, k = x.shape
    n = w.shape[1]
    tn = n if tn is None else tn
    tm = min(tm, m)
    return pl.pallas_call(
        _mm_kernel,
        out_shape=jax.ShapeDtypeStruct((m, n), out_dtype),
        grid=(n // tn, m // tm),
        in_specs=[pl.BlockSpec((tm, k), lambda j, i: (i, 0)),
                  pl.BlockSpec((k, tn), lambda j, i: (0, j))],
        out_specs=pl.BlockSpec((tm, tn), lambda j, i: (i, j)),
        compiler_params=_params("parallel", "parallel"),
        name="matmul",
    )(x, w)


def _layer_norm_rows(z, g, b):
    mu = jnp.mean(z, axis=-1, keepdims=True)
    zc = z - mu
    var = jnp.mean(zc * zc, axis=-1, keepdims=True)
    return zc * lax.rsqrt(var + NORM_EPS) * g + b


def _mm_ln_kernel(x_ref, w_ref, h_ref, g_ref, b_ref, o_ref, ob_ref):
    y = jnp.dot(x_ref[...], w_ref[...], preferred_element_type=F32)
    out = _layer_norm_rows(DN_ALPHA * h_ref[...] + y, g_ref[...], b_ref[...])
    o_ref[...] = out
    ob_ref[...] = out.astype(BF16)


def matmul_residual_ln(x, w, h, g, b, tm=512):
    m, k = x.shape
    d = w.shape[1]
    tm = min(tm, m)
    row = lambda i: (i, 0)
    fixed = lambda i: (0, 0)
    return pl.pallas_call(
        _mm_ln_kernel,
        out_shape=(jax.ShapeDtypeStruct((m, d), F32), jax.ShapeDtypeStruct((m, d), BF16)),
        grid=(m // tm,),
        in_specs=[pl.BlockSpec((tm, k), row), pl.BlockSpec((k, d), fixed),
                  pl.BlockSpec((tm, d), row), pl.BlockSpec((1, d), fixed),
                  pl.BlockSpec((1, d), fixed)],
        out_specs=(pl.BlockSpec((tm, d), row), pl.BlockSpec((tm, d), row)),
        compiler_params=_params("parallel"),
        name="matmul_residual_ln",
    )(x, w, h, g.reshape(1, d), b.reshape(1, d))


def _t5_bucket(dist):
    exact = REL_BUCKETS // 2
    d_f = jnp.maximum(dist, 1).astype(F32)
    large = exact + (jnp.log(d_f / exact) / math.log(REL_MAX_DIST / exact)
                     * (REL_BUCKETS - exact)).astype(I32)
    return jnp.where(dist < exact, dist, jnp.minimum(large, REL_BUCKETS - 1))


def _bias_by_distance(rel_bias, dist):
    return rel_bias[_t5_bucket(dist)]


def _toeplitz(f, rows, cols, off):
    length = f.shape[-1]
    period = rows + cols - 1
    j = jnp.arange(period)
    shift = jnp.where(j < cols, -j, period - j)
    v = f[:, jnp.clip(off + shift, 0, length - 1)]
    skew = jnp.tile(v, (1, rows))[:, :rows * (period - 1)].reshape(-1, rows, period - 1)
    return skew[:, :, :cols]


DSA_SUB = 32
LOG2E = math.log2(math.e)


def _dsa_select(k_sel, n_keys, t0, iq_ref, ik_ref, key_sc, mask_sc, cut_sc):
    rows = lax.broadcasted_iota(I32, (SEL_BLK, n_keys), 0) + t0
    cols = lax.broadcasted_iota(I32, (SEL_BLK, n_keys), 1)
    valid = cols <= rows

    ik = ik_ref[0, :n_keys, :].astype(BF16)
    w_all = iq_ref[0, :, A_IDX_HEADS * LANES:] * ((A_IDX_DIM * A_IDX_HEADS) ** -0.5)
    score = jnp.zeros((SEL_BLK, n_keys), F32)
    for h in range(A_IDX_HEADS):
        qh = iq_ref[0, :, h * LANES:(h + 1) * LANES].astype(BF16)
        rel = lax.dot_general(qh, ik, (((1,), (1,)), ((), ())), preferred_element_type=F32)
        score = score + w_all[:, A_IDX_DIM + h:A_IDX_DIM + h + 1] * jnp.maximum(rel, 0.0)
    score = jnp.where(score == 0.0, 0.0, score)

    bits = pltpu.bitcast(score, I32)
    key_sc[:, :n_keys] = jnp.where(valid, bits ^ ((bits >> 31) & 0x7FFFFFFF), INT_MIN)

    def search(it, ans_u):
        cand_u = ans_u | lax.shift_left(jnp.int32(1), 31 - it)
        cand_s = cand_u ^ INT_MIN
        cnt = jnp.sum(jnp.where(key_sc[:, :n_keys] >= cand_s, 1.0, 0.0), axis=-1, keepdims=True)
        return jnp.where(cnt >= k_sel, cand_u, ans_u)

    thr = lax.fori_loop(0, 32, search, jnp.zeros((SEL_BLK, 1), I32)) ^ INT_MIN

    key = key_sc[:, :n_keys]
    gt = key > thr
    eq = key == thr
    need = k_sel - jnp.sum(jnp.where(gt, 1.0, 0.0), axis=-1, keepdims=True)
    n_eq = jnp.sum(jnp.where(eq, 1.0, 0.0), axis=-1, keepdims=True)
    cut_sc[...] = jnp.full((SEL_BLK, 1), n_keys, I32)
    surplus = jnp.where((n_eq > need) & (thr != INT_MIN), 1.0, 0.0)

    @pl.when(jnp.max(surplus) > 0.0)
    def _():
        def tie_search(it, ans):
            cand = ans | lax.shift_left(jnp.int32(1), (n_keys.bit_length() - 1) - it)
            hit = (key_sc[:, :n_keys] == thr) & (cols < cand)
            cnt = jnp.sum(jnp.where(hit, 1.0, 0.0), axis=-1, keepdims=True)
            return jnp.where(cnt < need, cand, ans)
        cut_sc[...] = lax.fori_loop(0, n_keys.bit_length(), tie_search, jnp.zeros((SEL_BLK, 1), I32))

    selected = valid & (gt | (eq & (cols <= cut_sc[...])))
    mask_sc[:, :n_keys] = jnp.where(selected, 0.0, MASK_NEG)


def _dsa_kernel(k_sel, seq, q_ref, kv_ref, iq_ref, ik_ref, strip_ref, o_ref,
                key_sc, mask_sc, cut_sc, qs_sc, s_sc, p_sc, acc_sc, m_sc, a_sc, ve_sc):
    i = pl.program_id(1)
    t0 = i * SEL_BLK
    blk_per_chunk = KEY_CHUNK // Q_BLK

    @pl.when(i == 0)
    def _():
        ve_sc[:, :A_HEAD_DIM] = kv_ref[0, :, A_HEAD_DIM:]
        ve_sc[:, A_HEAD_DIM:] = jnp.ones((seq, A_HEAD_DIM), BF16)

    for j in range(seq // SEL_BLK):
        @pl.when(i == j)
        def _(j=j):
            _dsa_select(k_sel, (j + 1) * SEL_BLK, t0, iq_ref, ik_ref, key_sc, mask_sc, cut_sc)

    n_strip_blk = seq // Q_BLK - 1
    n_tile = KEY_CHUNK // LANES

    def attend(sub, carry):
        _dsa_attend(sub, i * (SEL_BLK // Q_BLK) + sub)
        return carry

    def _dsa_attend(sub, qi):
        r0 = pl.multiple_of(sub * Q_BLK, Q_BLK)
        for h in range(A_HEADS):
            qs_sc[h * Q_BLK:(h + 1) * Q_BLK, :] = q_ref[0, pl.ds(r0, Q_BLK),
                                                        h * A_HEAD_DIM:(h + 1) * A_HEAD_DIM]
        m_sc[...] = jnp.full(m_sc.shape, MASK_NEG, F32)
        acc_sc[...] = jnp.zeros(acc_sc.shape, F32)
        lax.fori_loop(0, (qi * Q_BLK + Q_BLK + KEY_CHUNK - 1) // KEY_CHUNK,
                      functools.partial(chunk, r0, qi), 0)
        for h in range(A_HEADS):
            r = slice(h * Q_BLK, (h + 1) * Q_BLK)
            o_ref[0, pl.ds(r0, Q_BLK), h * A_HEAD_DIM:(h + 1) * A_HEAD_DIM] = (
                acc_sc[r, :A_HEAD_DIM] / acc_sc[r, A_HEAD_DIM:]).astype(BF16)

    def chunk(r0, qi, c, carry):
        k0 = pl.multiple_of(c * KEY_CHUNK, KEY_CHUNK)
        kc = kv_ref[0, pl.ds(k0, KEY_CHUNK), :A_HEAD_DIM]
        s_sc[...] = lax.dot_general(qs_sc[...], kc, (((1,), (1,)), ((), ())),
                                    preferred_element_type=F32)
        w0 = (c * blk_per_chunk - qi + n_strip_blk) * Q_BLK
        for h in range(A_HEADS):
            for rb in range(Q_BLK // DSA_SUB):
                qr = slice(rb * DSA_SUB, (rb + 1) * DSA_SUB)
                mr = pl.ds(pl.multiple_of(r0 + rb * DSA_SUB, DSA_SUB), DSA_SUB)
                r = slice(h * Q_BLK + rb * DSA_SUB, h * Q_BLK + (rb + 1) * DSA_SUB)
                t = []
                for jt in range(n_tile):
                    ws = pl.ds(pl.multiple_of(w0 + jt * LANES, LANES), LANES)
                    ks = pl.ds(pl.multiple_of(k0 + jt * LANES, LANES), LANES)
                    t.append(s_sc[r, jt * LANES:(jt + 1) * LANES]
                             + strip_ref[h, qr, ws].astype(F32) + mask_sc[mr, ks])
                mx = functools.reduce(jnp.maximum, t)
                m_old = m_sc[r, :]
                m_new = jnp.maximum(m_old, jnp.max(mx, axis=-1, keepdims=True))
                a_sc[r, :] = jnp.exp2(m_old - m_new)
                m_sc[r, :] = m_new
                for jt in range(n_tile):
                    p_sc[r, jt * LANES:(jt + 1) * LANES] = jnp.exp2(t[jt] - m_new).astype(BF16)
        pv = jnp.dot(p_sc[...], ve_sc[pl.ds(k0, KEY_CHUNK), :], preferred_element_type=F32)
        for half in range(2):
            hs = slice(half * A_HEAD_DIM, (half + 1) * A_HEAD_DIM)
            acc_sc[:, hs] = acc_sc[:, hs] * a_sc[...] + pv[:, hs]
        return carry

    lax.fori_loop(0, SEL_BLK // Q_BLK, attend, 0)


def dsa_attention(qkv, idx, strip, bsz, seq):
    assert seq % SEL_BLK == 0 and SEL_BLK % KEY_CHUNK == 0 and KEY_CHUNK % Q_BLK == 0
    k_sel = min(A_TOPK_MAX, seq // 4)
    rows = A_HEADS * Q_BLK
    n_kv_blk = A_Q // (2 * A_HEAD_DIM)
    n_ik_blk = A_IDX_HEADS
    idx_w = (A_IDX_HEADS + 1) * LANES
    return pl.pallas_call(
        functools.partial(_dsa_kernel, k_sel, seq),
        out_shape=jax.ShapeDtypeStruct((bsz, seq, A_Q), BF16),
        grid=(bsz, seq // SEL_BLK),
        in_specs=[pl.BlockSpec((1, SEL_BLK, A_Q), lambda b, i: (b, i, 0)),
                  pl.BlockSpec((1, seq, 2 * A_HEAD_DIM), lambda b, i: (b, 0, n_kv_blk)),
                  pl.BlockSpec((1, SEL_BLK, idx_w), lambda b, i: (b, i, 0)),
                  pl.BlockSpec((1, seq, LANES), lambda b, i: (b, 0, n_ik_blk)),
                  pl.BlockSpec(strip.shape, lambda b, i: (0, 0, 0), pipeline_mode=pl.Buffered(1))],
        out_specs=pl.BlockSpec((1, SEL_BLK, A_Q), lambda b, i: (b, i, 0)),
        scratch_shapes=[pltpu.VMEM((SEL_BLK, seq), I32),
                        pltpu.VMEM((SEL_BLK, seq), F32),
                        pltpu.VMEM((SEL_BLK, 1), I32),
                        pltpu.VMEM((rows, A_HEAD_DIM), BF16),
                        pltpu.VMEM((rows, KEY_CHUNK), F32),
                        pltpu.VMEM((rows, KEY_CHUNK), BF16),
                        pltpu.VMEM((rows, 2 * A_HEAD_DIM), F32),
                        pltpu.VMEM((rows, LANES), F32),
                        pltpu.VMEM((rows, LANES), F32),
                        pltpu.VMEM((seq, 2 * A_HEAD_DIM), BF16)],
        compiler_params=_params("parallel", "arbitrary"),
        name="dsa_attention",
    )(qkv, qkv, idx, idx, strip)


def dsa_bias_strip(rel_bias, seq):
    width = seq + KEY_CHUNK - Q_BLK
    by_dist = _bias_by_distance(rel_bias, jnp.arange(seq)).T
    return (_toeplitz(by_dist, Q_BLK, width, seq - Q_BLK) * LOG2E).astype(BF16)


def dsa_weights(w_in):
    d = w_in.shape[0]
    wq = w_in[:, :A_Q] * (A_HEAD_DIM ** -0.5 * LOG2E)
    wkv = w_in[:, A_Q:A_Q + 2 * A_HEAD_DIM]
    o = A_Q + 2 * A_HEAD_DIM
    n_qi = A_IDX_HEADS * A_IDX_DIM
    wqi = w_in[:, o:o + n_qi].reshape(d, A_IDX_HEADS, A_IDX_DIM)
    wqi = jnp.pad(wqi, ((0, 0), (0, 0), (0, LANES - A_IDX_DIM))).reshape(d, A_IDX_HEADS * LANES)
    wkw = jnp.pad(w_in[:, o + n_qi:], ((0, 0), (0, LANES - A_IDX_DIM - A_IDX_HEADS)))
    return (jnp.concatenate([wq, wkv], axis=1).astype(BF16),
            jnp.concatenate([wqi, wkw], axis=1).astype(BF16))


def dsa_mixer(hb, w_in, w_o, strip, h, g, b, bsz, seq):
    w_qkv, w_idx = dsa_weights(w_in)
    qkv = matmul(hb, w_qkv, BF16, tn=w_qkv.shape[1] // 2)
    idx = matmul(hb, w_idx, F32)
    o = dsa_attention(qkv.reshape(bsz, seq, -1), idx.reshape(bsz, seq, -1), strip, bsz, seq)
    return matmul_residual_ln(o.reshape(bsz * seq, A_Q), w_o.astype(BF16), h, g, b)


def _dilated_kernel(q_ref, kp_ref, kc_ref, vp_ref, vc_ref, bias_ref, o_ref, lse_ref):
    c = pl.program_id(1)
    col = lax.broadcasted_iota(I32, (B_N, 2 * B_N), 1)
    has_prev = (col >= B_N) | (c > 0)
    lane = lax.broadcasted_iota(I32, (B_N, LANES), 1)
    low_half = lane < B_HEAD_DIM
    lse_tile = jnp.zeros((B_N, LANES), F32)
    ones = jnp.ones((2 * B_N, LANES), BF16)
    for pr in range(B_HEADS * B_HEAD_DIM // LANES):
        ps = slice(pr * LANES, (pr + 1) * LANES)
        q2 = q_ref[0, :, ps]
        kk = jnp.concatenate([kp_ref[0, :, ps], kc_ref[0, :, ps]], axis=0)
        ve = jnp.concatenate([jnp.concatenate([vp_ref[0, :, ps], vc_ref[0, :, ps]], axis=0), ones],
                             axis=1)
        out2 = None
        for hh in range(LANES // B_HEAD_DIM):
            h = pr * (LANES // B_HEAD_DIM) + hh
            mine = low_half if hh == 0 else jnp.logical_not(low_half)
            qh = jnp.where(mine, q2, jnp.zeros_like(q2))
            s = lax.dot_general(qh, kk, (((1,), (1,)), ((), ())), preferred_element_type=F32)
            s = jnp.where(has_prev, s + bias_ref[h], MASK_NEG)
            m = jnp.max(s, axis=-1, keepdims=True)
            pv = jnp.dot(jnp.exp2(s - m).astype(BF16), ve, preferred_element_type=F32)
            l = pv[:, LANES:]
            o_h = pv[:, :LANES] / l
            out2 = o_h if out2 is None else jnp.where(mine, o_h, out2)
            lse_tile = jnp.where(lane == h, (m + jnp.log2(l)) * (1.0 / LOG2E), lse_tile)
        o_ref[0, :, ps] = out2.astype(BF16)
    lse_ref[0] = lse_tile


def dilated_group(proj, bias, n_seq, length):
    width = B_HEADS * B_HEAD_DIM
    pv = proj.reshape(n_seq, length, 3 * width)
    blk = (1, B_N, width)

    def spec(which, prev):
        if prev:
            return pl.BlockSpec(blk, lambda s, c: (s, jnp.maximum(c - 1, 0), which))
        return pl.BlockSpec(blk, lambda s, c: (s, c, which))

    return pl.pallas_call(
        _dilated_kernel,
        out_shape=(jax.ShapeDtypeStruct((n_seq, length, width), BF16),
                   jax.ShapeDtypeStruct((n_seq, length, LANES), F32)),
        grid=(n_seq, length // B_N),
        in_specs=[spec(0, False), spec(1, True), spec(1, False), spec(2, True), spec(2, False),
                  pl.BlockSpec(bias.shape, lambda s, c: (0, 0, 0))],
        out_specs=(pl.BlockSpec(blk, lambda s, c: (s, c, 0)),
                   pl.BlockSpec((1, B_N, LANES), lambda s, c: (s, c, 0))),
        compiler_params=_params("parallel", "parallel"),
        name="dilated_group",
    )(pv, pv, pv, pv, pv, bias)


def dilated_bias(rel_bias, dil):
    ii = jnp.arange(B_N)[:, None]
    jj = jnp.arange(2 * B_N)[None, :]
    delta = B_N + ii - jj
    band = (delta >= 0) & (delta <= B_N)
    by_delta = _bias_by_distance(rel_bias, jnp.arange(2 * B_N) * dil).T
    bias = _toeplitz(by_delta, B_N, 2 * B_N, B_N)
    return jnp.where(band[None], bias.astype(F32) * LOG2E, MASK_NEG)


def _dilated_merge_kernel(dils, o0_ref, o1_ref, o2_ref, l0_ref, l1_ref, l2_ref, e_ref, w_ref,
                          h_ref, g_ref, b_ref, o_ref, ob_ref, *nat_sc):
    def natural(o_g, l_g, dil, scratch):
        if dil == 1:
            return o_g[0, 0].astype(F32), l_g[0, 0]
        o_sc, l_sc = scratch
        n_tiles, n = o_sc.shape[0], o_sc.shape[1] // dil
        for r in range(dil):
            for j in range(n_tiles):
                o_sc[j, pl.ds(r, n, stride=dil), :] = o_g[0, r, :, j * LANES:(j + 1) * LANES].astype(F32)
            l_sc[pl.ds(r, n, stride=dil), :] = l_g[0, r]
        return jnp.concatenate([o_sc[j] for j in range(n_tiles)], axis=-1), l_sc[...]

    nat = []
    for gi, (o_g, l_g) in enumerate(((o0_ref, l0_ref), (o1_ref, l1_ref), (o2_ref, l2_ref))):
        nat.append(natural(o_g, l_g, dils[gi], nat_sc[2 * gi:2 * gi + 2]))
    (v0, l0), (v1, l1), (v2, l2) = nat
    m = jnp.maximum(jnp.maximum(l0, l1), l2)
    e0, e1, e2 = jnp.exp(l0 - m), jnp.exp(l1 - m), jnp.exp(l2 - m)
    inv = 1.0 / (e0 + e1 + e2)
    e_mat = e_ref[...]

    def spread(wt):
        hi = wt.astype(BF16)
        lo = (wt - hi.astype(F32)).astype(BF16)
        return (jnp.dot(hi, e_mat, preferred_element_type=F32)
                + jnp.dot(lo, e_mat, preferred_element_type=F32))

    mix = spread(e0 * inv) * v0 + spread(e1 * inv) * v1 + spread(e2 * inv) * v2
    y = jnp.dot(mix.astype(BF16), w_ref[...], preferred_element_type=F32)
    out = _layer_norm_rows(DN_ALPHA * h_ref[...] + y, g_ref[...], b_ref[...])
    o_ref[...] = out
    ob_ref[...] = out.astype(BF16)


def dilated_merge(outs, lses, dils, w_o, h, g, b, bsz, seq, tm=512):
    m, d = h.shape
    width = B_HEADS * B_HEAD_DIM
    tm = min(tm, seq)
    n_j = seq // tm
    expand = (jnp.arange(LANES)[:, None] == (jnp.arange(width)[None, :] // B_HEAD_DIM)).astype(BF16)
    row = lambda bi, j: (bi * n_j + j, 0)
    fixed = lambda bi, j: (0, 0)
    grouped = lambda bi, j: (bi, 0, j, 0)
    o_specs = [pl.BlockSpec((1, dl, tm // dl, width), grouped) for dl in dils]
    l_specs = [pl.BlockSpec((1, dl, tm // dl, LANES), grouped) for dl in dils]
    scratch = []
    for dl in dils:
        if dl > 1:
            scratch += [pltpu.VMEM((width // LANES, tm, LANES), F32), pltpu.VMEM((tm, LANES), F32)]
        else:
            scratch += [pltpu.VMEM((8, LANES), F32), pltpu.VMEM((8, LANES), F32)]
    outs = [o.reshape(bsz, dl, seq // dl, width) for o, dl in zip(outs, dils)]
    lses = [l.reshape(bsz, dl, seq // dl, LANES) for l, dl in zip(lses, dils)]
    return pl.pallas_call(
        functools.partial(_dilated_merge_kernel, tuple(dils)),
        out_shape=(jax.ShapeDtypeStruct((m, d), F32), jax.ShapeDtypeStruct((m, d), BF16)),
        grid=(bsz, n_j),
        in_specs=o_specs + l_specs
                 + [pl.BlockSpec((LANES, width), fixed), pl.BlockSpec((width, d), fixed),
                    pl.BlockSpec((tm, d), row), pl.BlockSpec((1, d), fixed), pl.BlockSpec((1, d), fixed)],
        out_specs=(pl.BlockSpec((tm, d), row), pl.BlockSpec((tm, d), row)),
        scratch_shapes=scratch,
        compiler_params=_params("parallel", "parallel"),
        name="dilated_merge",
    )(*outs, *lses, expand, w_o.astype(BF16), h, g.reshape(1, d), b.reshape(1, d))


def dilated_mixer(hb_by_dil, w_in, w_o, rel_bias, h, g, b, bsz, seq):
    width = B_HEADS * B_HEAD_DIM
    outs, lses, dils = [], [], []
    for gi, (window, dil) in enumerate(B_GROUPS):
        assert window // dil == B_N and seq % window == 0
        w_g = w_in[:, gi * 3 * width:(gi + 1) * 3 * width]
        w_g = w_g.at[:, :width].multiply(B_HEAD_DIM ** -0.5 * LOG2E)
        proj = matmul(hb_by_dil[dil], w_g.astype(BF16), BF16, tn=3 * width // 2)
        o, lse = dilated_group(proj, dilated_bias(rel_bias, dil), bsz * dil, seq // dil)
        outs.append(o)
        lses.append(lse)
        dils.append(dil)
    return dilated_merge(outs, lses, dils, w_o, h, g, b, bsz, seq)


C_PAD = 128


def _mla_prep_kernel(x_ref, win_ref, qn_ref, kvn_ref, wqa_ref, wqb_ref, wka_ref, wv_ref,
                     cos_ref, sin_ref, vone_ref, q_ref, k_ref, v_ref):
    c = jnp.dot(x_ref[...], win_ref[...], preferred_element_type=F32)
    cos, sin = cos_ref[...], sin_ref[...]

    def rms(v, gain):
        return (v * lax.rsqrt(jnp.mean(v * v, axis=-1, keepdims=True) + RMS_EPS) * gain).astype(BF16)

    nq = rms(c[:, :C_Q_RANK], qn_ref[...])
    nkv = rms(c[:, C_Q_RANK:C_Q_RANK + C_KV_RANK], kvn_ref[...])
    o = C_Q_RANK + C_KV_RANK
    k_rope = c[:, o:o + C_PAD] * cos + c[:, o + C_PAD:o + 2 * C_PAD] * sin
    qa = jnp.dot(nq, wqa_ref[...], preferred_element_type=F32)
    qb = jnp.dot(nq, wqb_ref[...], preferred_element_type=F32)
    kn = jnp.dot(nkv, wka_ref[...], preferred_element_type=F32)
    for h in range(C_HEADS):
        hs = slice(h * C_PAD, (h + 1) * C_PAD)
        q_ref[:, hs] = (qa[:, hs] * cos + qb[:, hs] * sin).astype(BF16)
        k_ref[:, hs] = (kn[:, hs] + k_rope).astype(BF16)
    v_ref[...] = (jnp.dot(nkv, wv_ref[...], preferred_element_type=F32) + vone_ref[...]).astype(BF16)


def _rot_half_cols(w):
    half = w.shape[-1] // 2
    return jnp.concatenate([-w[..., half:], w[..., :half]], axis=-1)


def mla_prep(hb, w_in, q_norm, kv_norm, w_q_up, w_kv_up, bsz, seq, tm=512):
    d = w_in.shape[0]
    scale = (C_NOPE + C_ROPE) ** -0.5 * LOG2E
    pad_r = C_PAD - C_NOPE - C_ROPE
    w_kr = w_in[:, C_Q_RANK + C_KV_RANK:]

    def rope_slot(w):
        return jnp.pad(w, ((0, 0), (C_NOPE, pad_r)))

    win = jnp.concatenate([w_in[:, :C_Q_RANK + C_KV_RANK], rope_slot(w_kr),
                           rope_slot(_rot_half_cols(w_kr))], axis=1).astype(BF16)
    wq = w_q_up.reshape(C_Q_RANK, C_HEADS, C_NOPE + C_ROPE) * scale
    wqa = jnp.pad(wq, ((0, 0), (0, 0), (0, pad_r))).reshape(C_Q_RANK, C_HEADS * C_PAD).astype(BF16)
    wqb = jnp.pad(_rot_half_cols(wq[..., C_NOPE:]), ((0, 0), (0, 0), (C_NOPE, pad_r)))
    wqb = wqb.reshape(C_Q_RANK, C_HEADS * C_PAD).astype(BF16)
    wkv = w_kv_up.reshape(C_KV_RANK, C_HEADS, C_NOPE + C_V)
    wka = jnp.pad(wkv[..., :C_NOPE], ((0, 0), (0, 0), (0, C_PAD - C_NOPE)))
    wka = wka.reshape(C_KV_RANK, C_HEADS * C_PAD).astype(BF16)
    wv = jnp.pad(wkv[..., C_NOPE:], ((0, 0), (0, 0), (0, LANES - C_V)))
    wv = wv.reshape(C_KV_RANK, C_HEADS * LANES).astype(BF16)
    v_ones = jnp.tile(jnp.concatenate([jnp.zeros((1, C_V), F32), jnp.ones((1, LANES - C_V), F32)], axis=1),
                      (1, C_HEADS))

    half = C_ROPE // 2
    inv = ROPE_BASE ** (-jnp.arange(half, dtype=F32) / half)
    ang = jnp.arange(seq, dtype=F32)[:, None] * inv[None, :]
    ones, zeros = jnp.ones((seq, C_NOPE), F32), jnp.zeros((seq, pad_r), F32)
    cos = jnp.concatenate([ones, jnp.cos(ang), jnp.cos(ang), zeros], axis=1)
    sin = jnp.concatenate([0 * ones, jnp.sin(ang), jnp.sin(ang), zeros], axis=1)

    m = bsz * seq
    tm = min(tm, seq)
    n_pos_blk = seq // tm
    row = lambda i: (i, 0)
    fixed = lambda i: (0, 0)
    pos = lambda i: (i % n_pos_blk, 0)
    full = lambda a: pl.BlockSpec(a.shape, fixed)
    qw, vw = C_HEADS * C_PAD, C_HEADS * LANES
    return pl.pallas_call(
        _mla_prep_kernel,
        out_shape=(jax.ShapeDtypeStruct((m, qw), BF16), jax.ShapeDtypeStruct((m, qw), BF16),
                   jax.ShapeDtypeStruct((m, vw), BF16)),
        grid=(m // tm,),
        in_specs=[pl.BlockSpec((tm, d), row), full(win),
                  pl.BlockSpec((1, C_Q_RANK), fixed), pl.BlockSpec((1, C_KV_RANK), fixed),
                  full(wqa), full(wqb), full(wka), full(wv),
                  pl.BlockSpec((tm, C_PAD), pos), pl.BlockSpec((tm, C_PAD), pos), full(v_ones)],
        out_specs=(pl.BlockSpec((tm, qw), row), pl.BlockSpec((tm, qw), row),
                   pl.BlockSpec((tm, vw), row)),
        compiler_params=_params("parallel"),
        name="mla_prep",
    )(hb, win, q_norm.reshape(1, -1), kv_norm.reshape(1, -1), wqa, wqb, wka, wv, cos, sin, v_ones)


MLA_TQ = 512


MLA_SUB = 32


def _mla_attn_kernel(q_ref, k_ref, v_ref, o_ref, s_sc, p_sc, m_sc, a_sc, acc_sc):
    i = pl.program_id(2)
    tq = MLA_TQ
    n_tile = tq // LANES
    row_in_sub = lax.broadcasted_iota(I32, (MLA_SUB, LANES), 0)
    col_in_tile = lax.broadcasted_iota(I32, (MLA_SUB, LANES), 1)
    outs = []
    for hh in range(2):
        q = q_ref[0, :, hh * C_PAD:(hh + 1) * C_PAD]
        m_sc[...] = jnp.full(m_sc.shape, MASK_NEG, F32)
        acc_sc[...] = jnp.zeros(acc_sc.shape, F32)

        def step(c, diag):
            k0 = pl.multiple_of(c * tq, tq)
            kc = k_ref[0, pl.ds(k0, tq), hh * C_PAD:(hh + 1) * C_PAD]
            s_sc[...] = lax.dot_general(q, kc, (((1,), (1,)), ((), ())), preferred_element_type=F32)
            for rb in range(tq // MLA_SUB):
                r = slice(rb * MLA_SUB, (rb + 1) * MLA_SUB)
                t = [s_sc[r, jt * LANES:(jt + 1) * LANES] for jt in range(n_tile)]
                if diag:
                    t = [jnp.where(col_in_tile + jt * LANES <= row_in_sub + rb * MLA_SUB, t[jt], MASK_NEG)
                         for jt in range(n_tile)]
                m_old = m_sc[r, :]
                m_new = jnp.maximum(m_old, jnp.max(functools.reduce(jnp.maximum, t), axis=-1, keepdims=True))
                a_sc[r, :] = jnp.exp2(m_old - m_new)
                m_sc[r, :] = m_new
                for jt in range(n_tile):
                    p_sc[r, jt * LANES:(jt + 1) * LANES] = jnp.exp2(t[jt] - m_new).astype(BF16)
            pv = jnp.dot(p_sc[...], v_ref[0, pl.ds(k0, tq), hh * LANES:(hh + 1) * LANES],
                         preferred_element_type=F32)
            acc_sc[...] = acc_sc[...] * a_sc[...] + pv

        def body(c, carry):
            step(c, False)
            return carry

        lax.fori_loop(0, i, body, 0)
        step(i, True)
        outs.append(acc_sc[:, :C_V] / acc_sc[:, C_V:])
    o_ref[0] = jnp.concatenate(outs, axis=-1).astype(BF16)


def mla_attention(q, k, v, bsz, seq):
    qw, vw = C_HEADS * C_PAD, C_HEADS * C_V
    return pl.pallas_call(
        _mla_attn_kernel,
        out_shape=jax.ShapeDtypeStruct((bsz, seq, vw), BF16),
        grid=(bsz, C_HEADS // 2, seq // MLA_TQ),
        in_specs=[pl.BlockSpec((1, MLA_TQ, 2 * C_PAD), lambda b, hp, i: (b, i, hp)),
                  pl.BlockSpec((1, seq, 2 * C_PAD), lambda b, hp, i: (b, 0, hp)),
                  pl.BlockSpec((1, seq, 2 * LANES), lambda b, hp, i: (b, 0, hp))],
        out_specs=pl.BlockSpec((1, MLA_TQ, 2 * C_V), lambda b, hp, i: (b, i, hp)),
        scratch_shapes=[pltpu.VMEM((MLA_TQ, MLA_TQ), F32), pltpu.VMEM((MLA_TQ, MLA_TQ), BF16),
                        pltpu.VMEM((MLA_TQ, LANES), F32), pltpu.VMEM((MLA_TQ, LANES), F32),
                        pltpu.VMEM((MLA_TQ, LANES), F32)],
        compiler_params=_params("parallel", "parallel", "arbitrary"),
        name="mla_attention",
    )(q.reshape(bsz, seq, qw), k.reshape(bsz, seq, qw), v.reshape(bsz, seq, C_HEADS * LANES))


def mla_mixer(hb, w_in, q_norm, kv_norm, w_q_up, w_kv_up, w_o, h, g, b, bsz, seq):
    q, k, v = mla_prep(hb, w_in, q_norm, kv_norm, w_q_up, w_kv_up, bsz, seq)
    o = mla_attention(q, k, v, bsz, seq)
    return matmul_residual_ln(o.reshape(bsz * seq, -1), w_o.astype(BF16), h, g, b)


ROUTER_TM = 1024


def _router_kernel(h_ref, wr_ref, eb_ref, eidx_ref, gate_ref, slab_sc):
    tm = h_ref.shape[0]
    n_sub = tm // LANES
    per = N_EXPERTS // N_GROUPS
    h = h_ref[...]
    h_hi = h.astype(BF16)
    h_lo = (h - h_hi.astype(F32)).astype(BF16)
    nt = (((1,), (1,)), ((), ()))
    both = lax.dot_general(wr_ref[...], h_hi, nt, preferred_element_type=F32)
    logits = (both[:N_EXPERTS] + both[N_EXPERTS:]
              + lax.dot_general(wr_ref[:N_EXPERTS, :], h_lo, nt, preferred_element_type=F32))
    for j in range(n_sub):
        slab_sc[pl.ds(j, N_EXPERTS, stride=n_sub), :] = logits[:, j * LANES:(j + 1) * LANES]
    x = slab_sc[...].reshape(N_GROUPS, per, n_sub, LANES)
    s4 = 1.0 / (1.0 + jnp.exp(-x))
    c4 = s4 + eb_ref[...].reshape(N_GROUPS, per, n_sub, LANES)
    neg_inf = -jnp.inf
    shape4 = (N_GROUPS, per, n_sub, LANES)
    j_idx = lax.broadcasted_iota(I32, shape4, 1).astype(F32)
    g_idx = lax.broadcasted_iota(I32, shape4, 0).astype(F32)
    flat = g_idx * per + j_idx

    m1 = jnp.max(c4, axis=1, keepdims=True)
    first = jnp.min(jnp.where(c4 == m1, j_idx, per), axis=1, keepdims=True)
    m2 = jnp.max(jnp.where(j_idx == first, neg_inf, c4), axis=1, keepdims=True)
    gs = m1 + m2

    gi = lax.broadcasted_iota(I32, (N_GROUPS, 1, n_sub, LANES), 0).astype(F32)
    keep = jnp.zeros((N_GROUPS, 1, n_sub, LANES), jnp.bool_)
    cur = gs
    for _ in range(TOPK_GROUPS):
        mx = jnp.max(cur, axis=0, keepdims=True)
        pick = gi == jnp.min(jnp.where(cur == mx, gi, N_GROUPS), axis=0, keepdims=True)
        keep = keep | pick
        cur = jnp.where(pick, neg_inf, cur)
    cur = jnp.where(keep, c4, neg_inf)

    idxs, gates = [], []
    for _ in range(TOP_K):
        mx = jnp.max(cur, axis=(0, 1), keepdims=True)
        fi = jnp.min(jnp.where(cur == mx, flat, N_EXPERTS), axis=(0, 1), keepdims=True)
        pick = flat == fi
        gates.append(jnp.sum(jnp.where(pick, s4, 0.0), axis=(0, 1)))
        idxs.append(fi[0, 0])
        cur = jnp.where(pick, neg_inf, cur)
    total = functools.reduce(lambda u, v: u + v, gates)
    for k in range(TOP_K):
        eidx_ref[k] = idxs[k].astype(I32)
        gate_ref[k] = gates[k] / total * ROUTED_SCALE


def moe_router(h, w_router, e_bias, row0, m):
    d = h.shape[1]
    tm = min(ROUTER_TM, m)
    n_sub = tm // LANES
    blk0 = row0 // tm
    bias = jnp.broadcast_to(e_bias.reshape(N_EXPERTS, 1, 1), (N_EXPERTS, n_sub, LANES))
    w_hi = w_router.T.astype(BF16)
    w_split = jnp.concatenate([w_hi, (w_router.T - w_hi.astype(F32)).astype(BF16)], axis=0)
    out = jax.ShapeDtypeStruct((TOP_K, m // LANES, LANES), I32)
    eidx, gate = pl.pallas_call(
        _router_kernel,
        out_shape=(out, jax.ShapeDtypeStruct(out.shape, F32)),
        grid=(m // tm,),
        in_specs=[pl.BlockSpec((tm, d), lambda i: (i + blk0, 0)),
                  pl.BlockSpec((2 * N_EXPERTS, d), lambda i: (0, 0)),
                  pl.BlockSpec((N_EXPERTS, n_sub, LANES), lambda i: (0, 0, 0))],
        out_specs=(pl.BlockSpec((TOP_K, n_sub, LANES), lambda i: (0, i, 0)),
                   pl.BlockSpec((TOP_K, n_sub, LANES), lambda i: (0, i, 0))),
        scratch_shapes=[pltpu.VMEM((N_EXPERTS * n_sub, LANES), F32)],
        compiler_params=_params("parallel"),
        name="moe_router",
    )(h, w_split, bias)
    return eidx.reshape(TOP_K, m), gate.reshape(TOP_K, m)


def _silu(x):
    return x / (1.0 + jnp.exp(-x))


RANK_TM = 1024


def _rank_kernel(eidx_ref, tri_ref, dest_ref, cnt_ref, start_ref, cnt_sc, run_sc):
    p, i = pl.program_id(0), pl.program_id(1)
    tm = eidx_ref.shape[1]
    e = eidx_ref[...]
    ex = lax.broadcasted_iota(I32, (N_EXPERTS, tm), 0)
    onehot = jnp.zeros((N_EXPERTS, tm), F32)
    for k in range(TOP_K):
        onehot = onehot + jnp.where(e[k:k + 1, :] == ex, 1.0, 0.0)
    tile_cnt = jnp.sum(onehot, axis=-1, keepdims=True)

    @pl.when((p == 0) & (i == 0))
    def _():
        cnt_sc[...] = jnp.zeros(cnt_sc.shape, F32)

    @pl.when(p == 0)
    def _():
        cnt_sc[...] += tile_cnt

    @pl.when((p == 1) & (i == 0))
    def _():
        cnt = cnt_sc[...]
        padded = jnp.floor((cnt + (MOE_BLK - 1)) * (1.0 / MOE_BLK)) * MOE_BLK
        below = (lax.broadcasted_iota(I32, (N_EXPERTS, N_EXPERTS), 1)
                 < lax.broadcasted_iota(I32, (N_EXPERTS, N_EXPERTS), 0)).astype(F32)
        start = jnp.dot(below, jnp.broadcast_to(padded, (N_EXPERTS, LANES)),
                        precision=lax.Precision.HIGHEST, preferred_element_type=F32)
        run_sc[...] = start[:, :1]
        cnt_ref[...] = jnp.broadcast_to(cnt, (N_EXPERTS, LANES))
        start_ref[...] = start

    @pl.when(p == 1)
    def _():
        before = jnp.dot(onehot.astype(BF16), tri_ref[...], preferred_element_type=F32) + run_sc[...]
        rows = [jnp.sum(jnp.where(e[k:k + 1, :] == ex, before, 0.0), axis=0, keepdims=True)
                for k in range(TOP_K)]
        dest_ref[...] = jnp.concatenate(rows, axis=0).astype(I32)
        run_sc[...] += tile_cnt


def moe_rank(eidx):
    n_tok = eidx.shape[1]
    tm = min(RANK_TM, n_tok)
    tri = (jnp.arange(tm)[:, None] < jnp.arange(tm)[None, :]).astype(BF16)
    stat = jax.ShapeDtypeStruct((N_EXPERTS, LANES), F32)
    dest, cnt, start = pl.pallas_call(
        _rank_kernel,
        out_shape=(jax.ShapeDtypeStruct((TOP_K, n_tok), I32), stat, stat),
        grid=(2, n_tok // tm),
        in_specs=[pl.BlockSpec((TOP_K, tm), lambda p, i: (0, i)),
                  pl.BlockSpec((tm, tm), lambda p, i: (0, 0))],
        out_specs=(pl.BlockSpec((TOP_K, tm), lambda p, i: (0, i * p)),
                   pl.BlockSpec((N_EXPERTS, LANES), lambda p, i: (0, 0)),
                   pl.BlockSpec((N_EXPERTS, LANES), lambda p, i: (0, 0))),
        scratch_shapes=[pltpu.VMEM((N_EXPERTS, 1), F32), pltpu.VMEM((N_EXPERTS, 1), F32)],
        compiler_params=_params("arbitrary", "arbitrary"),
        name="moe_rank",
    )(eidx, tri)
    return dest, cnt[:, 0].astype(I32), start[:, 0].astype(I32)


def _experts_kernel(blk_e_ref, x_ref, wg_ref, wu_ref, wd_ref, o_ref, wg_sc, wu_sc, wd_sc):
    i = pl.program_id(0)

    @pl.when((i == 0) | (blk_e_ref[i] != blk_e_ref[jnp.maximum(i - 1, 0)]))
    def _():
        wg_sc[...] = wg_ref[...].astype(BF16)
        wu_sc[...] = wu_ref[...].astype(BF16)
        wd_sc[...] = wd_ref[...].astype(BF16)

    x = x_ref[...]
    gate = jnp.dot(x, wg_sc[...], preferred_element_type=F32)
    up = jnp.dot(x, wu_sc[...], preferred_element_type=F32)
    act = (_silu(gate) * up).astype(BF16)
    o_ref[...] = jnp.dot(act, wd_sc[...], preferred_element_type=F32).astype(o_ref.dtype)


def grouped_experts(x_sorted, blk_e, wg, wu, wd, layer):
    n_slot, d = x_sorted.shape
    n_blk = n_slot // MOE_BLK
    by_expert = lambda i, be: (layer, be[i], 0, 0)
    grid_spec = pltpu.PrefetchScalarGridSpec(
        num_scalar_prefetch=1,
        grid=(n_blk,),
        in_specs=[pl.BlockSpec((MOE_BLK, d), lambda i, be: (i, 0)),
                  pl.BlockSpec((None, None, d, D_EXPERT), by_expert),
                  pl.BlockSpec((None, None, d, D_EXPERT), by_expert),
                  pl.BlockSpec((None, None, D_EXPERT, d), by_expert)],
        out_specs=pl.BlockSpec((MOE_BLK, d), lambda i, be: (i, 0)),
        scratch_shapes=[pltpu.VMEM((d, D_EXPERT), BF16), pltpu.VMEM((d, D_EXPERT), BF16),
                        pltpu.VMEM((D_EXPERT, d), BF16)],
    )
    return pl.pallas_call(
        _experts_kernel,
        out_shape=jax.ShapeDtypeStruct((n_slot, d), BF16),
        grid_spec=grid_spec,
        compiler_params=_params("arbitrary"),
        name="grouped_experts",
    )(blk_e, x_sorted, wg, wu, wd)


def _moe_finish_kernel(dils, n_prev, hb_ref, y_ref, gt_ref, wg_ref, wu_ref, wd_ref, h_ref, g_ref, b_ref,
                       *rest):
    o_ref, ob_ref = rest[n_prev:n_prev + 2]
    rest = rest[n_prev + 2:]
    x = hb_ref[...]
    d = x.shape[1]
    act = (_silu(jnp.dot(x, wg_ref[...], preferred_element_type=F32))
           * jnp.dot(x, wu_ref[...], preferred_element_type=F32)).astype(BF16)
    ff = jnp.dot(act, wd_ref[...], preferred_element_type=F32)
    gt = gt_ref[...]
    for k in range(TOP_K):
        ff = ff + gt[:, k:k + 1] * y_ref[k].astype(F32)
    out = _layer_norm_rows(DN_ALPHA * h_ref[...] + ff, g_ref[...], b_ref[...])
    o_ref[...] = out
    ob_ref[...] = out.astype(BF16)
    if dils:
        perm_refs, nat_sc = rest[:-1], rest[-1]
        for j in range(d // LANES):
            nat_sc[j] = out[:, j * LANES:(j + 1) * LANES]
        for p_ref, dil in zip(perm_refs, dils):
            n = out.shape[0] // dil
            for r in range(dil):
                for j in range(d // LANES):
                    p_ref[0, r, :, j * LANES:(j + 1) * LANES] = (
                        nat_sc[j, pl.ds(r, n, stride=dil), :].astype(BF16))


def moe_finish(hb, y_tok, gate_tok, ws_g, ws_u, ws_d, h, g, b, seq, row0, m, dils=(), prev=None, tm=256):
    d = h.shape[1]
    tm = min(tm, seq)
    n_j = seq // tm
    blk0 = row0 // tm
    row = lambda i: (i, 0)
    row_in = lambda i: (i + blk0, 0)
    fixed = lambda i: (0, 0)
    n_all = h.shape[0]
    seq0 = row0 // seq
    out_shape = [jax.ShapeDtypeStruct((n_all, d), F32), jax.ShapeDtypeStruct((n_all, d), BF16)]
    out_specs = [pl.BlockSpec((tm, d), row_in), pl.BlockSpec((tm, d), row_in)]
    for dil in dils:
        out_shape.append(jax.ShapeDtypeStruct((n_all // seq, dil, seq // dil, d), BF16))
        out_specs.append(pl.BlockSpec((1, dil, tm // dil, d), lambda i: (i // n_j + seq0, 0, i % n_j, 0)))
    prev = list(prev) if prev is not None else []
    n_fixed_in = 9
    res = pl.pallas_call(
        functools.partial(_moe_finish_kernel, tuple(dils), len(prev)),
        out_shape=tuple(out_shape),
        input_output_aliases={n_fixed_in + j: j for j in range(len(prev))},
        grid=(m // tm,),
        in_specs=[pl.BlockSpec((tm, d), row_in), pl.BlockSpec((TOP_K, tm, d), lambda i: (0, i, 0)),
                  pl.BlockSpec((tm, TOP_K), row),
                  pl.BlockSpec(ws_g.shape, fixed), pl.BlockSpec(ws_u.shape, fixed),
                  pl.BlockSpec(ws_d.shape, fixed),
                  pl.BlockSpec((tm, d), row_in), pl.BlockSpec((1, d), fixed), pl.BlockSpec((1, d), fixed)]
                 + [pl.BlockSpec(memory_space=pl.ANY)] * len(prev),
        out_specs=tuple(out_specs),
        scratch_shapes=[pltpu.VMEM((d // LANES, tm, LANES), F32)] if dils else [],
        compiler_params=_params("parallel"),
        name="moe_finish",
    )(hb, y_tok, gate_tok, ws_g, ws_u, ws_d, h, g.reshape(1, d), b.reshape(1, d), *prev)
    return list(res)


SORT_E_SHIFT = 19


MOE_SPLITS = 2


def moe_layer(h, hb, w_router, e_bias, wg, wu, wd, layer, ws_g, ws_u, ws_d, g, b, seq, dils=()):
    n_all = h.shape[0]
    n_split = MOE_SPLITS if (n_all // seq) % MOE_SPLITS == 0 else 1
    outs = None
    for sp in range(n_split):
        outs = moe_tokens(h, hb, w_router, e_bias, wg, wu, wd, layer, ws_g, ws_u, ws_d, g, b, seq, dils,
                          sp * (n_all // n_split), n_all // n_split, outs)
    return outs[0], outs[1], [p.reshape(n_all, -1) for p in outs[2:]]


def moe_tokens(h, hb, w_router, e_bias, wg, wu, wd, layer, ws_g, ws_u, ws_d, g, b, seq, dils, row0, n_tok,
               prev):
    d = h.shape[1]
    n_asg = n_tok * TOP_K
    n_pad = N_EXPERTS * MOE_BLK
    assert n_asg <= 1 << (SORT_E_SHIFT - 1) and n_pad <= 1 << (SORT_E_SHIFT - 1)
    eidx, gate = moe_router(h, w_router, e_bias, row0, n_tok)
    dest, counts, pad_start = moe_rank(eidx)
    padded = (counts + MOE_BLK - 1) // MOE_BLK * MOE_BLK
    pad_end = pad_start + padded
    n_blk = n_asg // MOE_BLK + N_EXPERTS
    blk_start = jnp.arange(n_blk, dtype=I32) * MOE_BLK
    blk_e = jnp.minimum(jnp.sum(pad_end[None, :] <= blk_start[:, None], axis=1), N_EXPERTS - 1)

    key_real = (eidx.T.reshape(-1) << SORT_E_SHIFT) | jnp.arange(n_asg, dtype=I32)
    i_pad = jnp.arange(n_pad, dtype=I32)
    e_pad = jnp.sum(jnp.cumsum(padded - counts)[None, :] <= i_pad[:, None], axis=1).astype(I32)
    key_pad = (e_pad << SORT_E_SHIFT) | (1 << (SORT_E_SHIFT - 1)) | i_pad
    low = jnp.sort(jnp.concatenate([key_real, key_pad])) & ((1 << SORT_E_SHIFT) - 1)
    slot_tok = jnp.where(low < (1 << (SORT_E_SHIFT - 1)), low // TOP_K,
                         jnp.arange(n_asg + n_pad, dtype=I32) % n_tok)

    x_sorted = hb.at[slot_tok + row0].get(mode="promise_in_bounds")
    y = grouped_experts(x_sorted, blk_e.astype(I32), wg, wu, wd, layer)
    y_k = y.at[dest.reshape(-1)].get(mode="promise_in_bounds").reshape(TOP_K, n_tok, d)
    return moe_finish(hb, y_k, gate.T, ws_g, ws_u, ws_d, h, g, b, seq, row0, n_tok, dils, prev)


def kernel(x, rel_bias, a_w_in, a_w_o, b_w_in, b_w_o, c_w_in, c_q_norm, c_kv_norm, c_w_q_up,
           c_w_kv_up, c_w_o, ln_g, ln_b, moe_w_router, moe_bias, moe_w_gate, moe_w_up,
           moe_w_down, moe_ws_gate, moe_ws_up, moe_ws_down):
    bsz, seq, d = x.shape
    depth = ln_g.shape[0]
    h = x.reshape(bsz * seq, d)
    hb = h.astype(BF16)
    strip = dsa_bias_strip(rel_bias, seq)
    extra_dils = tuple(dil for _, dil in B_GROUPS if dil > 1)
    hb_perm = []
    for layer in range(depth):
        kind, slot = layer % N_MIXERS, layer // N_MIXERS
        g0, b0 = ln_g[layer, 0], ln_b[layer, 0]
        if kind == 0:
            h, hb = dsa_mixer(hb, a_w_in[slot], a_w_o[slot], strip, h, g0, b0, bsz, seq)
        elif kind == 1:
            hb_by_dil = {1: hb, **dict(zip(extra_dils, hb_perm))}
            h, hb = dilated_mixer(hb_by_dil, b_w_in[slot], b_w_o[slot], rel_bias, h, g0, b0, bsz, seq)
        else:
            h, hb = mla_mixer(hb, c_w_in[slot], c_q_norm[slot], c_kv_norm[slot], c_w_q_up[slot],
                              c_w_kv_up[slot], c_w_o[slot], h, g0, b0, bsz, seq)
        next_dilated = layer + 1 < depth and (layer + 1) % N_MIXERS == 1
        h, hb, hb_perm = moe_layer(h, hb, moe_w_router[layer], moe_bias[layer],
                                   moe_w_gate, moe_w_up, moe_w_down, layer,
                                   moe_ws_gate[layer].astype(BF16),
                                   moe_ws_up[layer].astype(BF16), moe_ws_down[layer].astype(BF16),
                                   ln_g[layer, 1], ln_b[layer, 1], seq,
                                   extra_dils if next_dilated else ())
    return h.reshape(bsz, seq, d)
```

```python
import functools
import math

import jax
import jax.numpy as jnp
from jax import lax
from jax.experimental import pallas as pl
from jax.experimental.pallas import tpu as pltpu

F32 = jnp.float32
BF16 = jnp.bfloat16
I32 = jnp.int32

LANES = 128
VMEM_LIMIT_BYTES = 56 * 1024 * 1024

D_MODEL = 1024
DEPTH = 4
N_MIXERS = 3
NORM_EPS = 1e-5
RMS_EPS = 1e-6
REL_BUCKETS = 32
REL_MAX_DIST = 2048
A_HEADS = 16
A_HEAD_DIM = 128
A_IDX_HEADS = 8
A_IDX_DIM = 64
A_TOPK_MAX = 256
A_Q = A_HEADS * A_HEAD_DIM
B_GROUPS = ((128, 1), (512, 4), (2048, 16))
B_HEADS = 16
B_HEAD_DIM = 64
B_N = 128
C_HEADS = 16
C_Q_RANK = 256
C_KV_RANK = 128
C_NOPE = 64
C_ROPE = 32
C_V = 64
ROPE_BASE = 10000.0
N_EXPERTS = 64
TOP_K = 8
N_GROUPS = 8
TOPK_GROUPS = 4
D_EXPERT = 256
ROUTED_SCALE = 2.5
DN_ALPHA = (2 * DEPTH) ** 0.25

Q_BLK = 128
SEL_BLK = 512
KEY_CHUNK = 256
MOE_BLK = 256
MASK_NEG = -1e30
INT_MIN = -(2 ** 31)


def _params(*sem):
    return pltpu.CompilerParams(dimension_semantics=sem, vmem_limit_bytes=VMEM_LIMIT_BYTES)


def _mm_kernel(x_ref, w_ref, o_ref):
    o_ref[...] = jnp.dot(x_ref[...], w_ref[...], preferred_element_type=F32).astype(o_ref.dtype)


def matmul(x, w, out_dtype, tm=1024, tn=None):
    m, k = x.shape
    n = w.shape[1]
    tn = n if tn is None else tn
    tm = min(tm, m)
    return pl.pallas_call(
        _mm_kernel,
        out_shape=jax.ShapeDtypeStruct((m, n), out_dtype),
        grid=(n // tn, m // tm),
        in_specs=[pl.BlockSpec((tm, k), lambda j, i: (i, 0)),
                  pl.BlockSpec((k, tn), lambda j, i: (0, j))],
        out_specs=pl.BlockSpec((tm, tn), lambda j, i: (i, j)),
        compiler_params=_params("parallel", "parallel"),
        name="matmul",
    )(x, w)


def _layer_norm_rows(z, g, b):
    mu = jnp.mean(z, axis=-1, keepdims=True)
    zc = z - mu
    var = jnp.mean(zc * zc, axis=-1, keepdims=True)
    return zc * lax.rsqrt(var + NORM_EPS) * g + b


def _mm_ln_kernel(x_ref, w_ref, h_ref, g_ref, b_ref, o_ref, ob_ref):
    y = jnp.dot(x_ref[...], w_ref[...], preferred_element_type=F32)
    out = _layer_norm_rows(DN_ALPHA * h_ref[...] + y, g_ref[...], b_ref[...])
    o_ref[...] = out
    ob_ref[...] = out.astype(BF16)


def matmul_residual_ln(x, w, h, g, b, tm=512):
    m, k = x.shape
    d = w.shape[1]
    tm = min(tm, m)
    row = lambda i: (i, 0)
    fixed = lambda i: (0, 0)
    return pl.pallas_call(
        _mm_ln_kernel,
        out_shape=(jax.ShapeDtypeStruct((m, d), F32), jax.ShapeDtypeStruct((m, d), BF16)),
        grid=(m // tm,),
        in_specs=[pl.BlockSpec((tm, k), row), pl.BlockSpec((k, d), fixed),
                  pl.BlockSpec((tm, d), row), pl.BlockSpec((1, d), fixed),
                  pl.BlockSpec((1, d), fixed)],
        out_specs=(pl.BlockSpec((tm, d), row), pl.BlockSpec((tm, d), row)),
        compiler_params=_params("parallel"),
        name="matmul_residual_ln",
    )(x, w, h, g.reshape(1, d), b.reshape(1, d))


def _t5_bucket(dist):
    exact = REL_BUCKETS // 2
    d_f = jnp.maximum(dist, 1).astype(F32)
    large = exact + (jnp.log(d_f / exact) / math.log(REL_MAX_DIST / exact)
                     * (REL_BUCKETS - exact)).astype(I32)
    return jnp.where(dist < exact, dist, jnp.minimum(large, REL_BUCKETS - 1))


def _bias_by_distance(rel_bias, dist):
    return rel_bias[_t5_bucket(dist)]


def _toeplitz(f, rows, cols, off):
    length = f.shape[-1]
    period = rows + cols - 1
    j = jnp.arange(period)
    shift = jnp.where(j < cols, -j, period - j)
    v = f[:, jnp.clip(off + shift, 0, length - 1)]
    skew = jnp.tile(v, (1, rows))[:, :rows * (period - 1)].reshape(-1, rows, period - 1)
    return skew[:, :, :cols]


DSA_SUB = 32
LOG2E = math.log2(math.e)


def _dsa_select(k_sel, n_keys, t0, iq_ref, ik_ref, key_sc, mask_sc, cut_sc):
    rows = lax.broadcasted_iota(I32, (SEL_BLK, n_keys), 0) + t0
    cols = lax.broadcasted_iota(I32, (SEL_BLK, n_keys), 1)
    valid = cols <= rows

    ik = ik_ref[0, :n_keys, :].astype(BF16)
    w_all = iq_ref[0, :, A_IDX_HEADS * LANES:] * ((A_IDX_DIM * A_IDX_HEADS) ** -0.5)
    score = jnp.zeros((SEL_BLK, n_keys), F32)
    for h in range(A_IDX_HEADS):
        qh = iq_ref[0, :, h * LANES:(h + 1) * LANES].astype(BF16)
        rel = lax.dot_general(qh, ik, (((1,), (1,)), ((), ())), preferred_element_type=F32)
        score = score + w_all[:, A_IDX_DIM + h:A_IDX_DIM + h + 1] * jnp.maximum(rel, 0.0)
    score = jnp.where(score == 0.0, 0.0, score)

    bits = pltpu.bitcast(score, I32)
    key_sc[:, :n_keys] = jnp.where(valid, bits ^ ((bits >> 31) & 0x7FFFFFFF), INT_MIN)

    def search(it, ans_u):
        cand_u = ans_u | lax.shift_left(jnp.int32(1), 31 - it)
        cand_s = cand_u ^ INT_MIN
        cnt = jnp.sum(jnp.where(key_sc[:, :n_keys] >= cand_s, 1.0, 0.0), axis=-1, keepdims=True)
        return jnp.where(cnt >= k_sel, cand_u, ans_u)

    thr = lax.fori_loop(0, 32, search, jnp.zeros((SEL_BLK, 1), I32)) ^ INT_MIN

    key = key_sc[:, :n_keys]
    gt = key > thr
    eq = key == thr
    need = k_sel - jnp.sum(jnp.where(gt, 1.0, 0.0), axis=-1, keepdims=True)
    n_eq = jnp.sum(jnp.where(eq, 1.0, 0.0), axis=-1, keepdims=True)
    cut_sc[...] = jnp.full((SEL_BLK, 1), n_keys, I32)
    surplus = jnp.where((n_eq > need) & (thr != INT_MIN), 1.0, 0.0)

    @pl.when(jnp.max(surplus) > 0.0)
    def _():
        def tie_search(it, ans):
            cand = ans | lax.shift_left(jnp.int32(1), (n_keys.bit_length() - 1) - it)
            hit = (key_sc[:, :n_keys] == thr) & (cols < cand)
            cnt = jnp.sum(jnp.where(hit, 1.0, 0.0), axis=-1, keepdims=True)
            return jnp.where(cnt < need, cand, ans)
        cut_sc[...] = lax.fori_loop(0, n_keys.bit_length(), tie_search, jnp.zeros((SEL_BLK, 1), I32))

    selected = valid & (gt | (eq & (cols <= cut_sc[...])))
    mask_sc[:, :n_keys] = jnp.where(selected, 0.0, MASK_NEG)


def _dsa_kernel(k_sel, seq, q_ref, kv_ref, iq_ref, ik_ref, strip_ref, o_ref,
                key_sc, mask_sc, cut_sc, qs_sc, s_sc, p_sc, acc_sc, m_sc, a_sc, ve_sc):
    i = pl.program_id(1)
    t0 = i * SEL_BLK
    blk_per_chunk = KEY_CHUNK // Q_BLK

    @pl.when(i == 0)
    def _():
        ve_sc[:, :A_HEAD_DIM] = kv_ref[0, :, A_HEAD_DIM:]
        ve_sc[:, A_HEAD_DIM:] = jnp.ones((seq, A_HEAD_DIM), BF16)

    for j in range(seq // SEL_BLK):
        @pl.when(i == j)
        def _(j=j):
            _dsa_select(k_sel, (j + 1) * SEL_BLK, t0, iq_ref, ik_ref, key_sc, mask_sc, cut_sc)

    n_strip_blk = seq // Q_BLK - 1
    n_tile = KEY_CHUNK // LANES

    def attend(sub, carry):
        _dsa_attend(sub, i * (SEL_BLK // Q_BLK) + sub)
        return carry

    def _dsa_attend(sub, qi):
        r0 = pl.multiple_of(sub * Q_BLK, Q_BLK)
        for h in range(A_HEADS):
            qs_sc[h * Q_BLK:(h + 1) * Q_BLK, :] = q_ref[0, pl.ds(r0, Q_BLK),
                                                        h * A_HEAD_DIM:(h + 1) * A_HEAD_DIM]
        m_sc[...] = jnp.full(m_sc.shape, MASK_NEG, F32)
        acc_sc[...] = jnp.zeros(acc_sc.shape, F32)
        lax.fori_loop(0, (qi * Q_BLK + Q_BLK + KEY_CHUNK - 1) // KEY_CHUNK,
                      functools.partial(chunk, r0, qi), 0)
        for h in range(A_HEADS):
            r = slice(h * Q_BLK, (h + 1) * Q_BLK)
            o_ref[0, pl.ds(r0, Q_BLK), h * A_HEAD_DIM:(h + 1) * A_HEAD_DIM] = (
                acc_sc[r, :A_HEAD_DIM] / acc_sc[r, A_HEAD_DIM:]).astype(BF16)

    def chunk(r0, qi, c, carry):
        k0 = pl.multiple_of(c * KEY_CHUNK, KEY_CHUNK)
        kc = kv_ref[0, pl.ds(k0, KEY_CHUNK), :A_HEAD_DIM]
        s_sc[...] = lax.dot_general(qs_sc[...], kc, (((1,), (1,)), ((), ())),
                                    preferred_element_type=F32)
        w0 = (c * blk_per_chunk - qi + n_strip_blk) * Q_BLK
        for h in range(A_HEADS):
            for rb in range(Q_BLK // DSA_SUB):
                qr = slice(rb * DSA_SUB, (rb + 1) * DSA_SUB)
                mr = pl.ds(pl.multiple_of(r0 + rb * DSA_SUB, DSA_SUB), DSA_SUB)
                r = slice(h * Q_BLK + rb * DSA_SUB, h * Q_BLK + (rb + 1) * DSA_SUB)
                t = []
                for jt in range(n_tile):
                    ws = pl.ds(pl.multiple_of(w0 + jt * LANES, LANES), LANES)
                    ks = pl.ds(pl.multiple_of(k0 + jt * LANES, LANES), LANES)
                    t.append(s_sc[r, jt * LANES:(jt + 1) * LANES]
                             + strip_ref[h, qr, ws].astype(F32) + mask_sc[mr, ks])
                mx = functools.reduce(jnp.maximum, t)
                m_old = m_sc[r, :]
                m_new = jnp.maximum(m_old, jnp.max(mx, axis=-1, keepdims=True))
                a_sc[r, :] = jnp.exp2(m_old - m_new)
                m_sc[r, :] = m_new
                for jt in range(n_tile):
                    p_sc[r, jt * LANES:(jt + 1) * LANES] = jnp.exp2(t[jt] - m_new).astype(BF16)
        pv = jnp.dot(p_sc[...], ve_sc[pl.ds(k0, KEY_CHUNK), :], preferred_element_type=F32)
        for half in range(2):
            hs = slice(half * A_HEAD_DIM, (half + 1) * A_HEAD_DIM)
            acc_sc[:, hs] = acc_sc[:, hs] * a_sc[...] + pv[:, hs]
        return carry

    lax.fori_loop(0, SEL_BLK // Q_BLK, attend, 0)


def dsa_attention(qkv, idx, strip, bsz, seq):
    assert seq % SEL_BLK == 0 and SEL_BLK % KEY_CHUNK == 0 and KEY_CHUNK % Q_BLK == 0
    k_sel = min(A_TOPK_MAX, seq // 4)
    rows = A_HEADS * Q_BLK
    n_kv_blk = A_Q // (2 * A_HEAD_DIM)
    n_ik_blk = A_IDX_HEADS
    idx_w = (A_IDX_HEADS + 1) * LANES
    return pl.pallas_call(
        functools.partial(_dsa_kernel, k_sel, seq),
        out_shape=jax.ShapeDtypeStruct((bsz, seq, A_Q), BF16),
        grid=(bsz, seq // SEL_BLK),
        in_specs=[pl.BlockSpec((1, SEL_BLK, A_Q), lambda b, i: (b, i, 0)),
                  pl.BlockSpec((1, seq, 2 * A_HEAD_DIM), lambda b, i: (b, 0, n_kv_blk)),
                  pl.BlockSpec((1, SEL_BLK, idx_w), lambda b, i: (b, i, 0)),
                  pl.BlockSpec((1, seq, LANES), lambda b, i: (b, 0, n_ik_blk)),
                  pl.BlockSpec(strip.shape, lambda b, i: (0, 0, 0), pipeline_mode=pl.Buffered(1))],
        out_specs=pl.BlockSpec((1, SEL_BLK, A_Q), lambda b, i: (b, i, 0)),
        scratch_shapes=[pltpu.VMEM((SEL_BLK, seq), I32),
                        pltpu.VMEM((SEL_BLK, seq), F32),
                        pltpu.VMEM((SEL_BLK, 1), I32),
                        pltpu.VMEM((rows, A_HEAD_DIM), BF16),
                        pltpu.VMEM((rows, KEY_CHUNK), F32),
                        pltpu.VMEM((rows, KEY_CHUNK), BF16),
                        pltpu.VMEM((rows, 2 * A_HEAD_DIM), F32),
                        pltpu.VMEM((rows, LANES), F32),
                        pltpu.VMEM((rows, LANES), F32),
                        pltpu.VMEM((seq, 2 * A_HEAD_DIM), BF16)],
        compiler_params=_params("parallel", "arbitrary"),
        name="dsa_attention",
    )(qkv, qkv, idx, idx, strip)


def dsa_bias_strip(rel_bias, seq):
    width = seq + KEY_CHUNK - Q_BLK
    by_dist = _bias_by_distance(rel_bias, jnp.arange(seq)).T
    return (_toeplitz(by_dist, Q_BLK, width, seq - Q_BLK) * LOG2E).astype(BF16)


def dsa_weights(w_in):
    d = w_in.shape[0]
    wq = w_in[:, :A_Q] * (A_HEAD_DIM ** -0.5 * LOG2E)
    wkv = w_in[:, A_Q:A_Q + 2 * A_HEAD_DIM]
    o = A_Q + 2 * A_HEAD_DIM
    n_qi = A_IDX_HEADS * A_IDX_DIM
    wqi = w_in[:, o:o + n_qi].reshape(d, A_IDX_HEADS, A_IDX_DIM)
    wqi = jnp.pad(wqi, ((0, 0), (0, 0), (0, LANES - A_IDX_DIM))).reshape(d, A_IDX_HEADS * LANES)
    wkw = jnp.pad(w_in[:, o + n_qi:], ((0, 0), (0, LANES - A_IDX_DIM - A_IDX_HEADS)))
    return (jnp.concatenate([wq, wkv], axis=1).astype(BF16),
            jnp.concatenate([wqi, wkw], axis=1).astype(BF16))


def dsa_mixer(hb, w_in, w_o, strip, h, g, b, bsz, seq):
    w_qkv, w_idx = dsa_weights(w_in)
    qkv = matmul(hb, w_qkv, BF16, tn=w_qkv.shape[1] // 2)
    idx = matmul(hb, w_idx, F32)
    o = dsa_attention(qkv.reshape(bsz, seq, -1), idx.reshape(bsz, seq, -1), strip, bsz, seq)
    return matmul_residual_ln(o.reshape(bsz * seq, A_Q), w_o.astype(BF16), h, g, b)


def _dilated_kernel(q_ref, kp_ref, kc_ref, vp_ref, vc_ref, bias_ref, o_ref, lse_ref):
    c = pl.program_id(1)
    col = lax.broadcasted_iota(I32, (B_N, 2 * B_N), 1)
    has_prev = (col >= B_N) | (c > 0)
    lane = lax.broadcasted_iota(I32, (B_N, LANES), 1)
    low_half = lane < B_HEAD_DIM
    lse_tile = jnp.zeros((B_N, LANES), F32)
    ones = jnp.ones((2 * B_N, LANES), BF16)
    for pr in range(B_HEADS * B_HEAD_DIM // LANES):
        ps = slice(pr * LANES, (pr + 1) * LANES)
        q2 = q_ref[0, :, ps]
        kk = jnp.concatenate([kp_ref[0, :, ps], kc_ref[0, :, ps]], axis=0)
        ve = jnp.concatenate([jnp.concatenate([vp_ref[0, :, ps], vc_ref[0, :, ps]], axis=0), ones],
                             axis=1)
        out2 = None
        for hh in range(LANES // B_HEAD_DIM):
            h = pr * (LANES // B_HEAD_DIM) + hh
            mine = low_half if hh == 0 else jnp.logical_not(low_half)
            qh = jnp.where(mine, q2, jnp.zeros_like(q2))
            s = lax.dot_general(qh, kk, (((1,), (1,)), ((), ())), preferred_element_type=F32)
            s = jnp.where(has_prev, s + bias_ref[h], MASK_NEG)
            m = jnp.max(s, axis=-1, keepdims=True)
            pv = jnp.dot(jnp.exp2(s - m).astype(BF16), ve, preferred_element_type=F32)
            l = pv[:, LANES:]
            o_h = pv[:, :LANES] / l
            out2 = o_h if out2 is None else jnp.where(mine, o_h, out2)
            lse_tile = jnp.where(lane == h, (m + jnp.log2(l)) * (1.0 / LOG2E), lse_tile)
        o_ref[0, :, ps] = out2.astype(BF16)
    lse_ref[0] = lse_tile


def dilated_group(proj, bias, n_seq, length):
    width = B_HEADS * B_HEAD_DIM
    pv = proj.reshape(n_seq, length, 3 * width)
    blk = (1, B_N, width)

    def spec(which, prev):
        if prev:
            return pl.BlockSpec(blk, lambda s, c: (s, jnp.maximum(c - 1, 0), which))
        return pl.BlockSpec(blk, lambda s, c: (s, c, which))

    return pl.pallas_call(
        _dilated_kernel,
        out_shape=(jax.ShapeDtypeStruct((n_seq, length, width), BF16),
                   jax.ShapeDtypeStruct((n_seq, length, LANES), F32)),
        grid=(n_seq, length // B_N),
        in_specs=[spec(0, False), spec(1, True), spec(1, False), spec(2, True), spec(2, False),
                  pl.BlockSpec(bias.shape, lambda s, c: (0, 0, 0))],
        out_specs=(pl.BlockSpec(blk, lambda s, c: (s, c, 0)),
                   pl.BlockSpec((1, B_N, LANES), lambda s, c: (s, c, 0))),
        compiler_params=_params("parallel", "parallel"),
        name="dilated_group",
    )(pv, pv, pv, pv, pv, bias)


def dilated_bias(rel_bias, dil):
    ii = jnp.arange(B_N)[:, None]
    jj = jnp.arange(2 * B_N)[None, :]
    delta = B_N + ii - jj
    band = (delta >= 0) & (delta <= B_N)
    by_delta = _bias_by_distance(rel_bias, jnp.arange(2 * B_N) * dil).T
    bias = _toeplitz(by_delta, B_N, 2 * B_N, B_N)
    return jnp.where(band[None], bias.astype(F32) * LOG2E, MASK_NEG)


def _dilated_merge_kernel(dils, o0_ref, o1_ref, o2_ref, l0_ref, l1_ref, l2_ref, e_ref, w_ref,
                          h_ref, g_ref, b_ref, o_ref, ob_ref, *nat_sc):
    def natural(o_g, l_g, dil, scratch):
        if dil == 1:
            return o_g[0, 0].astype(F32), l_g[0, 0]
        o_sc, l_sc = scratch
        n_tiles, n = o_sc.shape[0], o_sc.shape[1] // dil
        for r in range(dil):
            for j in range(n_tiles):
                o_sc[j, pl.ds(r, n, stride=dil), :] = o_g[0, r, :, j * LANES:(j + 1) * LANES].astype(F32)
            l_sc[pl.ds(r, n, stride=dil), :] = l_g[0, r]
        return jnp.concatenate([o_sc[j] for j in range(n_tiles)], axis=-1), l_sc[...]

    nat = []
    for gi, (o_g, l_g) in enumerate(((o0_ref, l0_ref), (o1_ref, l1_ref), (o2_ref, l2_ref))):
        nat.append(natural(o_g, l_g, dils[gi], nat_sc[2 * gi:2 * gi + 2]))
    (v0, l0), (v1, l1), (v2, l2) = nat
    m = jnp.maximum(jnp.maximum(l0, l1), l2)
    e0, e1, e2 = jnp.exp(l0 - m), jnp.exp(l1 - m), jnp.exp(l2 - m)
    inv = 1.0 / (e0 + e1 + e2)
    e_mat = e_ref[...]

    def spread(wt):
        hi = wt.astype(BF16)
        lo = (wt - hi.astype(F32)).astype(BF16)
        return (jnp.dot(hi, e_mat, preferred_element_type=F32)
                + jnp.dot(lo, e_mat, preferred_element_type=F32))

    mix = spread(e0 * inv) * v0 + spread(e1 * inv) * v1 + spread(e2 * inv) * v2
    y = jnp.dot(mix.astype(BF16), w_ref[...], preferred_element_type=F32)
    out = _layer_norm_rows(DN_ALPHA * h_ref[...] + y, g_ref[...], b_ref[...])
    o_ref[...] = out
    ob_ref[...] = out.astype(BF16)


def dilated_merge(outs, lses, dils, w_o, h, g, b, bsz, seq, tm=512):
    m, d = h.shape
    width = B_HEADS * B_HEAD_DIM
    tm = min(tm, seq)
    n_j = seq // tm
    expand = (jnp.arange(LANES)[:, None] == (jnp.arange(width)[None, :] // B_HEAD_DIM)).astype(BF16)
    row = lambda bi, j: (bi * n_j + j, 0)
    fixed = lambda bi, j: (0, 0)
    grouped = lambda bi, j: (bi, 0, j, 0)
    o_specs = [pl.BlockSpec((1, dl, tm // dl, width), grouped) for dl in dils]
    l_specs = [pl.BlockSpec((1, dl, tm // dl, LANES), grouped) for dl in dils]
    scratch = []
    for dl in dils:
        if dl > 1:
            scratch += [pltpu.VMEM((width // LANES, tm, LANES), F32), pltpu.VMEM((tm, LANES), F32)]
        else:
            scratch += [pltpu.VMEM((8, LANES), F32), pltpu.VMEM((8, LANES), F32)]
    outs = [o.reshape(bsz, dl, seq // dl, width) for o, dl in zip(outs, dils)]
    lses = [l.reshape(bsz, dl, seq // dl, LANES) for l, dl in zip(lses, dils)]
    return pl.pallas_call(
        functools.partial(_dilated_merge_kernel, tuple(dils)),
        out_shape=(jax.ShapeDtypeStruct((m, d), F32), jax.ShapeDtypeStruct((m, d), BF16)),
        grid=(bsz, n_j),
        in_specs=o_specs + l_specs
                 + [pl.BlockSpec((LANES, width), fixed), pl.BlockSpec((width, d), fixed),
                    pl.BlockSpec((tm, d), row), pl.BlockSpec((1, d), fixed), pl.BlockSpec((1, d), fixed)],
        out_specs=(pl.BlockSpec((tm, d), row), pl.BlockSpec((tm, d), row)),
        scratch_shapes=scratch,
        compiler_params=_params("parallel", "parallel"),
        name="dilated_merge",
    )(*outs, *lses, expand, w_o.astype(BF16), h, g.reshape(1, d), b.reshape(1, d))


def dilated_mixer(hb_by_dil, w_in, w_o, rel_bias, h, g, b, bsz, seq):
    width = B_HEADS * B_HEAD_DIM
    outs, lses, dils = [], [], []
    for gi, (window, dil) in enumerate(B_GROUPS):
        assert window // dil == B_N and seq % window == 0
        w_g = w_in[:, gi * 3 * width:(gi + 1) * 3 * width]
        w_g = w_g.at[:, :width].multiply(B_HEAD_DIM ** -0.5 * LOG2E)
        proj = matmul(hb_by_dil[dil], w_g.astype(BF16), BF16, tn=3 * width // 2)
        o, lse = dilated_group(proj, dilated_bias(rel_bias, dil), bsz * dil, seq // dil)
        outs.append(o)
        lses.append(lse)
        dils.append(dil)
    return dilated_merge(outs, lses, dils, w_o, h, g, b, bsz, seq)


C_PAD = 128


def _mla_prep_kernel(x_ref, win_ref, qn_ref, kvn_ref, wqa_ref, wqb_ref, wka_ref, wv_ref,
                     cos_ref, sin_ref, vone_ref, q_ref, k_ref, v_ref):
    c = jnp.dot(x_ref[...], win_ref[...], preferred_element_type=F32)
    cos, sin = cos_ref[...], sin_ref[...]

    def rms(v, gain):
        return (v * lax.rsqrt(jnp.mean(v * v, axis=-1, keepdims=True) + RMS_EPS) * gain).astype(BF16)

    nq = rms(c[:, :C_Q_RANK], qn_ref[...])
    nkv = rms(c[:, C_Q_RANK:C_Q_RANK + C_KV_RANK], kvn_ref[...])
    o = C_Q_RANK + C_KV_RANK
    k_rope = c[:, o:o + C_PAD] * cos + c[:, o + C_PAD:o + 2 * C_PAD] * sin
    qa = jnp.dot(nq, wqa_ref[...], preferred_element_type=F32)
    qb = jnp.dot(nq, wqb_ref[...], preferred_element_type=F32)
    kn = jnp.dot(nkv, wka_ref[...], preferred_element_type=F32)
    for h in range(C_HEADS):
        hs = slice(h * C_PAD, (h + 1) * C_PAD)
        q_ref[:, hs] = (qa[:, hs] * cos + qb[:, hs] * sin).astype(BF16)
        k_ref[:, hs] = (kn[:, hs] + k_rope).astype(BF16)
    v_ref[...] = (jnp.dot(nkv, wv_ref[...], preferred_element_type=F32) + vone_ref[...]).astype(BF16)


def _rot_half_cols(w):
    half = w.shape[-1] // 2
    return jnp.concatenate([-w[..., half:], w[..., :half]], axis=-1)


def mla_prep(hb, w_in, q_norm, kv_norm, w_q_up, w_kv_up, bsz, seq, tm=512):
    d = w_in.shape[0]
    scale = (C_NOPE + C_ROPE) ** -0.5 * LOG2E
    pad_r = C_PAD - C_NOPE - C_ROPE
    w_kr = w_in[:, C_Q_RANK + C_KV_RANK:]

    def rope_slot(w):
        return jnp.pad(w, ((0, 0), (C_NOPE, pad_r)))

    win = jnp.concatenate([w_in[:, :C_Q_RANK + C_KV_RANK], rope_slot(w_kr),
                           rope_slot(_rot_half_cols(w_kr))], axis=1).astype(BF16)
    wq = w_q_up.reshape(C_Q_RANK, C_HEADS, C_NOPE + C_ROPE) * scale
    wqa = jnp.pad(wq, ((0, 0), (0, 0), (0, pad_r))).reshape(C_Q_RANK, C_HEADS * C_PAD).astype(BF16)
    wqb = jnp.pad(_rot_half_cols(wq[..., C_NOPE:]), ((0, 0), (0, 0), (C_NOPE, pad_r)))
    wqb = wqb.reshape(C_Q_RANK, C_HEADS * C_PAD).astype(BF16)
    wkv = w_kv_up.reshape(C_KV_RANK, C_HEADS, C_NOPE + C_V)
    wka = jnp.pad(wkv[..., :C_NOPE], ((0, 0), (0, 0), (0, C_PAD - C_NOPE)))
    wka = wka.reshape(C_KV_RANK, C_HEADS * C_PAD).astype(BF16)
    wv = jnp.pad(wkv[..., C_NOPE:], ((0, 0), (0, 0), (0, LANES - C_V)))
    wv = wv.reshape(C_KV_RANK, C_HEADS * LANES).astype(BF16)
    v_ones = jnp.tile(jnp.concatenate([jnp.zeros((1, C_V), F32), jnp.ones((1, LANES - C_V), F32)], axis=1),
                      (1, C_HEADS))

    half = C_ROPE // 2
    inv = ROPE_BASE ** (-jnp.arange(half, dtype=F32) / half)
    ang = jnp.arange(seq, dtype=F32)[:, None] * inv[None, :]
    ones, zeros = jnp.ones((seq, C_NOPE), F32), jnp.zeros((seq, pad_r), F32)
    cos = jnp.concatenate([ones, jnp.cos(ang), jnp.cos(ang), zeros], axis=1)
    sin = jnp.concatenate([0 * ones, jnp.sin(ang), jnp.sin(ang), zeros], axis=1)

    m = bsz * seq
    tm = min(tm, seq)
    n_pos_blk = seq // tm
    row = lambda i: (i, 0)
    fixed = lambda i: (0, 0)
    pos = lambda i: (i % n_pos_blk, 0)
    full = lambda a: pl.BlockSpec(a.shape, fixed)
    qw, vw = C_HEADS * C_PAD, C_HEADS * LANES
    return pl.pallas_call(
        _mla_prep_kernel,
        out_shape=(jax.ShapeDtypeStruct((m, qw), BF16), jax.ShapeDtypeStruct((m, qw), BF16),
                   jax.ShapeDtypeStruct((m, vw), BF16)),
        grid=(m // tm,),
        in_specs=[pl.BlockSpec((tm, d), row), full(win),
                  pl.BlockSpec((1, C_Q_RANK), fixed), pl.BlockSpec((1, C_KV_RANK), fixed),
                  full(wqa), full(wqb), full(wka), full(wv),
                  pl.BlockSpec((tm, C_PAD), pos), pl.BlockSpec((tm, C_PAD), pos), full(v_ones)],
        out_specs=(pl.BlockSpec((tm, qw), row), pl.BlockSpec((tm, qw), row),
                   pl.BlockSpec((tm, vw), row)),
        compiler_params=_params("parallel"),
        name="mla_prep",
    )(hb, win, q_norm.reshape(1, -1), kv_norm.reshape(1, -1), wqa, wqb, wka, wv, cos, sin, v_ones)


MLA_TQ = 512


MLA_SUB = 32


def _mla_attn_kernel(q_ref, k_ref, v_ref, o_ref, s_sc, p_sc, m_sc, a_sc, acc_sc):
    i = pl.program_id(2)
    tq = MLA_TQ
    n_tile = tq // LANES
    row_in_sub = lax.broadcasted_iota(I32, (MLA_SUB, LANES), 0)
    col_in_tile = lax.broadcasted_iota(I32, (MLA_SUB, LANES), 1)
    outs = []
    for hh in range(2):
        q = q_ref[0, :, hh * C_PAD:(hh + 1) * C_PAD]
        m_sc[...] = jnp.full(m_sc.shape, MASK_NEG, F32)
        acc_sc[...] = jnp.zeros(acc_sc.shape, F32)

        def step(c, diag):
            k0 = pl.multiple_of(c * tq, tq)
            kc = k_ref[0, pl.ds(k0, tq), hh * C_PAD:(hh + 1) * C_PAD]
            s_sc[...] = lax.dot_general(q, kc, (((1,), (1,)), ((), ())), preferred_element_type=F32)
            for rb in range(tq // MLA_SUB):
                r = slice(rb * MLA_SUB, (rb + 1) * MLA_SUB)
                n_act = (rb * MLA_SUB + MLA_SUB - 1) // LANES + 1 if diag else n_tile
                t = [s_sc[r, jt * LANES:(jt + 1) * LANES] for jt in range(n_act)]
                if diag:
                    edge = n_act - 1
                    t[edge] = jnp.where(col_in_tile + edge * LANES <= row_in_sub + rb * MLA_SUB,
                                        t[edge], MASK_NEG)
                m_old = m_sc[r, :]
                m_new = jnp.maximum(m_old, jnp.max(functools.reduce(jnp.maximum, t), axis=-1, keepdims=True))
                a_sc[r, :] = jnp.exp2(m_old - m_new)
                m_sc[r, :] = m_new
                for jt in range(n_tile):
                    p_sc[r, jt * LANES:(jt + 1) * LANES] = (
                        jnp.exp2(t[jt] - m_new).astype(BF16) if jt < n_act
                        else jnp.zeros((MLA_SUB, LANES), BF16))
            pv = jnp.dot(p_sc[...], v_ref[0, pl.ds(k0, tq), hh * LANES:(hh + 1) * LANES],
                         preferred_element_type=F32)
            acc_sc[...] = acc_sc[...] * a_sc[...] + pv

        def body(c, carry):
            step(c, False)
            return carry

        lax.fori_loop(0, i, body, 0)
        step(i, True)
        outs.append(acc_sc[:, :C_V] / acc_sc[:, C_V:])
    o_ref[0] = jnp.concatenate(outs, axis=-1).astype(BF16)


def mla_attention(q, k, v, bsz, seq):
    qw, vw = C_HEADS * C_PAD, C_HEADS * C_V
    return pl.pallas_call(
        _mla_attn_kernel,
        out_shape=jax.ShapeDtypeStruct((bsz, seq, vw), BF16),
        grid=(bsz, C_HEADS // 2, seq // MLA_TQ),
        in_specs=[pl.BlockSpec((1, MLA_TQ, 2 * C_PAD), lambda b, hp, i: (b, i, hp)),
                  pl.BlockSpec((1, seq, 2 * C_PAD), lambda b, hp, i: (b, 0, hp)),
                  pl.BlockSpec((1, seq, 2 * LANES), lambda b, hp, i: (b, 0, hp))],
        out_specs=pl.BlockSpec((1, MLA_TQ, 2 * C_V), lambda b, hp, i: (b, i, hp)),
        scratch_shapes=[pltpu.VMEM((MLA_TQ, MLA_TQ), F32), pltpu.VMEM((MLA_TQ, MLA_TQ), BF16),
                        pltpu.VMEM((MLA_TQ, LANES), F32), pltpu.VMEM((MLA_TQ, LANES), F32),
                        pltpu.VMEM((MLA_TQ, LANES), F32)],
        compiler_params=_params("parallel", "parallel", "arbitrary"),
        name="mla_attention",
    )(q.reshape(bsz, seq, qw), k.reshape(bsz, seq, qw), v.reshape(bsz, seq, C_HEADS * LANES))


def mla_mixer(hb, w_in, q_norm, kv_norm, w_q_up, w_kv_up, w_o, h, g, b, bsz, seq):
    q, k, v = mla_prep(hb, w_in, q_norm, kv_norm, w_q_up, w_kv_up, bsz, seq)
    o = mla_attention(q, k, v, bsz, seq)
    return matmul_residual_ln(o.reshape(bsz * seq, -1), w_o.astype(BF16), h, g, b)


ROUTER_TM = 1024


def _router_kernel(h_ref, wr_ref, eb_ref, eidx_ref, gate_ref, slab_sc):
    tm = h_ref.shape[0]
    n_sub = tm // LANES
    per = N_EXPERTS // N_GROUPS
    h = h_ref[...]
    h_hi = h.astype(BF16)
    h_lo = (h - h_hi.astype(F32)).astype(BF16)
    nt = (((1,), (1,)), ((), ()))
    both = lax.dot_general(wr_ref[...], h_hi, nt, preferred_element_type=F32)
    logits = (both[:N_EXPERTS] + both[N_EXPERTS:]
              + lax.dot_general(wr_ref[:N_EXPERTS, :], h_lo, nt, preferred_element_type=F32))
    for j in range(n_sub):
        slab_sc[pl.ds(j, N_EXPERTS, stride=n_sub), :] = logits[:, j * LANES:(j + 1) * LANES]
    x = slab_sc[...].reshape(N_GROUPS, per, n_sub, LANES)
    s4 = 1.0 / (1.0 + jnp.exp(-x))
    c4 = s4 + eb_ref[...].reshape(N_GROUPS, per, n_sub, LANES)
    neg_inf = -jnp.inf
    shape4 = (N_GROUPS, per, n_sub, LANES)
    j_idx = lax.broadcasted_iota(I32, shape4, 1).astype(F32)
    g_idx = lax.broadcasted_iota(I32, shape4, 0).astype(F32)
    flat = g_idx * per + j_idx

    m1 = jnp.max(c4, axis=1, keepdims=True)
    first = jnp.min(jnp.where(c4 == m1, j_idx, per), axis=1, keepdims=True)
    m2 = jnp.max(jnp.where(j_idx == first, neg_inf, c4), axis=1, keepdims=True)
    gs = m1 + m2

    gi = lax.broadcasted_iota(I32, (N_GROUPS, 1, n_sub, LANES), 0).astype(F32)
    keep = jnp.zeros((N_GROUPS, 1, n_sub, LANES), jnp.bool_)
    cur = gs
    for _ in range(TOPK_GROUPS):
        mx = jnp.max(cur, axis=0, keepdims=True)
        pick = gi == jnp.min(jnp.where(cur == mx, gi, N_GROUPS), axis=0, keepdims=True)
        keep = keep | pick
        cur = jnp.where(pick, neg_inf, cur)
    cur = jnp.where(keep, c4, neg_inf)

    idxs, gates = [], []
    for _ in range(TOP_K):
        mx = jnp.max(cur, axis=(0, 1), keepdims=True)
        fi = jnp.min(jnp.where(cur == mx, flat, N_EXPERTS), axis=(0, 1), keepdims=True)
        pick = flat == fi
        gates.append(jnp.sum(jnp.where(pick, s4, 0.0), axis=(0, 1)))
        idxs.append(fi[0, 0])
        cur = jnp.where(pick, neg_inf, cur)
    total = functools.reduce(lambda u, v: u + v, gates)
    for k in range(TOP_K):
        eidx_ref[k] = idxs[k].astype(I32)
        gate_ref[k] = gates[k] / total * ROUTED_SCALE


def moe_router(h, w_router, e_bias, row0, m):
    d = h.shape[1]
    tm = min(ROUTER_TM, m)
    n_sub = tm // LANES
    blk0 = row0 // tm
    bias = jnp.broadcast_to(e_bias.reshape(N_EXPERTS, 1, 1), (N_EXPERTS, n_sub, LANES))
    w_hi = w_router.T.astype(BF16)
    w_split = jnp.concatenate([w_hi, (w_router.T - w_hi.astype(F32)).astype(BF16)], axis=0)
    out = jax.ShapeDtypeStruct((TOP_K, m // LANES, LANES), I32)
    eidx, gate = pl.pallas_call(
        _router_kernel,
        out_shape=(out, jax.ShapeDtypeStruct(out.shape, F32)),
        grid=(m // tm,),
        in_specs=[pl.BlockSpec((tm, d), lambda i: (i + blk0, 0)),
                  pl.BlockSpec((2 * N_EXPERTS, d), lambda i: (0, 0)),
                  pl.BlockSpec((N_EXPERTS, n_sub, LANES), lambda i: (0, 0, 0))],
        out_specs=(pl.BlockSpec((TOP_K, n_sub, LANES), lambda i: (0, i, 0)),
                   pl.BlockSpec((TOP_K, n_sub, LANES), lambda i: (0, i, 0))),
        scratch_shapes=[pltpu.VMEM((N_EXPERTS * n_sub, LANES), F32)],
        compiler_params=_params("parallel"),
        name="moe_router",
    )(h, w_split, bias)
    return eidx.reshape(TOP_K, m), gate.reshape(TOP_K, m)


def _silu(x):
    return x / (1.0 + jnp.exp(-x))


RANK_TM = 1024


def _rank_kernel(eidx_ref, tri_ref, dest_ref, cnt_ref, start_ref, cnt_sc, run_sc):
    p, i = pl.program_id(0), pl.program_id(1)
    tm = eidx_ref.shape[1]
    e = eidx_ref[...]
    ex = lax.broadcasted_iota(I32, (N_EXPERTS, tm), 0)
    onehot = jnp.zeros((N_EXPERTS, tm), F32)
    for k in range(TOP_K):
        onehot = onehot + jnp.where(e[k:k + 1, :] == ex, 1.0, 0.0)
    tile_cnt = jnp.sum(onehot, axis=-1, keepdims=True)

    @pl.when((p == 0) & (i == 0))
    def _():
        cnt_sc[...] = jnp.zeros(cnt_sc.shape, F32)

    @pl.when(p == 0)
    def _():
        cnt_sc[...] += tile_cnt

    @pl.when((p == 1) & (i == 0))
    def _():
        cnt = cnt_sc[...]
        padded = jnp.floor((cnt + (MOE_BLK - 1)) * (1.0 / MOE_BLK)) * MOE_BLK
        below = (lax.broadcasted_iota(I32, (N_EXPERTS, N_EXPERTS), 1)
                 < lax.broadcasted_iota(I32, (N_EXPERTS, N_EXPERTS), 0)).astype(F32)
        start = jnp.dot(below, jnp.broadcast_to(padded, (N_EXPERTS, LANES)),
                        precision=lax.Precision.HIGHEST, preferred_element_type=F32)
        run_sc[...] = start[:, :1]
        cnt_ref[...] = jnp.broadcast_to(cnt, (N_EXPERTS, LANES))
        start_ref[...] = start

    @pl.when(p == 1)
    def _():
        before = jnp.dot(onehot.astype(BF16), tri_ref[...], preferred_element_type=F32) + run_sc[...]
        rows = [jnp.sum(jnp.where(e[k:k + 1, :] == ex, before, 0.0), axis=0, keepdims=True)
                for k in range(TOP_K)]
        dest_ref[...] = jnp.concatenate(rows, axis=0).astype(I32)
        run_sc[...] += tile_cnt


def moe_rank(eidx):
    n_tok = eidx.shape[1]
    tm = min(RANK_TM, n_tok)
    tri = (jnp.arange(tm)[:, None] < jnp.arange(tm)[None, :]).astype(BF16)
    stat = jax.ShapeDtypeStruct((N_EXPERTS, LANES), F32)
    dest, cnt, start = pl.pallas_call(
        _rank_kernel,
        out_shape=(jax.ShapeDtypeStruct((TOP_K, n_tok), I32), stat, stat),
        grid=(2, n_tok // tm),
        in_specs=[pl.BlockSpec((TOP_K, tm), lambda p, i: (0, i)),
                  pl.BlockSpec((tm, tm), lambda p, i: (0, 0))],
        out_specs=(pl.BlockSpec((TOP_K, tm), lambda p, i: (0, i * p)),
                   pl.BlockSpec((N_EXPERTS, LANES), lambda p, i: (0, 0)),
                   pl.BlockSpec((N_EXPERTS, LANES), lambda p, i: (0, 0))),
        scratch_shapes=[pltpu.VMEM((N_EXPERTS, 1), F32), pltpu.VMEM((N_EXPERTS, 1), F32)],
        compiler_params=_params("arbitrary", "arbitrary"),
        name="moe_rank",
    )(eidx, tri)
    return dest, cnt[:, 0].astype(I32), start[:, 0].astype(I32)


def _experts_kernel(blk_e_ref, x_ref, wg_ref, wu_ref, wd_ref, o_ref, wg_sc, wu_sc, wd_sc):
    i = pl.program_id(0)

    @pl.when((i == 0) | (blk_e_ref[i] != blk_e_ref[jnp.maximum(i - 1, 0)]))
    def _():
        wg_sc[...] = wg_ref[...].astype(BF16)
        wu_sc[...] = wu_ref[...].astype(BF16)
        wd_sc[...] = wd_ref[...].astype(BF16)

    x = x_ref[...]
    gate = jnp.dot(x, wg_sc[...], preferred_element_type=F32)
    up = jnp.dot(x, wu_sc[...], preferred_element_type=F32)
    act = (_silu(gate) * up).astype(BF16)
    o_ref[...] = jnp.dot(act, wd_sc[...], preferred_element_type=F32).astype(o_ref.dtype)


def grouped_experts(x_sorted, blk_e, wg, wu, wd, layer):
    n_slot, d = x_sorted.shape
    n_blk = n_slot // MOE_BLK
    by_expert = lambda i, be: (layer, be[i], 0, 0)
    grid_spec = pltpu.PrefetchScalarGridSpec(
        num_scalar_prefetch=1,
        grid=(n_blk,),
        in_specs=[pl.BlockSpec((MOE_BLK, d), lambda i, be: (i, 0)),
                  pl.BlockSpec((None, None, d, D_EXPERT), by_expert),
                  pl.BlockSpec((None, None, d, D_EXPERT), by_expert),
                  pl.BlockSpec((None, None, D_EXPERT, d), by_expert)],
        out_specs=pl.BlockSpec((MOE_BLK, d), lambda i, be: (i, 0)),
        scratch_shapes=[pltpu.VMEM((d, D_EXPERT), BF16), pltpu.VMEM((d, D_EXPERT), BF16),
                        pltpu.VMEM((D_EXPERT, d), BF16)],
    )
    return pl.pallas_call(
        _experts_kernel,
        out_shape=jax.ShapeDtypeStruct((n_slot, d), BF16),
        grid_spec=grid_spec,
        compiler_params=_params("arbitrary"),
        name="grouped_experts",
    )(blk_e, x_sorted, wg, wu, wd)


def _moe_finish_kernel(dils, n_prev, hb_ref, y_ref, gt_ref, wg_ref, wu_ref, wd_ref, h_ref, g_ref, b_ref,
                       *rest):
    o_ref, ob_ref = rest[n_prev:n_prev + 2]
    rest = rest[n_prev + 2:]
    x = hb_ref[...]
    d = x.shape[1]
    act = (_silu(jnp.dot(x, wg_ref[...], preferred_element_type=F32))
           * jnp.dot(x, wu_ref[...], preferred_element_type=F32)).astype(BF16)
    ff = jnp.dot(act, wd_ref[...], preferred_element_type=F32)
    gt = gt_ref[...]
    for k in range(TOP_K):
        ff = ff + gt[:, k:k + 1] * y_ref[k].astype(F32)
    out = _layer_norm_rows(DN_ALPHA * h_ref[...] + ff, g_ref[...], b_ref[...])
    o_ref[...] = out
    ob_ref[...] = out.astype(BF16)
    if dils:
        perm_refs, nat_sc = rest[:-1], rest[-1]
        for j in range(d // LANES):
            nat_sc[j] = out[:, j * LANES:(j + 1) * LANES]
        for p_ref, dil in zip(perm_refs, dils):
            n = out.shape[0] // dil
            for r in range(dil):
                for j in range(d // LANES):
                    p_ref[0, r, :, j * LANES:(j + 1) * LANES] = (
                        nat_sc[j, pl.ds(r, n, stride=dil), :].astype(BF16))


def moe_finish(hb, y_tok, gate_tok, ws_g, ws_u, ws_d, h, g, b, seq, row0, m, dils=(), prev=None, tm=256):
    d = h.shape[1]
    tm = min(tm, seq)
    n_j = seq // tm
    blk0 = row0 // tm
    row = lambda i: (i, 0)
    row_in = lambda i: (i + blk0, 0)
    fixed = lambda i: (0, 0)
    n_all = h.shape[0]
    seq0 = row0 // seq
    out_shape = [jax.ShapeDtypeStruct((n_all, d), F32), jax.ShapeDtypeStruct((n_all, d), BF16)]
    out_specs = [pl.BlockSpec((tm, d), row_in), pl.BlockSpec((tm, d), row_in)]
    for dil in dils:
        out_shape.append(jax.ShapeDtypeStruct((n_all // seq, dil, seq // dil, d), BF16))
        out_specs.append(pl.BlockSpec((1, dil, tm // dil, d), lambda i: (i // n_j + seq0, 0, i % n_j, 0)))
    prev = list(prev) if prev is not None else []
    n_fixed_in = 9
    res = pl.pallas_call(
        functools.partial(_moe_finish_kernel, tuple(dils), len(prev)),
        out_shape=tuple(out_shape),
        input_output_aliases={n_fixed_in + j: j for j in range(len(prev))},
        grid=(m // tm,),
        in_specs=[pl.BlockSpec((tm, d), row_in), pl.BlockSpec((TOP_K, tm, d), lambda i: (0, i, 0)),
                  pl.BlockSpec((tm, TOP_K), row),
                  pl.BlockSpec(ws_g.shape, fixed), pl.BlockSpec(ws_u.shape, fixed),
                  pl.BlockSpec(ws_d.shape, fixed),
                  pl.BlockSpec((tm, d), row_in), pl.BlockSpec((1, d), fixed), pl.BlockSpec((1, d), fixed)]
                 + [pl.BlockSpec(memory_space=pl.ANY)] * len(prev),
        out_specs=tuple(out_specs),
        scratch_shapes=[pltpu.VMEM((d // LANES, tm, LANES), F32)] if dils else [],
        compiler_params=_params("parallel"),
        name="moe_finish",
    )(hb, y_tok, gate_tok, ws_g, ws_u, ws_d, h, g.reshape(1, d), b.reshape(1, d), *prev)
    return list(res)


SORT_E_SHIFT = 19


MOE_SPLITS = 2


def moe_layer(h, hb, w_router, e_bias, wg, wu, wd, layer, ws_g, ws_u, ws_d, g, b, seq, dils=()):
    n_all = h.shape[0]
    n_split = MOE_SPLITS if (n_all // seq) % MOE_SPLITS == 0 else 1
    outs = None
    for sp in range(n_split):
        outs = moe_tokens(h, hb, w_router, e_bias, wg, wu, wd, layer, ws_g, ws_u, ws_d, g, b, seq, dils,
                          sp * (n_all // n_split), n_all // n_split, outs)
    return outs[0], outs[1], [p.reshape(n_all, -1) for p in outs[2:]]


def moe_tokens(h, hb, w_router, e_bias, wg, wu, wd, layer, ws_g, ws_u, ws_d, g, b, seq, dils, row0, n_tok,
               prev):
    d = h.shape[1]
    n_asg = n_tok * TOP_K
    n_pad = N_EXPERTS * MOE_BLK
    assert n_asg <= 1 << (SORT_E_SHIFT - 1) and n_pad <= 1 << (SORT_E_SHIFT - 1)
    eidx, gate = moe_router(h, w_router, e_bias, row0, n_tok)
    dest, counts, pad_start = moe_rank(eidx)
    padded = (counts + MOE_BLK - 1) // MOE_BLK * MOE_BLK
    pad_end = pad_start + padded
    n_blk = n_asg // MOE_BLK + N_EXPERTS
    blk_start = jnp.arange(n_blk, dtype=I32) * MOE_BLK
    blk_e = jnp.minimum(jnp.sum(pad_end[None, :] <= blk_start[:, None], axis=1), N_EXPERTS - 1)

    key_real = (eidx.T.reshape(-1) << SORT_E_SHIFT) | jnp.arange(n_asg, dtype=I32)
    i_pad = jnp.arange(n_pad, dtype=I32)
    e_pad = jnp.sum(jnp.cumsum(padded - counts)[None, :] <= i_pad[:, None], axis=1).astype(I32)
    key_pad = (e_pad << SORT_E_SHIFT) | (1 << (SORT_E_SHIFT - 1)) | i_pad
    low = jnp.sort(jnp.concatenate([key_real, key_pad])) & ((1 << SORT_E_SHIFT) - 1)
    slot_tok = jnp.where(low < (1 << (SORT_E_SHIFT - 1)), low // TOP_K,
                         jnp.arange(n_asg + n_pad, dtype=I32) % n_tok)

    x_sorted = hb.at[slot_tok + row0].get(mode="promise_in_bounds")
    y = grouped_experts(x_sorted, blk_e.astype(I32), wg, wu, wd, layer)
    y_k = y.at[dest.reshape(-1)].get(mode="promise_in_bounds").reshape(TOP_K, n_tok, d)
    return moe_finish(hb, y_k, gate.T, ws_g, ws_u, ws_d, h, g, b, seq, row0, n_tok, dils, prev)


def kernel(x, rel_bias, a_w_in, a_w_o, b_w_in, b_w_o, c_w_in, c_q_norm, c_kv_norm, c_w_q_up,
           c_w_kv_up, c_w_o, ln_g, ln_b, moe_w_router, moe_bias, moe_w_gate, moe_w_up,
           moe_w_down, moe_ws_gate, moe_ws_up, moe_ws_down):
    bsz, seq, d = x.shape
    depth = ln_g.shape[0]
    h = x.reshape(bsz * seq, d)
    hb = h.astype(BF16)
    strip = dsa_bias_strip(rel_bias, seq)
    extra_dils = tuple(dil for _, dil in B_GROUPS if dil > 1)
    hb_perm = []
    for layer in range(depth):
        kind, slot = layer % N_MIXERS, layer // N_MIXERS
        g0, b0 = ln_g[layer, 0], ln_b[layer, 0]
        if kind == 0:
            h, hb = dsa_mixer(hb, a_w_in[slot], a_w_o[slot], strip, h, g0, b0, bsz, seq)
        elif kind == 1:
            hb_by_dil = {1: hb, **dict(zip(extra_dils, hb_perm))}
            h, hb = dilated_mixer(hb_by_dil, b_w_in[slot], b_w_o[slot], rel_bias, h, g0, b0, bsz, seq)
        else:
            h, hb = mla_mixer(hb, c_w_in[slot], c_q_norm[slot], c_kv_norm[slot], c_w_q_up[slot],
                              c_w_kv_up[slot], c_w_o[slot], h, g0, b0, bsz, seq)
        next_dilated = layer + 1 < depth and (layer + 1) % N_MIXERS == 1
        h, hb, hb_perm = moe_layer(h, hb, moe_w_router[layer], moe_bias[layer],
                                   moe_w_gate, moe_w_up, moe_w_down, layer,
                                   moe_ws_gate[layer].astype(BF16),
                                   moe_ws_up[layer].astype(BF16), moe_ws_down[layer].astype(BF16),
                                   ln_g[layer, 1], ln_b[layer, 1], seq,
                                   extra_dils if next_dilated else ())
    return h.reshape(bsz, seq, d)
```

```python
import functools
import math

import jax
import jax.numpy as jnp
from jax import lax
from jax.experimental import pallas as pl
from jax.experimental.pallas import tpu as pltpu

F32 = jnp.float32
BF16 = jnp.bfloat16
I32 = jnp.int32

LANES = 128
VMEM_LIMIT_BYTES = 56 * 1024 * 1024

D_MODEL = 1024
DEPTH = 4
N_MIXERS = 3
NORM_EPS = 1e-5
RMS_EPS = 1e-6
REL_BUCKETS = 32
REL_MAX_DIST = 2048
A_HEADS = 16
A_HEAD_DIM = 128
A_IDX_HEADS = 8
A_IDX_DIM = 64
A_TOPK_MAX = 256
A_Q = A_HEADS * A_HEAD_DIM
B_GROUPS = ((128, 1), (512, 4), (2048, 16))
B_HEADS = 16
B_HEAD_DIM = 64
B_N = 128
C_HEADS = 16
C_Q_RANK = 256
C_KV_RANK = 128
C_NOPE = 64
C_ROPE = 32
C_V = 64
ROPE_BASE = 10000.0
N_EXPERTS = 64
TOP_K = 8
N_GROUPS = 8
TOPK_GROUPS = 4
D_EXPERT = 256
ROUTED_SCALE = 2.5
DN_ALPHA = (2 * DEPTH) ** 0.25

Q_BLK = 128
SEL_BLK = 512
KEY_CHUNK = 256
MOE_BLK = 512
MASK_NEG = -1e30
INT_MIN = -(2 ** 31)


def _params(*sem):
    return pltpu.CompilerParams(dimension_semantics=sem, vmem_limit_bytes=VMEM_LIMIT_BYTES)


def _mm_kernel(x_ref, w_ref, o_ref):
    o_ref[...] = jnp.dot(x_ref[...], w_ref[...], preferred_element_type=F32).astype(o_ref.dtype)


def matmul(x, w, out_dtype, tm=1024, tn=None):
    m, k = x.shape
    n = w.shape[1]
    tn = n if tn is None else tn
    tm = min(tm, m)
    return pl.pallas_call(
        _mm_kernel,
        out_shape=jax.ShapeDtypeStruct((m, n), out_dtype),
        grid=(n // tn, m // tm),
        in_specs=[pl.BlockSpec((tm, k), lambda j, i: (i, 0)),
                  pl.BlockSpec((k, tn), lambda j, i: (0, j))],
        out_specs=pl.BlockSpec((tm, tn), lambda j, i: (i, j)),
        compiler_params=_params("parallel", "parallel"),
        name="matmul",
    )(x, w)


def _layer_norm_rows(z, g, b):
    mu = jnp.mean(z, axis=-1, keepdims=True)
    zc = z - mu
    var = jnp.mean(zc * zc, axis=-1, keepdims=True)
    return zc * lax.rsqrt(var + NORM_EPS) * g + b


def _mm_ln_kernel(x_ref, w_ref, h_ref, g_ref, b_ref, o_ref, ob_ref):
    y = jnp.dot(x_ref[...], w_ref[...], preferred_element_type=F32)
    out = _layer_norm_rows(DN_ALPHA * h_ref[...] + y, g_ref[...], b_ref[...])
    o_ref[...] = out
    ob_ref[...] = out.astype(BF16)


def matmul_residual_ln(x, w, h, g, b, tm=512):
    m, k = x.shape
    d = w.shape[1]
    tm = min(tm, m)
    row = lambda i: (i, 0)
    fixed = lambda i: (0, 0)
    return pl.pallas_call(
        _mm_ln_kernel,
        out_shape=(jax.ShapeDtypeStruct((m, d), F32), jax.ShapeDtypeStruct((m, d), BF16)),
        grid=(m // tm,),
        in_specs=[pl.BlockSpec((tm, k), row), pl.BlockSpec((k, d), fixed),
                  pl.BlockSpec((tm, d), row), pl.BlockSpec((1, d), fixed),
                  pl.BlockSpec((1, d), fixed)],
        out_specs=(pl.BlockSpec((tm, d), row), pl.BlockSpec((tm, d), row)),
        compiler_params=_params("parallel"),
        name="matmul_residual_ln",
    )(x, w, h, g.reshape(1, d), b.reshape(1, d))


def _t5_bucket(dist):
    exact = REL_BUCKETS // 2
    d_f = jnp.maximum(dist, 1).astype(F32)
    large = exact + (jnp.log(d_f / exact) / math.log(REL_MAX_DIST / exact)
                     * (REL_BUCKETS - exact)).astype(I32)
    return jnp.where(dist < exact, dist, jnp.minimum(large, REL_BUCKETS - 1))


def _bias_by_distance(rel_bias, dist):
    return rel_bias[_t5_bucket(dist)]


def _toeplitz(f, rows, cols, off):
    length = f.shape[-1]
    period = rows + cols - 1
    j = jnp.arange(period)
    shift = jnp.where(j < cols, -j, period - j)
    v = f[:, jnp.clip(off + shift, 0, length - 1)]
    skew = jnp.tile(v, (1, rows))[:, :rows * (period - 1)].reshape(-1, rows, period - 1)
    return skew[:, :, :cols]


DSA_SUB = 32
LOG2E = math.log2(math.e)


def _dsa_select(k_sel, n_keys, t0, iq_ref, ik_ref, key_sc, mask_sc, cut_sc):
    rows = lax.broadcasted_iota(I32, (SEL_BLK, n_keys), 0) + t0
    cols = lax.broadcasted_iota(I32, (SEL_BLK, n_keys), 1)
    valid = cols <= rows

    ik = ik_ref[0, :n_keys, :].astype(BF16)
    w_all = iq_ref[0, :, A_IDX_HEADS * LANES:] * ((A_IDX_DIM * A_IDX_HEADS) ** -0.5)
    score = jnp.zeros((SEL_BLK, n_keys), F32)
    for h in range(A_IDX_HEADS):
        qh = iq_ref[0, :, h * LANES:(h + 1) * LANES].astype(BF16)
        rel = lax.dot_general(qh, ik, (((1,), (1,)), ((), ())), preferred_element_type=F32)
        score = score + w_all[:, A_IDX_DIM + h:A_IDX_DIM + h + 1] * jnp.maximum(rel, 0.0)
    score = jnp.where(score == 0.0, 0.0, score)

    bits = pltpu.bitcast(score, I32)
    key_sc[:, :n_keys] = jnp.where(valid, bits ^ ((bits >> 31) & 0x7FFFFFFF), INT_MIN)

    def search(it, ans_u):
        cand_u = ans_u | lax.shift_left(jnp.int32(1), 31 - it)
        cand_s = cand_u ^ INT_MIN
        cnt = jnp.sum(jnp.where(key_sc[:, :n_keys] >= cand_s, 1.0, 0.0), axis=-1, keepdims=True)
        return jnp.where(cnt >= k_sel, cand_u, ans_u)

    thr = lax.fori_loop(0, 32, search, jnp.zeros((SEL_BLK, 1), I32)) ^ INT_MIN

    key = key_sc[:, :n_keys]
    gt = key > thr
    eq = key == thr
    need = k_sel - jnp.sum(jnp.where(gt, 1.0, 0.0), axis=-1, keepdims=True)
    n_eq = jnp.sum(jnp.where(eq, 1.0, 0.0), axis=-1, keepdims=True)
    cut_sc[...] = jnp.full((SEL_BLK, 1), n_keys, I32)
    surplus = jnp.where((n_eq > need) & (thr != INT_MIN), 1.0, 0.0)

    @pl.when(jnp.max(surplus) > 0.0)
    def _():
        def tie_search(it, ans):
            cand = ans | lax.shift_left(jnp.int32(1), (n_keys.bit_length() - 1) - it)
            hit = (key_sc[:, :n_keys] == thr) & (cols < cand)
            cnt = jnp.sum(jnp.where(hit, 1.0, 0.0), axis=-1, keepdims=True)
            return jnp.where(cnt < need, cand, ans)
        cut_sc[...] = lax.fori_loop(0, n_keys.bit_length(), tie_search, jnp.zeros((SEL_BLK, 1), I32))

    selected = valid & (gt | (eq & (cols <= cut_sc[...])))
    mask_sc[:, :n_keys] = jnp.where(selected, 0.0, MASK_NEG)


def _dsa_kernel(k_sel, seq, q_ref, kv_ref, iq_ref, ik_ref, strip_ref, o_ref,
                key_sc, mask_sc, cut_sc, qs_sc, s_sc, p_sc, acc_sc, m_sc, a_sc, ve_sc):
    i = pl.program_id(1)
    t0 = i * SEL_BLK
    blk_per_chunk = KEY_CHUNK // Q_BLK

    @pl.when(i == 0)
    def _():
        ve_sc[:, :A_HEAD_DIM] = kv_ref[0, :, A_HEAD_DIM:]
        ve_sc[:, A_HEAD_DIM:] = jnp.ones((seq, A_HEAD_DIM), BF16)

    for j in range(seq // SEL_BLK):
        @pl.when(i == j)
        def _(j=j):
            _dsa_select(k_sel, (j + 1) * SEL_BLK, t0, iq_ref, ik_ref, key_sc, mask_sc, cut_sc)

    n_strip_blk = seq // Q_BLK - 1
    n_tile = KEY_CHUNK // LANES

    def attend(sub, carry):
        _dsa_attend(sub, i * (SEL_BLK // Q_BLK) + sub)
        return carry

    def _dsa_attend(sub, qi):
        r0 = pl.multiple_of(sub * Q_BLK, Q_BLK)
        for h in range(A_HEADS):
            qs_sc[h * Q_BLK:(h + 1) * Q_BLK, :] = q_ref[0, pl.ds(r0, Q_BLK),
                                                        h * A_HEAD_DIM:(h + 1) * A_HEAD_DIM]
        m_sc[...] = jnp.full(m_sc.shape, MASK_NEG, F32)
        acc_sc[...] = jnp.zeros(acc_sc.shape, F32)
        lax.fori_loop(0, (qi * Q_BLK + Q_BLK + KEY_CHUNK - 1) // KEY_CHUNK,
                      functools.partial(chunk, r0, qi), 0)
        for h in range(A_HEADS):
            r = slice(h * Q_BLK, (h + 1) * Q_BLK)
            o_ref[0, pl.ds(r0, Q_BLK), h * A_HEAD_DIM:(h + 1) * A_HEAD_DIM] = (
                acc_sc[r, :A_HEAD_DIM] / acc_sc[r, A_HEAD_DIM:]).astype(BF16)

    def chunk(r0, qi, c, carry):
        k0 = pl.multiple_of(c * KEY_CHUNK, KEY_CHUNK)
        kc = kv_ref[0, pl.ds(k0, KEY_CHUNK), :A_HEAD_DIM]
        s_sc[...] = lax.dot_general(qs_sc[...], kc, (((1,), (1,)), ((), ())),
                                    preferred_element_type=F32)
        w0 = (c * blk_per_chunk - qi + n_strip_blk) * Q_BLK
        for h in range(A_HEADS):
            for rb in range(Q_BLK // DSA_SUB):
                qr = slice(rb * DSA_SUB, (rb + 1) * DSA_SUB)
                mr = pl.ds(pl.multiple_of(r0 + rb * DSA_SUB, DSA_SUB), DSA_SUB)
                r = slice(h * Q_BLK + rb * DSA_SUB, h * Q_BLK + (rb + 1) * DSA_SUB)
                t = []
                for jt in range(n_tile):
                    ws = pl.ds(pl.multiple_of(w0 + jt * LANES, LANES), LANES)
                    ks = pl.ds(pl.multiple_of(k0 + jt * LANES, LANES), LANES)
                    t.append(s_sc[r, jt * LANES:(jt + 1) * LANES]
                             + strip_ref[h, qr, ws].astype(F32) + mask_sc[mr, ks])
                mx = functools.reduce(jnp.maximum, t)
                m_old = m_sc[r, :]
                m_new = jnp.maximum(m_old, jnp.max(mx, axis=-1, keepdims=True))
                a_sc[r, :] = jnp.exp2(m_old - m_new)
                m_sc[r, :] = m_new
                for jt in range(n_tile):
                    p_sc[r, jt * LANES:(jt + 1) * LANES] = jnp.exp2(t[jt] - m_new).astype(BF16)
        pv = jnp.dot(p_sc[...], ve_sc[pl.ds(k0, KEY_CHUNK), :], preferred_element_type=F32)
        for half in range(2):
            hs = slice(half * A_HEAD_DIM, (half + 1) * A_HEAD_DIM)
            acc_sc[:, hs] = acc_sc[:, hs] * a_sc[...] + pv[:, hs]
        return carry

    lax.fori_loop(0, SEL_BLK // Q_BLK, attend, 0)


def dsa_attention(qkv, idx, strip, bsz, seq):
    assert seq % SEL_BLK == 0 and SEL_BLK % KEY_CHUNK == 0 and KEY_CHUNK % Q_BLK == 0
    k_sel = min(A_TOPK_MAX, seq // 4)
    rows = A_HEADS * Q_BLK
    n_kv_blk = A_Q // (2 * A_HEAD_DIM)
    n_ik_blk = A_IDX_HEADS
    idx_w = (A_IDX_HEADS + 1) * LANES
    return pl.pallas_call(
        functools.partial(_dsa_kernel, k_sel, seq),
        out_shape=jax.ShapeDtypeStruct((bsz, seq, A_Q), BF16),
        grid=(bsz, seq // SEL_BLK),
        in_specs=[pl.BlockSpec((1, SEL_BLK, A_Q), lambda b, i: (b, i, 0)),
                  pl.BlockSpec((1, seq, 2 * A_HEAD_DIM), lambda b, i: (b, 0, n_kv_blk)),
                  pl.BlockSpec((1, SEL_BLK, idx_w), lambda b, i: (b, i, 0)),
                  pl.BlockSpec((1, seq, LANES), lambda b, i: (b, 0, n_ik_blk)),
                  pl.BlockSpec(strip.shape, lambda b, i: (0, 0, 0), pipeline_mode=pl.Buffered(1))],
        out_specs=pl.BlockSpec((1, SEL_BLK, A_Q), lambda b, i: (b, i, 0)),
        scratch_shapes=[pltpu.VMEM((SEL_BLK, seq), I32),
                        pltpu.VMEM((SEL_BLK, seq), F32),
                        pltpu.VMEM((SEL_BLK, 1), I32),
                        pltpu.VMEM((rows, A_HEAD_DIM), BF16),
                        pltpu.VMEM((rows, KEY_CHUNK), F32),
                        pltpu.VMEM((rows, KEY_CHUNK), BF16),
                        pltpu.VMEM((rows, 2 * A_HEAD_DIM), F32),
                        pltpu.VMEM((rows, LANES), F32),
                        pltpu.VMEM((rows, LANES), F32),
                        pltpu.VMEM((seq, 2 * A_HEAD_DIM), BF16)],
        compiler_params=_params("parallel", "arbitrary"),
        name="dsa_attention",
    )(qkv, qkv, idx, idx, strip)


def dsa_bias_strip(rel_bias, seq):
    width = seq + KEY_CHUNK - Q_BLK
    by_dist = _bias_by_distance(rel_bias, jnp.arange(seq)).T
    return (_toeplitz(by_dist, Q_BLK, width, seq - Q_BLK) * LOG2E).astype(BF16)


def dsa_weights(w_in):
    d = w_in.shape[0]
    wq = w_in[:, :A_Q] * (A_HEAD_DIM ** -0.5 * LOG2E)
    wkv = w_in[:, A_Q:A_Q + 2 * A_HEAD_DIM]
    o = A_Q + 2 * A_HEAD_DIM
    n_qi = A_IDX_HEADS * A_IDX_DIM
    wqi = w_in[:, o:o + n_qi].reshape(d, A_IDX_HEADS, A_IDX_DIM)
    wqi = jnp.pad(wqi, ((0, 0), (0, 0), (0, LANES - A_IDX_DIM))).reshape(d, A_IDX_HEADS * LANES)
    wkw = jnp.pad(w_in[:, o + n_qi:], ((0, 0), (0, LANES - A_IDX_DIM - A_IDX_HEADS)))
    return (jnp.concatenate([wq, wkv], axis=1).astype(BF16),
            jnp.concatenate([wqi, wkw], axis=1).astype(BF16))


def dsa_mixer(hb, w_in, w_o, strip, h, g, b, bsz, seq):
    w_qkv, w_idx = dsa_weights(w_in)
    qkv = matmul(hb, w_qkv, BF16, tn=w_qkv.shape[1] // 2)
    idx = matmul(hb, w_idx, F32)
    o = dsa_attention(qkv.reshape(bsz, seq, -1), idx.reshape(bsz, seq, -1), strip, bsz, seq)
    return matmul_residual_ln(o.reshape(bsz * seq, A_Q), w_o.astype(BF16), h, g, b)


def _dilated_kernel(q_ref, kp_ref, kc_ref, vp_ref, vc_ref, bias_ref, o_ref, lse_ref):
    c = pl.program_id(1)
    col = lax.broadcasted_iota(I32, (B_N, 2 * B_N), 1)
    has_prev = (col >= B_N) | (c > 0)
    lane = lax.broadcasted_iota(I32, (B_N, LANES), 1)
    low_half = lane < B_HEAD_DIM
    lse_tile = jnp.zeros((B_N, LANES), F32)
    ones = jnp.ones((2 * B_N, LANES), BF16)
    for pr in range(B_HEADS * B_HEAD_DIM // LANES):
        ps = slice(pr * LANES, (pr + 1) * LANES)
        q2 = q_ref[0, :, ps]
        kk = jnp.concatenate([kp_ref[0, :, ps], kc_ref[0, :, ps]], axis=0)
        ve = jnp.concatenate([jnp.concatenate([vp_ref[0, :, ps], vc_ref[0, :, ps]], axis=0), ones],
                             axis=1)
        out2 = None
        for hh in range(LANES // B_HEAD_DIM):
            h = pr * (LANES // B_HEAD_DIM) + hh
            mine = low_half if hh == 0 else jnp.logical_not(low_half)
            qh = jnp.where(mine, q2, jnp.zeros_like(q2))
            s = lax.dot_general(qh, kk, (((1,), (1,)), ((), ())), preferred_element_type=F32)
            s = jnp.where(has_prev, s + bias_ref[h], MASK_NEG)
            m = jnp.max(s, axis=-1, keepdims=True)
            pv = jnp.dot(jnp.exp2(s - m).astype(BF16), ve, preferred_element_type=F32)
            l = pv[:, LANES:]
            o_h = pv[:, :LANES] / l
            out2 = o_h if out2 is None else jnp.where(mine, o_h, out2)
            lse_tile = jnp.where(lane == h, (m + jnp.log2(l)) * (1.0 / LOG2E), lse_tile)
        o_ref[0, :, ps] = out2.astype(BF16)
    lse_ref[0] = lse_tile


def dilated_group(proj, bias, n_seq, length):
    width = B_HEADS * B_HEAD_DIM
    pv = proj.reshape(n_seq, length, 3 * width)
    blk = (1, B_N, width)

    def spec(which, prev):
        if prev:
            return pl.BlockSpec(blk, lambda s, c: (s, jnp.maximum(c - 1, 0), which))
        return pl.BlockSpec(blk, lambda s, c: (s, c, which))

    return pl.pallas_call(
        _dilated_kernel,
        out_shape=(jax.ShapeDtypeStruct((n_seq, length, width), BF16),
                   jax.ShapeDtypeStruct((n_seq, length, LANES), F32)),
        grid=(n_seq, length // B_N),
        in_specs=[spec(0, False), spec(1, True), spec(1, False), spec(2, True), spec(2, False),
                  pl.BlockSpec(bias.shape, lambda s, c: (0, 0, 0))],
        out_specs=(pl.BlockSpec(blk, lambda s, c: (s, c, 0)),
                   pl.BlockSpec((1, B_N, LANES), lambda s, c: (s, c, 0))),
        compiler_params=_params("parallel", "parallel"),
        name="dilated_group",
    )(pv, pv, pv, pv, pv, bias)


def dilated_bias(rel_bias, dil):
    ii = jnp.arange(B_N)[:, None]
    jj = jnp.arange(2 * B_N)[None, :]
    delta = B_N + ii - jj
    band = (delta >= 0) & (delta <= B_N)
    by_delta = _bias_by_distance(rel_bias, jnp.arange(2 * B_N) * dil).T
    bias = _toeplitz(by_delta, B_N, 2 * B_N, B_N)
    return jnp.where(band[None], bias.astype(F32) * LOG2E, MASK_NEG)


def _dilated_merge_kernel(dils, o0_ref, o1_ref, o2_ref, l0_ref, l1_ref, l2_ref, e_ref, w_ref,
                          h_ref, g_ref, b_ref, o_ref, ob_ref, *nat_sc):
    def natural(o_g, l_g, dil, scratch):
        if dil == 1:
            return o_g[0, 0].astype(F32), l_g[0, 0]
        o_sc, l_sc = scratch
        n_tiles, n = o_sc.shape[0], o_sc.shape[1] // dil
        for r in range(dil):
            for j in range(n_tiles):
                o_sc[j, pl.ds(r, n, stride=dil), :] = o_g[0, r, :, j * LANES:(j + 1) * LANES].astype(F32)
            l_sc[pl.ds(r, n, stride=dil), :] = l_g[0, r]
        return jnp.concatenate([o_sc[j] for j in range(n_tiles)], axis=-1), l_sc[...]

    nat = []
    for gi, (o_g, l_g) in enumerate(((o0_ref, l0_ref), (o1_ref, l1_ref), (o2_ref, l2_ref))):
        nat.append(natural(o_g, l_g, dils[gi], nat_sc[2 * gi:2 * gi + 2]))
    (v0, l0), (v1, l1), (v2, l2) = nat
    m = jnp.maximum(jnp.maximum(l0, l1), l2)
    e0, e1, e2 = jnp.exp(l0 - m), jnp.exp(l1 - m), jnp.exp(l2 - m)
    inv = 1.0 / (e0 + e1 + e2)
    e_mat = e_ref[...]

    def spread(wt):
        hi = wt.astype(BF16)
        lo = (wt - hi.astype(F32)).astype(BF16)
        return (jnp.dot(hi, e_mat, preferred_element_type=F32)
                + jnp.dot(lo, e_mat, preferred_element_type=F32))

    mix = spread(e0 * inv) * v0 + spread(e1 * inv) * v1 + spread(e2 * inv) * v2
    y = jnp.dot(mix.astype(BF16), w_ref[...], preferred_element_type=F32)
    out = _layer_norm_rows(DN_ALPHA * h_ref[...] + y, g_ref[...], b_ref[...])
    o_ref[...] = out
    ob_ref[...] = out.astype(BF16)


def dilated_merge(outs, lses, dils, w_o, h, g, b, bsz, seq, tm=512):
    m, d = h.shape
    width = B_HEADS * B_HEAD_DIM
    tm = min(tm, seq)
    n_j = seq // tm
    expand = (jnp.arange(LANES)[:, None] == (jnp.arange(width)[None, :] // B_HEAD_DIM)).astype(BF16)
    row = lambda bi, j: (bi * n_j + j, 0)
    fixed = lambda bi, j: (0, 0)
    grouped = lambda bi, j: (bi, 0, j, 0)
    o_specs = [pl.BlockSpec((1, dl, tm // dl, width), grouped) for dl in dils]
    l_specs = [pl.BlockSpec((1, dl, tm // dl, LANES), grouped) for dl in dils]
    scratch = []
    for dl in dils:
        if dl > 1:
            scratch += [pltpu.VMEM((width // LANES, tm, LANES), F32), pltpu.VMEM((tm, LANES), F32)]
        else:
            scratch += [pltpu.VMEM((8, LANES), F32), pltpu.VMEM((8, LANES), F32)]
    outs = [o.reshape(bsz, dl, seq // dl, width) for o, dl in zip(outs, dils)]
    lses = [l.reshape(bsz, dl, seq // dl, LANES) for l, dl in zip(lses, dils)]
    return pl.pallas_call(
        functools.partial(_dilated_merge_kernel, tuple(dils)),
        out_shape=(jax.ShapeDtypeStruct((m, d), F32), jax.ShapeDtypeStruct((m, d), BF16)),
        grid=(bsz, n_j),
        in_specs=o_specs + l_specs
                 + [pl.BlockSpec((LANES, width), fixed), pl.BlockSpec((width, d), fixed),
                    pl.BlockSpec((tm, d), row), pl.BlockSpec((1, d), fixed), pl.BlockSpec((1, d), fixed)],
        out_specs=(pl.BlockSpec((tm, d), row), pl.BlockSpec((tm, d), row)),
        scratch_shapes=scratch,
        compiler_params=_params("parallel", "parallel"),
        name="dilated_merge",
    )(*outs, *lses, expand, w_o.astype(BF16), h, g.reshape(1, d), b.reshape(1, d))


def dilated_mixer(hb_by_dil, w_in, w_o, rel_bias, h, g, b, bsz, seq):
    width = B_HEADS * B_HEAD_DIM
    outs, lses, dils = [], [], []
    for gi, (window, dil) in enumerate(B_GROUPS):
        assert window // dil == B_N and seq % window == 0
        w_g = w_in[:, gi * 3 * width:(gi + 1) * 3 * width]
        w_g = w_g.at[:, :width].multiply(B_HEAD_DIM ** -0.5 * LOG2E)
        proj = matmul(hb_by_dil[dil], w_g.astype(BF16), BF16, tn=3 * width // 2)
        o, lse = dilated_group(proj, dilated_bias(rel_bias, dil), bsz * dil, seq // dil)
        outs.append(o)
        lses.append(lse)
        dils.append(dil)
    return dilated_merge(outs, lses, dils, w_o, h, g, b, bsz, seq)


C_PAD = 128


def _mla_prep_kernel(x_ref, win_ref, qn_ref, kvn_ref, wqa_ref, wqb_ref, wka_ref, wv_ref,
                     cos_ref, sin_ref, vone_ref, q_ref, k_ref, v_ref):
    c = jnp.dot(x_ref[...], win_ref[...], preferred_element_type=F32)
    cos, sin = cos_ref[...], sin_ref[...]

    def rms(v, gain):
        return (v * lax.rsqrt(jnp.mean(v * v, axis=-1, keepdims=True) + RMS_EPS) * gain).astype(BF16)

    nq = rms(c[:, :C_Q_RANK], qn_ref[...])
    nkv = rms(c[:, C_Q_RANK:C_Q_RANK + C_KV_RANK], kvn_ref[...])
    o = C_Q_RANK + C_KV_RANK
    k_rope = c[:, o:o + C_PAD] * cos + c[:, o + C_PAD:o + 2 * C_PAD] * sin
    qa = jnp.dot(nq, wqa_ref[...], preferred_element_type=F32)
    qb = jnp.dot(nq, wqb_ref[...], preferred_element_type=F32)
    kn = jnp.dot(nkv, wka_ref[...], preferred_element_type=F32)
    for h in range(C_HEADS):
        hs = slice(h * C_PAD, (h + 1) * C_PAD)
        q_ref[:, hs] = (qa[:, hs] * cos + qb[:, hs] * sin).astype(BF16)
        k_ref[:, hs] = (kn[:, hs] + k_rope).astype(BF16)
    v_ref[...] = (jnp.dot(nkv, wv_ref[...], preferred_element_type=F32) + vone_ref[...]).astype(BF16)


def _rot_half_cols(w):
    half = w.shape[-1] // 2
    return jnp.concatenate([-w[..., half:], w[..., :half]], axis=-1)


def mla_prep(hb, w_in, q_norm, kv_norm, w_q_up, w_kv_up, bsz, seq, tm=512):
    d = w_in.shape[0]
    scale = (C_NOPE + C_ROPE) ** -0.5 * LOG2E
    pad_r = C_PAD - C_NOPE - C_ROPE
    w_kr = w_in[:, C_Q_RANK + C_KV_RANK:]

    def rope_slot(w):
        return jnp.pad(w, ((0, 0), (C_NOPE, pad_r)))

    win = jnp.concatenate([w_in[:, :C_Q_RANK + C_KV_RANK], rope_slot(w_kr),
                           rope_slot(_rot_half_cols(w_kr))], axis=1).astype(BF16)
    wq = w_q_up.reshape(C_Q_RANK, C_HEADS, C_NOPE + C_ROPE) * scale
    wqa = jnp.pad(wq, ((0, 0), (0, 0), (0, pad_r))).reshape(C_Q_RANK, C_HEADS * C_PAD).astype(BF16)
    wqb = jnp.pad(_rot_half_cols(wq[..., C_NOPE:]), ((0, 0), (0, 0), (C_NOPE, pad_r)))
    wqb = wqb.reshape(C_Q_RANK, C_HEADS * C_PAD).astype(BF16)
    wkv = w_kv_up.reshape(C_KV_RANK, C_HEADS, C_NOPE + C_V)
    wka = jnp.pad(wkv[..., :C_NOPE], ((0, 0), (0, 0), (0, C_PAD - C_NOPE)))
    wka = wka.reshape(C_KV_RANK, C_HEADS * C_PAD).astype(BF16)
    wv = jnp.pad(wkv[..., C_NOPE:], ((0, 0), (0, 0), (0, LANES - C_V)))
    wv = wv.reshape(C_KV_RANK, C_HEADS * LANES).astype(BF16)
    v_ones = jnp.tile(jnp.concatenate([jnp.zeros((1, C_V), F32), jnp.ones((1, LANES - C_V), F32)], axis=1),
                      (1, C_HEADS))

    half = C_ROPE // 2
    inv = ROPE_BASE ** (-jnp.arange(half, dtype=F32) / half)
    ang = jnp.arange(seq, dtype=F32)[:, None] * inv[None, :]
    ones, zeros = jnp.ones((seq, C_NOPE), F32), jnp.zeros((seq, pad_r), F32)
    cos = jnp.concatenate([ones, jnp.cos(ang), jnp.cos(ang), zeros], axis=1)
    sin = jnp.concatenate([0 * ones, jnp.sin(ang), jnp.sin(ang), zeros], axis=1)

    m = bsz * seq
    tm = min(tm, seq)
    n_pos_blk = seq // tm
    row = lambda i: (i, 0)
    fixed = lambda i: (0, 0)
    pos = lambda i: (i % n_pos_blk, 0)
    full = lambda a: pl.BlockSpec(a.shape, fixed)
    qw, vw = C_HEADS * C_PAD, C_HEADS * LANES
    return pl.pallas_call(
        _mla_prep_kernel,
        out_shape=(jax.ShapeDtypeStruct((m, qw), BF16), jax.ShapeDtypeStruct((m, qw), BF16),
                   jax.ShapeDtypeStruct((m, vw), BF16)),
        grid=(m // tm,),
        in_specs=[pl.BlockSpec((tm, d), row), full(win),
                  pl.BlockSpec((1, C_Q_RANK), fixed), pl.BlockSpec((1, C_KV_RANK), fixed),
                  full(wqa), full(wqb), full(wka), full(wv),
                  pl.BlockSpec((tm, C_PAD), pos), pl.BlockSpec((tm, C_PAD), pos), full(v_ones)],
        out_specs=(pl.BlockSpec((tm, qw), row), pl.BlockSpec((tm, qw), row),
                   pl.BlockSpec((tm, vw), row)),
        compiler_params=_params("parallel"),
        name="mla_prep",
    )(hb, win, q_norm.reshape(1, -1), kv_norm.reshape(1, -1), wqa, wqb, wka, wv, cos, sin, v_ones)


MLA_TQ = 512


MLA_SUB = 32


def _mla_attn_kernel(q_ref, k_ref, v_ref, o_ref, s_sc, p_sc, m_sc, a_sc, acc_sc):
    i = pl.program_id(2)
    tq = MLA_TQ
    n_tile = tq // LANES
    row_in_sub = lax.broadcasted_iota(I32, (MLA_SUB, LANES), 0)
    col_in_tile = lax.broadcasted_iota(I32, (MLA_SUB, LANES), 1)
    outs = []
    for hh in range(2):
        q = q_ref[0, :, hh * C_PAD:(hh + 1) * C_PAD]
        m_sc[...] = jnp.full(m_sc.shape, MASK_NEG, F32)
        acc_sc[...] = jnp.zeros(acc_sc.shape, F32)

        def step(c, diag):
            k0 = pl.multiple_of(c * tq, tq)
            kc = k_ref[0, pl.ds(k0, tq), hh * C_PAD:(hh + 1) * C_PAD]
            s_sc[...] = lax.dot_general(q, kc, (((1,), (1,)), ((), ())), preferred_element_type=F32)
            for rb in range(tq // MLA_SUB):
                r = slice(rb * MLA_SUB, (rb + 1) * MLA_SUB)
                n_act = (rb * MLA_SUB + MLA_SUB - 1) // LANES + 1 if diag else n_tile
                t = [s_sc[r, jt * LANES:(jt + 1) * LANES] for jt in range(n_act)]
                if diag:
                    edge = n_act - 1
                    t[edge] = jnp.where(col_in_tile + edge * LANES <= row_in_sub + rb * MLA_SUB,
                                        t[edge], MASK_NEG)
                m_old = m_sc[r, :]
                m_new = jnp.maximum(m_old, jnp.max(functools.reduce(jnp.maximum, t), axis=-1, keepdims=True))
                a_sc[r, :] = jnp.exp2(m_old - m_new)
                m_sc[r, :] = m_new
                for jt in range(n_tile):
                    p_sc[r, jt * LANES:(jt + 1) * LANES] = (
                        jnp.exp2(t[jt] - m_new).astype(BF16) if jt < n_act
                        else jnp.zeros((MLA_SUB, LANES), BF16))
            pv = jnp.dot(p_sc[...], v_ref[0, pl.ds(k0, tq), hh * LANES:(hh + 1) * LANES],
                         preferred_element_type=F32)
            acc_sc[...] = acc_sc[...] * a_sc[...] + pv

        def body(c, carry):
            step(c, False)
            return carry

        lax.fori_loop(0, i, body, 0)
        step(i, True)
        outs.append(acc_sc[:, :C_V] / acc_sc[:, C_V:])
    o_ref[0] = jnp.concatenate(outs, axis=-1).astype(BF16)


def mla_attention(q, k, v, bsz, seq):
    qw, vw = C_HEADS * C_PAD, C_HEADS * C_V
    return pl.pallas_call(
        _mla_attn_kernel,
        out_shape=jax.ShapeDtypeStruct((bsz, seq, vw), BF16),
        grid=(bsz, C_HEADS // 2, seq // MLA_TQ),
        in_specs=[pl.BlockSpec((1, MLA_TQ, 2 * C_PAD), lambda b, hp, i: (b, i, hp)),
                  pl.BlockSpec((1, seq, 2 * C_PAD), lambda b, hp, i: (b, 0, hp)),
                  pl.BlockSpec((1, seq, 2 * LANES), lambda b, hp, i: (b, 0, hp))],
        out_specs=pl.BlockSpec((1, MLA_TQ, 2 * C_V), lambda b, hp, i: (b, i, hp)),
        scratch_shapes=[pltpu.VMEM((MLA_TQ, MLA_TQ), F32), pltpu.VMEM((MLA_TQ, MLA_TQ), BF16),
                        pltpu.VMEM((MLA_TQ, LANES), F32), pltpu.VMEM((MLA_TQ, LANES), F32),
                        pltpu.VMEM((MLA_TQ, LANES), F32)],
        compiler_params=_params("parallel", "parallel", "arbitrary"),
        name="mla_attention",
    )(q.reshape(bsz, seq, qw), k.reshape(bsz, seq, qw), v.reshape(bsz, seq, C_HEADS * LANES))


def mla_mixer(hb, w_in, q_norm, kv_norm, w_q_up, w_kv_up, w_o, h, g, b, bsz, seq):
    q, k, v = mla_prep(hb, w_in, q_norm, kv_norm, w_q_up, w_kv_up, bsz, seq)
    o = mla_attention(q, k, v, bsz, seq)
    return matmul_residual_ln(o.reshape(bsz * seq, -1), w_o.astype(BF16), h, g, b)


ROUTER_TM = 1024


def _router_kernel(h_ref, wr_ref, eb_ref, eidx_ref, gate_ref, slab_sc):
    tm = h_ref.shape[0]
    n_sub = tm // LANES
    per = N_EXPERTS // N_GROUPS
    h = h_ref[...]
    h_hi = h.astype(BF16)
    h_lo = (h - h_hi.astype(F32)).astype(BF16)
    nt = (((1,), (1,)), ((), ()))
    both = lax.dot_general(wr_ref[...], h_hi, nt, preferred_element_type=F32)
    logits = (both[:N_EXPERTS] + both[N_EXPERTS:]
              + lax.dot_general(wr_ref[:N_EXPERTS, :], h_lo, nt, preferred_element_type=F32))
    for j in range(n_sub):
        slab_sc[pl.ds(j, N_EXPERTS, stride=n_sub), :] = logits[:, j * LANES:(j + 1) * LANES]
    x = slab_sc[...].reshape(N_GROUPS, per, n_sub, LANES)
    s4 = 1.0 / (1.0 + jnp.exp(-x))
    c4 = s4 + eb_ref[...].reshape(N_GROUPS, per, n_sub, LANES)
    neg_inf = -jnp.inf
    shape4 = (N_GROUPS, per, n_sub, LANES)
    j_idx = lax.broadcasted_iota(I32, shape4, 1).astype(F32)
    g_idx = lax.broadcasted_iota(I32, shape4, 0).astype(F32)
    flat = g_idx * per + j_idx

    m1 = jnp.max(c4, axis=1, keepdims=True)
    first = jnp.min(jnp.where(c4 == m1, j_idx, per), axis=1, keepdims=True)
    m2 = jnp.max(jnp.where(j_idx == first, neg_inf, c4), axis=1, keepdims=True)
    gs = m1 + m2

    gi = lax.broadcasted_iota(I32, (N_GROUPS, 1, n_sub, LANES), 0).astype(F32)
    keep = jnp.zeros((N_GROUPS, 1, n_sub, LANES), jnp.bool_)
    cur = gs
    for _ in range(TOPK_GROUPS):
        mx = jnp.max(cur, axis=0, keepdims=True)
        pick = gi == jnp.min(jnp.where(cur == mx, gi, N_GROUPS), axis=0, keepdims=True)
        keep = keep | pick
        cur = jnp.where(pick, neg_inf, cur)
    cur = jnp.where(keep, c4, neg_inf)

    idxs, gates = [], []
    for _ in range(TOP_K):
        mx = jnp.max(cur, axis=(0, 1), keepdims=True)
        fi = jnp.min(jnp.where(cur == mx, flat, N_EXPERTS), axis=(0, 1), keepdims=True)
        pick = flat == fi
        gates.append(jnp.sum(jnp.where(pick, s4, 0.0), axis=(0, 1)))
        idxs.append(fi[0, 0])
        cur = jnp.where(pick, neg_inf, cur)
    total = functools.reduce(lambda u, v: u + v, gates)
    for k in range(TOP_K):
        eidx_ref[k] = idxs[k].astype(I32)
        gate_ref[k] = gates[k] / total * ROUTED_SCALE


def moe_router(h, w_router, e_bias, row0, m):
    d = h.shape[1]
    tm = min(ROUTER_TM, m)
    n_sub = tm // LANES
    blk0 = row0 // tm
    bias = jnp.broadcast_to(e_bias.reshape(N_EXPERTS, 1, 1), (N_EXPERTS, n_sub, LANES))
    w_hi = w_router.T.astype(BF16)
    w_split = jnp.concatenate([w_hi, (w_router.T - w_hi.astype(F32)).astype(BF16)], axis=0)
    out = jax.ShapeDtypeStruct((TOP_K, m // LANES, LANES), I32)
    eidx, gate = pl.pallas_call(
        _router_kernel,
        out_shape=(out, jax.ShapeDtypeStruct(out.shape, F32)),
        grid=(m // tm,),
        in_specs=[pl.BlockSpec((tm, d), lambda i: (i + blk0, 0)),
                  pl.BlockSpec((2 * N_EXPERTS, d), lambda i: (0, 0)),
                  pl.BlockSpec((N_EXPERTS, n_sub, LANES), lambda i: (0, 0, 0))],
        out_specs=(pl.BlockSpec((TOP_K, n_sub, LANES), lambda i: (0, i, 0)),
                   pl.BlockSpec((TOP_K, n_sub, LANES), lambda i: (0, i, 0))),
        scratch_shapes=[pltpu.VMEM((N_EXPERTS * n_sub, LANES), F32)],
        compiler_params=_params("parallel"),
        name="moe_router",
    )(h, w_split, bias)
    return eidx.reshape(TOP_K, m), gate.reshape(TOP_K, m)


def _silu(x):
    return x / (1.0 + jnp.exp(-x))


RANK_TM = 1024


def _rank_kernel(eidx_ref, tri_ref, dest_ref, cnt_ref, start_ref, cnt_sc, run_sc):
    p, i = pl.program_id(0), pl.program_id(1)
    tm = eidx_ref.shape[1]
    e = eidx_ref[...]
    ex = lax.broadcasted_iota(I32, (N_EXPERTS, tm), 0)
    onehot = jnp.zeros((N_EXPERTS, tm), F32)
    for k in range(TOP_K):
        onehot = onehot + jnp.where(e[k:k + 1, :] == ex, 1.0, 0.0)
    tile_cnt = jnp.sum(onehot, axis=-1, keepdims=True)

    @pl.when((p == 0) & (i == 0))
    def _():
        cnt_sc[...] = jnp.zeros(cnt_sc.shape, F32)

    @pl.when(p == 0)
    def _():
        cnt_sc[...] += tile_cnt

    @pl.when((p == 1) & (i == 0))
    def _():
        cnt = cnt_sc[...]
        padded = jnp.floor((cnt + (MOE_BLK - 1)) * (1.0 / MOE_BLK)) * MOE_BLK
        below = (lax.broadcasted_iota(I32, (N_EXPERTS, N_EXPERTS), 1)
                 < lax.broadcasted_iota(I32, (N_EXPERTS, N_EXPERTS), 0)).astype(F32)
        start = jnp.dot(below, jnp.broadcast_to(padded, (N_EXPERTS, LANES)),
                        precision=lax.Precision.HIGHEST, preferred_element_type=F32)
        run_sc[...] = start[:, :1]
        cnt_ref[...] = jnp.broadcast_to(cnt, (N_EXPERTS, LANES))
        start_ref[...] = start

    @pl.when(p == 1)
    def _():
        before = jnp.dot(onehot.astype(BF16), tri_ref[...], preferred_element_type=F32) + run_sc[...]
        rows = [jnp.sum(jnp.where(e[k:k + 1, :] == ex, before, 0.0), axis=0, keepdims=True)
                for k in range(TOP_K)]
        dest_ref[...] = jnp.concatenate(rows, axis=0).astype(I32)
        run_sc[...] += tile_cnt


def moe_rank(eidx):
    n_tok = eidx.shape[1]
    tm = min(RANK_TM, n_tok)
    tri = (jnp.arange(tm)[:, None] < jnp.arange(tm)[None, :]).astype(BF16)
    stat = jax.ShapeDtypeStruct((N_EXPERTS, LANES), F32)
    dest, cnt, start = pl.pallas_call(
        _rank_kernel,
        out_shape=(jax.ShapeDtypeStruct((TOP_K, n_tok), I32), stat, stat),
        grid=(2, n_tok // tm),
        in_specs=[pl.BlockSpec((TOP_K, tm), lambda p, i: (0, i)),
                  pl.BlockSpec((tm, tm), lambda p, i: (0, 0))],
        out_specs=(pl.BlockSpec((TOP_K, tm), lambda p, i: (0, i * p)),
                   pl.BlockSpec((N_EXPERTS, LANES), lambda p, i: (0, 0)),
                   pl.BlockSpec((N_EXPERTS, LANES), lambda p, i: (0, 0))),
        scratch_shapes=[pltpu.VMEM((N_EXPERTS, 1), F32), pltpu.VMEM((N_EXPERTS, 1), F32)],
        compiler_params=_params("arbitrary", "arbitrary"),
        name="moe_rank",
    )(eidx, tri)
    return dest, cnt[:, 0].astype(I32), start[:, 0].astype(I32)


def _experts_kernel(blk_e_ref, x_ref, wg_ref, wu_ref, wd_ref, o_ref, wg_sc, wu_sc, wd_sc):
    i = pl.program_id(0)

    @pl.when((i == 0) | (blk_e_ref[i] != blk_e_ref[jnp.maximum(i - 1, 0)]))
    def _():
        wg_sc[...] = wg_ref[...].astype(BF16)
        wu_sc[...] = wu_ref[...].astype(BF16)
        wd_sc[...] = wd_ref[...].astype(BF16)

    x = x_ref[...]
    gate = jnp.dot(x, wg_sc[...], preferred_element_type=F32)
    up = jnp.dot(x, wu_sc[...], preferred_element_type=F32)
    act = (_silu(gate) * up).astype(BF16)
    o_ref[...] = jnp.dot(act, wd_sc[...], preferred_element_type=F32).astype(o_ref.dtype)


def grouped_experts(x_sorted, blk_e, wg, wu, wd, layer):
    n_slot, d = x_sorted.shape
    n_blk = n_slot // MOE_BLK
    by_expert = lambda i, be: (layer, be[i], 0, 0)
    grid_spec = pltpu.PrefetchScalarGridSpec(
        num_scalar_prefetch=1,
        grid=(n_blk,),
        in_specs=[pl.BlockSpec((MOE_BLK, d), lambda i, be: (i, 0)),
                  pl.BlockSpec((None, None, d, D_EXPERT), by_expert),
                  pl.BlockSpec((None, None, d, D_EXPERT), by_expert),
                  pl.BlockSpec((None, None, D_EXPERT, d), by_expert)],
        out_specs=pl.BlockSpec((MOE_BLK, d), lambda i, be: (i, 0)),
        scratch_shapes=[pltpu.VMEM((d, D_EXPERT), BF16), pltpu.VMEM((d, D_EXPERT), BF16),
                        pltpu.VMEM((D_EXPERT, d), BF16)],
    )
    return pl.pallas_call(
        _experts_kernel,
        out_shape=jax.ShapeDtypeStruct((n_slot, d), BF16),
        grid_spec=grid_spec,
        compiler_params=_params("arbitrary"),
        name="grouped_experts",
    )(blk_e, x_sorted, wg, wu, wd)


def _moe_finish_kernel(dils, n_prev, hb_ref, y_ref, gt_ref, wg_ref, wu_ref, wd_ref, h_ref, g_ref, b_ref,
                       *rest):
    o_ref, ob_ref = rest[n_prev:n_prev + 2]
    rest = rest[n_prev + 2:]
    x = hb_ref[...]
    d = x.shape[1]
    act = (_silu(jnp.dot(x, wg_ref[...], preferred_element_type=F32))
           * jnp.dot(x, wu_ref[...], preferred_element_type=F32)).astype(BF16)
    ff = jnp.dot(act, wd_ref[...], preferred_element_type=F32)
    gt = gt_ref[...]
    for k in range(TOP_K):
        ff = ff + gt[:, k:k + 1] * y_ref[k].astype(F32)
    out = _layer_norm_rows(DN_ALPHA * h_ref[...] + ff, g_ref[...], b_ref[...])
    o_ref[...] = out
    ob_ref[...] = out.astype(BF16)
    if dils:
        perm_refs, nat_sc = rest[:-1], rest[-1]
        for j in range(d // LANES):
            nat_sc[j] = out[:, j * LANES:(j + 1) * LANES]
        for p_ref, dil in zip(perm_refs, dils):
            n = out.shape[0] // dil
            for r in range(dil):
                for j in range(d // LANES):
                    p_ref[0, r, :, j * LANES:(j + 1) * LANES] = (
                        nat_sc[j, pl.ds(r, n, stride=dil), :].astype(BF16))


def moe_finish(hb, y_tok, gate_tok, ws_g, ws_u, ws_d, h, g, b, seq, row0, m, dils=(), prev=None, tm=256):
    d = h.shape[1]
    tm = min(tm, seq)
    n_j = seq // tm
    blk0 = row0 // tm
    row = lambda i: (i, 0)
    row_in = lambda i: (i + blk0, 0)
    fixed = lambda i: (0, 0)
    n_all = h.shape[0]
    seq0 = row0 // seq
    out_shape = [jax.ShapeDtypeStruct((n_all, d), F32), jax.ShapeDtypeStruct((n_all, d), BF16)]
    out_specs = [pl.BlockSpec((tm, d), row_in), pl.BlockSpec((tm, d), row_in)]
    for dil in dils:
        out_shape.append(jax.ShapeDtypeStruct((n_all // seq, dil, seq // dil, d), BF16))
        out_specs.append(pl.BlockSpec((1, dil, tm // dil, d), lambda i: (i // n_j + seq0, 0, i % n_j, 0)))
    prev = list(prev) if prev is not None else []
    n_fixed_in = 9
    res = pl.pallas_call(
        functools.partial(_moe_finish_kernel, tuple(dils), len(prev)),
        out_shape=tuple(out_shape),
        input_output_aliases={n_fixed_in + j: j for j in range(len(prev))},
        grid=(m // tm,),
        in_specs=[pl.BlockSpec((tm, d), row_in), pl.BlockSpec((TOP_K, tm, d), lambda i: (0, i, 0)),
                  pl.BlockSpec((tm, TOP_K), row),
                  pl.BlockSpec(ws_g.shape, fixed), pl.BlockSpec(ws_u.shape, fixed),
                  pl.BlockSpec(ws_d.shape, fixed),
                  pl.BlockSpec((tm, d), row_in), pl.BlockSpec((1, d), fixed), pl.BlockSpec((1, d), fixed)]
                 + [pl.BlockSpec(memory_space=pl.ANY)] * len(prev),
        out_specs=tuple(out_specs),
        scratch_shapes=[pltpu.VMEM((d // LANES, tm, LANES), F32)] if dils else [],
        compiler_params=_params("parallel"),
        name="moe_finish",
    )(hb, y_tok, gate_tok, ws_g, ws_u, ws_d, h, g.reshape(1, d), b.reshape(1, d), *prev)
    return list(res)


SORT_E_SHIFT = 19


MOE_SPLITS = 2


def moe_layer(h, hb, w_router, e_bias, wg, wu, wd, layer, ws_g, ws_u, ws_d, g, b, seq, dils=()):
    n_all = h.shape[0]
    n_split = MOE_SPLITS if (n_all // seq) % MOE_SPLITS == 0 else 1
    outs = None
    for sp in range(n_split):
        outs = moe_tokens(h, hb, w_router, e_bias, wg, wu, wd, layer, ws_g, ws_u, ws_d, g, b, seq, dils,
                          sp * (n_all // n_split), n_all // n_split, outs)
    return outs[0], outs[1], [p.reshape(n_all, -1) for p in outs[2:]]


def moe_tokens(h, hb, w_router, e_bias, wg, wu, wd, layer, ws_g, ws_u, ws_d, g, b, seq, dils, row0, n_tok,
               prev):
    d = h.shape[1]
    n_asg = n_tok * TOP_K
    n_pad = N_EXPERTS * MOE_BLK
    assert n_asg <= 1 << (SORT_E_SHIFT - 1) and n_pad <= 1 << (SORT_E_SHIFT - 1)
    eidx, gate = moe_router(h, w_router, e_bias, row0, n_tok)
    dest, counts, pad_start = moe_rank(eidx)
    padded = (counts + MOE_BLK - 1) // MOE_BLK * MOE_BLK
    pad_end = pad_start + padded
    n_blk = n_asg // MOE_BLK + N_EXPERTS
    blk_start = jnp.arange(n_blk, dtype=I32) * MOE_BLK
    blk_e = jnp.minimum(jnp.sum(pad_end[None, :] <= blk_start[:, None], axis=1), N_EXPERTS - 1)

    key_real = (eidx.T.reshape(-1) << SORT_E_SHIFT) | jnp.arange(n_asg, dtype=I32)
    i_pad = jnp.arange(n_pad, dtype=I32)
    e_pad = jnp.sum(jnp.cumsum(padded - counts)[None, :] <= i_pad[:, None], axis=1).astype(I32)
    key_pad = (e_pad << SORT_E_SHIFT) | (1 << (SORT_E_SHIFT - 1)) | i_pad
    low = jnp.sort(jnp.concatenate([key_real, key_pad])) & ((1 << SORT_E_SHIFT) - 1)
    slot_tok = jnp.where(low < (1 << (SORT_E_SHIFT - 1)), low // TOP_K,
                         jnp.arange(n_asg + n_pad, dtype=I32) % n_tok)

    x_sorted = hb.at[slot_tok + row0].get(mode="promise_in_bounds")
    y = grouped_experts(x_sorted, blk_e.astype(I32), wg, wu, wd, layer)
    y_k = y.at[dest.reshape(-1)].get(mode="promise_in_bounds").reshape(TOP_K, n_tok, d)
    return moe_finish(hb, y_k, gate.T, ws_g, ws_u, ws_d, h, g, b, seq, row0, n_tok, dils, prev)


def kernel(x, rel_bias, a_w_in, a_w_o, b_w_in, b_w_o, c_w_in, c_q_norm, c_kv_norm, c_w_q_up,
           c_w_kv_up, c_w_o, ln_g, ln_b, moe_w_router, moe_bias, moe_w_gate, moe_w_up,
           moe_w_down, moe_ws_gate, moe_ws_up, moe_ws_down):
    bsz, seq, d = x.shape
    depth = ln_g.shape[0]
    h = x.reshape(bsz * seq, d)
    hb = h.astype(BF16)
    strip = dsa_bias_strip(rel_bias, seq)
    extra_dils = tuple(dil for _, dil in B_GROUPS if dil > 1)
    hb_perm = []
    for layer in range(depth):
        kind, slot = layer % N_MIXERS, layer // N_MIXERS
        g0, b0 = ln_g[layer, 0], ln_b[layer, 0]
        if kind == 0:
            h, hb = dsa_mixer(hb, a_w_in[slot], a_w_o[slot], strip, h, g0, b0, bsz, seq)
        elif kind == 1:
            hb_by_dil = {1: hb, **dict(zip(extra_dils, hb_perm))}
            h, hb = dilated_mixer(hb_by_dil, b_w_in[slot], b_w_o[slot], rel_bias, h, g0, b0, bsz, seq)
        else:
            h, hb = mla_mixer(hb, c_w_in[slot], c_q_norm[slot], c_kv_norm[slot], c_w_q_up[slot],
                              c_w_kv_up[slot], c_w_o[slot], h, g0, b0, bsz, seq)
        next_dilated = layer + 1 < depth and (layer + 1) % N_MIXERS == 1
        h, hb, hb_perm = moe_layer(h, hb, moe_w_router[layer], moe_bias[layer],
                                   moe_w_gate, moe_w_up, moe_w_down, layer,
                                   moe_ws_gate[layer].astype(BF16),
                                   moe_ws_up[layer].astype(BF16), moe_ws_down[layer].astype(BF16),
                                   ln_g[layer, 1], ln_b[layer, 1], seq,
                                   extra_dils if next_dilated else ())
    return h.reshape(bsz, seq, d)
```

```python
import functools
import math

import jax
import jax.numpy as jnp
from jax import lax
from jax.experimental import pallas as pl
from jax.experimental.pallas import tpu as pltpu

F32 = jnp.float32
BF16 = jnp.bfloat16
I32 = jnp.int32

LANES = 128
VMEM_LIMIT_BYTES = 56 * 1024 * 1024

D_MODEL = 1024
DEPTH = 4
N_MIXERS = 3
NORM_EPS = 1e-5
RMS_EPS = 1e-6
REL_BUCKETS = 32
REL_MAX_DIST = 2048
A_HEADS = 16
A_HEAD_DIM = 128
A_IDX_HEADS = 8
A_IDX_DIM = 64
A_TOPK_MAX = 256
A_Q = A_HEADS * A_HEAD_DIM
B_GROUPS = ((128, 1), (512, 4), (2048, 16))
B_HEADS = 16
B_HEAD_DIM = 64
B_N = 128
C_HEADS = 16
C_Q_RANK = 256
C_KV_RANK = 128
C_NOPE = 64
C_ROPE = 32
C_V = 64
ROPE_BASE = 10000.0
N_EXPERTS = 64
TOP_K = 8
N_GROUPS = 8
TOPK_GROUPS = 4
D_EXPERT = 256
ROUTED_SCALE = 2.5
DN_ALPHA = (2 * DEPTH) ** 0.25

Q_BLK = 128
SEL_BLK = 512
KEY_CHUNK = 256
MOE_BLK = 512
MASK_NEG = -1e30
INT_MIN = -(2 ** 31)


def _params(*sem):
    return pltpu.CompilerParams(dimension_semantics=sem, vmem_limit_bytes=VMEM_LIMIT_BYTES)


def _mm_kernel(x_ref, w_ref, o_ref):
    o_ref[...] = jnp.dot(x_ref[...], w_ref[...], preferred_element_type=F32).astype(o_ref.dtype)


def matmul(x, w, out_dtype, tm=1024, tn=None):
    m, k = x.shape
    n = w.shape[1]
    tn = n if tn is None else tn
    tm = min(tm, m)
    return pl.pallas_call(
        _mm_kernel,
        out_shape=jax.ShapeDtypeStruct((m, n), out_dtype),
        grid=(n // tn, m // tm),
        in_specs=[pl.BlockSpec((tm, k), lambda j, i: (i, 0)),
                  pl.BlockSpec((k, tn), lambda j, i: (0, j))],
        out_specs=pl.BlockSpec((tm, tn), lambda j, i: (i, j)),
        compiler_params=_params("parallel", "parallel"),
        name="matmul",
    )(x, w)


def _layer_norm_rows(z, g, b):
    mu = jnp.mean(z, axis=-1, keepdims=True)
    zc = z - mu
    var = jnp.mean(zc * zc, axis=-1, keepdims=True)
    return zc * lax.rsqrt(var + NORM_EPS) * g + b


def _mm_ln_kernel(x_ref, w_ref, h_ref, g_ref, b_ref, o_ref, ob_ref):
    y = jnp.dot(x_ref[...], w_ref[...], preferred_element_type=F32)
    out = _layer_norm_rows(DN_ALPHA * h_ref[...] + y, g_ref[...], b_ref[...])
    o_ref[...] = out
    ob_ref[...] = out.astype(BF16)


def matmul_residual_ln(x, w, h, g, b, tm=1024):
    m, k = x.shape
    d = w.shape[1]
    tm = min(tm, m)
    row = lambda i: (i, 0)
    fixed = lambda i: (0, 0)
    return pl.pallas_call(
        _mm_ln_kernel,
        out_shape=(jax.ShapeDtypeStruct((m, d), F32), jax.ShapeDtypeStruct((m, d), BF16)),
        grid=(m // tm,),
        in_specs=[pl.BlockSpec((tm, k), row), pl.BlockSpec((k, d), fixed),
                  pl.BlockSpec((tm, d), row), pl.BlockSpec((1, d), fixed),
                  pl.BlockSpec((1, d), fixed)],
        out_specs=(pl.BlockSpec((tm, d), row), pl.BlockSpec((tm, d), row)),
        compiler_params=_params("parallel"),
        name="matmul_residual_ln",
    )(x, w, h, g.reshape(1, d), b.reshape(1, d))


def _t5_bucket(dist):
    exact = REL_BUCKETS // 2
    d_f = jnp.maximum(dist, 1).astype(F32)
    large = exact + (jnp.log(d_f / exact) / math.log(REL_MAX_DIST / exact)
                     * (REL_BUCKETS - exact)).astype(I32)
    return jnp.where(dist < exact, dist, jnp.minimum(large, REL_BUCKETS - 1))


def _bias_by_distance(rel_bias, dist):
    return rel_bias[_t5_bucket(dist)]


def _toeplitz(f, rows, cols, off):
    length = f.shape[-1]
    period = rows + cols - 1
    j = jnp.arange(period)
    shift = jnp.where(j < cols, -j, period - j)
    v = f[:, jnp.clip(off + shift, 0, length - 1)]
    skew = jnp.tile(v, (1, rows))[:, :rows * (period - 1)].reshape(-1, rows, period - 1)
    return skew[:, :, :cols]


DSA_SUB = 32
LOG2E = math.log2(math.e)


def _dsa_select(k_sel, n_keys, t0, iq_ref, ik_ref, key_sc, mask_sc, cut_sc):
    rows = lax.broadcasted_iota(I32, (SEL_BLK, n_keys), 0) + t0
    cols = lax.broadcasted_iota(I32, (SEL_BLK, n_keys), 1)
    valid = cols <= rows

    ik = ik_ref[0, :n_keys, :].astype(BF16)
    w_all = iq_ref[0, :, A_IDX_HEADS * LANES:] * ((A_IDX_DIM * A_IDX_HEADS) ** -0.5)
    score = jnp.zeros((SEL_BLK, n_keys), F32)
    for h in range(A_IDX_HEADS):
        qh = iq_ref[0, :, h * LANES:(h + 1) * LANES].astype(BF16)
        rel = lax.dot_general(qh, ik, (((1,), (1,)), ((), ())), preferred_element_type=F32)
        score = score + w_all[:, A_IDX_DIM + h:A_IDX_DIM + h + 1] * jnp.maximum(rel, 0.0)
    score = jnp.where(score == 0.0, 0.0, score)

    bits = pltpu.bitcast(score, I32)
    key_sc[:, :n_keys] = jnp.where(valid, bits ^ ((bits >> 31) & 0x7FFFFFFF), INT_MIN)

    def search(it, ans_u):
        cand_u = ans_u | lax.shift_left(jnp.int32(1), 31 - it)
        cand_s = cand_u ^ INT_MIN
        cnt = jnp.sum(jnp.where(key_sc[:, :n_keys] >= cand_s, 1.0, 0.0), axis=-1, keepdims=True)
        return jnp.where(cnt >= k_sel, cand_u, ans_u)

    thr = lax.fori_loop(0, 32, search, jnp.zeros((SEL_BLK, 1), I32)) ^ INT_MIN

    key = key_sc[:, :n_keys]
    gt = key > thr
    eq = key == thr
    need = k_sel - jnp.sum(jnp.where(gt, 1.0, 0.0), axis=-1, keepdims=True)
    n_eq = jnp.sum(jnp.where(eq, 1.0, 0.0), axis=-1, keepdims=True)
    cut_sc[...] = jnp.full((SEL_BLK, 1), n_keys, I32)
    surplus = jnp.where((n_eq > need) & (thr != INT_MIN), 1.0, 0.0)

    @pl.when(jnp.max(surplus) > 0.0)
    def _():
        def tie_search(it, ans):
            cand = ans | lax.shift_left(jnp.int32(1), (n_keys.bit_length() - 1) - it)
            hit = (key_sc[:, :n_keys] == thr) & (cols < cand)
            cnt = jnp.sum(jnp.where(hit, 1.0, 0.0), axis=-1, keepdims=True)
            return jnp.where(cnt < need, cand, ans)
        cut_sc[...] = lax.fori_loop(0, n_keys.bit_length(), tie_search, jnp.zeros((SEL_BLK, 1), I32))

    selected = valid & (gt | (eq & (cols <= cut_sc[...])))
    mask_sc[:, :n_keys] = jnp.where(selected, 0.0, MASK_NEG)


def _dsa_kernel(k_sel, seq, q_ref, kv_ref, iq_ref, ik_ref, strip_ref, o_ref,
                key_sc, mask_sc, cut_sc, qs_sc, s_sc, p_sc, acc_sc, m_sc, a_sc, ve_sc):
    i = pl.program_id(1)
    t0 = i * SEL_BLK
    blk_per_chunk = KEY_CHUNK // Q_BLK

    @pl.when(i == 0)
    def _():
        ve_sc[:, :A_HEAD_DIM] = kv_ref[0, :, A_HEAD_DIM:]
        ve_sc[:, A_HEAD_DIM:] = jnp.ones((seq, A_HEAD_DIM), BF16)

    for j in range(seq // SEL_BLK):
        @pl.when(i == j)
        def _(j=j):
            _dsa_select(k_sel, (j + 1) * SEL_BLK, t0, iq_ref, ik_ref, key_sc, mask_sc, cut_sc)

    n_strip_blk = seq // Q_BLK - 1
    n_tile = KEY_CHUNK // LANES

    def attend(sub, carry):
        _dsa_attend(sub, i * (SEL_BLK // Q_BLK) + sub)
        return carry

    def _dsa_attend(sub, qi):
        r0 = pl.multiple_of(sub * Q_BLK, Q_BLK)
        for h in range(A_HEADS):
            qs_sc[h * Q_BLK:(h + 1) * Q_BLK, :] = q_ref[0, pl.ds(r0, Q_BLK),
                                                        h * A_HEAD_DIM:(h + 1) * A_HEAD_DIM]
        m_sc[...] = jnp.full(m_sc.shape, MASK_NEG, F32)
        acc_sc[...] = jnp.zeros(acc_sc.shape, F32)
        lax.fori_loop(0, (qi * Q_BLK + Q_BLK + KEY_CHUNK - 1) // KEY_CHUNK,
                      functools.partial(chunk, r0, qi), 0)
        for h in range(A_HEADS):
            r = slice(h * Q_BLK, (h + 1) * Q_BLK)
            o_ref[0, pl.ds(r0, Q_BLK), h * A_HEAD_DIM:(h + 1) * A_HEAD_DIM] = (
                acc_sc[r, :A_HEAD_DIM] / acc_sc[r, A_HEAD_DIM:]).astype(BF16)

    def chunk(r0, qi, c, carry):
        k0 = pl.multiple_of(c * KEY_CHUNK, KEY_CHUNK)
        kc = kv_ref[0, pl.ds(k0, KEY_CHUNK), :A_HEAD_DIM]
        s_sc[...] = lax.dot_general(qs_sc[...], kc, (((1,), (1,)), ((), ())),
                                    preferred_element_type=F32)
        w0 = (c * blk_per_chunk - qi + n_strip_blk) * Q_BLK
        for h in range(A_HEADS):
            for rb in range(Q_BLK // DSA_SUB):
                qr = slice(rb * DSA_SUB, (rb + 1) * DSA_SUB)
                mr = pl.ds(pl.multiple_of(r0 + rb * DSA_SUB, DSA_SUB), DSA_SUB)
                r = slice(h * Q_BLK + rb * DSA_SUB, h * Q_BLK + (rb + 1) * DSA_SUB)
                t = []
                for jt in range(n_tile):
                    ws = pl.ds(pl.multiple_of(w0 + jt * LANES, LANES), LANES)
                    ks = pl.ds(pl.multiple_of(k0 + jt * LANES, LANES), LANES)
                    t.append(s_sc[r, jt * LANES:(jt + 1) * LANES]
                             + strip_ref[h, qr, ws].astype(F32) + mask_sc[mr, ks])
                mx = functools.reduce(jnp.maximum, t)
                m_old = m_sc[r, :]
                m_new = jnp.maximum(m_old, jnp.max(mx, axis=-1, keepdims=True))
                a_sc[r, :] = jnp.exp2(m_old - m_new)
                m_sc[r, :] = m_new
                for jt in range(n_tile):
                    p_sc[r, jt * LANES:(jt + 1) * LANES] = jnp.exp2(t[jt] - m_new).astype(BF16)
        pv = jnp.dot(p_sc[...], ve_sc[pl.ds(k0, KEY_CHUNK), :], preferred_element_type=F32)
        for half in range(2):
            hs = slice(half * A_HEAD_DIM, (half + 1) * A_HEAD_DIM)
            acc_sc[:, hs] = acc_sc[:, hs] * a_sc[...] + pv[:, hs]
        return carry

    lax.fori_loop(0, SEL_BLK // Q_BLK, attend, 0)


def dsa_attention(qkv, idx, strip, bsz, seq):
    assert seq % SEL_BLK == 0 and SEL_BLK % KEY_CHUNK == 0 and KEY_CHUNK % Q_BLK == 0
    k_sel = min(A_TOPK_MAX, seq // 4)
    rows = A_HEADS * Q_BLK
    n_kv_blk = A_Q // (2 * A_HEAD_DIM)
    n_ik_blk = A_IDX_HEADS
    idx_w = (A_IDX_HEADS + 1) * LANES
    return pl.pallas_call(
        functools.partial(_dsa_kernel, k_sel, seq),
        out_shape=jax.ShapeDtypeStruct((bsz, seq, A_Q), BF16),
        grid=(bsz, seq // SEL_BLK),
        in_specs=[pl.BlockSpec((1, SEL_BLK, A_Q), lambda b, i: (b, i, 0)),
                  pl.BlockSpec((1, seq, 2 * A_HEAD_DIM), lambda b, i: (b, 0, n_kv_blk)),
                  pl.BlockSpec((1, SEL_BLK, idx_w), lambda b, i: (b, i, 0)),
                  pl.BlockSpec((1, seq, LANES), lambda b, i: (b, 0, n_ik_blk)),
                  pl.BlockSpec(strip.shape, lambda b, i: (0, 0, 0), pipeline_mode=pl.Buffered(1))],
        out_specs=pl.BlockSpec((1, SEL_BLK, A_Q), lambda b, i: (b, i, 0)),
        scratch_shapes=[pltpu.VMEM((SEL_BLK, seq), I32),
                        pltpu.VMEM((SEL_BLK, seq), F32),
                        pltpu.VMEM((SEL_BLK, 1), I32),
                        pltpu.VMEM((rows, A_HEAD_DIM), BF16),
                        pltpu.VMEM((rows, KEY_CHUNK), F32),
                        pltpu.VMEM((rows, KEY_CHUNK), BF16),
                        pltpu.VMEM((rows, 2 * A_HEAD_DIM), F32),
                        pltpu.VMEM((rows, LANES), F32),
                        pltpu.VMEM((rows, LANES), F32),
                        pltpu.VMEM((seq, 2 * A_HEAD_DIM), BF16)],
        compiler_params=_params("parallel", "arbitrary"),
        name="dsa_attention",
    )(qkv, qkv, idx, idx, strip)


def dsa_bias_strip(rel_bias, seq):
    width = seq + KEY_CHUNK - Q_BLK
    by_dist = _bias_by_distance(rel_bias, jnp.arange(seq)).T
    return (_toeplitz(by_dist, Q_BLK, width, seq - Q_BLK) * LOG2E).astype(BF16)


def dsa_weights(w_in):
    d = w_in.shape[0]
    wq = w_in[:, :A_Q] * (A_HEAD_DIM ** -0.5 * LOG2E)
    wkv = w_in[:, A_Q:A_Q + 2 * A_HEAD_DIM]
    o = A_Q + 2 * A_HEAD_DIM
    n_qi = A_IDX_HEADS * A_IDX_DIM
    wqi = w_in[:, o:o + n_qi].reshape(d, A_IDX_HEADS, A_IDX_DIM)
    wqi = jnp.pad(wqi, ((0, 0), (0, 0), (0, LANES - A_IDX_DIM))).reshape(d, A_IDX_HEADS * LANES)
    wkw = jnp.pad(w_in[:, o + n_qi:], ((0, 0), (0, LANES - A_IDX_DIM - A_IDX_HEADS)))
    return (jnp.concatenate([wq, wkv], axis=1).astype(BF16),
            jnp.concatenate([wqi, wkw], axis=1).astype(BF16))


def dsa_mixer(hb, w_in, w_o, strip, h, g, b, bsz, seq):
    w_qkv, w_idx = dsa_weights(w_in)
    qkv = matmul(hb, w_qkv, BF16, tn=w_qkv.shape[1] // 2)
    idx = matmul(hb, w_idx, F32)
    o = dsa_attention(qkv.reshape(bsz, seq, -1), idx.reshape(bsz, seq, -1), strip, bsz, seq)
    return matmul_residual_ln(o.reshape(bsz * seq, A_Q), w_o.astype(BF16), h, g, b)


def _dilated_kernel(q_ref, kp_ref, kc_ref, vp_ref, vc_ref, bias_ref, o_ref, lse_ref):
    c = pl.program_id(1)
    col = lax.broadcasted_iota(I32, (B_N, 2 * B_N), 1)
    has_prev = (col >= B_N) | (c > 0)
    lane = lax.broadcasted_iota(I32, (B_N, LANES), 1)
    low_half = lane < B_HEAD_DIM
    lse_tile = jnp.zeros((B_N, LANES), F32)
    ones = jnp.ones((2 * B_N, LANES), BF16)
    for pr in range(B_HEADS * B_HEAD_DIM // LANES):
        ps = slice(pr * LANES, (pr + 1) * LANES)
        q2 = q_ref[0, :, ps]
        kk = jnp.concatenate([kp_ref[0, :, ps], kc_ref[0, :, ps]], axis=0)
        ve = jnp.concatenate([jnp.concatenate([vp_ref[0, :, ps], vc_ref[0, :, ps]], axis=0), ones],
                             axis=1)
        out2 = None
        for hh in range(LANES // B_HEAD_DIM):
            h = pr * (LANES // B_HEAD_DIM) + hh
            mine = low_half if hh == 0 else jnp.logical_not(low_half)
            qh = jnp.where(mine, q2, jnp.zeros_like(q2))
            s = lax.dot_general(qh, kk, (((1,), (1,)), ((), ())), preferred_element_type=F32)
            s = jnp.where(has_prev, s + bias_ref[h], MASK_NEG)
            m = jnp.max(s, axis=-1, keepdims=True)
            pv = jnp.dot(jnp.exp2(s - m).astype(BF16), ve, preferred_element_type=F32)
            l = pv[:, LANES:]
            o_h = pv[:, :LANES] / l
            out2 = o_h if out2 is None else jnp.where(mine, o_h, out2)
            lse_tile = jnp.where(lane == h, (m + jnp.log2(l)) * (1.0 / LOG2E), lse_tile)
        o_ref[0, :, ps] = out2.astype(BF16)
    lse_ref[0] = lse_tile


def dilated_group(proj, bias, n_seq, length):
    width = B_HEADS * B_HEAD_DIM
    pv = proj.reshape(n_seq, length, 3 * width)
    blk = (1, B_N, width)

    def spec(which, prev):
        if prev:
            return pl.BlockSpec(blk, lambda s, c: (s, jnp.maximum(c - 1, 0), which))
        return pl.BlockSpec(blk, lambda s, c: (s, c, which))

    return pl.pallas_call(
        _dilated_kernel,
        out_shape=(jax.ShapeDtypeStruct((n_seq, length, width), BF16),
                   jax.ShapeDtypeStruct((n_seq, length, LANES), F32)),
        grid=(n_seq, length // B_N),
        in_specs=[spec(0, False), spec(1, True), spec(1, False), spec(2, True), spec(2, False),
                  pl.BlockSpec(bias.shape, lambda s, c: (0, 0, 0))],
        out_specs=(pl.BlockSpec(blk, lambda s, c: (s, c, 0)),
                   pl.BlockSpec((1, B_N, LANES), lambda s, c: (s, c, 0))),
        compiler_params=_params("parallel", "parallel"),
        name="dilated_group",
    )(pv, pv, pv, pv, pv, bias)


def dilated_bias(rel_bias, dil):
    ii = jnp.arange(B_N)[:, None]
    jj = jnp.arange(2 * B_N)[None, :]
    delta = B_N + ii - jj
    band = (delta >= 0) & (delta <= B_N)
    by_delta = _bias_by_distance(rel_bias, jnp.arange(2 * B_N) * dil).T
    bias = _toeplitz(by_delta, B_N, 2 * B_N, B_N)
    return jnp.where(band[None], bias.astype(F32) * LOG2E, MASK_NEG)


def _dilated_merge_kernel(dils, o0_ref, o1_ref, o2_ref, l0_ref, l1_ref, l2_ref, e_ref, w_ref,
                          h_ref, g_ref, b_ref, o_ref, ob_ref, *nat_sc):
    def natural(o_g, l_g, dil, scratch):
        if dil == 1:
            return o_g[0, 0].astype(F32), l_g[0, 0]
        o_sc, l_sc = scratch
        n_tiles, n = o_sc.shape[0], o_sc.shape[1] // dil
        for r in range(dil):
            for j in range(n_tiles):
                o_sc[j, pl.ds(r, n, stride=dil), :] = o_g[0, r, :, j * LANES:(j + 1) * LANES].astype(F32)
            l_sc[pl.ds(r, n, stride=dil), :] = l_g[0, r]
        return jnp.concatenate([o_sc[j] for j in range(n_tiles)], axis=-1), l_sc[...]

    nat = []
    for gi, (o_g, l_g) in enumerate(((o0_ref, l0_ref), (o1_ref, l1_ref), (o2_ref, l2_ref))):
        nat.append(natural(o_g, l_g, dils[gi], nat_sc[2 * gi:2 * gi + 2]))
    (v0, l0), (v1, l1), (v2, l2) = nat
    m = jnp.maximum(jnp.maximum(l0, l1), l2)
    e0, e1, e2 = jnp.exp(l0 - m), jnp.exp(l1 - m), jnp.exp(l2 - m)
    inv = 1.0 / (e0 + e1 + e2)
    e_mat = e_ref[...]

    def spread(wt):
        hi = wt.astype(BF16)
        lo = (wt - hi.astype(F32)).astype(BF16)
        return (jnp.dot(hi, e_mat, preferred_element_type=F32)
                + jnp.dot(lo, e_mat, preferred_element_type=F32))

    mix = spread(e0 * inv) * v0 + spread(e1 * inv) * v1 + spread(e2 * inv) * v2
    y = jnp.dot(mix.astype(BF16), w_ref[...], preferred_element_type=F32)
    out = _layer_norm_rows(DN_ALPHA * h_ref[...] + y, g_ref[...], b_ref[...])
    o_ref[...] = out
    ob_ref[...] = out.astype(BF16)


def dilated_merge(outs, lses, dils, w_o, h, g, b, bsz, seq, tm=512):
    m, d = h.shape
    width = B_HEADS * B_HEAD_DIM
    tm = min(tm, seq)
    n_j = seq // tm
    expand = (jnp.arange(LANES)[:, None] == (jnp.arange(width)[None, :] // B_HEAD_DIM)).astype(BF16)
    row = lambda bi, j: (bi * n_j + j, 0)
    fixed = lambda bi, j: (0, 0)
    grouped = lambda bi, j: (bi, 0, j, 0)
    o_specs = [pl.BlockSpec((1, dl, tm // dl, width), grouped) for dl in dils]
    l_specs = [pl.BlockSpec((1, dl, tm // dl, LANES), grouped) for dl in dils]
    scratch = []
    for dl in dils:
        if dl > 1:
            scratch += [pltpu.VMEM((width // LANES, tm, LANES), F32), pltpu.VMEM((tm, LANES), F32)]
        else:
            scratch += [pltpu.VMEM((8, LANES), F32), pltpu.VMEM((8, LANES), F32)]
    outs = [o.reshape(bsz, dl, seq // dl, width) for o, dl in zip(outs, dils)]
    lses = [l.reshape(bsz, dl, seq // dl, LANES) for l, dl in zip(lses, dils)]
    return pl.pallas_call(
        functools.partial(_dilated_merge_kernel, tuple(dils)),
        out_shape=(jax.ShapeDtypeStruct((m, d), F32), jax.ShapeDtypeStruct((m, d), BF16)),
        grid=(bsz, n_j),
        in_specs=o_specs + l_specs
                 + [pl.BlockSpec((LANES, width), fixed), pl.BlockSpec((width, d), fixed),
                    pl.BlockSpec((tm, d), row), pl.BlockSpec((1, d), fixed), pl.BlockSpec((1, d), fixed)],
        out_specs=(pl.BlockSpec((tm, d), row), pl.BlockSpec((tm, d), row)),
        scratch_shapes=scratch,
        compiler_params=_params("parallel", "parallel"),
        name="dilated_merge",
    )(*outs, *lses, expand, w_o.astype(BF16), h, g.reshape(1, d), b.reshape(1, d))


def dilated_mixer(hb_by_dil, w_in, w_o, rel_bias, h, g, b, bsz, seq):
    width = B_HEADS * B_HEAD_DIM
    outs, lses, dils = [], [], []
    for gi, (window, dil) in enumerate(B_GROUPS):
        assert window // dil == B_N and seq % window == 0
        w_g = w_in[:, gi * 3 * width:(gi + 1) * 3 * width]
        w_g = w_g.at[:, :width].multiply(B_HEAD_DIM ** -0.5 * LOG2E)
        proj = matmul(hb_by_dil[dil], w_g.astype(BF16), BF16, tn=3 * width // 2)
        o, lse = dilated_group(proj, dilated_bias(rel_bias, dil), bsz * dil, seq // dil)
        outs.append(o)
        lses.append(lse)
        dils.append(dil)
    return dilated_merge(outs, lses, dils, w_o, h, g, b, bsz, seq)


C_PAD = 128


def _mla_prep_kernel(x_ref, win_ref, qn_ref, kvn_ref, wqa_ref, wqb_ref, wka_ref, wv_ref,
                     cos_ref, sin_ref, vone_ref, q_ref, k_ref, v_ref):
    c = jnp.dot(x_ref[...], win_ref[...], preferred_element_type=F32)
    cos, sin = cos_ref[...], sin_ref[...]

    def rms(v, gain):
        return (v * lax.rsqrt(jnp.mean(v * v, axis=-1, keepdims=True) + RMS_EPS) * gain).astype(BF16)

    nq = rms(c[:, :C_Q_RANK], qn_ref[...])
    nkv = rms(c[:, C_Q_RANK:C_Q_RANK + C_KV_RANK], kvn_ref[...])
    o = C_Q_RANK + C_KV_RANK
    k_rope = c[:, o:o + C_PAD] * cos + c[:, o + C_PAD:o + 2 * C_PAD] * sin
    qa = jnp.dot(nq, wqa_ref[...], preferred_element_type=F32)
    qb = jnp.dot(nq, wqb_ref[...], preferred_element_type=F32)
    kn = jnp.dot(nkv, wka_ref[...], preferred_element_type=F32)
    for h in range(C_HEADS):
        hs = slice(h * C_PAD, (h + 1) * C_PAD)
        q_ref[:, hs] = (qa[:, hs] * cos + qb[:, hs] * sin).astype(BF16)
        k_ref[:, hs] = (kn[:, hs] + k_rope).astype(BF16)
    v_ref[...] = (jnp.dot(nkv, wv_ref[...], preferred_element_type=F32) + vone_ref[...]).astype(BF16)


def _rot_half_cols(w):
    half = w.shape[-1] // 2
    return jnp.concatenate([-w[..., half:], w[..., :half]], axis=-1)


def mla_prep(hb, w_in, q_norm, kv_norm, w_q_up, w_kv_up, bsz, seq, tm=512):
    d = w_in.shape[0]
    scale = (C_NOPE + C_ROPE) ** -0.5 * LOG2E
    pad_r = C_PAD - C_NOPE - C_ROPE
    w_kr = w_in[:, C_Q_RANK + C_KV_RANK:]

    def rope_slot(w):
        return jnp.pad(w, ((0, 0), (C_NOPE, pad_r)))

    win = jnp.concatenate([w_in[:, :C_Q_RANK + C_KV_RANK], rope_slot(w_kr),
                           rope_slot(_rot_half_cols(w_kr))], axis=1).astype(BF16)
    wq = w_q_up.reshape(C_Q_RANK, C_HEADS, C_NOPE + C_ROPE) * scale
    wqa = jnp.pad(wq, ((0, 0), (0, 0), (0, pad_r))).reshape(C_Q_RANK, C_HEADS * C_PAD).astype(BF16)
    wqb = jnp.pad(_rot_half_cols(wq[..., C_NOPE:]), ((0, 0), (0, 0), (C_NOPE, pad_r)))
    wqb = wqb.reshape(C_Q_RANK, C_HEADS * C_PAD).astype(BF16)
    wkv = w_kv_up.reshape(C_KV_RANK, C_HEADS, C_NOPE + C_V)
    wka = jnp.pad(wkv[..., :C_NOPE], ((0, 0), (0, 0), (0, C_PAD - C_NOPE)))
    wka = wka.reshape(C_KV_RANK, C_HEADS * C_PAD).astype(BF16)
    wv = jnp.pad(wkv[..., C_NOPE:], ((0, 0), (0, 0), (0, LANES - C_V)))
    wv = wv.reshape(C_KV_RANK, C_HEADS * LANES).astype(BF16)
    v_ones = jnp.tile(jnp.concatenate([jnp.zeros((1, C_V), F32), jnp.ones((1, LANES - C_V), F32)], axis=1),
                      (1, C_HEADS))

    half = C_ROPE // 2
    inv = ROPE_BASE ** (-jnp.arange(half, dtype=F32) / half)
    ang = jnp.arange(seq, dtype=F32)[:, None] * inv[None, :]
    ones, zeros = jnp.ones((seq, C_NOPE), F32), jnp.zeros((seq, pad_r), F32)
    cos = jnp.concatenate([ones, jnp.cos(ang), jnp.cos(ang), zeros], axis=1)
    sin = jnp.concatenate([0 * ones, jnp.sin(ang), jnp.sin(ang), zeros], axis=1)

    m = bsz * seq
    tm = min(tm, seq)
    n_pos_blk = seq // tm
    row = lambda i: (i, 0)
    fixed = lambda i: (0, 0)
    pos = lambda i: (i % n_pos_blk, 0)
    full = lambda a: pl.BlockSpec(a.shape, fixed)
    qw, vw = C_HEADS * C_PAD, C_HEADS * LANES
    return pl.pallas_call(
        _mla_prep_kernel,
        out_shape=(jax.ShapeDtypeStruct((m, qw), BF16), jax.ShapeDtypeStruct((m, qw), BF16),
                   jax.ShapeDtypeStruct((m, vw), BF16)),
        grid=(m // tm,),
        in_specs=[pl.BlockSpec((tm, d), row), full(win),
                  pl.BlockSpec((1, C_Q_RANK), fixed), pl.BlockSpec((1, C_KV_RANK), fixed),
                  full(wqa), full(wqb), full(wka), full(wv),
                  pl.BlockSpec((tm, C_PAD), pos), pl.BlockSpec((tm, C_PAD), pos), full(v_ones)],
        out_specs=(pl.BlockSpec((tm, qw), row), pl.BlockSpec((tm, qw), row),
                   pl.BlockSpec((tm, vw), row)),
        compiler_params=_params("parallel"),
        name="mla_prep",
    )(hb, win, q_norm.reshape(1, -1), kv_norm.reshape(1, -1), wqa, wqb, wka, wv, cos, sin, v_ones)


MLA_TQ = 512


MLA_SUB = 32


def _mla_attn_kernel(q_ref, k_ref, v_ref, o_ref, s_sc, p_sc, m_sc, a_sc, acc_sc):
    i = pl.program_id(2)
    tq = MLA_TQ
    n_tile = tq // LANES
    row_in_sub = lax.broadcasted_iota(I32, (MLA_SUB, LANES), 0)
    col_in_tile = lax.broadcasted_iota(I32, (MLA_SUB, LANES), 1)
    outs = []
    for hh in range(2):
        q = q_ref[0, :, hh * C_PAD:(hh + 1) * C_PAD]
        m_sc[...] = jnp.full(m_sc.shape, MASK_NEG, F32)
        acc_sc[...] = jnp.zeros(acc_sc.shape, F32)

        def step(c, diag):
            k0 = pl.multiple_of(c * tq, tq)
            kc = k_ref[0, pl.ds(k0, tq), hh * C_PAD:(hh + 1) * C_PAD]
            s_sc[...] = lax.dot_general(q, kc, (((1,), (1,)), ((), ())), preferred_element_type=F32)
            for rb in range(tq // MLA_SUB):
                r = slice(rb * MLA_SUB, (rb + 1) * MLA_SUB)
                n_act = (rb * MLA_SUB + MLA_SUB - 1) // LANES + 1 if diag else n_tile
                t = [s_sc[r, jt * LANES:(jt + 1) * LANES] for jt in range(n_act)]
                if diag:
                    edge = n_act - 1
                    t[edge] = jnp.where(col_in_tile + edge * LANES <= row_in_sub + rb * MLA_SUB,
                                        t[edge], MASK_NEG)
                m_old = m_sc[r, :]
                m_new = jnp.maximum(m_old, jnp.max(functools.reduce(jnp.maximum, t), axis=-1, keepdims=True))
                a_sc[r, :] = jnp.exp2(m_old - m_new)
                m_sc[r, :] = m_new
                for jt in range(n_tile):
                    p_sc[r, jt * LANES:(jt + 1) * LANES] = (
                        jnp.exp2(t[jt] - m_new).astype(BF16) if jt < n_act
                        else jnp.zeros((MLA_SUB, LANES), BF16))
            pv = jnp.dot(p_sc[...], v_ref[0, pl.ds(k0, tq), hh * LANES:(hh + 1) * LANES],
                         preferred_element_type=F32)
            acc_sc[...] = acc_sc[...] * a_sc[...] + pv

        def body(c, carry):
            step(c, False)
            return carry

        lax.fori_loop(0, i, body, 0)
        step(i, True)
        outs.append(acc_sc[:, :C_V] / acc_sc[:, C_V:])
    o_ref[0] = jnp.concatenate(outs, axis=-1).astype(BF16)


def mla_attention(q, k, v, bsz, seq):
    qw, vw = C_HEADS * C_PAD, C_HEADS * C_V
    return pl.pallas_call(
        _mla_attn_kernel,
        out_shape=jax.ShapeDtypeStruct((bsz, seq, vw), BF16),
        grid=(bsz, C_HEADS // 2, seq // MLA_TQ),
        in_specs=[pl.BlockSpec((1, MLA_TQ, 2 * C_PAD), lambda b, hp, i: (b, i, hp)),
                  pl.BlockSpec((1, seq, 2 * C_PAD), lambda b, hp, i: (b, 0, hp)),
                  pl.BlockSpec((1, seq, 2 * LANES), lambda b, hp, i: (b, 0, hp))],
        out_specs=pl.BlockSpec((1, MLA_TQ, 2 * C_V), lambda b, hp, i: (b, i, hp)),
        scratch_shapes=[pltpu.VMEM((MLA_TQ, MLA_TQ), F32), pltpu.VMEM((MLA_TQ, MLA_TQ), BF16),
                        pltpu.VMEM((MLA_TQ, LANES), F32), pltpu.VMEM((MLA_TQ, LANES), F32),
                        pltpu.VMEM((MLA_TQ, LANES), F32)],
        compiler_params=_params("parallel", "parallel", "arbitrary"),
        name="mla_attention",
    )(q.reshape(bsz, seq, qw), k.reshape(bsz, seq, qw), v.reshape(bsz, seq, C_HEADS * LANES))


def mla_mixer(hb, w_in, q_norm, kv_norm, w_q_up, w_kv_up, w_o, h, g, b, bsz, seq):
    q, k, v = mla_prep(hb, w_in, q_norm, kv_norm, w_q_up, w_kv_up, bsz, seq)
    o = mla_attention(q, k, v, bsz, seq)
    return matmul_residual_ln(o.reshape(bsz * seq, -1), w_o.astype(BF16), h, g, b)


ROUTER_TM = 1024


def _router_kernel(h_ref, wr_ref, eb_ref, eidx_ref, gate_ref, slab_sc):
    tm = h_ref.shape[0]
    n_sub = tm // LANES
    per = N_EXPERTS // N_GROUPS
    h = h_ref[...]
    h_hi = h.astype(BF16)
    h_lo = (h - h_hi.astype(F32)).astype(BF16)
    nt = (((1,), (1,)), ((), ()))
    both = lax.dot_general(wr_ref[...], h_hi, nt, preferred_element_type=F32)
    logits = (both[:N_EXPERTS] + both[N_EXPERTS:]
              + lax.dot_general(wr_ref[:N_EXPERTS, :], h_lo, nt, preferred_element_type=F32))
    for j in range(n_sub):
        slab_sc[pl.ds(j, N_EXPERTS, stride=n_sub), :] = logits[:, j * LANES:(j + 1) * LANES]
    x = slab_sc[...].reshape(N_GROUPS, per, n_sub, LANES)
    s4 = 1.0 / (1.0 + jnp.exp(-x))
    c4 = s4 + eb_ref[...].reshape(N_GROUPS, per, n_sub, LANES)
    neg_inf = -jnp.inf
    shape4 = (N_GROUPS, per, n_sub, LANES)
    j_idx = lax.broadcasted_iota(I32, shape4, 1).astype(F32)
    g_idx = lax.broadcasted_iota(I32, shape4, 0).astype(F32)
    flat = g_idx * per + j_idx

    m1 = jnp.max(c4, axis=1, keepdims=True)
    first = jnp.min(jnp.where(c4 == m1, j_idx, per), axis=1, keepdims=True)
    m2 = jnp.max(jnp.where(j_idx == first, neg_inf, c4), axis=1, keepdims=True)
    gs = m1 + m2

    gi = lax.broadcasted_iota(I32, (N_GROUPS, 1, n_sub, LANES), 0).astype(F32)
    keep = jnp.zeros((N_GROUPS, 1, n_sub, LANES), jnp.bool_)
    cur = gs
    for _ in range(TOPK_GROUPS):
        mx = jnp.max(cur, axis=0, keepdims=True)
        pick = gi == jnp.min(jnp.where(cur == mx, gi, N_GROUPS), axis=0, keepdims=True)
        keep = keep | pick
        cur = jnp.where(pick, neg_inf, cur)
    cur = jnp.where(keep, c4, neg_inf)

    idxs, gates = [], []
    for _ in range(TOP_K):
        mx = jnp.max(cur, axis=(0, 1), keepdims=True)
        fi = jnp.min(jnp.where(cur == mx, flat, N_EXPERTS), axis=(0, 1), keepdims=True)
        pick = flat == fi
        gates.append(jnp.sum(jnp.where(pick, s4, 0.0), axis=(0, 1)))
        idxs.append(fi[0, 0])
        cur = jnp.where(pick, neg_inf, cur)
    total = functools.reduce(lambda u, v: u + v, gates)
    for k in range(TOP_K):
        eidx_ref[k] = idxs[k].astype(I32)
        gate_ref[k] = gates[k] / total * ROUTED_SCALE


def moe_router(h, w_router, e_bias, row0, m):
    d = h.shape[1]
    tm = min(ROUTER_TM, m)
    n_sub = tm // LANES
    blk0 = row0 // tm
    bias = jnp.broadcast_to(e_bias.reshape(N_EXPERTS, 1, 1), (N_EXPERTS, n_sub, LANES))
    w_hi = w_router.T.astype(BF16)
    w_split = jnp.concatenate([w_hi, (w_router.T - w_hi.astype(F32)).astype(BF16)], axis=0)
    out = jax.ShapeDtypeStruct((TOP_K, m // LANES, LANES), I32)
    eidx, gate = pl.pallas_call(
        _router_kernel,
        out_shape=(out, jax.ShapeDtypeStruct(out.shape, F32)),
        grid=(m // tm,),
        in_specs=[pl.BlockSpec((tm, d), lambda i: (i + blk0, 0)),
                  pl.BlockSpec((2 * N_EXPERTS, d), lambda i: (0, 0)),
                  pl.BlockSpec((N_EXPERTS, n_sub, LANES), lambda i: (0, 0, 0))],
        out_specs=(pl.BlockSpec((TOP_K, n_sub, LANES), lambda i: (0, i, 0)),
                   pl.BlockSpec((TOP_K, n_sub, LANES), lambda i: (0, i, 0))),
        scratch_shapes=[pltpu.VMEM((N_EXPERTS * n_sub, LANES), F32)],
        compiler_params=_params("parallel"),
        name="moe_router",
    )(h, w_split, bias)
    return eidx.reshape(TOP_K, m), gate.reshape(TOP_K, m)


def _silu(x):
    return x / (1.0 + jnp.exp(-x))


RANK_TM = 1024


def _rank_kernel(eidx_ref, tri_ref, dest_ref, cnt_ref, start_ref, cnt_sc, run_sc):
    p, i = pl.program_id(0), pl.program_id(1)
    tm = eidx_ref.shape[1]
    e = eidx_ref[...]
    ex = lax.broadcasted_iota(I32, (N_EXPERTS, tm), 0)
    onehot = jnp.zeros((N_EXPERTS, tm), F32)
    for k in range(TOP_K):
        onehot = onehot + jnp.where(e[k:k + 1, :] == ex, 1.0, 0.0)
    tile_cnt = jnp.sum(onehot, axis=-1, keepdims=True)

    @pl.when((p == 0) & (i == 0))
    def _():
        cnt_sc[...] = jnp.zeros(cnt_sc.shape, F32)

    @pl.when(p == 0)
    def _():
        cnt_sc[...] += tile_cnt

    @pl.when((p == 1) & (i == 0))
    def _():
        cnt = cnt_sc[...]
        padded = jnp.floor((cnt + (MOE_BLK - 1)) * (1.0 / MOE_BLK)) * MOE_BLK
        below = (lax.broadcasted_iota(I32, (N_EXPERTS, N_EXPERTS), 1)
                 < lax.broadcasted_iota(I32, (N_EXPERTS, N_EXPERTS), 0)).astype(F32)
        start = jnp.dot(below, jnp.broadcast_to(padded, (N_EXPERTS, LANES)),
                        precision=lax.Precision.HIGHEST, preferred_element_type=F32)
        run_sc[...] = start[:, :1]
        cnt_ref[...] = jnp.broadcast_to(cnt, (N_EXPERTS, LANES))
        start_ref[...] = start

    @pl.when(p == 1)
    def _():
        before = jnp.dot(onehot.astype(BF16), tri_ref[...], preferred_element_type=F32) + run_sc[...]
        rows = [jnp.sum(jnp.where(e[k:k + 1, :] == ex, before, 0.0), axis=0, keepdims=True)
                for k in range(TOP_K)]
        dest_ref[...] = jnp.concatenate(rows, axis=0).astype(I32)
        run_sc[...] += tile_cnt


def moe_rank(eidx):
    n_tok = eidx.shape[1]
    tm = min(RANK_TM, n_tok)
    tri = (jnp.arange(tm)[:, None] < jnp.arange(tm)[None, :]).astype(BF16)
    stat = jax.ShapeDtypeStruct((N_EXPERTS, LANES), F32)
    dest, cnt, start = pl.pallas_call(
        _rank_kernel,
        out_shape=(jax.ShapeDtypeStruct((TOP_K, n_tok), I32), stat, stat),
        grid=(2, n_tok // tm),
        in_specs=[pl.BlockSpec((TOP_K, tm), lambda p, i: (0, i)),
                  pl.BlockSpec((tm, tm), lambda p, i: (0, 0))],
        out_specs=(pl.BlockSpec((TOP_K, tm), lambda p, i: (0, i * p)),
                   pl.BlockSpec((N_EXPERTS, LANES), lambda p, i: (0, 0)),
                   pl.BlockSpec((N_EXPERTS, LANES), lambda p, i: (0, 0))),
        scratch_shapes=[pltpu.VMEM((N_EXPERTS, 1), F32), pltpu.VMEM((N_EXPERTS, 1), F32)],
        compiler_params=_params("arbitrary", "arbitrary"),
        name="moe_rank",
    )(eidx, tri)
    return dest, cnt[:, 0].astype(I32), start[:, 0].astype(I32)


def _experts_kernel(blk_e_ref, x_ref, wg_ref, wu_ref, wd_ref, o_ref, wg_sc, wu_sc, wd_sc):
    i = pl.program_id(0)

    @pl.when((i == 0) | (blk_e_ref[i] != blk_e_ref[jnp.maximum(i - 1, 0)]))
    def _():
        wg_sc[...] = wg_ref[...].astype(BF16)
        wu_sc[...] = wu_ref[...].astype(BF16)
        wd_sc[...] = wd_ref[...].astype(BF16)

    x = x_ref[...]
    gate = jnp.dot(x, wg_sc[...], preferred_element_type=F32)
    up = jnp.dot(x, wu_sc[...], preferred_element_type=F32)
    act = (_silu(gate) * up).astype(BF16)
    o_ref[...] = jnp.dot(act, wd_sc[...], preferred_element_type=F32).astype(o_ref.dtype)


def grouped_experts(x_sorted, blk_e, wg, wu, wd, layer):
    n_slot, d = x_sorted.shape
    n_blk = n_slot // MOE_BLK
    by_expert = lambda i, be: (layer, be[i], 0, 0)
    grid_spec = pltpu.PrefetchScalarGridSpec(
        num_scalar_prefetch=1,
        grid=(n_blk,),
        in_specs=[pl.BlockSpec((MOE_BLK, d), lambda i, be: (i, 0)),
                  pl.BlockSpec((None, None, d, D_EXPERT), by_expert),
                  pl.BlockSpec((None, None, d, D_EXPERT), by_expert),
                  pl.BlockSpec((None, None, D_EXPERT, d), by_expert)],
        out_specs=pl.BlockSpec((MOE_BLK, d), lambda i, be: (i, 0)),
        scratch_shapes=[pltpu.VMEM((d, D_EXPERT), BF16), pltpu.VMEM((d, D_EXPERT), BF16),
                        pltpu.VMEM((D_EXPERT, d), BF16)],
    )
    return pl.pallas_call(
        _experts_kernel,
        out_shape=jax.ShapeDtypeStruct((n_slot, d), BF16),
        grid_spec=grid_spec,
        compiler_params=_params("arbitrary"),
        name="grouped_experts",
    )(blk_e, x_sorted, wg, wu, wd)


def _moe_finish_kernel(dils, n_prev, hb_ref, y_ref, gt_ref, wg_ref, wu_ref, wd_ref, h_ref, g_ref, b_ref,
                       *rest):
    o_ref, ob_ref = rest[n_prev:n_prev + 2]
    rest = rest[n_prev + 2:]
    x = hb_ref[...]
    d = x.shape[1]
    act = (_silu(jnp.dot(x, wg_ref[...], preferred_element_type=F32))
           * jnp.dot(x, wu_ref[...], preferred_element_type=F32)).astype(BF16)
    ff = jnp.dot(act, wd_ref[...], preferred_element_type=F32)
    gt = gt_ref[...]
    for k in range(TOP_K):
        ff = ff + gt[:, k:k + 1] * y_ref[k].astype(F32)
    out = _layer_norm_rows(DN_ALPHA * h_ref[...] + ff, g_ref[...], b_ref[...])
    o_ref[...] = out
    ob_ref[...] = out.astype(BF16)
    if dils:
        perm_refs, nat_sc = rest[:-1], rest[-1]
        for j in range(d // LANES):
            nat_sc[j] = out[:, j * LANES:(j + 1) * LANES]
        for p_ref, dil in zip(perm_refs, dils):
            n = out.shape[0] // dil
            for r in range(dil):
                for j in range(d // LANES):
                    p_ref[0, r, :, j * LANES:(j + 1) * LANES] = (
                        nat_sc[j, pl.ds(r, n, stride=dil), :].astype(BF16))


def moe_finish(hb, y_tok, gate_tok, ws_g, ws_u, ws_d, h, g, b, seq, row0, m, dils=(), prev=None, tm=512):
    d = h.shape[1]
    tm = min(tm, seq)
    n_j = seq // tm
    blk0 = row0 // tm
    row = lambda i: (i, 0)
    row_in = lambda i: (i + blk0, 0)
    fixed = lambda i: (0, 0)
    n_all = h.shape[0]
    seq0 = row0 // seq
    out_shape = [jax.ShapeDtypeStruct((n_all, d), F32), jax.ShapeDtypeStruct((n_all, d), BF16)]
    out_specs = [pl.BlockSpec((tm, d), row_in), pl.BlockSpec((tm, d), row_in)]
    for dil in dils:
        out_shape.append(jax.ShapeDtypeStruct((n_all // seq, dil, seq // dil, d), BF16))
        out_specs.append(pl.BlockSpec((1, dil, tm // dil, d), lambda i: (i // n_j + seq0, 0, i % n_j, 0)))
    prev = list(prev) if prev is not None else []
    n_fixed_in = 9
    res = pl.pallas_call(
        functools.partial(_moe_finish_kernel, tuple(dils), len(prev)),
        out_shape=tuple(out_shape),
        input_output_aliases={n_fixed_in + j: j for j in range(len(prev))},
        grid=(m // tm,),
        in_specs=[pl.BlockSpec((tm, d), row_in), pl.BlockSpec((TOP_K, tm, d), lambda i: (0, i, 0)),
                  pl.BlockSpec((tm, TOP_K), row),
                  pl.BlockSpec(ws_g.shape, fixed), pl.BlockSpec(ws_u.shape, fixed),
                  pl.BlockSpec(ws_d.shape, fixed),
                  pl.BlockSpec((tm, d), row_in), pl.BlockSpec((1, d), fixed), pl.BlockSpec((1, d), fixed)]
                 + [pl.BlockSpec(memory_space=pl.ANY)] * len(prev),
        out_specs=tuple(out_specs),
        scratch_shapes=[pltpu.VMEM((d // LANES, tm, LANES), F32)] if dils else [],
        compiler_params=_params("parallel"),
        name="moe_finish",
    )(hb, y_tok, gate_tok, ws_g, ws_u, ws_d, h, g.reshape(1, d), b.reshape(1, d), *prev)
    return list(res)


SORT_E_SHIFT = 19


MOE_SPLITS = 2


def moe_layer(h, hb, w_router, e_bias, wg, wu, wd, layer, ws_g, ws_u, ws_d, g, b, seq, dils=()):
    n_all = h.shape[0]
    n_split = MOE_SPLITS if (n_all // seq) % MOE_SPLITS == 0 else 1
    outs = None
    for sp in range(n_split):
        outs = moe_tokens(h, hb, w_router, e_bias, wg, wu, wd, layer, ws_g, ws_u, ws_d, g, b, seq, dils,
                          sp * (n_all // n_split), n_all // n_split, outs)
    return outs[0], outs[1], [p.reshape(n_all, -1) for p in outs[2:]]


def moe_tokens(h, hb, w_router, e_bias, wg, wu, wd, layer, ws_g, ws_u, ws_d, g, b, seq, dils, row0, n_tok,
               prev):
    d = h.shape[1]
    n_asg = n_tok * TOP_K
    n_pad = N_EXPERTS * MOE_BLK
    assert n_asg <= 1 << (SORT_E_SHIFT - 1) and n_pad <= 1 << (SORT_E_SHIFT - 1)
    eidx, gate = moe_router(h, w_router, e_bias, row0, n_tok)
    dest, counts, pad_start = moe_rank(eidx)
    padded = (counts + MOE_BLK - 1) // MOE_BLK * MOE_BLK
    pad_end = pad_start + padded
    n_blk = n_asg // MOE_BLK + N_EXPERTS
    blk_start = jnp.arange(n_blk, dtype=I32) * MOE_BLK
    blk_e = jnp.minimum(jnp.sum(pad_end[None, :] <= blk_start[:, None], axis=1), N_EXPERTS - 1)

    key_real = (eidx.T.reshape(-1) << SORT_E_SHIFT) | jnp.arange(n_asg, dtype=I32)
    i_pad = jnp.arange(n_pad, dtype=I32)
    e_pad = jnp.sum(jnp.cumsum(padded - counts)[None, :] <= i_pad[:, None], axis=1).astype(I32)
    key_pad = (e_pad << SORT_E_SHIFT) | (1 << (SORT_E_SHIFT - 1)) | i_pad
    low = jnp.sort(jnp.concatenate([key_real, key_pad])) & ((1 << SORT_E_SHIFT) - 1)
    slot_tok = jnp.where(low < (1 << (SORT_E_SHIFT - 1)), low // TOP_K,
                         jnp.arange(n_asg + n_pad, dtype=I32) % n_tok)

    x_sorted = hb.at[slot_tok + row0].get(mode="promise_in_bounds")
    y = grouped_experts(x_sorted, blk_e.astype(I32), wg, wu, wd, layer)
    y_k = y.at[dest.reshape(-1)].get(mode="promise_in_bounds").reshape(TOP_K, n_tok, d)
    return moe_finish(hb, y_k, gate.T, ws_g, ws_u, ws_d, h, g, b, seq, row0, n_tok, dils, prev)


def kernel(x, rel_bias, a_w_in, a_w_o, b_w_in, b_w_o, c_w_in, c_q_norm, c_kv_norm, c_w_q_up,
           c_w_kv_up, c_w_o, ln_g, ln_b, moe_w_router, moe_bias, moe_w_gate, moe_w_up,
           moe_w_down, moe_ws_gate, moe_ws_up, moe_ws_down):
    bsz, seq, d = x.shape
    depth = ln_g.shape[0]
    h = x.reshape(bsz * seq, d)
    hb = h.astype(BF16)
    strip = dsa_bias_strip(rel_bias, seq)
    extra_dils = tuple(dil for _, dil in B_GROUPS if dil > 1)
    hb_perm = []
    for layer in range(depth):
        kind, slot = layer % N_MIXERS, layer // N_MIXERS
        g0, b0 = ln_g[layer, 0], ln_b[layer, 0]
        if kind == 0:
            h, hb = dsa_mixer(hb, a_w_in[slot], a_w_o[slot], strip, h, g0, b0, bsz, seq)
        elif kind == 1:
            hb_by_dil = {1: hb, **dict(zip(extra_dils, hb_perm))}
            h, hb = dilated_mixer(hb_by_dil, b_w_in[slot], b_w_o[slot], rel_bias, h, g0, b0, bsz, seq)
        else:
            h, hb = mla_mixer(hb, c_w_in[slot], c_q_norm[slot], c_kv_norm[slot], c_w_q_up[slot],
                              c_w_kv_up[slot], c_w_o[slot], h, g0, b0, bsz, seq)
        next_dilated = layer + 1 < depth and (layer + 1) % N_MIXERS == 1
        h, hb, hb_perm = moe_layer(h, hb, moe_w_router[layer], moe_bias[layer],
                                   moe_w_gate, moe_w_up, moe_w_down, layer,
                                   moe_ws_gate[layer].astype(BF16),
                                   moe_ws_up[layer].astype(BF16), moe_ws_down[layer].astype(BF16),
                                   ln_g[layer, 1], ln_b[layer, 1], seq,
                                   extra_dils if next_dilated else ())
    return h.reshape(bsz, seq, d)
```

```python
import functools
import math

import jax
import jax.numpy as jnp
from jax import lax
from jax.experimental import pallas as pl
from jax.experimental.pallas import tpu as pltpu

F32 = jnp.float32
BF16 = jnp.bfloat16
I32 = jnp.int32

LANES = 128
VMEM_LIMIT_BYTES = 56 * 1024 * 1024

D_MODEL = 1024
DEPTH = 4
N_MIXERS = 3
NORM_EPS = 1e-5
RMS_EPS = 1e-6
REL_BUCKETS = 32
REL_MAX_DIST = 2048
A_HEADS = 16
A_HEAD_DIM = 128
A_IDX_HEADS = 8
A_IDX_DIM = 64
A_TOPK_MAX = 256
A_Q = A_HEADS * A_HEAD_DIM
B_GROUPS = ((128, 1), (512, 4), (2048, 16))
B_HEADS = 16
B_HEAD_DIM = 64
B_N = 128
C_HEADS = 16
C_Q_RANK = 256
C_KV_RANK = 128
C_NOPE = 64
C_ROPE = 32
C_V = 64
ROPE_BASE = 10000.0
N_EXPERTS = 64
TOP_K = 8
N_GROUPS = 8
TOPK_GROUPS = 4
D_EXPERT = 256
ROUTED_SCALE = 2.5
DN_ALPHA = (2 * DEPTH) ** 0.25

Q_BLK = 128
SEL_BLK = 512
KEY_CHUNK = 256
MOE_BLK = 512
MASK_NEG = -1e30
INT_MIN = -(2 ** 31)


def _params(*sem):
    return pltpu.CompilerParams(dimension_semantics=sem, vmem_limit_bytes=VMEM_LIMIT_BYTES)


def _mm_kernel(x_ref, w_ref, o_ref):
    o_ref[...] = jnp.dot(x_ref[...], w_ref[...], preferred_element_type=F32).astype(o_ref.dtype)


def matmul(x, w, out_dtype, tm=1024, tn=None):
    m, k = x.shape
    n = w.shape[1]
    tn = n if tn is None else tn
    tm = min(tm, m)
    return pl.pallas_call(
        _mm_kernel,
        out_shape=jax.ShapeDtypeStruct((m, n), out_dtype),
        grid=(n // tn, m // tm),
        in_specs=[pl.BlockSpec((tm, k), lambda j, i: (i, 0)),
                  pl.BlockSpec((k, tn), lambda j, i: (0, j))],
        out_specs=pl.BlockSpec((tm, tn), lambda j, i: (i, j)),
        compiler_params=_params("parallel", "parallel"),
        name="matmul",
    )(x, w)


def _layer_norm_rows(z, g, b):
    mu = jnp.mean(z, axis=-1, keepdims=True)
    zc = z - mu
    var = jnp.mean(zc * zc, axis=-1, keepdims=True)
    return zc * lax.rsqrt(var + NORM_EPS) * g + b


def _mm_ln_kernel(x_ref, w_ref, h_ref, g_ref, b_ref, o_ref, ob_ref):
    y = jnp.dot(x_ref[...], w_ref[...], preferred_element_type=F32)
    out = _layer_norm_rows(DN_ALPHA * h_ref[...] + y, g_ref[...], b_ref[...])
    o_ref[...] = out
    ob_ref[...] = out.astype(BF16)


def matmul_residual_ln(x, w, h, g, b, tm=1024):
    m, k = x.shape
    d = w.shape[1]
    tm = min(tm, m)
    row = lambda i: (i, 0)
    fixed = lambda i: (0, 0)
    return pl.pallas_call(
        _mm_ln_kernel,
        out_shape=(jax.ShapeDtypeStruct((m, d), F32), jax.ShapeDtypeStruct((m, d), BF16)),
        grid=(m // tm,),
        in_specs=[pl.BlockSpec((tm, k), row), pl.BlockSpec((k, d), fixed),
                  pl.BlockSpec((tm, d), row), pl.BlockSpec((1, d), fixed),
                  pl.BlockSpec((1, d), fixed)],
        out_specs=(pl.BlockSpec((tm, d), row), pl.BlockSpec((tm, d), row)),
        compiler_params=_params("parallel"),
        name="matmul_residual_ln",
    )(x, w, h, g.reshape(1, d), b.reshape(1, d))


def _t5_bucket(dist):
    exact = REL_BUCKETS // 2
    d_f = jnp.maximum(dist, 1).astype(F32)
    large = exact + (jnp.log(d_f / exact) / math.log(REL_MAX_DIST / exact)
                     * (REL_BUCKETS - exact)).astype(I32)
    return jnp.where(dist < exact, dist, jnp.minimum(large, REL_BUCKETS - 1))


def _bias_by_distance(rel_bias, dist):
    return rel_bias[_t5_bucket(dist)]


def _toeplitz(f, rows, cols, off):
    length = f.shape[-1]
    period = rows + cols - 1
    j = jnp.arange(period)
    shift = jnp.where(j < cols, -j, period - j)
    v = f[:, jnp.clip(off + shift, 0, length - 1)]
    skew = jnp.tile(v, (1, rows))[:, :rows * (period - 1)].reshape(-1, rows, period - 1)
    return skew[:, :, :cols]


DSA_SUB = 32
LOG2E = math.log2(math.e)


def _dsa_select(k_sel, n_keys, t0, iq_ref, ik_ref, key_sc, mask_sc, cut_sc):
    rows = lax.broadcasted_iota(I32, (SEL_BLK, n_keys), 0) + t0
    cols = lax.broadcasted_iota(I32, (SEL_BLK, n_keys), 1)
    valid = cols <= rows

    ik = ik_ref[0, :n_keys, :].astype(BF16)
    w_all = iq_ref[0, :, A_IDX_HEADS * LANES:] * ((A_IDX_DIM * A_IDX_HEADS) ** -0.5)
    score = jnp.zeros((SEL_BLK, n_keys), F32)
    for h in range(A_IDX_HEADS):
        qh = iq_ref[0, :, h * LANES:(h + 1) * LANES].astype(BF16)
        rel = lax.dot_general(qh, ik, (((1,), (1,)), ((), ())), preferred_element_type=F32)
        score = score + w_all[:, A_IDX_DIM + h:A_IDX_DIM + h + 1] * jnp.maximum(rel, 0.0)
    score = jnp.where(score == 0.0, 0.0, score)

    bits = pltpu.bitcast(score, I32)
    key_sc[:, :n_keys] = jnp.where(valid, bits ^ ((bits >> 31) & 0x7FFFFFFF), INT_MIN)

    def search(it, ans_u):
        cand_u = ans_u | lax.shift_left(jnp.int32(1), 31 - it)
        cand_s = cand_u ^ INT_MIN
        cnt = jnp.sum(jnp.where(key_sc[:, :n_keys] >= cand_s, 1.0, 0.0), axis=-1, keepdims=True)
        return jnp.where(cnt >= k_sel, cand_u, ans_u)

    thr = lax.fori_loop(0, 32, search, jnp.zeros((SEL_BLK, 1), I32)) ^ INT_MIN

    key = key_sc[:, :n_keys]
    gt = key > thr
    eq = key == thr
    need = k_sel - jnp.sum(jnp.where(gt, 1.0, 0.0), axis=-1, keepdims=True)
    n_eq = jnp.sum(jnp.where(eq, 1.0, 0.0), axis=-1, keepdims=True)
    cut_sc[...] = jnp.full((SEL_BLK, 1), n_keys, I32)
    surplus = jnp.where((n_eq > need) & (thr != INT_MIN), 1.0, 0.0)

    @pl.when(jnp.max(surplus) > 0.0)
    def _():
        def tie_search(it, ans):
            cand = ans | lax.shift_left(jnp.int32(1), (n_keys.bit_length() - 1) - it)
            hit = (key_sc[:, :n_keys] == thr) & (cols < cand)
            cnt = jnp.sum(jnp.where(hit, 1.0, 0.0), axis=-1, keepdims=True)
            return jnp.where(cnt < need, cand, ans)
        cut_sc[...] = lax.fori_loop(0, n_keys.bit_length(), tie_search, jnp.zeros((SEL_BLK, 1), I32))

    selected = valid & (gt | (eq & (cols <= cut_sc[...])))
    mask_sc[:, :n_keys] = jnp.where(selected, 0.0, MASK_NEG)


def _dsa_kernel(k_sel, seq, q_ref, kv_ref, iq_ref, ik_ref, strip_ref, o_ref,
                key_sc, mask_sc, cut_sc, qs_sc, s_sc, p_sc, acc_sc, m_sc, a_sc, ve_sc):
    i = pl.program_id(1)
    t0 = i * SEL_BLK
    blk_per_chunk = KEY_CHUNK // Q_BLK

    @pl.when(i == 0)
    def _():
        ve_sc[:, :A_HEAD_DIM] = kv_ref[0, :, A_HEAD_DIM:]
        ve_sc[:, A_HEAD_DIM:] = jnp.ones((seq, A_HEAD_DIM), BF16)

    for j in range(seq // SEL_BLK):
        @pl.when(i == j)
        def _(j=j):
            _dsa_select(k_sel, (j + 1) * SEL_BLK, t0, iq_ref, ik_ref, key_sc, mask_sc, cut_sc)

    n_strip_blk = seq // Q_BLK - 1
    n_tile = KEY_CHUNK // LANES

    def attend(sub, carry):
        _dsa_attend(sub, i * (SEL_BLK // Q_BLK) + sub)
        return carry

    def _dsa_attend(sub, qi):
        r0 = pl.multiple_of(sub * Q_BLK, Q_BLK)
        for h in range(A_HEADS):
            qs_sc[h * Q_BLK:(h + 1) * Q_BLK, :] = q_ref[0, pl.ds(r0, Q_BLK),
                                                        h * A_HEAD_DIM:(h + 1) * A_HEAD_DIM]
        m_sc[...] = jnp.full(m_sc.shape, MASK_NEG, F32)
        acc_sc[...] = jnp.zeros(acc_sc.shape, F32)
        lax.fori_loop(0, (qi * Q_BLK + Q_BLK + KEY_CHUNK - 1) // KEY_CHUNK,
                      functools.partial(chunk, r0, qi), 0)
        for h in range(A_HEADS):
            r = slice(h * Q_BLK, (h + 1) * Q_BLK)
            o_ref[0, pl.ds(r0, Q_BLK), h * A_HEAD_DIM:(h + 1) * A_HEAD_DIM] = (
                acc_sc[r, :A_HEAD_DIM] / acc_sc[r, A_HEAD_DIM:]).astype(BF16)

    def chunk(r0, qi, c, carry):
        k0 = pl.multiple_of(c * KEY_CHUNK, KEY_CHUNK)
        kc = kv_ref[0, pl.ds(k0, KEY_CHUNK), :A_HEAD_DIM]
        s_sc[...] = lax.dot_general(qs_sc[...], kc, (((1,), (1,)), ((), ())),
                                    preferred_element_type=F32)
        w0 = (c * blk_per_chunk - qi + n_strip_blk) * Q_BLK
        for h in range(A_HEADS):
            for rb in range(Q_BLK // DSA_SUB):
                qr = slice(rb * DSA_SUB, (rb + 1) * DSA_SUB)
                mr = pl.ds(pl.multiple_of(r0 + rb * DSA_SUB, DSA_SUB), DSA_SUB)
                r = slice(h * Q_BLK + rb * DSA_SUB, h * Q_BLK + (rb + 1) * DSA_SUB)
                t = []
                for jt in range(n_tile):
                    ws = pl.ds(pl.multiple_of(w0 + jt * LANES, LANES), LANES)
                    ks = pl.ds(pl.multiple_of(k0 + jt * LANES, LANES), LANES)
                    t.append(s_sc[r, jt * LANES:(jt + 1) * LANES]
                             + strip_ref[h, qr, ws].astype(F32) + mask_sc[mr, ks])
                mx = functools.reduce(jnp.maximum, t)
                m_old = m_sc[r, :]
                m_new = jnp.maximum(m_old, jnp.max(mx, axis=-1, keepdims=True))
                a_sc[r, :] = jnp.exp2(m_old - m_new)
                m_sc[r, :] = m_new
                for jt in range(n_tile):
                    p_sc[r, jt * LANES:(jt + 1) * LANES] = jnp.exp2(t[jt] - m_new).astype(BF16)
        pv = jnp.dot(p_sc[...], ve_sc[pl.ds(k0, KEY_CHUNK), :], preferred_element_type=F32)
        for half in range(2):
            hs = slice(half * A_HEAD_DIM, (half + 1) * A_HEAD_DIM)
            acc_sc[:, hs] = acc_sc[:, hs] * a_sc[...] + pv[:, hs]
        return carry

    lax.fori_loop(0, SEL_BLK // Q_BLK, attend, 0)


def dsa_attention(qkv, idx, strip, bsz, seq):
    assert seq % SEL_BLK == 0 and SEL_BLK % KEY_CHUNK == 0 and KEY_CHUNK % Q_BLK == 0
    k_sel = min(A_TOPK_MAX, seq // 4)
    rows = A_HEADS * Q_BLK
    n_kv_blk = A_Q // (2 * A_HEAD_DIM)
    n_ik_blk = A_IDX_HEADS
    idx_w = (A_IDX_HEADS + 1) * LANES
    return pl.pallas_call(
        functools.partial(_dsa_kernel, k_sel, seq),
        out_shape=jax.ShapeDtypeStruct((bsz, seq, A_Q), BF16),
        grid=(bsz, seq // SEL_BLK),
        in_specs=[pl.BlockSpec((1, SEL_BLK, A_Q), lambda b, i: (b, i, 0)),
                  pl.BlockSpec((1, seq, 2 * A_HEAD_DIM), lambda b, i: (b, 0, n_kv_blk)),
                  pl.BlockSpec((1, SEL_BLK, idx_w), lambda b, i: (b, i, 0)),
                  pl.BlockSpec((1, seq, LANES), lambda b, i: (b, 0, n_ik_blk)),
                  pl.BlockSpec(strip.shape, lambda b, i: (0, 0, 0), pipeline_mode=pl.Buffered(1))],
        out_specs=pl.BlockSpec((1, SEL_BLK, A_Q), lambda b, i: (b, i, 0)),
        scratch_shapes=[pltpu.VMEM((SEL_BLK, seq), I32),
                        pltpu.VMEM((SEL_BLK, seq), F32),
                        pltpu.VMEM((SEL_BLK, 1), I32),
                        pltpu.VMEM((rows, A_HEAD_DIM), BF16),
                        pltpu.VMEM((rows, KEY_CHUNK), F32),
                        pltpu.VMEM((rows, KEY_CHUNK), BF16),
                        pltpu.VMEM((rows, 2 * A_HEAD_DIM), F32),
                        pltpu.VMEM((rows, LANES), F32),
                        pltpu.VMEM((rows, LANES), F32),
                        pltpu.VMEM((seq, 2 * A_HEAD_DIM), BF16)],
        compiler_params=_params("parallel", "arbitrary"),
        name="dsa_attention",
    )(qkv, qkv, idx, idx, strip)


def dsa_bias_strip(rel_bias, seq):
    width = seq + KEY_CHUNK - Q_BLK
    by_dist = _bias_by_distance(rel_bias, jnp.arange(seq)).T
    return (_toeplitz(by_dist, Q_BLK, width, seq - Q_BLK) * LOG2E).astype(BF16)


def dsa_weights(w_in):
    d = w_in.shape[0]
    wq = w_in[:, :A_Q] * (A_HEAD_DIM ** -0.5 * LOG2E)
    wkv = w_in[:, A_Q:A_Q + 2 * A_HEAD_DIM]
    o = A_Q + 2 * A_HEAD_DIM
    n_qi = A_IDX_HEADS * A_IDX_DIM
    wqi = w_in[:, o:o + n_qi].reshape(d, A_IDX_HEADS, A_IDX_DIM)
    wqi = jnp.pad(wqi, ((0, 0), (0, 0), (0, LANES - A_IDX_DIM))).reshape(d, A_IDX_HEADS * LANES)
    wkw = jnp.pad(w_in[:, o + n_qi:], ((0, 0), (0, LANES - A_IDX_DIM - A_IDX_HEADS)))
    return (jnp.concatenate([wq, wkv], axis=1).astype(BF16),
            jnp.concatenate([wqi, wkw], axis=1).astype(BF16))


def dsa_mixer(hb, w_in, w_o, strip, h, g, b, bsz, seq):
    w_qkv, w_idx = dsa_weights(w_in)
    qkv = matmul(hb, w_qkv, BF16, tn=w_qkv.shape[1] // 2)
    idx = matmul(hb, w_idx, F32)
    o = dsa_attention(qkv.reshape(bsz, seq, -1), idx.reshape(bsz, seq, -1), strip, bsz, seq)
    return matmul_residual_ln(o.reshape(bsz * seq, A_Q), w_o.astype(BF16), h, g, b)


def _dilated_kernel(q_ref, kp_ref, kc_ref, vp_ref, vc_ref, bias_ref, o_ref, lse_ref):
    c = pl.program_id(1)
    col = lax.broadcasted_iota(I32, (B_N, 2 * B_N), 1)
    has_prev = (col >= B_N) | (c > 0)
    lane = lax.broadcasted_iota(I32, (B_N, LANES), 1)
    low_half = lane < B_HEAD_DIM
    lse_tile = jnp.zeros((B_N, LANES), F32)
    ones = jnp.ones((2 * B_N, LANES), BF16)
    for pr in range(B_HEADS * B_HEAD_DIM // LANES):
        ps = slice(pr * LANES, (pr + 1) * LANES)
        q2 = q_ref[0, :, ps]
        kk = jnp.concatenate([kp_ref[0, :, ps], kc_ref[0, :, ps]], axis=0)
        ve = jnp.concatenate([jnp.concatenate([vp_ref[0, :, ps], vc_ref[0, :, ps]], axis=0), ones],
                             axis=1)
        out2 = None
        for hh in range(LANES // B_HEAD_DIM):
            h = pr * (LANES // B_HEAD_DIM) + hh
            mine = low_half if hh == 0 else jnp.logical_not(low_half)
            qh = jnp.where(mine, q2, jnp.zeros_like(q2))
            s = lax.dot_general(qh, kk, (((1,), (1,)), ((), ())), preferred_element_type=F32)
            s = jnp.where(has_prev, s + bias_ref[h], MASK_NEG)
            m = jnp.max(s, axis=-1, keepdims=True)
            pv = jnp.dot(jnp.exp2(s - m).astype(BF16), ve, preferred_element_type=F32)
            l = pv[:, LANES:]
            o_h = pv[:, :LANES] / l
            out2 = o_h if out2 is None else jnp.where(mine, o_h, out2)
            lse_tile = jnp.where(lane == h, (m + jnp.log2(l)) * (1.0 / LOG2E), lse_tile)
        o_ref[0, :, ps] = out2.astype(BF16)
    lse_ref[0] = lse_tile


def dilated_group(proj, bias, n_seq, length):
    width = B_HEADS * B_HEAD_DIM
    pv = proj.reshape(n_seq, length, 3 * width)
    blk = (1, B_N, width)

    def spec(which, prev):
        if prev:
            return pl.BlockSpec(blk, lambda s, c: (s, jnp.maximum(c - 1, 0), which))
        return pl.BlockSpec(blk, lambda s, c: (s, c, which))

    return pl.pallas_call(
        _dilated_kernel,
        out_shape=(jax.ShapeDtypeStruct((n_seq, length, width), BF16),
                   jax.ShapeDtypeStruct((n_seq, length, LANES), F32)),
        grid=(n_seq, length // B_N),
        in_specs=[spec(0, False), spec(1, True), spec(1, False), spec(2, True), spec(2, False),
                  pl.BlockSpec(bias.shape, lambda s, c: (0, 0, 0))],
        out_specs=(pl.BlockSpec(blk, lambda s, c: (s, c, 0)),
                   pl.BlockSpec((1, B_N, LANES), lambda s, c: (s, c, 0))),
        compiler_params=_params("parallel", "parallel"),
        name="dilated_group",
    )(pv, pv, pv, pv, pv, bias)


def dilated_bias(rel_bias, dil):
    ii = jnp.arange(B_N)[:, None]
    jj = jnp.arange(2 * B_N)[None, :]
    delta = B_N + ii - jj
    band = (delta >= 0) & (delta <= B_N)
    by_delta = _bias_by_distance(rel_bias, jnp.arange(2 * B_N) * dil).T
    bias = _toeplitz(by_delta, B_N, 2 * B_N, B_N)
    return jnp.where(band[None], bias.astype(F32) * LOG2E, MASK_NEG)


def _dilated_merge_kernel(dils, o0_ref, o1_ref, o2_ref, l0_ref, l1_ref, l2_ref, e_ref, w_ref,
                          h_ref, g_ref, b_ref, o_ref, ob_ref, *nat_sc):
    def natural(o_g, l_g, dil, scratch):
        if dil == 1:
            return o_g[0, 0].astype(F32), l_g[0, 0]
        o_sc, l_sc = scratch
        n_tiles, n = o_sc.shape[0], o_sc.shape[1] // dil
        for r in range(dil):
            for j in range(n_tiles):
                o_sc[j, pl.ds(r, n, stride=dil), :] = o_g[0, r, :, j * LANES:(j + 1) * LANES].astype(F32)
            l_sc[pl.ds(r, n, stride=dil), :] = l_g[0, r]
        return jnp.concatenate([o_sc[j] for j in range(n_tiles)], axis=-1), l_sc[...]

    nat = []
    for gi, (o_g, l_g) in enumerate(((o0_ref, l0_ref), (o1_ref, l1_ref), (o2_ref, l2_ref))):
        nat.append(natural(o_g, l_g, dils[gi], nat_sc[2 * gi:2 * gi + 2]))
    (v0, l0), (v1, l1), (v2, l2) = nat
    m = jnp.maximum(jnp.maximum(l0, l1), l2)
    e0, e1, e2 = jnp.exp(l0 - m), jnp.exp(l1 - m), jnp.exp(l2 - m)
    inv = 1.0 / (e0 + e1 + e2)
    e_mat = e_ref[...]

    def spread(wt):
        hi = wt.astype(BF16)
        lo = (wt - hi.astype(F32)).astype(BF16)
        return (jnp.dot(hi, e_mat, preferred_element_type=F32)
                + jnp.dot(lo, e_mat, preferred_element_type=F32))

    mix = spread(e0 * inv) * v0 + spread(e1 * inv) * v1 + spread(e2 * inv) * v2
    y = jnp.dot(mix.astype(BF16), w_ref[...], preferred_element_type=F32)
    out = _layer_norm_rows(DN_ALPHA * h_ref[...] + y, g_ref[...], b_ref[...])
    o_ref[...] = out
    ob_ref[...] = out.astype(BF16)


def dilated_merge(outs, lses, dils, w_o, h, g, b, bsz, seq, tm=512):
    m, d = h.shape
    width = B_HEADS * B_HEAD_DIM
    tm = min(tm, seq)
    n_j = seq // tm
    expand = (jnp.arange(LANES)[:, None] == (jnp.arange(width)[None, :] // B_HEAD_DIM)).astype(BF16)
    row = lambda bi, j: (bi * n_j + j, 0)
    fixed = lambda bi, j: (0, 0)
    grouped = lambda bi, j: (bi, 0, j, 0)
    o_specs = [pl.BlockSpec((1, dl, tm // dl, width), grouped) for dl in dils]
    l_specs = [pl.BlockSpec((1, dl, tm // dl, LANES), grouped) for dl in dils]
    scratch = []
    for dl in dils:
        if dl > 1:
            scratch += [pltpu.VMEM((width // LANES, tm, LANES), F32), pltpu.VMEM((tm, LANES), F32)]
        else:
            scratch += [pltpu.VMEM((8, LANES), F32), pltpu.VMEM((8, LANES), F32)]
    outs = [o.reshape(bsz, dl, seq // dl, width) for o, dl in zip(outs, dils)]
    lses = [l.reshape(bsz, dl, seq // dl, LANES) for l, dl in zip(lses, dils)]
    return pl.pallas_call(
        functools.partial(_dilated_merge_kernel, tuple(dils)),
        out_shape=(jax.ShapeDtypeStruct((m, d), F32), jax.ShapeDtypeStruct((m, d), BF16)),
        grid=(bsz, n_j),
        in_specs=o_specs + l_specs
                 + [pl.BlockSpec((LANES, width), fixed), pl.BlockSpec((width, d), fixed),
                    pl.BlockSpec((tm, d), row), pl.BlockSpec((1, d), fixed), pl.BlockSpec((1, d), fixed)],
        out_specs=(pl.BlockSpec((tm, d), row), pl.BlockSpec((tm, d), row)),
        scratch_shapes=scratch,
        compiler_params=_params("parallel", "parallel"),
        name="dilated_merge",
    )(*outs, *lses, expand, w_o.astype(BF16), h, g.reshape(1, d), b.reshape(1, d))


def dilated_mixer(hb_by_dil, w_in, w_o, rel_bias, h, g, b, bsz, seq):
    width = B_HEADS * B_HEAD_DIM
    outs, lses, dils = [], [], []
    for gi, (window, dil) in enumerate(B_GROUPS):
        assert window // dil == B_N and seq % window == 0
        w_g = w_in[:, gi * 3 * width:(gi + 1) * 3 * width]
        w_g = w_g.at[:, :width].multiply(B_HEAD_DIM ** -0.5 * LOG2E)
        proj = matmul(hb_by_dil[dil], w_g.astype(BF16), BF16, tn=3 * width // 2)
        o, lse = dilated_group(proj, dilated_bias(rel_bias, dil), bsz * dil, seq // dil)
        outs.append(o)
        lses.append(lse)
        dils.append(dil)
    return dilated_merge(outs, lses, dils, w_o, h, g, b, bsz, seq)


C_PAD = 128


def _mla_prep_kernel(x_ref, win_ref, qn_ref, kvn_ref, wqa_ref, wqb_ref, wka_ref, wv_ref,
                     cos_ref, sin_ref, vone_ref, q_ref, k_ref, v_ref):
    c = jnp.dot(x_ref[...], win_ref[...], preferred_element_type=F32)
    cos, sin = cos_ref[...], sin_ref[...]

    def rms(v, gain):
        return (v * lax.rsqrt(jnp.mean(v * v, axis=-1, keepdims=True) + RMS_EPS) * gain).astype(BF16)

    nq = rms(c[:, :C_Q_RANK], qn_ref[...])
    nkv = rms(c[:, C_Q_RANK:C_Q_RANK + C_KV_RANK], kvn_ref[...])
    o = C_Q_RANK + C_KV_RANK
    k_rope = c[:, o:o + C_PAD] * cos + c[:, o + C_PAD:o + 2 * C_PAD] * sin
    qa = jnp.dot(nq, wqa_ref[...], preferred_element_type=F32)
    qb = jnp.dot(nq, wqb_ref[...], preferred_element_type=F32)
    kn = jnp.dot(nkv, wka_ref[...], preferred_element_type=F32)
    for h in range(C_HEADS):
        hs = slice(h * C_PAD, (h + 1) * C_PAD)
        q_ref[:, hs] = (qa[:, hs] * cos + qb[:, hs] * sin).astype(BF16)
        k_ref[:, hs] = (kn[:, hs] + k_rope).astype(BF16)
    v_ref[...] = (jnp.dot(nkv, wv_ref[...], preferred_element_type=F32) + vone_ref[...]).astype(BF16)


def _rot_half_cols(w):
    half = w.shape[-1] // 2
    return jnp.concatenate([-w[..., half:], w[..., :half]], axis=-1)


def mla_prep(hb, w_in, q_norm, kv_norm, w_q_up, w_kv_up, bsz, seq, tm=512):
    d = w_in.shape[0]
    scale = (C_NOPE + C_ROPE) ** -0.5 * LOG2E
    pad_r = C_PAD - C_NOPE - C_ROPE
    w_kr = w_in[:, C_Q_RANK + C_KV_RANK:]

    def rope_slot(w):
        return jnp.pad(w, ((0, 0), (C_NOPE, pad_r)))

    win = jnp.concatenate([w_in[:, :C_Q_RANK + C_KV_RANK], rope_slot(w_kr),
                           rope_slot(_rot_half_cols(w_kr))], axis=1).astype(BF16)
    wq = w_q_up.reshape(C_Q_RANK, C_HEADS, C_NOPE + C_ROPE) * scale
    wqa = jnp.pad(wq, ((0, 0), (0, 0), (0, pad_r))).reshape(C_Q_RANK, C_HEADS * C_PAD).astype(BF16)
    wqb = jnp.pad(_rot_half_cols(wq[..., C_NOPE:]), ((0, 0), (0, 0), (C_NOPE, pad_r)))
    wqb = wqb.reshape(C_Q_RANK, C_HEADS * C_PAD).astype(BF16)
    wkv = w_kv_up.reshape(C_KV_RANK, C_HEADS, C_NOPE + C_V)
    wka = jnp.pad(wkv[..., :C_NOPE], ((0, 0), (0, 0), (0, C_PAD - C_NOPE)))
    wka = wka.reshape(C_KV_RANK, C_HEADS * C_PAD).astype(BF16)
    wv = jnp.pad(wkv[..., C_NOPE:], ((0, 0), (0, 0), (0, LANES - C_V)))
    wv = wv.reshape(C_KV_RANK, C_HEADS * LANES).astype(BF16)
    v_ones = jnp.tile(jnp.concatenate([jnp.zeros((1, C_V), F32), jnp.ones((1, LANES - C_V), F32)], axis=1),
                      (1, C_HEADS))

    half = C_ROPE // 2
    inv = ROPE_BASE ** (-jnp.arange(half, dtype=F32) / half)
    ang = jnp.arange(seq, dtype=F32)[:, None] * inv[None, :]
    ones, zeros = jnp.ones((seq, C_NOPE), F32), jnp.zeros((seq, pad_r), F32)
    cos = jnp.concatenate([ones, jnp.cos(ang), jnp.cos(ang), zeros], axis=1)
    sin = jnp.concatenate([0 * ones, jnp.sin(ang), jnp.sin(ang), zeros], axis=1)

    m = bsz * seq
    tm = min(tm, seq)
    n_pos_blk = seq // tm
    row = lambda i: (i, 0)
    fixed = lambda i: (0, 0)
    pos = lambda i: (i % n_pos_blk, 0)
    full = lambda a: pl.BlockSpec(a.shape, fixed)
    qw, vw = C_HEADS * C_PAD, C_HEADS * LANES
    return pl.pallas_call(
        _mla_prep_kernel,
        out_shape=(jax.ShapeDtypeStruct((m, qw), BF16), jax.ShapeDtypeStruct((m, qw), BF16),
                   jax.ShapeDtypeStruct((m, vw), BF16)),
        grid=(m // tm,),
        in_specs=[pl.BlockSpec((tm, d), row), full(win),
                  pl.BlockSpec((1, C_Q_RANK), fixed), pl.BlockSpec((1, C_KV_RANK), fixed),
                  full(wqa), full(wqb), full(wka), full(wv),
                  pl.BlockSpec((tm, C_PAD), pos), pl.BlockSpec((tm, C_PAD), pos), full(v_ones)],
        out_specs=(pl.BlockSpec((tm, qw), row), pl.BlockSpec((tm, qw), row),
                   pl.BlockSpec((tm, vw), row)),
        compiler_params=_params("parallel"),
        name="mla_prep",
    )(hb, win, q_norm.reshape(1, -1), kv_norm.reshape(1, -1), wqa, wqb, wka, wv, cos, sin, v_ones)


MLA_TQ = 512


MLA_SUB = 32


def _mla_attn_kernel(q_ref, k_ref, v_ref, o_ref, s_sc, p_sc, m_sc, a_sc, acc_sc):
    i = pl.program_id(2)
    tq = MLA_TQ
    n_tile = tq // LANES
    row_in_sub = lax.broadcasted_iota(I32, (MLA_SUB, LANES), 0)
    col_in_tile = lax.broadcasted_iota(I32, (MLA_SUB, LANES), 1)
    outs = []
    for hh in range(2):
        q = q_ref[0, :, hh * C_PAD:(hh + 1) * C_PAD]
        m_sc[...] = jnp.full(m_sc.shape, MASK_NEG, F32)
        acc_sc[...] = jnp.zeros(acc_sc.shape, F32)

        def step(c, diag):
            k0 = pl.multiple_of(c * tq, tq)
            kc = k_ref[0, pl.ds(k0, tq), hh * C_PAD:(hh + 1) * C_PAD]
            s_sc[...] = lax.dot_general(q, kc, (((1,), (1,)), ((), ())), preferred_element_type=F32)
            for rb in range(tq // MLA_SUB):
                r = slice(rb * MLA_SUB, (rb + 1) * MLA_SUB)
                n_act = (rb * MLA_SUB + MLA_SUB - 1) // LANES + 1 if diag else n_tile
                t = [s_sc[r, jt * LANES:(jt + 1) * LANES] for jt in range(n_act)]
                if diag:
                    edge = n_act - 1
                    t[edge] = jnp.where(col_in_tile + edge * LANES <= row_in_sub + rb * MLA_SUB,
                                        t[edge], MASK_NEG)
                m_old = m_sc[r, :]
                m_new = jnp.maximum(m_old, jnp.max(functools.reduce(jnp.maximum, t), axis=-1, keepdims=True))
                a_sc[r, :] = jnp.exp2(m_old - m_new)
                m_sc[r, :] = m_new
                for jt in range(n_tile):
                    p_sc[r, jt * LANES:(jt + 1) * LANES] = (
                        jnp.exp2(t[jt] - m_new).astype(BF16) if jt < n_act
                        else jnp.zeros((MLA_SUB, LANES), BF16))
            pv = jnp.dot(p_sc[...], v_ref[0, pl.ds(k0, tq), hh * LANES:(hh + 1) * LANES],
                         preferred_element_type=F32)
            acc_sc[...] = acc_sc[...] * a_sc[...] + pv

        def body(c, carry):
            step(c, False)
            return carry

        lax.fori_loop(0, i, body, 0)
        step(i, True)
        outs.append(acc_sc[:, :C_V] / acc_sc[:, C_V:])
    o_ref[0] = jnp.concatenate(outs, axis=-1).astype(BF16)


def mla_attention(q, k, v, bsz, seq):
    qw, vw = C_HEADS * C_PAD, C_HEADS * C_V
    return pl.pallas_call(
        _mla_attn_kernel,
        out_shape=jax.ShapeDtypeStruct((bsz, seq, vw), BF16),
        grid=(bsz, C_HEADS // 2, seq // MLA_TQ),
        in_specs=[pl.BlockSpec((1, MLA_TQ, 2 * C_PAD), lambda b, hp, i: (b, i, hp)),
                  pl.BlockSpec((1, seq, 2 * C_PAD), lambda b, hp, i: (b, 0, hp)),
                  pl.BlockSpec((1, seq, 2 * LANES), lambda b, hp, i: (b, 0, hp))],
        out_specs=pl.BlockSpec((1, MLA_TQ, 2 * C_V), lambda b, hp, i: (b, i, hp)),
        scratch_shapes=[pltpu.VMEM((MLA_TQ, MLA_TQ), F32), pltpu.VMEM((MLA_TQ, MLA_TQ), BF16),
                        pltpu.VMEM((MLA_TQ, LANES), F32), pltpu.VMEM((MLA_TQ, LANES), F32),
                        pltpu.VMEM((MLA_TQ, LANES), F32)],
        compiler_params=_params("parallel", "parallel", "arbitrary"),
        name="mla_attention",
    )(q.reshape(bsz, seq, qw), k.reshape(bsz, seq, qw), v.reshape(bsz, seq, C_HEADS * LANES))


def mla_mixer(hb, w_in, q_norm, kv_norm, w_q_up, w_kv_up, w_o, h, g, b, bsz, seq):
    q, k, v = mla_prep(hb, w_in, q_norm, kv_norm, w_q_up, w_kv_up, bsz, seq)
    o = mla_attention(q, k, v, bsz, seq)
    return matmul_residual_ln(o.reshape(bsz * seq, -1), w_o.astype(BF16), h, g, b)


ROUTER_TM = 1024


def _router_kernel(h_ref, wr_ref, eb_ref, eidx_ref, gate_ref, slab_sc):
    tm = h_ref.shape[0]
    n_sub = tm // LANES
    per = N_EXPERTS // N_GROUPS
    h = h_ref[...]
    h_hi = h.astype(BF16)
    h_lo = (h - h_hi.astype(F32)).astype(BF16)
    nt = (((1,), (1,)), ((), ()))
    both = lax.dot_general(wr_ref[...], h_hi, nt, preferred_element_type=F32)
    logits = (both[:N_EXPERTS] + both[N_EXPERTS:]
              + lax.dot_general(wr_ref[:N_EXPERTS, :], h_lo, nt, preferred_element_type=F32))
    for j in range(n_sub):
        slab_sc[pl.ds(j, N_EXPERTS, stride=n_sub), :] = logits[:, j * LANES:(j + 1) * LANES]
    x = slab_sc[...].reshape(N_GROUPS, per, n_sub, LANES)
    s4 = 1.0 / (1.0 + jnp.exp(-x))
    c4 = s4 + eb_ref[...].reshape(N_GROUPS, per, n_sub, LANES)
    neg_inf = -jnp.inf
    shape4 = (N_GROUPS, per, n_sub, LANES)
    j_idx = lax.broadcasted_iota(I32, shape4, 1).astype(F32)
    g_idx = lax.broadcasted_iota(I32, shape4, 0).astype(F32)
    flat = g_idx * per + j_idx

    m1 = jnp.max(c4, axis=1, keepdims=True)
    first = jnp.min(jnp.where(c4 == m1, j_idx, per), axis=1, keepdims=True)
    m2 = jnp.max(jnp.where(j_idx == first, neg_inf, c4), axis=1, keepdims=True)
    gs = m1 + m2

    gi = lax.broadcasted_iota(I32, (N_GROUPS, 1, n_sub, LANES), 0).astype(F32)
    keep = jnp.zeros((N_GROUPS, 1, n_sub, LANES), jnp.bool_)
    cur = gs
    for _ in range(TOPK_GROUPS):
        mx = jnp.max(cur, axis=0, keepdims=True)
        pick = gi == jnp.min(jnp.where(cur == mx, gi, N_GROUPS), axis=0, keepdims=True)
        keep = keep | pick
        cur = jnp.where(pick, neg_inf, cur)
    cur = jnp.where(keep, c4, neg_inf)

    idxs, gates = [], []
    for _ in range(TOP_K):
        mx = jnp.max(cur, axis=(0, 1), keepdims=True)
        fi = jnp.min(jnp.where(cur == mx, flat, N_EXPERTS), axis=(0, 1), keepdims=True)
        pick = flat == fi
        gates.append(jnp.sum(jnp.where(pick, s4, 0.0), axis=(0, 1)))
        idxs.append(fi[0, 0])
        cur = jnp.where(pick, neg_inf, cur)
    total = functools.reduce(lambda u, v: u + v, gates)
    for k in range(TOP_K):
        eidx_ref[k] = idxs[k].astype(I32)
        gate_ref[k] = gates[k] / total * ROUTED_SCALE


def moe_router(h, w_router, e_bias, row0, m):
    d = h.shape[1]
    tm = min(ROUTER_TM, m)
    n_sub = tm // LANES
    blk0 = row0 // tm
    bias = jnp.broadcast_to(e_bias.reshape(N_EXPERTS, 1, 1), (N_EXPERTS, n_sub, LANES))
    w_hi = w_router.T.astype(BF16)
    w_split = jnp.concatenate([w_hi, (w_router.T - w_hi.astype(F32)).astype(BF16)], axis=0)
    out = jax.ShapeDtypeStruct((TOP_K, m // LANES, LANES), I32)
    eidx, gate = pl.pallas_call(
        _router_kernel,
        out_shape=(out, jax.ShapeDtypeStruct(out.shape, F32)),
        grid=(m // tm,),
        in_specs=[pl.BlockSpec((tm, d), lambda i: (i + blk0, 0)),
                  pl.BlockSpec((2 * N_EXPERTS, d), lambda i: (0, 0)),
                  pl.BlockSpec((N_EXPERTS, n_sub, LANES), lambda i: (0, 0, 0))],
        out_specs=(pl.BlockSpec((TOP_K, n_sub, LANES), lambda i: (0, i, 0)),
                   pl.BlockSpec((TOP_K, n_sub, LANES), lambda i: (0, i, 0))),
        scratch_shapes=[pltpu.VMEM((N_EXPERTS * n_sub, LANES), F32)],
        compiler_params=_params("parallel"),
        name="moe_router",
    )(h, w_split, bias)
    return eidx.reshape(TOP_K, m), gate.reshape(TOP_K, m)


def _silu(x):
    return x / (1.0 + jnp.exp(-x))


RANK_TM = 1024


def _rank_kernel(eidx_ref, tri_ref, dest_ref, cnt_ref, start_ref, cnt_sc, run_sc):
    p, i = pl.program_id(0), pl.program_id(1)
    tm = eidx_ref.shape[1]
    e = eidx_ref[...]
    ex = lax.broadcasted_iota(I32, (N_EXPERTS, tm), 0)
    onehot = jnp.zeros((N_EXPERTS, tm), F32)
    for k in range(TOP_K):
        onehot = onehot + jnp.where(e[k:k + 1, :] == ex, 1.0, 0.0)
    tile_cnt = jnp.sum(onehot, axis=-1, keepdims=True)

    @pl.when((p == 0) & (i == 0))
    def _():
        cnt_sc[...] = jnp.zeros(cnt_sc.shape, F32)

    @pl.when(p == 0)
    def _():
        cnt_sc[...] += tile_cnt

    @pl.when((p == 1) & (i == 0))
    def _():
        cnt = cnt_sc[...]
        padded = jnp.floor((cnt + (MOE_BLK - 1)) * (1.0 / MOE_BLK)) * MOE_BLK
        below = (lax.broadcasted_iota(I32, (N_EXPERTS, N_EXPERTS), 1)
                 < lax.broadcasted_iota(I32, (N_EXPERTS, N_EXPERTS), 0)).astype(F32)
        start = jnp.dot(below, jnp.broadcast_to(padded, (N_EXPERTS, LANES)),
                        precision=lax.Precision.HIGHEST, preferred_element_type=F32)
        run_sc[...] = start[:, :1]
        cnt_ref[...] = jnp.broadcast_to(cnt, (N_EXPERTS, LANES))
        start_ref[...] = start

    @pl.when(p == 1)
    def _():
        before = jnp.dot(onehot.astype(BF16), tri_ref[...], preferred_element_type=F32) + run_sc[...]
        rows = [jnp.sum(jnp.where(e[k:k + 1, :] == ex, before, 0.0), axis=0, keepdims=True)
                for k in range(TOP_K)]
        dest_ref[...] = jnp.concatenate(rows, axis=0).astype(I32)
        run_sc[...] += tile_cnt


def moe_rank(eidx):
    n_tok = eidx.shape[1]
    tm = min(RANK_TM, n_tok)
    tri = (jnp.arange(tm)[:, None] < jnp.arange(tm)[None, :]).astype(BF16)
    stat = jax.ShapeDtypeStruct((N_EXPERTS, LANES), F32)
    dest, cnt, start = pl.pallas_call(
        _rank_kernel,
        out_shape=(jax.ShapeDtypeStruct((TOP_K, n_tok), I32), stat, stat),
        grid=(2, n_tok // tm),
        in_specs=[pl.BlockSpec((TOP_K, tm), lambda p, i: (0, i)),
                  pl.BlockSpec((tm, tm), lambda p, i: (0, 0))],
        out_specs=(pl.BlockSpec((TOP_K, tm), lambda p, i: (0, i * p)),
                   pl.BlockSpec((N_EXPERTS, LANES), lambda p, i: (0, 0)),
                   pl.BlockSpec((N_EXPERTS, LANES), lambda p, i: (0, 0))),
        scratch_shapes=[pltpu.VMEM((N_EXPERTS, 1), F32), pltpu.VMEM((N_EXPERTS, 1), F32)],
        compiler_params=_params("arbitrary", "arbitrary"),
        name="moe_rank",
    )(eidx, tri)
    return dest, cnt[:, 0].astype(I32), start[:, 0].astype(I32)


def _experts_kernel(blk_e_ref, x_ref, wg_ref, wu_ref, wd_ref, o_ref, wg_sc, wu_sc, wd_sc):
    i = pl.program_id(0)

    @pl.when((i == 0) | (blk_e_ref[i] != blk_e_ref[jnp.maximum(i - 1, 0)]))
    def _():
        wg_sc[...] = wg_ref[...].astype(BF16)
        wu_sc[...] = wu_ref[...].astype(BF16)
        wd_sc[...] = wd_ref[...].astype(BF16)

    x = x_ref[...]
    gate = jnp.dot(x, wg_sc[...], preferred_element_type=F32)
    up = jnp.dot(x, wu_sc[...], preferred_element_type=F32)
    act = (_silu(gate) * up).astype(BF16)
    o_ref[...] = jnp.dot(act, wd_sc[...], preferred_element_type=F32).astype(o_ref.dtype)


def grouped_experts(x_sorted, blk_e, wg, wu, wd, layer):
    n_slot, d = x_sorted.shape
    n_blk = n_slot // MOE_BLK
    by_expert = lambda i, be: (layer, be[i], 0, 0)
    grid_spec = pltpu.PrefetchScalarGridSpec(
        num_scalar_prefetch=1,
        grid=(n_blk,),
        in_specs=[pl.BlockSpec((MOE_BLK, d), lambda i, be: (i, 0)),
                  pl.BlockSpec((None, None, d, D_EXPERT), by_expert),
                  pl.BlockSpec((None, None, d, D_EXPERT), by_expert),
                  pl.BlockSpec((None, None, D_EXPERT, d), by_expert)],
        out_specs=pl.BlockSpec((MOE_BLK, d), lambda i, be: (i, 0)),
        scratch_shapes=[pltpu.VMEM((d, D_EXPERT), BF16), pltpu.VMEM((d, D_EXPERT), BF16),
                        pltpu.VMEM((D_EXPERT, d), BF16)],
    )
    return pl.pallas_call(
        _experts_kernel,
        out_shape=jax.ShapeDtypeStruct((n_slot, d), BF16),
        grid_spec=grid_spec,
        compiler_params=_params("arbitrary"),
        name="grouped_experts",
    )(blk_e, x_sorted, wg, wu, wd)


def _moe_finish_kernel(dils, n_prev, hb_ref, y_ref, gt_ref, wg_ref, wu_ref, wd_ref, h_ref, g_ref, b_ref,
                       *rest):
    o_ref, ob_ref = rest[n_prev:n_prev + 2]
    rest = rest[n_prev + 2:]
    x = hb_ref[...]
    d = x.shape[1]
    act = (_silu(jnp.dot(x, wg_ref[...], preferred_element_type=F32))
           * jnp.dot(x, wu_ref[...], preferred_element_type=F32)).astype(BF16)
    ff = jnp.dot(act, wd_ref[...], preferred_element_type=F32)
    gt = gt_ref[...]
    for k in range(TOP_K):
        ff = ff + gt[:, k:k + 1] * y_ref[k].astype(F32)
    out = _layer_norm_rows(DN_ALPHA * h_ref[...] + ff, g_ref[...], b_ref[...])
    o_ref[...] = out
    ob_ref[...] = out.astype(BF16)
    if dils:
        perm_refs, nat_sc = rest[:-1], rest[-1]
        for j in range(d // LANES):
            nat_sc[j] = out[:, j * LANES:(j + 1) * LANES]
        for p_ref, dil in zip(perm_refs, dils):
            n = out.shape[0] // dil
            for r in range(dil):
                for j in range(d // LANES):
                    p_ref[0, r, :, j * LANES:(j + 1) * LANES] = (
                        nat_sc[j, pl.ds(r, n, stride=dil), :].astype(BF16))


def moe_finish(hb, y_tok, gate_tok, ws_g, ws_u, ws_d, h, g, b, seq, row0, m, dils=(), prev=None, tm=512):
    d = h.shape[1]
    tm = min(tm, seq)
    n_j = seq // tm
    blk0 = row0 // tm
    row = lambda i: (i, 0)
    row_in = lambda i: (i + blk0, 0)
    fixed = lambda i: (0, 0)
    n_all = h.shape[0]
    seq0 = row0 // seq
    out_shape = [jax.ShapeDtypeStruct((n_all, d), F32), jax.ShapeDtypeStruct((n_all, d), BF16)]
    out_specs = [pl.BlockSpec((tm, d), row_in), pl.BlockSpec((tm, d), row_in)]
    for dil in dils:
        out_shape.append(jax.ShapeDtypeStruct((n_all // seq, dil, seq // dil, d), BF16))
        out_specs.append(pl.BlockSpec((1, dil, tm // dil, d), lambda i: (i // n_j + seq0, 0, i % n_j, 0)))
    prev = list(prev) if prev is not None else []
    n_fixed_in = 9
    res = pl.pallas_call(
        functools.partial(_moe_finish_kernel, tuple(dils), len(prev)),
        out_shape=tuple(out_shape),
        input_output_aliases={n_fixed_in + j: j for j in range(len(prev))},
        grid=(m // tm,),
        in_specs=[pl.BlockSpec((tm, d), row_in), pl.BlockSpec((TOP_K, tm, d), lambda i: (0, i, 0)),
                  pl.BlockSpec((tm, TOP_K), row),
                  pl.BlockSpec(ws_g.shape, fixed), pl.BlockSpec(ws_u.shape, fixed),
                  pl.BlockSpec(ws_d.shape, fixed),
                  pl.BlockSpec((tm, d), row_in), pl.BlockSpec((1, d), fixed), pl.BlockSpec((1, d), fixed)]
                 + [pl.BlockSpec(memory_space=pl.ANY)] * len(prev),
        out_specs=tuple(out_specs),
        scratch_shapes=[pltpu.VMEM((d // LANES, tm, LANES), F32)] if dils else [],
        compiler_params=_params("parallel"),
        name="moe_finish",
    )(hb, y_tok, gate_tok, ws_g, ws_u, ws_d, h, g.reshape(1, d), b.reshape(1, d), *prev)
    return list(res)


SORT_E_SHIFT = 19


MOE_SPLITS = 1


def moe_layer(h, hb, w_router, e_bias, wg, wu, wd, layer, ws_g, ws_u, ws_d, g, b, seq, dils=()):
    n_all = h.shape[0]
    n_split = MOE_SPLITS if (n_all // seq) % MOE_SPLITS == 0 else 1
    outs = None
    for sp in range(n_split):
        outs = moe_tokens(h, hb, w_router, e_bias, wg, wu, wd, layer, ws_g, ws_u, ws_d, g, b, seq, dils,
                          sp * (n_all // n_split), n_all // n_split, outs)
    return outs[0], outs[1], [p.reshape(n_all, -1) for p in outs[2:]]


def moe_tokens(h, hb, w_router, e_bias, wg, wu, wd, layer, ws_g, ws_u, ws_d, g, b, seq, dils, row0, n_tok,
               prev):
    d = h.shape[1]
    n_asg = n_tok * TOP_K
    n_pad = N_EXPERTS * MOE_BLK
    assert n_asg <= 1 << (SORT_E_SHIFT - 1) and n_pad <= 1 << (SORT_E_SHIFT - 1)
    eidx, gate = moe_router(h, w_router, e_bias, row0, n_tok)
    dest, counts, pad_start = moe_rank(eidx)
    padded = (counts + MOE_BLK - 1) // MOE_BLK * MOE_BLK
    pad_end = pad_start + padded
    n_blk = n_asg // MOE_BLK + N_EXPERTS
    blk_start = jnp.arange(n_blk, dtype=I32) * MOE_BLK
    blk_e = jnp.minimum(jnp.sum(pad_end[None, :] <= blk_start[:, None], axis=1), N_EXPERTS - 1)

    key_real = (eidx.T.reshape(-1) << SORT_E_SHIFT) | jnp.arange(n_asg, dtype=I32)
    i_pad = jnp.arange(n_pad, dtype=I32)
    e_pad = jnp.sum(jnp.cumsum(padded - counts)[None, :] <= i_pad[:, None], axis=1).astype(I32)
    key_pad = (e_pad << SORT_E_SHIFT) | (1 << (SORT_E_SHIFT - 1)) | i_pad
    low = jnp.sort(jnp.concatenate([key_real, key_pad])) & ((1 << SORT_E_SHIFT) - 1)
    slot_tok = jnp.where(low < (1 << (SORT_E_SHIFT - 1)), low // TOP_K,
                         jnp.arange(n_asg + n_pad, dtype=I32) % n_tok)

    x_sorted = hb.at[slot_tok + row0].get(mode="promise_in_bounds")
    y = grouped_experts(x_sorted, blk_e.astype(I32), wg, wu, wd, layer)
    y_k = y.at[dest.reshape(-1)].get(mode="promise_in_bounds").reshape(TOP_K, n_tok, d)
    return moe_finish(hb, y_k, gate.T, ws_g, ws_u, ws_d, h, g, b, seq, row0, n_tok, dils, prev)


def kernel(x, rel_bias, a_w_in, a_w_o, b_w_in, b_w_o, c_w_in, c_q_norm, c_kv_norm, c_w_q_up,
           c_w_kv_up, c_w_o, ln_g, ln_b, moe_w_router, moe_bias, moe_w_gate, moe_w_up,
           moe_w_down, moe_ws_gate, moe_ws_up, moe_ws_down):
    bsz, seq, d = x.shape
    depth = ln_g.shape[0]
    h = x.reshape(bsz * seq, d)
    hb = h.astype(BF16)
    strip = dsa_bias_strip(rel_bias, seq)
    extra_dils = tuple(dil for _, dil in B_GROUPS if dil > 1)
    hb_perm = []
    for layer in range(depth):
        kind, slot = layer % N_MIXERS, layer // N_MIXERS
        g0, b0 = ln_g[layer, 0], ln_b[layer, 0]
        if kind == 0:
            h, hb = dsa_mixer(hb, a_w_in[slot], a_w_o[slot], strip, h, g0, b0, bsz, seq)
        elif kind == 1:
            hb_by_dil = {1: hb, **dict(zip(extra_dils, hb_perm))}
            h, hb = dilated_mixer(hb_by_dil, b_w_in[slot], b_w_o[slot], rel_bias, h, g0, b0, bsz, seq)
        else:
            h, hb = mla_mixer(hb, c_w_in[slot], c_q_norm[slot], c_kv_norm[slot], c_w_q_up[slot],
                              c_w_kv_up[slot], c_w_o[slot], h, g0, b0, bsz, seq)
        next_dilated = layer + 1 < depth and (layer + 1) % N_MIXERS == 1
        h, hb, hb_perm = moe_layer(h, hb, moe_w_router[layer], moe_bias[layer],
                                   moe_w_gate, moe_w_up, moe_w_down, layer,
                                   moe_ws_gate[layer].astype(BF16),
                                   moe_ws_up[layer].astype(BF16), moe_ws_down[layer].astype(BF16),
                                   ln_g[layer, 1], ln_b[layer, 1], seq,
                                   extra_dils if next_dilated else ())
    return h.reshape(bsz, seq, d)
```
